```python
import math
import jax, jax.numpy as jnp
from jax import lax
import numpy as np

D_MODEL = 1024
BATCH = 8
SEQ = 2048
DEPTH = 1
DEC_BATCH = 2
DEC_SEQ = 8192
PAST_LEN = 128

HEAD_DIM = 64
D_MIX = D_MODEL
D_NA = D_MIX // 2
D_WA = D_MIX - D_NA
N_HEADS_NA = D_NA // HEAD_DIM
N_HEADS_WA = D_WA // HEAD_DIM
N_KV_WA = 2
GQA_GROUP = N_HEADS_WA // N_KV_WA
D_KV_WA = N_KV_WA * HEAD_DIM
D_IN_PROJ = 3 * D_NA + D_WA + 2 * D_KV_WA
GRID_W = 64
NA_WIN_R = 8
NA_WIN_C = 16
NA_QB = 16
NA_KB = NA_QB + NA_WIN_C
NA_NB = GRID_W // NA_QB
WA_WINDOW = 128
WA_BLOCK = 128
T5_BUCKETS = 32
T5_MAX_EXACT = 8
T5_MAX_DIST = 128
N_EXPERTS = 32
TOP_K = 4
D_EXPERT = D_MODEL
SWIGLU_LIMIT = 7.0
SWIGLU_ALPHA = 1.702
MOE_BLOCK = 256
RMS_EPS = 1e-6
NEG_INF = -1e30

kernel_name = "hybrid_natten_swa_moe_encoder"


def rms_norm(x, g):
    xf = x.astype(jnp.float32)
    y = xf * lax.rsqrt(jnp.mean(xf * xf, axis=-1, keepdims=True) + RMS_EPS)
    return (y * g.astype(jnp.float32)).astype(x.dtype)


def t5_bucket(rel):
    half = T5_BUCKETS // 2
    ret = (rel > 0).astype(jnp.int32) * half
    n = jnp.abs(rel)
    nf = jnp.maximum(n, 1).astype(jnp.float32)
    large = T5_MAX_EXACT + (jnp.log(nf / T5_MAX_EXACT) / math.log(T5_MAX_DIST / T5_MAX_EXACT)
                            * (half - T5_MAX_EXACT)).astype(jnp.int32)
    large = jnp.minimum(large, half - 1)
    return ret + jnp.where(n < T5_MAX_EXACT, n, large)


def neighbourhood_attention(q, k, v, rpb):
    B, T, _ = q.shape
    rows = T // GRID_W
    wr = min(NA_WIN_R, rows)
    scale = HEAD_DIM ** -0.5
    qb = q.reshape(B, rows, NA_NB, NA_QB, N_HEADS_NA, HEAD_DIM)
    kg = k.reshape(B, rows, GRID_W, N_HEADS_NA, HEAD_DIM)
    vg = v.reshape(B, rows, GRID_W, N_HEADS_NA, HEAD_DIM)
    r = jnp.arange(rows)
    row_start = jnp.clip(r - wr // 2, 0, rows - wr)
    row_idx = row_start[:, None] + jnp.arange(wr)[None, :]
    nb = jnp.arange(NA_NB)
    blk_start = jnp.clip(nb * NA_QB - NA_WIN_C // 2, 0, GRID_W - NA_KB)
    col_idx = blk_start[:, None] + jnp.arange(NA_KB)[None, :]
    ri = row_idx[:, :, None, None]
    ci = col_idx[None, None, :, :]
    kk = kg[:, ri, ci]
    vv = vg[:, ri, ci]
    s = jnp.einsum('brnqhd,brwnkhd->bhrnqwk', qb, kk).astype(jnp.float32) * scale
    q_col = nb[:, None] * NA_QB + jnp.arange(NA_QB)[None, :]
    win_start = jnp.clip(q_col - NA_WIN_C // 2, 0, GRID_W - NA_WIN_C)
    kc = col_idx[:, None, :]
    col_valid = (kc >= win_start[..., None]) & (kc < win_start[..., None] + NA_WIN_C)
    dc = jnp.clip(kc - q_col[..., None], -(NA_WIN_C - 1), NA_WIN_C - 1) + NA_WIN_C - 1
    dr = row_idx - r[:, None] + NA_WIN_R - 1
    bias = rpb[:, dr[:, None, None, :, None], dc[None, :, :, None, :]]
    s = jnp.where(col_valid[None, None, None, :, :, None, :],
                  s + bias[None].astype(jnp.float32), NEG_INF)
    sh = s.shape
    p = jax.nn.softmax(s.reshape(sh[:-2] + (wr * NA_KB,)), axis=-1).reshape(sh)
    o = jnp.einsum('bhrnqwk,brwnkhd->brnqhd', p.astype(v.dtype), vv)
    return o.reshape(B, T, D_NA)


def window_gqa_attention(q, k, v, t5_table, sink):
    B, T, _ = q.shape
    nb = T // WA_BLOCK
    scale = HEAD_DIM ** -0.5
    qb = q.reshape(B, nb, WA_BLOCK, N_KV_WA, GQA_GROUP, HEAD_DIM)
    pad = ((0, 0), (WA_BLOCK, WA_BLOCK), (0, 0))
    kp = jnp.pad(k, pad).reshape(B, nb + 2, WA_BLOCK, N_KV_WA, HEAD_DIM)
    vp = jnp.pad(v, pad).reshape(B, nb + 2, WA_BLOCK, N_KV_WA, HEAD_DIM)
    kb = jnp.concatenate([kp[:, :-2], kp[:, 1:-1], kp[:, 2:]], axis=2)
    vb = jnp.concatenate([vp[:, :-2], vp[:, 1:-1], vp[:, 2:]], axis=2)
    s = jnp.einsum('bnqhgd,bnkhd->bnhgqk', qb, kb).astype(jnp.float32) * scale
    rel = jnp.arange(3 * WA_BLOCK)[None, :] - WA_BLOCK - jnp.arange(WA_BLOCK)[:, None]
    bias = t5_table[t5_bucket(rel)]
    bias = jnp.transpose(bias, (2, 0, 1)).reshape(N_KV_WA, GQA_GROUP, WA_BLOCK, 3 * WA_BLOCK)
    key_pos = jnp.arange(nb)[:, None] * WA_BLOCK - WA_BLOCK + jnp.arange(3 * WA_BLOCK)[None, :]
    valid = (jnp.abs(rel) <= WA_WINDOW)[None] & ((key_pos >= 0) & (key_pos < T))[:, None, :]
    s = jnp.where(valid[None, :, None, None], s + bias[None, None].astype(jnp.float32), NEG_INF)
    sink_logit = jnp.broadcast_to(sink.reshape(N_KV_WA, GQA_GROUP, 1, 1).astype(jnp.float32),
                                  s.shape[:-1] + (1,))
    p = jax.nn.softmax(jnp.concatenate([s, sink_logit], axis=-1), axis=-1)[..., :-1]
    o = jnp.einsum('bnhgqk,bnkhd->bnqhgd', p.astype(v.dtype), vb)
    return o.reshape(B, T, D_WA)


def clamped_swiglu(gu):
    glu = jnp.minimum(gu[..., :D_EXPERT], SWIGLU_LIMIT)
    lin = jnp.clip(gu[..., D_EXPERT:], -SWIGLU_LIMIT, SWIGLU_LIMIT)
    return glu * jax.nn.sigmoid(SWIGLU_ALPHA * glu) * (lin + 1.0)


def routed_moe(h, w_router, b_router, w_gate_up, b_gate_up, w_down, b_down):
    B, T, D = h.shape
    xf = h.reshape(B * T, D)
    n_tok = B * T
    n_assign = n_tok * TOP_K
    logits = (xf @ w_router + b_router).astype(jnp.float32)
    top_logit, top_idx = lax.top_k(logits, TOP_K)
    gates = jax.nn.softmax(top_logit, axis=-1)
    flat_e = top_idx.reshape(-1).astype(jnp.int32)
    flat_g = gates.reshape(-1)
    order = jnp.argsort(flat_e)
    sorted_e = flat_e[order]
    counts = jnp.bincount(flat_e, length=N_EXPERTS)
    padded = ((counts + MOE_BLOCK - 1) // MOE_BLOCK) * MOE_BLOCK
    pad_end = jnp.cumsum(padded)
    pad_start = pad_end - padded
    start = jnp.cumsum(counts) - counts
    dest = pad_start[sorted_e] + jnp.arange(n_assign) - start[sorted_e]
    n_blocks = -(-n_assign // MOE_BLOCK) + N_EXPERTS
    n_rows = n_blocks * MOE_BLOCK
    row_tok = jnp.zeros((n_rows,), jnp.int32).at[dest].set((order // TOP_K).astype(jnp.int32))
    row_gate = jnp.zeros((n_rows,), jnp.float32).at[dest].set(flat_g[order])
    blk_e = jnp.minimum(jnp.searchsorted(pad_end, jnp.arange(n_blocks) * MOE_BLOCK, side='right'),
                        N_EXPERTS - 1)
    xs = xf[row_tok].reshape(n_blocks, MOE_BLOCK, D)

    def expert_block(args):
        xb, e = args
        a = clamped_swiglu(xb @ w_gate_up[e] + b_gate_up[e])
        return a @ w_down[e] + b_down[e]

    ys = lax.map(expert_block, (xs, blk_e)).reshape(n_rows, D)
    y = jnp.zeros_like(xf).at[row_tok].add(ys * row_gate[:, None].astype(ys.dtype))
    return y.reshape(B, T, D)


def encoder_layer(x, c, w_ada, b_ada, pre_mix, post_mix, pre_ffn, post_ffn, w_in, na_rel_bias,
                  t5_rel_bias, wa_sink, gn_na, gn_wa, w_out, w_router, b_router,
                  w_gate_up, b_gate_up, w_down, b_down):
    mod = jax.nn.silu(c) @ w_ada + b_ada
    sh1, sc1, g1, sh2, sc2, g2 = jnp.split(mod, 6, axis=-1)
    h = rms_norm(x, pre_mix) * (1.0 + sc1[:, None, :]) + sh1[:, None, :]
    proj = h @ w_in
    o0 = 3 * D_NA
    q_a, k_a, v_a = proj[..., :D_NA], proj[..., D_NA:2 * D_NA], proj[..., 2 * D_NA:o0]
    q_b = proj[..., o0:o0 + D_WA]
    k_b = proj[..., o0 + D_WA:o0 + D_WA + D_KV_WA]
    v_b = proj[..., o0 + D_WA + D_KV_WA:]
    out_a = neighbourhood_attention(q_a, k_a, v_a, na_rel_bias)
    out_b = window_gqa_attention(q_b, k_b, v_b, t5_rel_bias, wa_sink)
    mixed = jnp.concatenate([rms_norm(out_a, gn_na), rms_norm(out_b, gn_wa)], axis=-1) @ w_out
    x = x + g1[:, None, :] * rms_norm(mixed, post_mix)
    h = rms_norm(x, pre_ffn) * (1.0 + sc2[:, None, :]) + sh2[:, None, :]
    m = routed_moe(h, w_router, b_router, w_gate_up, b_gate_up, w_down, b_down)
    x = x + g2[:, None, :] * rms_norm(m, post_ffn)
    return x


def setup_inputs(seed: int = 0) -> dict:
    key = jax.random.key(seed)
    ks = jax.random.split(key, 24)

    def nrm(k, shape, scale):
        return jax.random.normal(k, shape, jnp.float32) * scale

    D = D_MODEL
    return {
        "x_prompt": nrm(ks[0], (BATCH, SEQ, D), 1.0),
        "x_sample": nrm(ks[1], (DEC_BATCH, DEC_SEQ, D), 1.0),
        "c_prompt": nrm(ks[2], (BATCH, D), 1.0),
        "c_sample": nrm(ks[3], (DEC_BATCH, D), 1.0),
        "w_ada": nrm(ks[4], (DEPTH, D, 6 * D), 0.5 * D ** -0.5),
        "b_ada": nrm(ks[5], (DEPTH, 6 * D), 0.02),
        "pre_mix": 1.0 + nrm(ks[6], (DEPTH, D), 0.05),
        "post_mix": 1.0 + nrm(ks[7], (DEPTH, D), 0.05),
        "pre_ffn": 1.0 + nrm(ks[8], (DEPTH, D), 0.05),
        "post_ffn": 1.0 + nrm(ks[9], (DEPTH, D), 0.05),
        "w_in": nrm(ks[10], (DEPTH, D, D_IN_PROJ), D ** -0.5),
        "na_rel_bias": nrm(ks[11], (DEPTH, N_HEADS_NA, 2 * NA_WIN_R - 1, 2 * NA_WIN_C - 1), 0.1),
        "t5_rel_bias": nrm(ks[12], (T5_BUCKETS, N_HEADS_WA), 0.1),
        "wa_sink": nrm(ks[13], (DEPTH, N_HEADS_WA), 0.5),
        "gn_na": 1.0 + nrm(ks[14], (DEPTH, D_NA), 0.05),
        "gn_wa": 1.0 + nrm(ks[15], (DEPTH, D_WA), 0.05),
        "w_out": nrm(ks[16], (DEPTH, D_MIX, D), D_MIX ** -0.5),
        "w_router": nrm(ks[17], (DEPTH, D, N_EXPERTS), D ** -0.5),
        "b_router": nrm(ks[18], (DEPTH, N_EXPERTS), 0.01),
        "w_gate_up": nrm(ks[19], (DEPTH, N_EXPERTS, D, 2 * D_EXPERT), D ** -0.5),
        "b_gate_up": nrm(ks[20], (DEPTH, N_EXPERTS, 2 * D_EXPERT), 0.02),
        "w_down": nrm(ks[21], (DEPTH, N_EXPERTS, D_EXPERT, D), D_EXPERT ** -0.5),
        "b_down": nrm(ks[22], (DEPTH, N_EXPERTS, D), 0.02),
    }


def reference(x_prompt, x_sample, c_prompt, c_sample, w_ada, b_ada, pre_mix, post_mix, pre_ffn,
              post_ffn, w_in, na_rel_bias, t5_rel_bias, wa_sink, gn_na, gn_wa, w_out, w_router,
              b_router, w_gate_up, b_gate_up, w_down, b_down):
    y_prompt = x_prompt
    y_sample = x_sample
    for l in range(DEPTH):
        layer_params = (w_ada[l], b_ada[l], pre_mix[l], post_mix[l], pre_ffn[l], post_ffn[l],
                        w_in[l], na_rel_bias[l], t5_rel_bias, wa_sink[l], gn_na[l], gn_wa[l],
                        w_out[l], w_router[l], b_router[l], w_gate_up[l], b_gate_up[l],
                        w_down[l], b_down[l])
        y_prompt = encoder_layer(y_prompt, c_prompt, *layer_params)
        y_sample = encoder_layer(y_sample, c_sample, *layer_params)
    return (y_prompt, y_sample)
```

```python
import functools
import math

import jax
import jax.numpy as jnp
import numpy as np
from jax import lax
from jax.experimental import pallas as pl
from jax.experimental.pallas import tpu as pltpu

F32 = jnp.float32
BF16 = jnp.bfloat16

D_MODEL = 1024
HEAD_DIM = 64
D_NA = 512
D_WA = 512
N_HEADS_NA = 8
N_HEADS_WA = 8
N_KV_WA = 2
GQA_GROUP = 4
D_KV_WA = 128
D_IN_PROJ = 3 * D_NA + D_WA + 2 * D_KV_WA
GRID_W = 64
NA_WIN_R = 8
NA_WIN_C = 16
WA_WINDOW = 128
WA_BLOCK = 128
T5_BUCKETS = 32
T5_MAX_EXACT = 8
T5_MAX_DIST = 128
N_EXPERTS = 32
TOP_K = 4
D_EXPERT = D_MODEL
SWIGLU_LIMIT = 7.0
SWIGLU_ALPHA = 1.702
RMS_EPS = 1e-6
NEG_INF = -1e30

VMEM_LIMIT_BYTES = 56 * 1024 * 1024
LANES = 128

TOKEN_TILE = 512
NA_ROWS_PER_STEP = 8
EXPERT_BLOCK = 256
COMBINE_TILE = 256


def _cparams(*sem):
    return pltpu.CompilerParams(dimension_semantics=sem, vmem_limit_bytes=VMEM_LIMIT_BYTES)


def _rms(x, g):
    return x * lax.rsqrt(jnp.mean(x * x, axis=-1, keepdims=True) + RMS_EPS) * g


def _ada_kernel(c_ref, w_ref, b_ref, o_ref):
    c = c_ref[...]
    s = c * (1.0 / (1.0 + jnp.exp(-c)))
    o_ref[...] = jnp.dot(s, w_ref[...], preferred_element_type=F32,
                         precision=lax.Precision.HIGHEST) + b_ref[...]


def _ada(c_all, w_ada, b_ada):
    nb = c_all.shape[0]
    return pl.pallas_call(
        _ada_kernel,
        grid=(6,),
        in_specs=[pl.BlockSpec((nb, D_MODEL), lambda j: (0, 0)),
                  pl.BlockSpec((D_MODEL, D_MODEL), lambda j: (0, j)),
                  pl.BlockSpec((1, D_MODEL), lambda j: (0, j))],
        out_specs=pl.BlockSpec((nb, D_MODEL), lambda j: (0, j)),
        out_shape=jax.ShapeDtypeStruct((nb, 6 * D_MODEL), F32),
        compiler_params=_cparams("arbitrary"),
        name="ada",
    )(c_all, w_ada, b_ada.reshape(1, -1))


def _inproj_kernel(x_ref, mod_ref, g_ref, w_ref, o_ref):
    x = x_ref[0]
    h = _rms(x, g_ref[...]) * (1.0 + mod_ref[0, 1:2, :]) + mod_ref[0, 0:1, :]
    o_ref[0] = jnp.dot(h.astype(BF16), w_ref[...], preferred_element_type=F32).astype(BF16)


def _in_proj(x, mod, pre_mix, w_in_bf16):
    B, T, D = x.shape
    tm = TOKEN_TILE
    return pl.pallas_call(
        _inproj_kernel,
        grid=(B, T // tm),
        in_specs=[pl.BlockSpec((1, tm, D), lambda b, i: (b, i, 0)),
                  pl.BlockSpec((1, 6, D), lambda b, i: (b, 0, 0)),
                  pl.BlockSpec((1, D), lambda b, i: (0, 0)),
                  pl.BlockSpec((D, D_IN_PROJ), lambda b, i: (0, 0))],
        out_specs=pl.BlockSpec((1, tm, D_IN_PROJ), lambda b, i: (b, i, 0)),
        out_shape=jax.ShapeDtypeStruct((B, T, D_IN_PROJ), BF16),
        compiler_params=_cparams("parallel", "parallel"),
        name="in_proj",
    )(x, mod, pre_mix.reshape(1, D), w_in_bf16)


def _na_bias_table(rpb):
    qc = np.arange(GRID_W)[:, None]
    kc = np.arange(GRID_W)[None, :]
    win_start = np.clip(qc - NA_WIN_C // 2, 0, GRID_W - NA_WIN_C)
    valid = (kc >= win_start) & (kc < win_start + NA_WIN_C)
    dc = np.clip(kc - qc, -(NA_WIN_C - 1), NA_WIN_C - 1) + NA_WIN_C - 1
    full = rpb[:, :, dc]
    full = jnp.where(valid[None, None], full.astype(F32), NEG_INF)
    two = jnp.concatenate([full[:, :-1], full[:, 1:]], axis=-1)
    two = two.reshape(N_HEADS_NA // 2, 2, 2 * NA_WIN_R - 2, GRID_W, 2 * GRID_W)
    return jnp.transpose(two, (0, 2, 1, 3, 4)).reshape(
        N_HEADS_NA // 2, 2 * NA_WIN_R - 2, 2 * GRID_W, 2 * GRID_W)


def _na_kernel(q_ref, kp_ref, kc_ref, kn_ref, vp_ref, vc_ref, vn_ref, bias_ref, gn_ref, o_ref,
               k_scr, v_scr, *, rows):
    i = pl.program_id(1)
    R = NA_ROWS_PER_STEP
    blk = R * GRID_W
    k_scr[0:blk] = kp_ref[0]
    k_scr[blk:2 * blk] = kc_ref[0]
    k_scr[2 * blk:3 * blk] = kn_ref[0]
    v_scr[0:blk] = vp_ref[0]
    v_scr[blk:2 * blk] = vc_ref[0]
    v_scr[2 * blk:3 * blk] = vn_ref[0]
    lane = lax.broadcasted_iota(jnp.int32, (1, LANES), 1)
    lo = lane < HEAD_DIM
    scale = HEAD_DIM ** -0.5
    gn = gn_ref[...]

    def row_body(j, carry):
        r = i * R + j
        row_start = jnp.clip(r - NA_WIN_R // 2, 0, rows - NA_WIN_R)
        shift = r - row_start
        koff = pl.multiple_of((row_start - i * R + R) * GRID_W, GRID_W)
        qoff = pl.multiple_of(j * GRID_W, GRID_W)
        outs = []
        for p in range(N_HEADS_NA // 2):
            cs = slice(p * LANES, (p + 1) * LANES)
            q2 = q_ref[0, pl.ds(qoff, GRID_W), cs]
            zero = jnp.zeros_like(q2)
            qq = jnp.concatenate([jnp.where(lo, q2, zero), jnp.where(lo, zero, q2)], axis=0)
            k2 = k_scr[pl.ds(koff, NA_WIN_R * GRID_W), cs]
            v2 = v_scr[pl.ds(koff, NA_WIN_R * GRID_W), cs]
            s = lax.dot_general(qq, k2, (((1,), (1,)), ((), ())),
                                preferred_element_type=F32) * scale
            parts = []
            for m in range(NA_WIN_R // 2):
                d = 2 * m - shift + (NA_WIN_R - 1)
                parts.append(s[:, m * LANES:(m + 1) * LANES] + bias_ref[p, d])
            s = jnp.concatenate(parts, axis=-1)
            mx = jnp.max(s, axis=-1, keepdims=True)
            e = jnp.exp(s - mx)
            l = jnp.sum(e, axis=-1, keepdims=True)
            o2 = jnp.dot(e.astype(BF16), v2, preferred_element_type=F32) / l
            outs.append(jnp.where(lo, o2[:GRID_W], o2[GRID_W:]))
        o = jnp.concatenate(outs, axis=-1)
        o_ref[0, pl.ds(qoff, GRID_W), :] = _rms(o, gn).astype(BF16)
        return carry

    lax.fori_loop(0, R, row_body, 0)


def _na_attention(proj, bias_tab, gn_na):
    B, T, _ = proj.shape
    rows = T // GRID_W
    R = NA_ROWS_PER_STEP
    blk = R * GRID_W
    nblk = rows // R
    qspec = pl.BlockSpec((1, blk, D_NA), lambda b, i: (b, i, 0))

    def kv(col, delta):
        return pl.BlockSpec((1, blk, D_NA),
                            lambda b, i: (b, jnp.clip(i + delta, 0, nblk - 1), col))

    return pl.pallas_call(
        functools.partial(_na_kernel, rows=rows),
        grid=(B, nblk),
        in_specs=[qspec, kv(1, -1), kv(1, 0), kv(1, 1), kv(2, -1), kv(2, 0), kv(2, 1),
                  pl.BlockSpec(bias_tab.shape, lambda b, i: (0, 0, 0, 0)),
                  pl.BlockSpec((1, D_NA), lambda b, i: (0, 0))],
        out_specs=pl.BlockSpec((1, blk, D_NA), lambda b, i: (b, i, 0)),
        out_shape=jax.ShapeDtypeStruct((B, T, D_NA), BF16),
        scratch_shapes=[pltpu.VMEM((3 * blk, D_NA), BF16), pltpu.VMEM((3 * blk, D_NA), BF16)],
        compiler_params=_cparams("parallel", "parallel"),
        name="na_attn",
    )(proj, proj, proj, proj, proj, proj, proj, bias_tab, gn_na.reshape(1, D_NA))


def _wa_head_perm():
    cols = []
    for p in range(GQA_GROUP):
        cols.append(np.arange(p * HEAD_DIM, (p + 1) * HEAD_DIM))
        cols.append(np.arange((p + GQA_GROUP) * HEAD_DIM, (p + GQA_GROUP + 1) * HEAD_DIM))
    return np.concatenate(cols)


def _t5_bucket(rel):
    half = T5_BUCKETS // 2
    ret = (rel > 0).astype(jnp.int32) * half
    n = jnp.abs(rel)
    nf = jnp.maximum(n, 1).astype(F32)
    large = T5_MAX_EXACT + (jnp.log(nf / T5_MAX_EXACT) / math.log(T5_MAX_DIST / T5_MAX_EXACT)
                            * (half - T5_MAX_EXACT)).astype(jnp.int32)
    large = jnp.minimum(large, half - 1)
    return ret + jnp.where(n < T5_MAX_EXACT, n, large)


def _wa_bias_table(t5_table):
    rel = jnp.arange(3 * WA_BLOCK)[None, :] - WA_BLOCK - jnp.arange(WA_BLOCK)[:, None]
    bias = t5_table[_t5_bucket(rel)].astype(F32)
    bias = jnp.where((jnp.abs(rel) <= WA_WINDOW)[..., None], bias, NEG_INF)
    bias = jnp.transpose(bias, (2, 0, 1))
    return jnp.concatenate([bias[:GQA_GROUP], bias[GQA_GROUP:]], axis=1)


def _wa_kernel(sink_ref, q_ref, kp_ref, kc_ref, kn_ref, vp_ref, vc_ref, vn_ref, bias_ref, gn_ref,
               o_ref):
    i = pl.program_id(1)
    nb = pl.num_programs(1)
    W = WA_BLOCK
    lane = lax.broadcasted_iota(jnp.int32, (1, LANES), 1)
    lo = lane < HEAD_DIM
    row = lax.broadcasted_iota(jnp.int32, (2 * W, 1), 0)
    scale = HEAD_DIM ** -0.5
    kcat = jnp.concatenate([kp_ref[0], kc_ref[0], kn_ref[0]], axis=0)
    vcat = jnp.concatenate([vp_ref[0], vc_ref[0], vn_ref[0]], axis=0)
    pen_prev = jnp.where(i == 0, NEG_INF, 0.0).astype(F32)
    pen_next = jnp.where(i == nb - 1, NEG_INF, 0.0).astype(F32)
    col = lax.broadcasted_iota(jnp.int32, (1, 3 * W), 1)
    pen = jnp.where(col < W, pen_prev, jnp.where(col >= 2 * W, pen_next, 0.0))
    outs = []
    for p in range(GQA_GROUP):
        q2 = q_ref[0, :, p * LANES:(p + 1) * LANES]
        zero = jnp.zeros_like(q2)
        qq = jnp.concatenate([jnp.where(lo, q2, zero), jnp.where(lo, zero, q2)], axis=0)
        s = lax.dot_general(qq, kcat, (((1,), (1,)), ((), ())),
                            preferred_element_type=F32) * scale
        s = s + bias_ref[p] + pen
        sink = jnp.where(row < W, sink_ref[p], sink_ref[p + GQA_GROUP])
        mx = jnp.maximum(jnp.max(s, axis=-1, keepdims=True), sink)
        e = jnp.exp(s - mx)
        l = jnp.sum(e, axis=-1, keepdims=True) + jnp.exp(sink - mx)
        o2 = jnp.dot(e.astype(BF16), vcat, preferred_element_type=F32) / l
        outs.append(jnp.where(lo, o2[:W], o2[W:]))
    o = jnp.concatenate(outs, axis=-1)
    o_ref[0] = _rms(o, gn_ref[...]).astype(BF16)


def _wa_attention(proj, bias_tab, sink, gn_wa_perm):
    B, T, _ = proj.shape
    W = WA_BLOCK
    nb = T // W
    q_col = 3 * D_NA // D_WA
    k_col = (3 * D_NA + D_WA) // D_KV_WA
    v_col = k_col + 1

    def kv(col, delta):
        return pl.BlockSpec((1, W, D_KV_WA),
                            lambda b, i: (b, jnp.clip(i + delta, 0, nb - 1), col))

    return pl.pallas_call(
        _wa_kernel,
        grid=(B, nb),
        in_specs=[pl.BlockSpec(memory_space=pltpu.SMEM),
                  pl.BlockSpec((1, W, D_WA), lambda b, i: (b, i, q_col)),
                  kv(k_col, -1), kv(k_col, 0), kv(k_col, 1),
                  kv(v_col, -1), kv(v_col, 0), kv(v_col, 1),
                  pl.BlockSpec(bias_tab.shape, lambda b, i: (0, 0, 0)),
                  pl.BlockSpec((1, D_WA), lambda b, i: (0, 0))],
        out_specs=pl.BlockSpec((1, W, D_WA), lambda b, i: (b, i, 0)),
        out_shape=jax.ShapeDtypeStruct((B, T, D_WA), BF16),
        compiler_params=_cparams("parallel", "parallel"),
        name="wa_attn",
    )(sink, proj, proj, proj, proj, proj, proj, proj, bias_tab, gn_wa_perm.reshape(1, D_WA))


def _postmix_kernel(na_ref, wa_ref, x_ref, mod_ref, wo_ref, pm_ref, pf_ref, wr_ref, br_ref,
                    x1_ref, h2_ref, idx_ref, gate_ref):
    a = jnp.concatenate([na_ref[0], wa_ref[0]], axis=-1)
    mixed = jnp.dot(a, wo_ref[...], preferred_element_type=F32)
    x1 = x_ref[0] + mod_ref[0, 2:3, :] * _rms(mixed, pm_ref[...])
    x1_ref[0] = x1
    h2 = _rms(x1, pf_ref[...]) * (1.0 + mod_ref[0, 4:5, :]) + mod_ref[0, 3:4, :]
    h2_ref[0] = h2
    logits = lax.dot_general(wr_ref[...], h2, (((1,), (1,)), ((), ())),
                             preferred_element_type=F32,
                             precision=lax.Precision.HIGHEST) + br_ref[...]
    eidx = lax.broadcasted_iota(jnp.int32, logits.shape, 0)
    tops, idxs = [], []
    l = logits
    for _ in range(TOP_K):
        m = jnp.max(l, axis=0, keepdims=True)
        sel = jnp.min(jnp.where(l == m, eidx, N_EXPERTS), axis=0, keepdims=True)
        tops.append(m)
        idxs.append(sel)
        l = jnp.where(eidx == sel, -jnp.inf, l)
    es = [jnp.exp(t - tops[0]) for t in tops]
    tot = es[0] + es[1] + es[2] + es[3]
    idx_ref[0] = jnp.concatenate(idxs, axis=0)
    gate_ref[0] = jnp.concatenate([e / tot for e in es], axis=0)


def _post_mix(na, wa, x, mod, w_out_bf16, post_mix, pre_ffn, w_router_t, b_router):
    B, T, D = x.shape
    tm = TOKEN_TILE
    nt = T // tm
    tok = lambda b, i: (b, i, 0)
    const2 = lambda b, i: (0, 0)
    return pl.pallas_call(
        _postmix_kernel,
        grid=(B, nt),
        in_specs=[pl.BlockSpec((1, tm, D_NA), tok),
                  pl.BlockSpec((1, tm, D_WA), tok),
                  pl.BlockSpec((1, tm, D), tok),
                  pl.BlockSpec((1, 6, D), lambda b, i: (b, 0, 0)),
                  pl.BlockSpec((D, D), const2),
                  pl.BlockSpec((1, D), const2),
                  pl.BlockSpec((1, D), const2),
                  pl.BlockSpec((N_EXPERTS, D), const2),
                  pl.BlockSpec((N_EXPERTS, 1), const2)],
        out_specs=[pl.BlockSpec((1, tm, D), tok),
                   pl.BlockSpec((1, tm, D), tok),
                   pl.BlockSpec((1, TOP_K, tm), lambda b, i: (b * nt + i, 0, 0)),
                   pl.BlockSpec((1, TOP_K, tm), lambda b, i: (b * nt + i, 0, 0))],
        out_shape=[jax.ShapeDtypeStruct((B, T, D), F32),
                   jax.ShapeDtypeStruct((B, T, D), F32),
                   jax.ShapeDtypeStruct((B * nt, TOP_K, tm), jnp.int32),
                   jax.ShapeDtypeStruct((B * nt, TOP_K, tm), F32)],
        compiler_params=_cparams("parallel", "parallel"),
        name="post_mix",
    )(na, wa, x, mod, w_out_bf16, post_mix.reshape(1, D), pre_ffn.reshape(1, D),
      w_router_t, b_router.reshape(N_EXPERTS, 1))


def _expert_kernel(blk_e_ref, nused_ref, rows_hbm, h_hbm, wgu_ref, bgu_ref, wd_ref, bd_ref, o_ref,
                   idx_smem, xbuf, wgu_bf, wd_bf, idx_sem, row_sem):
    i = pl.program_id(0)
    n_used = nused_ref[0]
    BM = EXPERT_BLOCK

    def idx_copy(blk, slot):
        return pltpu.make_async_copy(rows_hbm.at[blk], idx_smem.at[slot], idx_sem.at[slot])

    def row_copy(tok, r, slot):
        return pltpu.make_async_copy(h_hbm.at[pl.ds(tok, 1), :],
                                     xbuf.at[slot, pl.ds(r, 1), :], row_sem.at[slot])

    def start_rows(slot):
        def body(r, c):
            row_copy(idx_smem[slot, 0, r], r, slot).start()
            return c
        lax.fori_loop(0, BM, body, 0)

    def wait_rows(slot):
        def body(r, c):
            row_copy(0, r, slot).wait()
            return c
        lax.fori_loop(0, BM, body, 0)

    slot = lax.rem(i, 2)

    @pl.when(i == 0)
    def _():
        idx_copy(0, 0).start()
        idx_copy(0, 0).wait()
        start_rows(0)

        @pl.when(n_used > 1)
        def _():
            idx_copy(1, 1).start()

    @pl.when(i < n_used)
    def _():
        @pl.when(i + 1 < n_used)
        def _():
            idx_copy(i + 1, 1 - slot).wait()
            start_rows(1 - slot)

        new_expert = jnp.logical_or(i == 0, blk_e_ref[i] != blk_e_ref[jnp.maximum(i - 1, 0)])

        @pl.when(new_expert)
        def _():
            wgu_bf[...] = wgu_ref[0].astype(BF16)
            wd_bf[...] = wd_ref[0].astype(BF16)

        wait_rows(slot)

        @pl.when(i + 2 < n_used)
        def _():
            idx_copy(i + 2, slot).start()

        x = xbuf[slot].astype(BF16)
        gu = jnp.dot(x, wgu_bf[...], preferred_element_type=F32) + bgu_ref[0]
        glu = jnp.minimum(gu[:, :D_EXPERT], SWIGLU_LIMIT)
        lin = jnp.clip(gu[:, D_EXPERT:], -SWIGLU_LIMIT, SWIGLU_LIMIT)
        act = glu * (1.0 / (1.0 + jnp.exp(-SWIGLU_ALPHA * glu))) * (lin + 1.0)
        o_ref[...] = jnp.dot(act.astype(BF16), wd_bf[...], preferred_element_type=F32) + bd_ref[0]

    @pl.when(i >= n_used)
    def _():
        o_ref[...] = jnp.zeros_like(o_ref)


def _experts(h2_flat, row_tok, blk_e, n_used, w_gate_up, b_gate_up, w_down, b_down):
    N, D = h2_flat.shape
    BM = EXPERT_BLOCK
    n_blocks = blk_e.shape[0]
    grid_spec = pltpu.PrefetchScalarGridSpec(
        num_scalar_prefetch=2,
        grid=(n_blocks,),
        in_specs=[pl.BlockSpec(memory_space=pl.ANY),
                  pl.BlockSpec(memory_space=pl.ANY),
                  pl.BlockSpec((1, D, 2 * D_EXPERT), lambda i, be, nu: (be[i], 0, 0)),
                  pl.BlockSpec((1, 1, 2 * D_EXPERT), lambda i, be, nu: (be[i], 0, 0)),
                  pl.BlockSpec((1, D_EXPERT, D), lambda i, be, nu: (be[i], 0, 0)),
                  pl.BlockSpec((1, 1, D), lambda i, be, nu: (be[i], 0, 0))],
        out_specs=pl.BlockSpec((BM, D), lambda i, be, nu: (i, 0)),
        scratch_shapes=[pltpu.SMEM((2, 1, BM), jnp.int32),
                        pltpu.VMEM((2, BM, D), F32),
                        pltpu.VMEM((D, 2 * D_EXPERT), BF16),
                        pltpu.VMEM((D_EXPERT, D), BF16),
                        pltpu.SemaphoreType.DMA((2,)),
                        pltpu.SemaphoreType.DMA((2,))],
    )
    return pl.pallas_call(
        _expert_kernel,
        grid_spec=grid_spec,
        out_shape=jax.ShapeDtypeStruct((n_blocks * BM, D), F32),
        compiler_params=_cparams("arbitrary"),
        name="experts",
    )(blk_e, n_used, row_tok.reshape(n_blocks, 1, BM), h2_flat, w_gate_up,
      b_gate_up.reshape(N_EXPERTS, 1, -1), w_down, b_down.reshape(N_EXPERTS, 1, -1))


def _combine_kernel(dest_hbm, ys_hbm, gate_ref, x1_ref, mod_ref, pf_ref, o_ref,
                    idx_smem, ybuf, idx_sem, row_sem, *, n_steps):
    step = pl.program_id(0)
    tc = COMBINE_TILE
    n_rows = TOP_K * tc

    def idx_copy(s, slot):
        return pltpu.make_async_copy(dest_hbm.at[s], idx_smem.at[slot], idx_sem.at[slot])

    def row_copy(src, r, slot):
        return pltpu.make_async_copy(ys_hbm.at[pl.ds(src, 1), :],
                                     ybuf.at[slot, pl.ds(r, 1), :], row_sem.at[slot])

    def start_rows(slot):
        def body(r, c):
            row_copy(idx_smem[slot, 0, r], r, slot).start()
            return c
        lax.fori_loop(0, n_rows, body, 0)

    def wait_rows(slot):
        def body(r, c):
            row_copy(0, r, slot).wait()
            return c
        lax.fori_loop(0, n_rows, body, 0)

    slot = lax.rem(step, 2)

    @pl.when(step == 0)
    def _():
        idx_copy(0, 0).start()
        idx_copy(0, 0).wait()
        start_rows(0)
        if n_steps > 1:
            idx_copy(1, 1).start()

    @pl.when(step + 1 < n_steps)
    def _():
        idx_copy(step + 1, 1 - slot).wait()
        start_rows(1 - slot)

    wait_rows(slot)

    @pl.when(step + 2 < n_steps)
    def _():
        idx_copy(step + 2, slot).start()

    g = gate_ref[0]
    m = ybuf[slot, 0:tc, :] * g[:, 0:1]
    for k in range(1, TOP_K):
        m = m + ybuf[slot, k * tc:(k + 1) * tc, :] * g[:, k:k + 1]
    o_ref[0] = x1_ref[0] + mod_ref[0, 5:6, :] * _rms(m, pf_ref[...])


def _combine(dest, ys, gates_tok, x1, mod, post_ffn):
    B, T, D = x1.shape
    tc = COMBINE_TILE
    nt = T // tc
    n_steps = B * nt
    return pl.pallas_call(
        functools.partial(_combine_kernel, n_steps=n_steps),
        grid=(n_steps,),
        in_specs=[pl.BlockSpec(memory_space=pl.ANY),
                  pl.BlockSpec(memory_space=pl.ANY),
                  pl.BlockSpec((1, tc, TOP_K), lambda s: (s, 0, 0)),
                  pl.BlockSpec((1, tc, D), lambda s: (s // nt, s % nt, 0)),
                  pl.BlockSpec((1, 6, D), lambda s: (s // nt, 0, 0)),
                  pl.BlockSpec((1, D), lambda s: (0, 0))],
        out_specs=pl.BlockSpec((1, tc, D), lambda s: (s // nt, s % nt, 0)),
        out_shape=jax.ShapeDtypeStruct((B, T, D), F32),
        scratch_shapes=[pltpu.SMEM((2, 1, TOP_K * tc), jnp.int32),
                        pltpu.VMEM((2, TOP_K * tc, D), F32),
                        pltpu.SemaphoreType.DMA((2,)),
                        pltpu.SemaphoreType.DMA((2,))],
        compiler_params=_cparams("arbitrary"),
        name="combine",
    )(dest.reshape(n_steps, 1, TOP_K * tc), ys, gates_tok.reshape(n_steps, tc, TOP_K), x1, mod,
      post_ffn.reshape(1, D))


def _routing_plan(top_idx):
    N = top_idx.shape[0]
    BM = EXPERT_BLOCK
    n_blocks = -(-(N * TOP_K) // BM) + N_EXPERTS
    onehot = (top_idx[:, :, None] == jnp.arange(N_EXPERTS, dtype=jnp.int32)).sum(axis=1).astype(jnp.int32)
    csum = jnp.cumsum(onehot, axis=0)
    counts = csum[-1]
    rank = csum - onehot
    padded = ((counts + BM - 1) // BM) * BM
    pad_end = jnp.cumsum(padded)
    pad_start = pad_end - padded
    dest = jnp.take_along_axis(rank + pad_start[None, :], top_idx, axis=1).astype(jnp.int32)
    tok = jnp.broadcast_to(jnp.arange(N, dtype=jnp.int32)[:, None], (N, TOP_K))
    row_tok = jnp.zeros((n_blocks * BM,), jnp.int32).at[dest.reshape(-1)].set(tok.reshape(-1))
    blk_e = jnp.minimum(jnp.searchsorted(pad_end, jnp.arange(n_blocks, dtype=jnp.int32) * BM,
                                         side='right'), N_EXPERTS - 1).astype(jnp.int32)
    n_used = (pad_end[-1:] // BM).astype(jnp.int32)
    return row_tok, dest, blk_e, n_used


def _encoder_layer(x, mod, p):
    B, T, D = x.shape
    proj = _in_proj(x, mod, p["pre_mix"], p["w_in"])
    na = _na_attention(proj, p["na_bias"], p["gn_na"])
    wa = _wa_attention(proj, p["wa_bias"], p["wa_sink"], p["gn_wa"])
    x1, h2, idx_t, gate_t = _post_mix(na, wa, x, mod, p["w_out"], p["post_mix"], p["pre_ffn"],
                                      p["w_router_t"], p["b_router"])
    N = B * T
    top_idx = jnp.transpose(idx_t, (0, 2, 1)).reshape(N, TOP_K)
    gates = jnp.transpose(gate_t, (0, 2, 1)).reshape(N, TOP_K)
    row_tok, dest, blk_e, n_used = _routing_plan(top_idx)
    ys = _experts(h2.reshape(N, D), row_tok, blk_e, n_used, p["w_gate_up"], p["b_gate_up"],
                  p["w_down"], p["b_down"])
    tc = COMBINE_TILE
    dest_steps = jnp.transpose(dest.reshape(N // tc, tc, TOP_K), (0, 2, 1))
    return _combine(dest_steps, ys, gates, x1, mod, p["post_ffn"])


def kernel(x_prompt, x_sample, c_prompt, c_sample, w_ada, b_ada, pre_mix, post_mix, pre_ffn, post_ffn,
           w_in, na_rel_bias, t5_rel_bias, wa_sink, gn_na, gn_wa, w_out, w_router, b_router,
           w_gate_up, b_gate_up, w_down, b_down):
    depth = w_ada.shape[0]
    y_prompt, y_sample = x_prompt, x_sample
    bp = x_prompt.shape[0]
    perm = _wa_head_perm()
    for l in range(depth):
        c_all = jnp.concatenate([c_prompt, c_sample], axis=0)
        mod = _ada(c_all, w_ada[l], b_ada[l]).reshape(c_all.shape[0], 6, D_MODEL)
        w_in_l = w_in[l]
        qb0 = 3 * D_NA
        w_in_l = jnp.concatenate([w_in_l[:, :qb0], w_in_l[:, qb0:qb0 + D_WA][:, perm],
                                  w_in_l[:, qb0 + D_WA:]], axis=1).astype(BF16)
        w_out_l = jnp.concatenate([w_out[l][:D_NA], w_out[l][D_NA:][perm]], axis=0).astype(BF16)
        p = dict(pre_mix=pre_mix[l], post_mix=post_mix[l], pre_ffn=pre_ffn[l], post_ffn=post_ffn[l],
                 w_in=w_in_l, na_bias=_na_bias_table(na_rel_bias[l]),
                 wa_bias=_wa_bias_table(t5_rel_bias), wa_sink=wa_sink[l], gn_na=gn_na[l],
                 gn_wa=gn_wa[l][perm], w_out=w_out_l, w_router_t=w_router[l].T,
                 b_router=b_router[l], w_gate_up=w_gate_up[l], b_gate_up=b_gate_up[l],
                 w_down=w_down[l], b_down=b_down[l])
        y_prompt = _encoder_layer(y_prompt, mod[:bp], p)
        y_sample = _encoder_layer(y_sample, mod[bp:], p)
    return (y_prompt, y_sample)
```

```python
import functools
import math

import jax
import jax.numpy as jnp
import numpy as np
from jax import lax
from jax.experimental import pallas as pl
from jax.experimental.pallas import tpu as pltpu

F32 = jnp.float32
BF16 = jnp.bfloat16

D_MODEL = 1024
HEAD_DIM = 64
D_NA = 512
D_WA = 512
N_HEADS_NA = 8
N_HEADS_WA = 8
N_KV_WA = 2
GQA_GROUP = 4
D_KV_WA = 128
D_IN_PROJ = 3 * D_NA + D_WA + 2 * D_KV_WA
GRID_W = 64
NA_WIN_R = 8
NA_WIN_C = 16
WA_WINDOW = 128
WA_BLOCK = 128
T5_BUCKETS = 32
T5_MAX_EXACT = 8
T5_MAX_DIST = 128
N_EXPERTS = 32
TOP_K = 4
D_EXPERT = D_MODEL
SWIGLU_LIMIT = 7.0
SWIGLU_ALPHA = 1.702
RMS_EPS = 1e-6
NEG_INF = -1e30

VMEM_LIMIT_BYTES = 56 * 1024 * 1024
LANES = 128

TOKEN_TILE = 512
NA_ROWS_PER_STEP = 8
EXPERT_BLOCK = 256
COMBINE_TILE = 256


def _cparams(*sem):
    return pltpu.CompilerParams(dimension_semantics=sem, vmem_limit_bytes=VMEM_LIMIT_BYTES)


def _rms(x, g):
    return x * lax.rsqrt(jnp.mean(x * x, axis=-1, keepdims=True) + RMS_EPS) * g


def _seq_bounds(s, groups, block):
    base = 0
    start, length = None, None
    for n_seq, seq_len in groups:
        per = seq_len // block
        g_start = base + ((s - base) // per) * per
        if start is None:
            start, length = g_start, per
        else:
            inside = s >= base
            start = jnp.where(inside, g_start, start)
            length = jnp.where(inside, per, length)
        base += n_seq * per
    return start, length


def _batch_of_tile(s, groups, block):
    base_blk, base_seq = 0, 0
    out = None
    for n_seq, seq_len in groups:
        per = seq_len // block
        b = base_seq + (s - base_blk) // per
        out = b if out is None else jnp.where(s >= base_blk, b, out)
        base_blk += n_seq * per
        base_seq += n_seq
    return out


def _ada_kernel(c_ref, w_ref, b_ref, o_ref):
    c = c_ref[...]
    s = c * (1.0 / (1.0 + jnp.exp(-c)))
    o_ref[...] = jnp.dot(s, w_ref[...], preferred_element_type=F32,
                         precision=lax.Precision.HIGHEST) + b_ref[...]


def _ada(c_all, w_ada, b_ada):
    nb = c_all.shape[0]
    return pl.pallas_call(
        _ada_kernel,
        grid=(6,),
        in_specs=[pl.BlockSpec((nb, D_MODEL), lambda j: (0, 0)),
                  pl.BlockSpec((D_MODEL, D_MODEL), lambda j: (0, j)),
                  pl.BlockSpec((1, D_MODEL), lambda j: (0, j))],
        out_specs=pl.BlockSpec((nb, D_MODEL), lambda j: (0, j)),
        out_shape=jax.ShapeDtypeStruct((nb, 6 * D_MODEL), F32),
        compiler_params=_cparams("arbitrary"),
        name="ada",
    )(c_all, w_ada, b_ada.reshape(1, -1))


def _inproj_kernel(xp_ref, xs_ref, mod_ref, g_ref, w_ref, o_ref, *, n_p):
    s = pl.program_id(0)

    def body(x_ref):
        h = _rms(x_ref[...], g_ref[...]) * (1.0 + mod_ref[0, 1:2, :]) + mod_ref[0, 0:1, :]
        o_ref[...] = jnp.dot(h.astype(BF16), w_ref[...], preferred_element_type=F32).astype(BF16)

    pl.when(s < n_p)(lambda: body(xp_ref))
    pl.when(s >= n_p)(lambda: body(xs_ref))


def _in_proj(xp, xs, mod, pre_mix, w_in_bf16, groups):
    D = D_MODEL
    tm = TOKEN_TILE
    n_p, n_s = xp.shape[0] // tm, xs.shape[0] // tm
    return pl.pallas_call(
        functools.partial(_inproj_kernel, n_p=n_p),
        grid=(n_p + n_s,),
        in_specs=[pl.BlockSpec((tm, D), lambda s: (jnp.minimum(s, n_p - 1), 0)),
                  pl.BlockSpec((tm, D), lambda s: (jnp.maximum(s - n_p, 0), 0)),
                  pl.BlockSpec((1, 6, D), lambda s: (_batch_of_tile(s, groups, tm), 0, 0)),
                  pl.BlockSpec((1, D), lambda s: (0, 0)),
                  pl.BlockSpec((D, D_IN_PROJ), lambda s: (0, 0))],
        out_specs=pl.BlockSpec((tm, D_IN_PROJ), lambda s: (s, 0)),
        out_shape=jax.ShapeDtypeStruct(((n_p + n_s) * tm, D_IN_PROJ), BF16),
        compiler_params=_cparams("arbitrary"),
        name="in_proj",
    )(xp, xs, mod, pre_mix.reshape(1, D), w_in_bf16)


def _na_bias_table(rpb):
    qc = np.arange(GRID_W)[:, None]
    kc = np.arange(GRID_W)[None, :]
    win_start = np.clip(qc - NA_WIN_C // 2, 0, GRID_W - NA_WIN_C)
    valid = (kc >= win_start) & (kc < win_start + NA_WIN_C)
    dc = np.clip(kc - qc, -(NA_WIN_C - 1), NA_WIN_C - 1) + NA_WIN_C - 1
    full = rpb[:, :, dc]
    full = jnp.where(valid[None, None], full.astype(F32), NEG_INF)
    two = jnp.concatenate([full[:, :-1], full[:, 1:]], axis=-1)
    two = two.reshape(N_HEADS_NA // 2, 2, 2 * NA_WIN_R - 2, GRID_W, 2 * GRID_W)
    return jnp.transpose(two, (0, 2, 1, 3, 4)).reshape(
        N_HEADS_NA // 2, 2 * NA_WIN_R - 2, 2 * GRID_W, 2 * GRID_W)


def _na_kernel(q_ref, kp_ref, kc_ref, kn_ref, vp_ref, vc_ref, vn_ref, bias_ref, gn_ref, o_ref,
               k_scr, v_scr, *, groups):
    s = pl.program_id(0)
    R = NA_ROWS_PER_STEP
    blk = R * GRID_W
    seq_start, seq_blocks = _seq_bounds(s, groups, blk)
    i = s - seq_start
    rows = seq_blocks * R
    k_scr[0:blk] = kp_ref[...]
    k_scr[blk:2 * blk] = kc_ref[...]
    k_scr[2 * blk:3 * blk] = kn_ref[...]
    v_scr[0:blk] = vp_ref[...]
    v_scr[blk:2 * blk] = vc_ref[...]
    v_scr[2 * blk:3 * blk] = vn_ref[...]
    lane = lax.broadcasted_iota(jnp.int32, (1, LANES), 1)
    lo = lane < HEAD_DIM
    scale = HEAD_DIM ** -0.5
    gn = gn_ref[...]

    def row_body(j, carry):
        r = i * R + j
        row_start = jnp.clip(r - NA_WIN_R // 2, 0, rows - NA_WIN_R)
        shift = r - row_start
        koff = pl.multiple_of((row_start - i * R + R) * GRID_W, GRID_W)
        qoff = pl.multiple_of(j * GRID_W, GRID_W)
        outs = []
        for p in range(N_HEADS_NA // 2):
            cs = slice(p * LANES, (p + 1) * LANES)
            q2 = q_ref[pl.ds(qoff, GRID_W), cs]
            zero = jnp.zeros_like(q2)
            qq = jnp.concatenate([jnp.where(lo, q2, zero), jnp.where(lo, zero, q2)], axis=0)
            k2 = k_scr[pl.ds(koff, NA_WIN_R * GRID_W), cs]
            v2 = v_scr[pl.ds(koff, NA_WIN_R * GRID_W), cs]
            sc = lax.dot_general(qq, k2, (((1,), (1,)), ((), ())),
                                 preferred_element_type=F32) * scale
            parts = []
            for m in range(NA_WIN_R // 2):
                d = 2 * m - shift + (NA_WIN_R - 1)
                parts.append(sc[:, m * LANES:(m + 1) * LANES] + bias_ref[p, d])
            sc = jnp.concatenate(parts, axis=-1)
            mx = jnp.max(sc, axis=-1, keepdims=True)
            e = jnp.exp(sc - mx)
            l = jnp.sum(e, axis=-1, keepdims=True)
            o2 = jnp.dot(e.astype(BF16), v2, preferred_element_type=F32) / l
            outs.append(jnp.where(lo, o2[:GRID_W], o2[GRID_W:]))
        o = jnp.concatenate(outs, axis=-1)
        o_ref[pl.ds(qoff, GRID_W), :] = _rms(o, gn).astype(BF16)
        return carry

    lax.fori_loop(0, R, row_body, 0)


def _na_attention(proj, bias_tab, gn_na, groups):
    n_tok = proj.shape[0]
    blk = NA_ROWS_PER_STEP * GRID_W

    def kv(col, delta):
        def index_map(s):
            start, length = _seq_bounds(s, groups, blk)
            return (jnp.clip(s + delta, start, start + length - 1), col)
        return pl.BlockSpec((blk, D_NA), index_map)

    return pl.pallas_call(
        functools.partial(_na_kernel, groups=groups),
        grid=(n_tok // blk,),
        in_specs=[pl.BlockSpec((blk, D_NA), lambda s: (s, 0)),
                  kv(1, -1), kv(1, 0), kv(1, 1), kv(2, -1), kv(2, 0), kv(2, 1),
                  pl.BlockSpec(bias_tab.shape, lambda s: (0, 0, 0, 0)),
                  pl.BlockSpec((1, D_NA), lambda s: (0, 0))],
        out_specs=pl.BlockSpec((blk, D_NA), lambda s: (s, 0)),
        out_shape=jax.ShapeDtypeStruct((n_tok, D_NA), BF16),
        scratch_shapes=[pltpu.VMEM((3 * blk, D_NA), BF16), pltpu.VMEM((3 * blk, D_NA), BF16)],
        compiler_params=_cparams("arbitrary"),
        name="na_attn",
    )(proj, proj, proj, proj, proj, proj, proj, bias_tab, gn_na.reshape(1, D_NA))


def _wa_head_perm():
    cols = []
    for p in range(GQA_GROUP):
        cols.append(np.arange(p * HEAD_DIM, (p + 1) * HEAD_DIM))
        cols.append(np.arange((p + GQA_GROUP) * HEAD_DIM, (p + GQA_GROUP + 1) * HEAD_DIM))
    return np.concatenate(cols)


def _t5_bucket(rel):
    half = T5_BUCKETS // 2
    ret = (rel > 0).astype(jnp.int32) * half
    n = jnp.abs(rel)
    nf = jnp.maximum(n, 1).astype(F32)
    large = T5_MAX_EXACT + (jnp.log(nf / T5_MAX_EXACT) / math.log(T5_MAX_DIST / T5_MAX_EXACT)
                            * (half - T5_MAX_EXACT)).astype(jnp.int32)
    large = jnp.minimum(large, half - 1)
    return ret + jnp.where(n < T5_MAX_EXACT, n, large)


def _wa_bias_table(t5_table):
    rel = jnp.arange(3 * WA_BLOCK)[None, :] - WA_BLOCK - jnp.arange(WA_BLOCK)[:, None]
    bias = t5_table[_t5_bucket(rel)].astype(F32)
    bias = jnp.where((jnp.abs(rel) <= WA_WINDOW)[..., None], bias, NEG_INF)
    bias = jnp.transpose(bias, (2, 0, 1))
    return jnp.concatenate([bias[:GQA_GROUP], bias[GQA_GROUP:]], axis=1)


def _wa_kernel(sink_ref, q_ref, kp_ref, kc_ref, kn_ref, vp_ref, vc_ref, vn_ref, bias_ref, gn_ref,
               o_ref, *, groups):
    s = pl.program_id(0)
    W = WA_BLOCK
    seq_start, seq_blocks = _seq_bounds(s, groups, W)
    lane = lax.broadcasted_iota(jnp.int32, (1, LANES), 1)
    lo = lane < HEAD_DIM
    row = lax.broadcasted_iota(jnp.int32, (2 * W, 1), 0)
    scale = HEAD_DIM ** -0.5
    kcat = jnp.concatenate([kp_ref[...], kc_ref[...], kn_ref[...]], axis=0)
    vcat = jnp.concatenate([vp_ref[...], vc_ref[...], vn_ref[...]], axis=0)
    pen_prev = jnp.where(s == seq_start, NEG_INF, 0.0).astype(F32)
    pen_next = jnp.where(s == seq_start + seq_blocks - 1, NEG_INF, 0.0).astype(F32)
    col = lax.broadcasted_iota(jnp.int32, (1, 3 * W), 1)
    pen = jnp.where(col < W, pen_prev, jnp.where(col >= 2 * W, pen_next, 0.0))
    outs = []
    for p in range(GQA_GROUP):
        q2 = q_ref[:, p * LANES:(p + 1) * LANES]
        zero = jnp.zeros_like(q2)
        qq = jnp.concatenate([jnp.where(lo, q2, zero), jnp.where(lo, zero, q2)], axis=0)
        sc = lax.dot_general(qq, kcat, (((1,), (1,)), ((), ())),
                             preferred_element_type=F32) * scale
        sc = sc + bias_ref[p] + pen
        sink = jnp.where(row < W, sink_ref[p], sink_ref[p + GQA_GROUP])
        mx = jnp.maximum(jnp.max(sc, axis=-1, keepdims=True), sink)
        e = jnp.exp(sc - mx)
        l = jnp.sum(e, axis=-1, keepdims=True) + jnp.exp(sink - mx)
        o2 = jnp.dot(e.astype(BF16), vcat, preferred_element_type=F32) / l
        outs.append(jnp.where(lo, o2[:W], o2[W:]))
    o = jnp.concatenate(outs, axis=-1)
    o_ref[...] = _rms(o, gn_ref[...]).astype(BF16)


def _wa_attention(proj, bias_tab, sink, gn_wa_perm, groups):
    n_tok = proj.shape[0]
    W = WA_BLOCK
    q_col = 3 * D_NA // D_WA
    k_col = (3 * D_NA + D_WA) // D_KV_WA
    v_col = k_col + 1

    def kv(col, delta):
        def index_map(s):
            start, length = _seq_bounds(s, groups, W)
            return (jnp.clip(s + delta, start, start + length - 1), col)
        return pl.BlockSpec((W, D_KV_WA), index_map)

    return pl.pallas_call(
        functools.partial(_wa_kernel, groups=groups),
        grid=(n_tok // W,),
        in_specs=[pl.BlockSpec(memory_space=pltpu.SMEM),
                  pl.BlockSpec((W, D_WA), lambda s: (s, q_col)),
                  kv(k_col, -1), kv(k_col, 0), kv(k_col, 1),
                  kv(v_col, -1), kv(v_col, 0), kv(v_col, 1),
                  pl.BlockSpec(bias_tab.shape, lambda s: (0, 0, 0)),
                  pl.BlockSpec((1, D_WA), lambda s: (0, 0))],
        out_specs=pl.BlockSpec((W, D_WA), lambda s: (s, 0)),
        out_shape=jax.ShapeDtypeStruct((n_tok, D_WA), BF16),
        compiler_params=_cparams("arbitrary"),
        name="wa_attn",
    )(sink, proj, proj, proj, proj, proj, proj, proj, bias_tab, gn_wa_perm.reshape(1, D_WA))


def _postmix_kernel(na_ref, wa_ref, xp_ref, xs_ref, mod_ref, wo_ref, pm_ref, pf_ref, wr_ref, br_ref,
                    tri_ref, x1_ref, h2_ref, idx_ref, gate_ref, rank_ref, cnt_ref, carry, *, n_p):
    s = pl.program_id(0)

    @pl.when(s == 0)
    def _():
        carry[...] = jnp.zeros_like(carry)

    def body(x_ref):
        a = jnp.concatenate([na_ref[...], wa_ref[...]], axis=-1)
        mixed = jnp.dot(a, wo_ref[...], preferred_element_type=F32)
        x1 = x_ref[...] + mod_ref[0, 2:3, :] * _rms(mixed, pm_ref[...])
        x1_ref[...] = x1
        h2 = _rms(x1, pf_ref[...]) * (1.0 + mod_ref[0, 4:5, :]) + mod_ref[0, 3:4, :]
        h2_ref[...] = h2
        logits = lax.dot_general(wr_ref[...], h2, (((1,), (1,)), ((), ())),
                                 preferred_element_type=F32,
                                 precision=lax.Precision.HIGHEST) + br_ref[...]
        eidx = lax.broadcasted_iota(jnp.int32, logits.shape, 0)
        tops, sels = [], []
        l = logits
        for _ in range(TOP_K):
            m = jnp.max(l, axis=0, keepdims=True)
            sel = jnp.min(jnp.where(l == m, eidx, N_EXPERTS), axis=0, keepdims=True)
            tops.append(m)
            sels.append(sel)
            l = jnp.where(eidx == sel, -jnp.inf, l)
        es = [jnp.exp(t - tops[0]) for t in tops]
        tot = es[0] + es[1] + es[2] + es[3]
        idx_ref[0] = jnp.concatenate(sels, axis=0)
        gate_ref[0] = jnp.concatenate([e / tot for e in es], axis=0)
        hits = [eidx == sel for sel in sels]
        onehot = sum(h.astype(F32) for h in hits)
        prefix = jnp.dot(onehot.astype(BF16), tri_ref[...], preferred_element_type=F32)
        rank_e = carry[...] + prefix - onehot
        rank_ref[0] = jnp.concatenate(
            [jnp.sum(jnp.where(h, rank_e, 0.0), axis=0, keepdims=True) for h in hits],
            axis=0).astype(jnp.int32)
        carry[...] = carry[...] + jnp.sum(onehot, axis=1, keepdims=True)
        cnt_ref[...] = carry[...]

    pl.when(s < n_p)(lambda: body(xp_ref))
    pl.when(s >= n_p)(lambda: body(xs_ref))


def _post_mix(na, wa, xp, xs, mod, w_out_bf16, post_mix, pre_ffn, w_router_t, b_router, groups):
    D = D_MODEL
    tm = TOKEN_TILE
    n_p, n_s = xp.shape[0] // tm, xs.shape[0] // tm
    nt = n_p + n_s
    tri = (np.arange(tm)[:, None] <= np.arange(tm)[None, :]).astype(np.float32)
    tok = lambda s: (s, 0)
    const2 = lambda s: (0, 0)
    tile3 = lambda s: (s, 0, 0)
    return pl.pallas_call(
        functools.partial(_postmix_kernel, n_p=n_p),
        grid=(nt,),
        in_specs=[pl.BlockSpec((tm, D_NA), tok),
                  pl.BlockSpec((tm, D_WA), tok),
                  pl.BlockSpec((tm, D), lambda s: (jnp.minimum(s, n_p - 1), 0)),
                  pl.BlockSpec((tm, D), lambda s: (jnp.maximum(s - n_p, 0), 0)),
                  pl.BlockSpec((1, 6, D), lambda s: (_batch_of_tile(s, groups, tm), 0, 0)),
                  pl.BlockSpec((D, D), const2),
                  pl.BlockSpec((1, D), const2),
                  pl.BlockSpec((1, D), const2),
                  pl.BlockSpec((N_EXPERTS, D), const2),
                  pl.BlockSpec((N_EXPERTS, 1), const2),
                  pl.BlockSpec((tm, tm), const2)],
        out_specs=[pl.BlockSpec((tm, D), tok),
                   pl.BlockSpec((tm, D), tok),
                   pl.BlockSpec((1, TOP_K, tm), tile3),
                   pl.BlockSpec((1, TOP_K, tm), tile3),
                   pl.BlockSpec((1, TOP_K, tm), tile3),
                   pl.BlockSpec((N_EXPERTS, 1), const2)],
        out_shape=[jax.ShapeDtypeStruct((nt * tm, D), F32),
                   jax.ShapeDtypeStruct((nt * tm, D), F32),
                   jax.ShapeDtypeStruct((nt, TOP_K, tm), jnp.int32),
                   jax.ShapeDtypeStruct((nt, TOP_K, tm), F32),
                   jax.ShapeDtypeStruct((nt, TOP_K, tm), jnp.int32),
                   jax.ShapeDtypeStruct((N_EXPERTS, 1), F32)],
        scratch_shapes=[pltpu.VMEM((N_EXPERTS, 1), F32)],
        compiler_params=_cparams("arbitrary"),
        name="post_mix",
    )(na, wa, xp, xs, mod, w_out_bf16, post_mix.reshape(1, D), pre_ffn.reshape(1, D),
      w_router_t, b_router.reshape(N_EXPERTS, 1), jnp.asarray(tri, BF16))


def _expert_kernel(blk_e_ref, nvalid_ref, nused_ref, src_hbm, dst_hbm, h_hbm, wgu_ref, bgu_ref,
                   wd_ref, bd_ref, y_hbm, src_smem, dst_smem, xbuf, obuf, wgu_bf, wd_bf,
                   idx_sem, g_sem, s_sem):
    i = pl.program_id(0)
    n_used = nused_ref[0]
    BM = EXPERT_BLOCK

    def idx_copies(blk, slot3):
        return (pltpu.make_async_copy(src_hbm.at[blk], src_smem.at[slot3], idx_sem.at[slot3]),
                pltpu.make_async_copy(dst_hbm.at[blk], dst_smem.at[slot3], idx_sem.at[slot3]))

    def start_idx(blk, slot3):
        for c in idx_copies(blk, slot3):
            c.start()

    def wait_idx(blk, slot3):
        for c in idx_copies(blk, slot3):
            c.wait()

    def start_gather(slot3, slot):
        for r in range(BM):
            tok = src_smem[slot3, 0, r]
            pltpu.make_async_copy(h_hbm.at[pl.ds(tok, 1), :], xbuf.at[slot, pl.ds(r, 1), :],
                                  g_sem.at[slot]).start()

    def wait_gather(slot):
        pltpu.make_async_copy(h_hbm.at[pl.ds(0, BM), :], xbuf.at[slot], g_sem.at[slot]).wait()

    def scatter_row(r, dst, slot):
        return pltpu.make_async_copy(obuf.at[slot, pl.ds(r, 1), :], y_hbm.at[pl.ds(dst, 1), :],
                                     s_sem.at[slot])

    def start_scatter(slot3, slot, nv):
        @pl.when(nv == BM)
        def _():
            for r in range(BM):
                scatter_row(r, dst_smem[slot3, 0, r], slot).start()

        @pl.when(nv < BM)
        def _():
            def body(r, c):
                scatter_row(r, dst_smem[slot3, 0, r], slot).start()
                return c
            lax.fori_loop(0, nv, body, 0)

    def wait_scatter(slot, nv):
        @pl.when(nv == BM)
        def _():
            pltpu.make_async_copy(obuf.at[slot], y_hbm.at[pl.ds(0, BM), :], s_sem.at[slot]).wait()

        @pl.when(nv < BM)
        def _():
            def body(r, c):
                scatter_row(r, 0, slot).wait()
                return c
            lax.fori_loop(0, nv, body, 0)

    s2 = lax.rem(i, 2)
    s3 = lax.rem(i, 3)

    @pl.when(i == 0)
    def _():
        start_idx(0, 0)

        @pl.when(n_used > 1)
        def _():
            start_idx(1, 1)

        wait_idx(0, 0)
        start_gather(0, 0)

    @pl.when(i < n_used)
    def _():
        @pl.when(i + 2 < n_used)
        def _():
            start_idx(i + 2, lax.rem(i + 2, 3))

        @pl.when(i + 1 < n_used)
        def _():
            wait_idx(i + 1, lax.rem(i + 1, 3))
            start_gather(lax.rem(i + 1, 3), 1 - s2)

        new_expert = jnp.logical_or(i == 0, blk_e_ref[i] != blk_e_ref[jnp.maximum(i - 1, 0)])

        @pl.when(new_expert)
        def _():
            wgu_bf[...] = wgu_ref[0].astype(BF16)
            wd_bf[...] = wd_ref[0].astype(BF16)

        wait_gather(s2)

        @pl.when(i >= 2)
        def _():
            wait_scatter(s2, nvalid_ref[jnp.maximum(i - 2, 0)])

        x = xbuf[s2].astype(BF16)
        gu = jnp.dot(x, wgu_bf[...], preferred_element_type=F32) + bgu_ref[0]
        glu = jnp.minimum(gu[:, :D_EXPERT], SWIGLU_LIMIT)
        lin = jnp.clip(gu[:, D_EXPERT:], -SWIGLU_LIMIT, SWIGLU_LIMIT)
        act = glu * (1.0 / (1.0 + jnp.exp(-SWIGLU_ALPHA * glu))) * (lin + 1.0)
        obuf[s2] = jnp.dot(act.astype(BF16), wd_bf[...], preferred_element_type=F32) + bd_ref[0]

        start_scatter(s3, s2, nvalid_ref[i])

        @pl.when(i == n_used - 1)
        def _():
            wait_scatter(s2, nvalid_ref[i])

            @pl.when(i >= 1)
            def _():
                wait_scatter(1 - s2, nvalid_ref[jnp.maximum(i - 1, 0)])


def _experts(h2, src_row, dst_row, blk_e, nvalid, n_used, w_gate_up, b_gate_up, w_down, b_down):
    N, D = h2.shape
    BM = EXPERT_BLOCK
    n_blocks = blk_e.shape[0]
    wmap = lambda i, be, nv, nu: (be[i], 0, 0)
    grid_spec = pltpu.PrefetchScalarGridSpec(
        num_scalar_prefetch=3,
        grid=(n_blocks,),
        in_specs=[pl.BlockSpec(memory_space=pl.ANY),
                  pl.BlockSpec(memory_space=pl.ANY),
                  pl.BlockSpec(memory_space=pl.ANY),
                  pl.BlockSpec((1, D, 2 * D_EXPERT), wmap),
                  pl.BlockSpec((1, 1, 2 * D_EXPERT), wmap),
                  pl.BlockSpec((1, D_EXPERT, D), wmap),
                  pl.BlockSpec((1, 1, D), wmap)],
        out_specs=pl.BlockSpec(memory_space=pl.ANY),
        scratch_shapes=[pltpu.SMEM((3, 1, BM), jnp.int32),
                        pltpu.SMEM((3, 1, BM), jnp.int32),
                        pltpu.VMEM((2, BM, D), F32),
                        pltpu.VMEM((2, BM, D), F32),
                        pltpu.VMEM((D, 2 * D_EXPERT), BF16),
                        pltpu.VMEM((D_EXPERT, D), BF16),
                        pltpu.SemaphoreType.DMA((3,)),
                        pltpu.SemaphoreType.DMA((2,)),
                        pltpu.SemaphoreType.DMA((2,))],
    )
    return pl.pallas_call(
        _expert_kernel,
        grid_spec=grid_spec,
        out_shape=jax.ShapeDtypeStruct((TOP_K * N, D), F32),
        compiler_params=_cparams("arbitrary"),
        name="experts",
    )(blk_e, nvalid, n_used, src_row.reshape(n_blocks, 1, BM), dst_row.reshape(n_blocks, 1, BM), h2,
      w_gate_up, b_gate_up.reshape(N_EXPERTS, 1, -1), w_down, b_down.reshape(N_EXPERTS, 1, -1))


def _combine_kernel(y0_ref, y1_ref, y2_ref, y3_ref, gate_ref, x1_ref, mod_ref, pf_ref,
                    op_ref, os_ref, *, n_p):
    s = pl.program_id(0)
    g = gate_ref[...]
    m = y0_ref[...] * g[:, 0:1]
    for k, y_ref in enumerate((y1_ref, y2_ref, y3_ref), start=1):
        m = m + y_ref[...] * g[:, k:k + 1]
    res = x1_ref[...] + mod_ref[0, 5:6, :] * _rms(m, pf_ref[...])

    @pl.when(s < n_p)
    def _():
        op_ref[...] = res

    @pl.when(s >= n_p)
    def _():
        os_ref[...] = res


def _combine(y4, gates_tok, x1, mod, post_ffn, n_tok_p, n_tok_s, groups):
    D = D_MODEL
    tc = COMBINE_TILE
    n_p, n_s = n_tok_p // tc, n_tok_s // tc
    nt = n_p + n_s

    def slot(k):
        return pl.BlockSpec((tc, D), lambda s: (k * nt + s, 0))

    return pl.pallas_call(
        functools.partial(_combine_kernel, n_p=n_p),
        grid=(nt,),
        in_specs=[slot(0), slot(1), slot(2), slot(3),
                  pl.BlockSpec((tc, TOP_K), lambda s: (s, 0)),
                  pl.BlockSpec((tc, D), lambda s: (s, 0)),
                  pl.BlockSpec((1, 6, D), lambda s: (_batch_of_tile(s, groups, tc), 0, 0)),
                  pl.BlockSpec((1, D), lambda s: (0, 0))],
        out_specs=[pl.BlockSpec((tc, D), lambda s: (jnp.minimum(s, n_p - 1), 0)),
                   pl.BlockSpec((tc, D), lambda s: (jnp.maximum(s - n_p, 0), 0))],
        out_shape=[jax.ShapeDtypeStruct((n_tok_p, D), F32),
                   jax.ShapeDtypeStruct((n_tok_s, D), F32)],
        compiler_params=_cparams("arbitrary"),
        name="combine",
    )(y4, y4, y4, y4, gates_tok, x1, mod, post_ffn.reshape(1, D))


def _routing_plan(idx_t, rank_t, counts, n_tok):
    BM = EXPERT_BLOCK
    nt, _, tm = idx_t.shape
    n_blocks = (n_tok * TOP_K) // BM + N_EXPERTS
    padded = ((counts + BM - 1) // BM) * BM
    pad_end = jnp.cumsum(padded)
    pad_start = pad_end - padded
    experts = jnp.arange(N_EXPERTS, dtype=jnp.int32)
    start_of = jnp.sum(jnp.where(idx_t[..., None] == experts, pad_start, 0), axis=-1)
    dest = (start_of + rank_t).astype(jnp.int32)
    t_of = (jnp.arange(nt, dtype=jnp.int32)[:, None, None] * tm
            + jnp.arange(tm, dtype=jnp.int32)[None, None, :])
    slot_row = jnp.arange(TOP_K, dtype=jnp.int32)[None, :, None] * n_tok + t_of
    inv = jnp.zeros((n_blocks * BM,), jnp.int32).at[dest.reshape(-1)].set(slot_row.reshape(-1))
    src_row = inv % n_tok
    blk_first = jnp.arange(n_blocks, dtype=jnp.int32) * BM
    blk_e = jnp.minimum(jnp.sum((pad_end[None, :] <= blk_first[:, None]).astype(jnp.int32), axis=1),
                        N_EXPERTS - 1)
    of_blk = blk_e[:, None] == experts
    real_end = jnp.sum(jnp.where(of_blk, pad_start + counts, 0), axis=1)
    nvalid = jnp.clip(real_end - blk_first, 0, BM).astype(jnp.int32)
    n_used = (pad_end[-1:] // BM).astype(jnp.int32)
    return src_row, inv, blk_e, nvalid, n_used


def kernel(x_prompt, x_sample, c_prompt, c_sample, w_ada, b_ada, pre_mix, post_mix, pre_ffn, post_ffn,
           w_in, na_rel_bias, t5_rel_bias, wa_sink, gn_na, gn_wa, w_out, w_router, b_router,
           w_gate_up, b_gate_up, w_down, b_down):
    depth = w_ada.shape[0]
    D = D_MODEL
    groups = (x_prompt.shape[:2], x_sample.shape[:2])
    n_tok_p = x_prompt.shape[0] * x_prompt.shape[1]
    n_tok_s = x_sample.shape[0] * x_sample.shape[1]
    n_tok = n_tok_p + n_tok_s
    yp = x_prompt.reshape(n_tok_p, D)
    ys = x_sample.reshape(n_tok_s, D)
    perm = _wa_head_perm()
    c_all = jnp.concatenate([c_prompt, c_sample], axis=0)
    for l in range(depth):
        mod = _ada(c_all, w_ada[l], b_ada[l]).reshape(c_all.shape[0], 6, D)
        qb0 = 3 * D_NA
        w_in_l = jnp.concatenate([w_in[l][:, :qb0], w_in[l][:, qb0:qb0 + D_WA][:, perm],
                                  w_in[l][:, qb0 + D_WA:]], axis=1).astype(BF16)
        w_out_l = jnp.concatenate([w_out[l][:D_NA], w_out[l][D_NA:][perm]], axis=0).astype(BF16)
        proj = _in_proj(yp, ys, mod, pre_mix[l], w_in_l, groups)
        na = _na_attention(proj, _na_bias_table(na_rel_bias[l]), gn_na[l], groups)
        wa = _wa_attention(proj, _wa_bias_table(t5_rel_bias), wa_sink[l], gn_wa[l][perm], groups)
        x1, h2, idx_t, gate_t, rank_t, counts = _post_mix(
            na, wa, yp, ys, mod, w_out_l, post_mix[l], pre_ffn[l], w_router[l].T, b_router[l], groups)
        src_row, dst_row, blk_e, nvalid, n_used = _routing_plan(
            idx_t, rank_t, counts.reshape(N_EXPERTS).astype(jnp.int32), n_tok)
        y4 = _experts(h2, src_row, dst_row, blk_e, nvalid, n_used, w_gate_up[l], b_gate_up[l],
                      w_down[l], b_down[l])
        gates_tok = jnp.transpose(gate_t, (0, 2, 1)).reshape(n_tok, TOP_K)
        yp, ys = _combine(y4, gates_tok, x1, mod, post_ffn[l], n_tok_p, n_tok_s, groups)
    return (yp.reshape(x_prompt.shape), ys.reshape(x_sample.shape))
```

```python
import functools
import math

import jax
import jax.numpy as jnp
import numpy as np
from jax import lax
from jax.experimental import pallas as pl
from jax.experimental.pallas import tpu as pltpu

F32 = jnp.float32
BF16 = jnp.bfloat16

D_MODEL = 1024
HEAD_DIM = 64
D_NA = 512
D_WA = 512
N_HEADS_NA = 8
N_HEADS_WA = 8
N_KV_WA = 2
GQA_GROUP = 4
D_KV_WA = 128
D_IN_PROJ = 3 * D_NA + D_WA + 2 * D_KV_WA
GRID_W = 64
NA_WIN_R = 8
NA_WIN_C = 16
WA_WINDOW = 128
WA_BLOCK = 128
T5_BUCKETS = 32
T5_MAX_EXACT = 8
T5_MAX_DIST = 128
N_EXPERTS = 32
TOP_K = 4
D_EXPERT = D_MODEL
SWIGLU_LIMIT = 7.0
SWIGLU_ALPHA = 1.702
RMS_EPS = 1e-6
NEG_INF = -1e30

VMEM_LIMIT_BYTES = 56 * 1024 * 1024
LANES = 128
ROW_TILE = D_MODEL // LANES

TOKEN_TILE = 512
NA_ROWS_PER_STEP = 8
EXPERT_BLOCK = 256
X_SLOTS = 3
IDX_SLOTS = X_SLOTS + 2
COMBINE_TILE = 256


def _cparams(*sem):
    return pltpu.CompilerParams(dimension_semantics=sem, vmem_limit_bytes=VMEM_LIMIT_BYTES)


def _rms(x, g):
    return x * lax.rsqrt(jnp.mean(x * x, axis=-1, keepdims=True) + RMS_EPS) * g


def _seq_bounds(s, groups, block):
    base = 0
    start, length = None, None
    for n_seq, seq_len in groups:
        per = seq_len // block
        g_start = base + ((s - base) // per) * per
        if start is None:
            start, length = g_start, per
        else:
            inside = s >= base
            start = jnp.where(inside, g_start, start)
            length = jnp.where(inside, per, length)
        base += n_seq * per
    return start, length


def _batch_of_tile(s, groups, block):
    base_blk, base_seq = 0, 0
    out = None
    for n_seq, seq_len in groups:
        per = seq_len // block
        b = base_seq + (s - base_blk) // per
        out = b if out is None else jnp.where(s >= base_blk, b, out)
        base_blk += n_seq * per
        base_seq += n_seq
    return out


def _ada_kernel(c_ref, w_ref, b_ref, o_ref):
    c = c_ref[...]
    s = c * (1.0 / (1.0 + jnp.exp(-c)))
    o_ref[...] = jnp.dot(s, w_ref[...], preferred_element_type=F32,
                         precision=lax.Precision.HIGHEST) + b_ref[...]


def _ada(c_all, w_ada, b_ada):
    nb = c_all.shape[0]
    return pl.pallas_call(
        _ada_kernel,
        grid=(6,),
        in_specs=[pl.BlockSpec((nb, D_MODEL), lambda j: (0, 0)),
                  pl.BlockSpec((D_MODEL, D_MODEL), lambda j: (0, j)),
                  pl.BlockSpec((1, D_MODEL), lambda j: (0, j))],
        out_specs=pl.BlockSpec((nb, D_MODEL), lambda j: (0, j)),
        out_shape=jax.ShapeDtypeStruct((nb, 6 * D_MODEL), F32),
        compiler_params=_cparams("arbitrary"),
        name="ada",
    )(c_all, w_ada, b_ada.reshape(1, -1))


def _inproj_kernel(xp_ref, xs_ref, mod_ref, g_ref, w_ref, o_ref, *, n_p):
    s = pl.program_id(0)

    def body(x_ref):
        h = _rms(x_ref[...], g_ref[...]) * (1.0 + mod_ref[0, 1:2, :]) + mod_ref[0, 0:1, :]
        o_ref[...] = jnp.dot(h.astype(BF16), w_ref[...], preferred_element_type=F32).astype(BF16)

    pl.when(s < n_p)(lambda: body(xp_ref))
    pl.when(s >= n_p)(lambda: body(xs_ref))


def _in_proj(xp, xs, mod, pre_mix, w_in_bf16, groups):
    D = D_MODEL
    tm = TOKEN_TILE
    n_p, n_s = xp.shape[0] // tm, xs.shape[0] // tm
    return pl.pallas_call(
        functools.partial(_inproj_kernel, n_p=n_p),
        grid=(n_p + n_s,),
        in_specs=[pl.BlockSpec((tm, D), lambda s: (jnp.minimum(s, n_p - 1), 0)),
                  pl.BlockSpec((tm, D), lambda s: (jnp.maximum(s - n_p, 0), 0)),
                  pl.BlockSpec((1, 6, D), lambda s: (_batch_of_tile(s, groups, tm), 0, 0)),
                  pl.BlockSpec((1, D), lambda s: (0, 0)),
                  pl.BlockSpec((D, D_IN_PROJ), lambda s: (0, 0))],
        out_specs=pl.BlockSpec((tm, D_IN_PROJ), lambda s: (s, 0)),
        out_shape=jax.ShapeDtypeStruct(((n_p + n_s) * tm, D_IN_PROJ), BF16),
        compiler_params=_cparams("arbitrary"),
        name="in_proj",
    )(xp, xs, mod, pre_mix.reshape(1, D), w_in_bf16)


def _na_bias_table(rpb):
    qc = np.arange(GRID_W)[:, None]
    kc = np.arange(GRID_W)[None, :]
    win_start = np.clip(qc - NA_WIN_C // 2, 0, GRID_W - NA_WIN_C)
    valid = (kc >= win_start) & (kc < win_start + NA_WIN_C)
    dc = np.clip(kc - qc, -(NA_WIN_C - 1), NA_WIN_C - 1) + NA_WIN_C - 1
    full = rpb[:, :, dc]
    full = jnp.where(valid[None, None], full.astype(F32), NEG_INF)
    two = jnp.concatenate([full[:, :-1], full[:, 1:]], axis=-1)
    two = two.reshape(N_HEADS_NA // 2, 2, 2 * NA_WIN_R - 2, GRID_W, 2 * GRID_W)
    return jnp.transpose(two, (0, 2, 1, 3, 4)).reshape(
        N_HEADS_NA // 2, 2 * NA_WIN_R - 2, 2 * GRID_W, 2 * GRID_W)


def _na_kernel(q_ref, kp_ref, kc_ref, kn_ref, vp_ref, vc_ref, vn_ref, bias_ref, gn_ref, o_ref,
               k_scr, v_scr, *, groups):
    s = pl.program_id(0)
    R = NA_ROWS_PER_STEP
    blk = R * GRID_W
    seq_start, seq_blocks = _seq_bounds(s, groups, blk)
    i = s - seq_start
    rows = seq_blocks * R
    k_scr[0:blk] = kp_ref[...]
    k_scr[blk:2 * blk] = kc_ref[...]
    k_scr[2 * blk:3 * blk] = kn_ref[...]
    v_scr[0:blk] = vp_ref[...]
    v_scr[blk:2 * blk] = vc_ref[...]
    v_scr[2 * blk:3 * blk] = vn_ref[...]
    lane = lax.broadcasted_iota(jnp.int32, (1, LANES), 1)
    lo = lane < HEAD_DIM
    gn = gn_ref[...]

    def row_body(j, carry):
        r = i * R + j
        row_start = jnp.clip(r - NA_WIN_R // 2, 0, rows - NA_WIN_R)
        shift = r - row_start
        koff = pl.multiple_of((row_start - i * R + R) * GRID_W, GRID_W)
        qoff = pl.multiple_of(j * GRID_W, GRID_W)
        outs = []
        for p in range(N_HEADS_NA // 2):
            cs = slice(p * LANES, (p + 1) * LANES)
            q2 = q_ref[pl.ds(qoff, GRID_W), cs]
            zero = jnp.zeros_like(q2)
            qq = jnp.concatenate([jnp.where(lo, q2, zero), jnp.where(lo, zero, q2)], axis=0)
            k2 = k_scr[pl.ds(koff, NA_WIN_R * GRID_W), cs]
            v2 = v_scr[pl.ds(koff, NA_WIN_R * GRID_W), cs]
            sc = lax.dot_general(qq, k2, (((1,), (1,)), ((), ())),
                                 preferred_element_type=F32)
            parts = []
            for m in range(NA_WIN_R // 2):
                d = 2 * m - shift + (NA_WIN_R - 1)
                parts.append(sc[:, m * LANES:(m + 1) * LANES] + bias_ref[p, d])
            sc = jnp.concatenate(parts, axis=-1)
            mx = jnp.max(sc, axis=-1, keepdims=True)
            e = jnp.exp(sc - mx)
            l = jnp.sum(e, axis=-1, keepdims=True)
            o2 = jnp.dot(e.astype(BF16), v2, preferred_element_type=F32) / l
            outs.append(jnp.where(lo, o2[:GRID_W], o2[GRID_W:]))
        o = jnp.concatenate(outs, axis=-1)
        o_ref[pl.ds(qoff, GRID_W), :] = _rms(o, gn).astype(BF16)
        return carry

    lax.fori_loop(0, R, row_body, 0)


def _na_attention(proj, bias_tab, gn_na, groups):
    n_tok = proj.shape[0]
    blk = NA_ROWS_PER_STEP * GRID_W

    def kv(col, delta):
        def index_map(s):
            start, length = _seq_bounds(s, groups, blk)
            return (jnp.clip(s + delta, start, start + length - 1), col)
        return pl.BlockSpec((blk, D_NA), index_map)

    return pl.pallas_call(
        functools.partial(_na_kernel, groups=groups),
        grid=(n_tok // blk,),
        in_specs=[pl.BlockSpec((blk, D_NA), lambda s: (s, 0)),
                  kv(1, -1), kv(1, 0), kv(1, 1), kv(2, -1), kv(2, 0), kv(2, 1),
                  pl.BlockSpec(bias_tab.shape, lambda s: (0, 0, 0, 0)),
                  pl.BlockSpec((1, D_NA), lambda s: (0, 0))],
        out_specs=pl.BlockSpec((blk, D_NA), lambda s: (s, 0)),
        out_shape=jax.ShapeDtypeStruct((n_tok, D_NA), BF16),
        scratch_shapes=[pltpu.VMEM((3 * blk, D_NA), BF16), pltpu.VMEM((3 * blk, D_NA), BF16)],
        compiler_params=_cparams("arbitrary"),
        name="na_attn",
    )(proj, proj, proj, proj, proj, proj, proj, bias_tab, gn_na.reshape(1, D_NA))


def _wa_slab_order(a, axis):
    shape = a.shape
    split = shape[:axis] + (N_KV_WA, GQA_GROUP, HEAD_DIM) + shape[axis + 1:]
    return jnp.swapaxes(a.reshape(split), axis, axis + 1).reshape(shape)


def _t5_bucket(rel):
    half = T5_BUCKETS // 2
    ret = (rel > 0).astype(jnp.int32) * half
    n = jnp.abs(rel)
    nf = jnp.maximum(n, 1).astype(F32)
    large = T5_MAX_EXACT + (jnp.log(nf / T5_MAX_EXACT) / math.log(T5_MAX_DIST / T5_MAX_EXACT)
                            * (half - T5_MAX_EXACT)).astype(jnp.int32)
    large = jnp.minimum(large, half - 1)
    return ret + jnp.where(n < T5_MAX_EXACT, n, large)


def _wa_bias_table(t5_table):
    rel = jnp.arange(3 * WA_BLOCK)[None, :] - WA_BLOCK - jnp.arange(WA_BLOCK)[:, None]
    bias = t5_table[_t5_bucket(rel)].astype(F32)
    bias = jnp.where((jnp.abs(rel) <= WA_WINDOW)[..., None], bias, NEG_INF)
    bias = jnp.transpose(bias, (2, 0, 1))
    return jnp.concatenate([bias[:GQA_GROUP], bias[GQA_GROUP:]], axis=1)


def _wa_kernel(sink_ref, q_ref, kp_ref, kc_ref, kn_ref, vp_ref, vc_ref, vn_ref, bias_ref, gn_ref,
               o_ref, *, groups):
    s = pl.program_id(0)
    W = WA_BLOCK
    seq_start, seq_blocks = _seq_bounds(s, groups, W)
    lane = lax.broadcasted_iota(jnp.int32, (1, LANES), 1)
    lo = lane < HEAD_DIM
    row = lax.broadcasted_iota(jnp.int32, (2 * W, 1), 0)
    kcat = jnp.concatenate([kp_ref[...], kc_ref[...], kn_ref[...]], axis=0)
    vcat = jnp.concatenate([vp_ref[...], vc_ref[...], vn_ref[...]], axis=0)
    pen_prev = jnp.where(s == seq_start, NEG_INF, 0.0).astype(F32)
    pen_next = jnp.where(s == seq_start + seq_blocks - 1, NEG_INF, 0.0).astype(F32)
    col = lax.broadcasted_iota(jnp.int32, (1, 3 * W), 1)
    pen = jnp.where(col < W, pen_prev, jnp.where(col >= 2 * W, pen_next, 0.0))
    outs = []
    for p in range(GQA_GROUP):
        q2 = q_ref[:, p * LANES:(p + 1) * LANES]
        zero = jnp.zeros_like(q2)
        qq = jnp.concatenate([jnp.where(lo, q2, zero), jnp.where(lo, zero, q2)], axis=0)
        sc = lax.dot_general(qq, kcat, (((1,), (1,)), ((), ())),
                             preferred_element_type=F32)
        sc = sc + bias_ref[p] + pen
        sink = jnp.where(row < W, sink_ref[p], sink_ref[p + GQA_GROUP])
        mx = jnp.maximum(jnp.max(sc, axis=-1, keepdims=True), sink)
        e = jnp.exp(sc - mx)
        l = jnp.sum(e, axis=-1, keepdims=True) + jnp.exp(sink - mx)
        o2 = jnp.dot(e.astype(BF16), vcat, preferred_element_type=F32) / l
        outs.append(jnp.where(lo, o2[:W], o2[W:]))
    o = jnp.concatenate(outs, axis=-1)
    o_ref[...] = _rms(o, gn_ref[...]).astype(BF16)


def _wa_attention(proj, bias_tab, sink, gn_wa_perm, groups):
    n_tok = proj.shape[0]
    W = WA_BLOCK
    q_col = 3 * D_NA // D_WA
    k_col = (3 * D_NA + D_WA) // D_KV_WA
    v_col = k_col + 1

    def kv(col, delta):
        def index_map(s):
            start, length = _seq_bounds(s, groups, W)
            return (jnp.clip(s + delta, start, start + length - 1), col)
        return pl.BlockSpec((W, D_KV_WA), index_map)

    return pl.pallas_call(
        functools.partial(_wa_kernel, groups=groups),
        grid=(n_tok // W,),
        in_specs=[pl.BlockSpec(memory_space=pltpu.SMEM),
                  pl.BlockSpec((W, D_WA), lambda s: (s, q_col)),
                  kv(k_col, -1), kv(k_col, 0), kv(k_col, 1),
                  kv(v_col, -1), kv(v_col, 0), kv(v_col, 1),
                  pl.BlockSpec(bias_tab.shape, lambda s: (0, 0, 0)),
                  pl.BlockSpec((1, D_WA), lambda s: (0, 0))],
        out_specs=pl.BlockSpec((W, D_WA), lambda s: (s, 0)),
        out_shape=jax.ShapeDtypeStruct((n_tok, D_WA), BF16),
        compiler_params=_cparams("arbitrary"),
        name="wa_attn",
    )(sink, proj, proj, proj, proj, proj, proj, proj, bias_tab, gn_wa_perm.reshape(1, D_WA))


def _postmix_kernel(na_ref, wa_ref, xp_ref, xs_ref, mod_ref, wo_ref, pm_ref, pf_ref, wr_ref, br_ref,
                    tri_ref, x1_ref, h2_ref, idx_ref, gate_ref, rank_ref, cnt_ref, carry, *, n_p):
    s = pl.program_id(0)

    @pl.when(s == 0)
    def _():
        carry[...] = jnp.zeros_like(carry)

    def body(x_ref):
        a = jnp.concatenate([na_ref[...], wa_ref[...]], axis=-1)
        mixed = jnp.dot(a, wo_ref[...], preferred_element_type=F32)
        x1 = x_ref[...] + mod_ref[0, 2:3, :] * _rms(mixed, pm_ref[...])
        x1_ref[...] = x1
        h2 = _rms(x1, pf_ref[...]) * (1.0 + mod_ref[0, 4:5, :]) + mod_ref[0, 3:4, :]
        for c in range(ROW_TILE):
            h2_ref[pl.ds(c, TOKEN_TILE, stride=ROW_TILE), :] = h2[:, c * LANES:(c + 1) * LANES]
        logits = lax.dot_general(wr_ref[...], h2, (((1,), (1,)), ((), ())),
                                 preferred_element_type=F32,
                                 precision=lax.Precision.HIGHEST) + br_ref[...]
        eidx = lax.broadcasted_iota(jnp.int32, logits.shape, 0)
        tops, sels = [], []
        l = logits
        for _ in range(TOP_K):
            m = jnp.max(l, axis=0, keepdims=True)
            sel = jnp.min(jnp.where(l == m, eidx, N_EXPERTS), axis=0, keepdims=True)
            tops.append(m)
            sels.append(sel)
            l = jnp.where(eidx == sel, -jnp.inf, l)
        es = [jnp.exp(t - tops[0]) for t in tops]
        tot = es[0] + es[1] + es[2] + es[3]
        idx_ref[0] = jnp.concatenate(sels, axis=0)
        gate_ref[0] = jnp.concatenate([e / tot for e in es], axis=0)
        hits = [eidx == sel for sel in sels]
        onehot = sum(h.astype(F32) for h in hits)
        prefix = jnp.dot(onehot.astype(BF16), tri_ref[...], preferred_element_type=F32)
        rank_e = carry[...] + prefix - onehot
        rank_ref[0] = jnp.concatenate(
            [jnp.sum(jnp.where(h, rank_e, 0.0), axis=0, keepdims=True) for h in hits],
            axis=0).astype(jnp.int32)
        carry[...] = carry[...] + jnp.sum(onehot, axis=1, keepdims=True)
        cnt_ref[...] = carry[...]

    pl.when(s < n_p)(lambda: body(xp_ref))
    pl.when(s >= n_p)(lambda: body(xs_ref))


def _post_mix(na, wa, xp, xs, mod, w_out_bf16, post_mix, pre_ffn, w_router_t, b_router, groups):
    D = D_MODEL
    tm = TOKEN_TILE
    n_p, n_s = xp.shape[0] // tm, xs.shape[0] // tm
    nt = n_p + n_s
    tri = (np.arange(tm)[:, None] <= np.arange(tm)[None, :]).astype(np.float32)
    tok = lambda s: (s, 0)
    const2 = lambda s: (0, 0)
    tile3 = lambda s: (s, 0, 0)
    return pl.pallas_call(
        functools.partial(_postmix_kernel, n_p=n_p),
        grid=(nt,),
        in_specs=[pl.BlockSpec((tm, D_NA), tok),
                  pl.BlockSpec((tm, D_WA), tok),
                  pl.BlockSpec((tm, D), lambda s: (jnp.minimum(s, n_p - 1), 0)),
                  pl.BlockSpec((tm, D), lambda s: (jnp.maximum(s - n_p, 0), 0)),
                  pl.BlockSpec((1, 6, D), lambda s: (_batch_of_tile(s, groups, tm), 0, 0)),
                  pl.BlockSpec((D, D), const2),
                  pl.BlockSpec((1, D), const2),
                  pl.BlockSpec((1, D), const2),
                  pl.BlockSpec((N_EXPERTS, D), const2),
                  pl.BlockSpec((N_EXPERTS, 1), const2),
                  pl.BlockSpec((tm, tm), const2)],
        out_specs=[pl.BlockSpec((tm, D), tok),
                   pl.BlockSpec((tm * ROW_TILE, LANES), tok),
                   pl.BlockSpec((1, TOP_K, tm), tile3),
                   pl.BlockSpec((1, TOP_K, tm), tile3),
                   pl.BlockSpec((1, TOP_K, tm), tile3),
                   pl.BlockSpec((N_EXPERTS, 1), const2)],
        out_shape=[jax.ShapeDtypeStruct((nt * tm, D), F32),
                   jax.ShapeDtypeStruct((nt * tm * ROW_TILE, LANES), F32),
                   jax.ShapeDtypeStruct((nt, TOP_K, tm), jnp.int32),
                   jax.ShapeDtypeStruct((nt, TOP_K, tm), F32),
                   jax.ShapeDtypeStruct((nt, TOP_K, tm), jnp.int32),
                   jax.ShapeDtypeStruct((N_EXPERTS, 1), F32)],
        scratch_shapes=[pltpu.VMEM((N_EXPERTS, 1), F32)],
        compiler_params=_cparams("arbitrary"),
        name="post_mix",
    )(na, wa, xp, xs, mod, w_out_bf16, post_mix.reshape(1, D), pre_ffn.reshape(1, D),
      w_router_t, b_router.reshape(N_EXPERTS, 1), jnp.asarray(tri, BF16))


def _expert_kernel(blk_e_ref, src_hbm, dst_hbm, h_hbm, wgu_ref, bgu_ref, wd_ref, bd_ref, y_hbm,
                   src_smem, dst_smem, xbuf, xb, act_bf, obuf, wgu_bf, wd_bf, idx_sem, g_sem, s_sem, *, n_blocks):
    i = pl.program_id(0)
    BM = EXPERT_BLOCK
    RT = ROW_TILE
    last = n_blocks - 1

    def idx_copies(blk, slot):
        return (pltpu.make_async_copy(src_hbm.at[blk], src_smem.at[slot], idx_sem.at[slot]),
                pltpu.make_async_copy(dst_hbm.at[blk], dst_smem.at[slot], idx_sem.at[slot]))

    def start_idx(blk, slot):
        for c in idx_copies(blk, slot):
            c.start()

    def wait_idx(blk, slot):
        for c in idx_copies(blk, slot):
            c.wait()

    def start_gather(islot, xslot):
        for r in range(BM):
            row = pl.multiple_of(src_smem[islot, 0, r], RT)
            pltpu.make_async_copy(h_hbm.at[pl.ds(row, RT), :], xbuf.at[xslot, pl.ds(r * RT, RT), :],
                                  g_sem.at[xslot]).start()

    def wait_gather(xslot):
        pltpu.make_async_copy(h_hbm.at[pl.ds(0, BM * RT), :], xbuf.at[xslot], g_sem.at[xslot]).wait()

    def start_scatter(islot, oslot):
        for r in range(BM):
            row = pl.multiple_of(dst_smem[islot, 0, r], RT)
            pltpu.make_async_copy(obuf.at[oslot, pl.ds(r * RT, RT), :], y_hbm.at[pl.ds(row, RT), :],
                                  s_sem.at[oslot]).start()

    def wait_scatter(oslot):
        pltpu.make_async_copy(obuf.at[oslot], y_hbm.at[pl.ds(0, BM * RT), :], s_sem.at[oslot]).wait()

    xs = lax.rem(i, X_SLOTS)
    os_ = lax.rem(i, 2)

    @pl.when(i == 0)
    def _():
        for b in range(min(3, n_blocks)):
            start_idx(b, b)
        for b in range(min(2, n_blocks)):
            wait_idx(b, b)
            start_gather(b, b)

    @pl.when(i + 3 < n_blocks)
    def _():
        start_idx(i + 3, lax.rem(i + 3, IDX_SLOTS))

    @pl.when(i + 2 < n_blocks)
    def _():
        wait_idx(i + 2, lax.rem(i + 2, IDX_SLOTS))

    wait_gather(xs)

    @pl.when(i >= 2)
    def _():
        wait_scatter(os_)

    new_expert = jnp.logical_or(i == 0, blk_e_ref[i] != blk_e_ref[jnp.maximum(i - 1, 0)])

    @pl.when(new_expert)
    def _():
        wgu_bf[...] = wgu_ref[0].astype(BF16)
        wd_bf[...] = wd_ref[0].astype(BF16)

    def up_phase():
        for c in range(RT):
            xb[:, c * LANES:(c + 1) * LANES] = xbuf[xs, pl.ds(c, BM, stride=RT), :].astype(BF16)
        start_gather(lax.rem(jnp.minimum(i + 2, last), IDX_SLOTS), lax.rem(i + 2, X_SLOTS))
        gu = jnp.dot(xb[...], wgu_bf[...], preferred_element_type=F32) + bgu_ref[0]
        glu = jnp.minimum(gu[:, :D_EXPERT], SWIGLU_LIMIT)
        lin = jnp.clip(gu[:, D_EXPERT:], -SWIGLU_LIMIT, SWIGLU_LIMIT)
        act = glu * (1.0 / (1.0 + jnp.exp(-SWIGLU_ALPHA * glu))) * (lin + 1.0)
        act_bf[...] = act.astype(BF16)

    def down_phase(scatter_prev):
        if scatter_prev:
            start_scatter(lax.rem(i - 1, IDX_SLOTS), 1 - os_)
        y = jnp.dot(act_bf[...], wd_bf[...], preferred_element_type=F32) + bd_ref[0]
        for c in range(RT):
            obuf[os_, pl.ds(c, BM, stride=RT), :] = y[:, c * LANES:(c + 1) * LANES]

    pl.when(i >= 0)(up_phase)
    pl.when(i == 0)(lambda: down_phase(False))
    pl.when(i > 0)(lambda: down_phase(True))

    @pl.when(i == last)
    def _():
        start_scatter(lax.rem(i, IDX_SLOTS), os_)
        wait_scatter(os_)
        if n_blocks > 1:
            wait_scatter(1 - os_)
        wait_gather(lax.rem(i + 1, X_SLOTS))
        wait_gather(lax.rem(i + 2, X_SLOTS))


def _experts(h2, src_row, dst_row, blk_e, w_gate_up, b_gate_up, w_down, b_down):
    D = D_MODEL
    BM = EXPERT_BLOCK
    n_blocks = blk_e.shape[0]
    wmap = lambda i, be: (be[i], 0, 0)
    grid_spec = pltpu.PrefetchScalarGridSpec(
        num_scalar_prefetch=1,
        grid=(n_blocks,),
        in_specs=[pl.BlockSpec(memory_space=pl.ANY),
                  pl.BlockSpec(memory_space=pl.ANY),
                  pl.BlockSpec(memory_space=pl.ANY),
                  pl.BlockSpec((1, D, 2 * D_EXPERT), wmap),
                  pl.BlockSpec((1, 1, 2 * D_EXPERT), wmap),
                  pl.BlockSpec((1, D_EXPERT, D), wmap),
                  pl.BlockSpec((1, 1, D), wmap)],
        out_specs=pl.BlockSpec(memory_space=pl.ANY),
        scratch_shapes=[pltpu.SMEM((IDX_SLOTS, 1, BM), jnp.int32),
                        pltpu.SMEM((IDX_SLOTS, 1, BM), jnp.int32),
                        pltpu.VMEM((X_SLOTS, BM * ROW_TILE, LANES), F32),
                        pltpu.VMEM((BM, D), BF16),
                        pltpu.VMEM((BM, D_EXPERT), BF16),
                        pltpu.VMEM((2, BM * ROW_TILE, LANES), F32),
                        pltpu.VMEM((D, 2 * D_EXPERT), BF16),
                        pltpu.VMEM((D_EXPERT, D), BF16),
                        pltpu.SemaphoreType.DMA((IDX_SLOTS,)),
                        pltpu.SemaphoreType.DMA((X_SLOTS,)),
                        pltpu.SemaphoreType.DMA((2,))],
    )
    return pl.pallas_call(
        functools.partial(_expert_kernel, n_blocks=n_blocks),
        grid_spec=grid_spec,
        out_shape=jax.ShapeDtypeStruct((n_blocks * BM * ROW_TILE, LANES), F32),
        compiler_params=_cparams("arbitrary"),
        name="experts",
    )(blk_e, src_row.reshape(n_blocks, 1, BM), dst_row.reshape(n_blocks, 1, BM), h2,
      w_gate_up, b_gate_up.reshape(N_EXPERTS, 1, -1), w_down, b_down.reshape(N_EXPERTS, 1, -1))


def _combine_kernel(y0_ref, y1_ref, y2_ref, y3_ref, gate_ref, x1_ref, mod_ref, pf_ref,
                    op_ref, os_ref, *, n_p):
    s = pl.program_id(0)
    g = gate_ref[...]

    def rows(y_ref):
        return jnp.concatenate([y_ref[pl.ds(c, COMBINE_TILE, stride=ROW_TILE), :]
                                for c in range(ROW_TILE)], axis=-1)

    m = rows(y0_ref) * g[:, 0:1]
    for k, y_ref in enumerate((y1_ref, y2_ref, y3_ref), start=1):
        m = m + rows(y_ref) * g[:, k:k + 1]
    res = x1_ref[...] + mod_ref[0, 5:6, :] * _rms(m, pf_ref[...])

    @pl.when(s < n_p)
    def _():
        op_ref[...] = res

    @pl.when(s >= n_p)
    def _():
        os_ref[...] = res


def _combine(y4, gates_tok, x1, mod, post_ffn, n_tok_p, n_tok_s, groups):
    D = D_MODEL
    tc = COMBINE_TILE
    n_p, n_s = n_tok_p // tc, n_tok_s // tc
    nt = n_p + n_s

    def slot(k):
        return pl.BlockSpec((tc * ROW_TILE, LANES), lambda s: (k * nt + s, 0))

    return pl.pallas_call(
        functools.partial(_combine_kernel, n_p=n_p),
        grid=(nt,),
        in_specs=[slot(0), slot(1), slot(2), slot(3),
                  pl.BlockSpec((tc, TOP_K), lambda s: (s, 0)),
                  pl.BlockSpec((tc, D), lambda s: (s, 0)),
                  pl.BlockSpec((1, 6, D), lambda s: (_batch_of_tile(s, groups, tc), 0, 0)),
                  pl.BlockSpec((1, D), lambda s: (0, 0))],
        out_specs=[pl.BlockSpec((tc, D), lambda s: (jnp.minimum(s, n_p - 1), 0)),
                   pl.BlockSpec((tc, D), lambda s: (jnp.maximum(s - n_p, 0), 0))],
        out_shape=[jax.ShapeDtypeStruct((n_tok_p, D), F32),
                   jax.ShapeDtypeStruct((n_tok_s, D), F32)],
        compiler_params=_cparams("arbitrary"),
        name="combine",
    )(y4, y4, y4, y4, gates_tok, x1, mod, post_ffn.reshape(1, D))


def _routing_plan(idx_t, rank_t, counts, n_tok):
    BM = EXPERT_BLOCK
    nt, _, tm = idx_t.shape
    n_blocks = (n_tok * TOP_K) // BM + N_EXPERTS
    padded = ((counts + BM - 1) // BM) * BM
    pad_end = jnp.cumsum(padded)
    pad_start = pad_end - padded
    experts = jnp.arange(N_EXPERTS, dtype=jnp.int32)
    start_of = jnp.sum(jnp.where(idx_t[..., None] == experts, pad_start, 0), axis=-1)
    dest = (start_of + rank_t).astype(jnp.int32)
    t_of = (jnp.arange(nt, dtype=jnp.int32)[:, None, None] * tm
            + jnp.arange(tm, dtype=jnp.int32)[None, None, :])
    slot_row = jnp.arange(TOP_K, dtype=jnp.int32)[None, :, None] * n_tok + t_of
    blk_first = jnp.arange(n_blocks, dtype=jnp.int32) * BM
    blk_e = jnp.minimum(jnp.sum((pad_end[None, :] <= blk_first[:, None]).astype(jnp.int32), axis=1),
                        N_EXPERTS - 1)
    of_blk = blk_e[:, None] == experts
    per_blk = lambda v: jnp.sum(jnp.where(of_blk, v, 0), axis=1)[:, None]
    pos = blk_first[:, None] + jnp.arange(BM, dtype=jnp.int32)[None, :]
    real_before = per_blk(jnp.cumsum(counts) - counts) + jnp.minimum(pos - per_blk(pad_start),
                                                                     per_blk(counts))
    spare = n_tok * TOP_K + pos - real_before
    inv = spare.reshape(-1).astype(jnp.int32).at[dest.reshape(-1)].set(slot_row.reshape(-1))
    src_row = jnp.where(inv < n_tok * TOP_K, inv % n_tok, 0) * ROW_TILE
    dst_row = inv * ROW_TILE
    return src_row, dst_row, blk_e


def kernel(x_prompt, x_sample, c_prompt, c_sample, w_ada, b_ada, pre_mix, post_mix, pre_ffn, post_ffn,
           w_in, na_rel_bias, t5_rel_bias, wa_sink, gn_na, gn_wa, w_out, w_router, b_router,
           w_gate_up, b_gate_up, w_down, b_down):
    depth = w_ada.shape[0]
    D = D_MODEL
    groups = (x_prompt.shape[:2], x_sample.shape[:2])
    n_tok_p = x_prompt.shape[0] * x_prompt.shape[1]
    n_tok_s = x_sample.shape[0] * x_sample.shape[1]
    n_tok = n_tok_p + n_tok_s
    yp = x_prompt.reshape(n_tok_p, D)
    ys = x_sample.reshape(n_tok_s, D)
    c_all = jnp.concatenate([c_prompt, c_sample], axis=0)
    for l in range(depth):
        mod = _ada(c_all, w_ada[l], b_ada[l]).reshape(c_all.shape[0], 6, D)
        qb0 = 3 * D_NA
        scale = HEAD_DIM ** -0.5
        w_in_l = jnp.concatenate([w_in[l][:, :D_NA] * scale, w_in[l][:, D_NA:qb0],
                                  _wa_slab_order(w_in[l][:, qb0:qb0 + D_WA] * scale, axis=1),
                                  w_in[l][:, qb0 + D_WA:]], axis=1).astype(BF16)
        w_out_l = jnp.concatenate([w_out[l][:D_NA], _wa_slab_order(w_out[l][D_NA:], axis=0)],
                                  axis=0).astype(BF16)
        proj = _in_proj(yp, ys, mod, pre_mix[l], w_in_l, groups)
        na = _na_attention(proj, _na_bias_table(na_rel_bias[l]), gn_na[l], groups)
        wa = _wa_attention(proj, _wa_bias_table(t5_rel_bias), wa_sink[l],
                           _wa_slab_order(gn_wa[l], axis=0), groups)
        x1, h2, idx_t, gate_t, rank_t, counts = _post_mix(
            na, wa, yp, ys, mod, w_out_l, post_mix[l], pre_ffn[l], w_router[l].T, b_router[l], groups)
        src_row, dst_row, blk_e = _routing_plan(
            idx_t, rank_t, counts.reshape(N_EXPERTS).astype(jnp.int32), n_tok)
        y4 = _experts(h2, src_row, dst_row, blk_e, w_gate_up[l], b_gate_up[l], w_down[l], b_down[l])
        gates_tok = jnp.transpose(gate_t, (0, 2, 1)).reshape(n_tok, TOP_K)
        yp, ys = _combine(y4, gates_tok, x1, mod, post_ffn[l], n_tok_p, n_tok_s, groups)
    return (yp.reshape(x_prompt.shape), ys.reshape(x_sample.shape))
```

```python
import functools
import math

import jax
import jax.numpy as jnp
import numpy as np
from jax import lax
from jax.experimental import pallas as pl
from jax.experimental.pallas import tpu as pltpu

F32 = jnp.float32
BF16 = jnp.bfloat16

D_MODEL = 1024
HEAD_DIM = 64
D_NA = 512
D_WA = 512
N_HEADS_NA = 8
N_HEADS_WA = 8
N_KV_WA = 2
GQA_GROUP = 4
D_KV_WA = 128
D_IN_PROJ = 3 * D_NA + D_WA + 2 * D_KV_WA
GRID_W = 64
NA_WIN_R = 8
NA_WIN_C = 16
WA_WINDOW = 128
WA_BLOCK = 128
T5_BUCKETS = 32
T5_MAX_EXACT = 8
T5_MAX_DIST = 128
N_EXPERTS = 32
TOP_K = 4
D_EXPERT = D_MODEL
SWIGLU_LIMIT = 7.0
SWIGLU_ALPHA = 1.702
RMS_EPS = 1e-6
NEG_INF = -1e30

VMEM_LIMIT_BYTES = 56 * 1024 * 1024
LANES = 128
ROW_TILE = D_MODEL // LANES

TOKEN_TILE = 512
NA_ROWS_PER_STEP = 8
EXPERT_BLOCK = 256
X_SLOTS = 3
IDX_SLOTS = X_SLOTS + 2
O_SLOTS = 3
GATHER_PACING = 3
SCATTER_PACING = 0
COMBINE_TILE = 256


def _cparams(*sem):
    return pltpu.CompilerParams(dimension_semantics=sem, vmem_limit_bytes=VMEM_LIMIT_BYTES)


def _rms(x, g):
    return x * lax.rsqrt(jnp.mean(x * x, axis=-1, keepdims=True) + RMS_EPS) * g


def _seq_bounds(s, groups, block):
    base = 0
    start, length = None, None
    for n_seq, seq_len in groups:
        per = seq_len // block
        g_start = base + ((s - base) // per) * per
        if start is None:
            start, length = g_start, per
        else:
            inside = s >= base
            start = jnp.where(inside, g_start, start)
            length = jnp.where(inside, per, length)
        base += n_seq * per
    return start, length


def _batch_of_tile(s, groups, block):
    base_blk, base_seq = 0, 0
    out = None
    for n_seq, seq_len in groups:
        per = seq_len // block
        b = base_seq + (s - base_blk) // per
        out = b if out is None else jnp.where(s >= base_blk, b, out)
        base_blk += n_seq * per
        base_seq += n_seq
    return out


def _ada_kernel(c_ref, w_ref, b_ref, o_ref):
    c = c_ref[...]
    s = c * (1.0 / (1.0 + jnp.exp(-c)))
    o_ref[...] = jnp.dot(s, w_ref[...], preferred_element_type=F32,
                         precision=lax.Precision.HIGHEST) + b_ref[...]


def _ada(c_all, w_ada, b_ada):
    nb = c_all.shape[0]
    return pl.pallas_call(
        _ada_kernel,
        grid=(6,),
        in_specs=[pl.BlockSpec((nb, D_MODEL), lambda j: (0, 0)),
                  pl.BlockSpec((D_MODEL, D_MODEL), lambda j: (0, j)),
                  pl.BlockSpec((1, D_MODEL), lambda j: (0, j))],
        out_specs=pl.BlockSpec((nb, D_MODEL), lambda j: (0, j)),
        out_shape=jax.ShapeDtypeStruct((nb, 6 * D_MODEL), F32),
        compiler_params=_cparams("arbitrary"),
        name="ada",
    )(c_all, w_ada, b_ada.reshape(1, -1))


def _inproj_kernel(xp_ref, xs_ref, mod_ref, g_ref, w_ref, o_ref, *, n_p):
    s = pl.program_id(0)

    def body(x_ref):
        h = _rms(x_ref[...], g_ref[...]) * (1.0 + mod_ref[0, 1:2, :]) + mod_ref[0, 0:1, :]
        o_ref[...] = jnp.dot(h.astype(BF16), w_ref[...], preferred_element_type=F32).astype(BF16)

    pl.when(s < n_p)(lambda: body(xp_ref))
    pl.when(s >= n_p)(lambda: body(xs_ref))


def _in_proj(xp, xs, mod, pre_mix, w_in_bf16, groups):
    D = D_MODEL
    tm = TOKEN_TILE
    n_p, n_s = xp.shape[0] // tm, xs.shape[0] // tm
    return pl.pallas_call(
        functools.partial(_inproj_kernel, n_p=n_p),
        grid=(n_p + n_s,),
        in_specs=[pl.BlockSpec((tm, D), lambda s: (jnp.minimum(s, n_p - 1), 0)),
                  pl.BlockSpec((tm, D), lambda s: (jnp.maximum(s - n_p, 0), 0)),
                  pl.BlockSpec((1, 6, D), lambda s: (_batch_of_tile(s, groups, tm), 0, 0)),
                  pl.BlockSpec((1, D), lambda s: (0, 0)),
                  pl.BlockSpec((D, D_IN_PROJ), lambda s: (0, 0))],
        out_specs=pl.BlockSpec((tm, D_IN_PROJ), lambda s: (s, 0)),
        out_shape=jax.ShapeDtypeStruct(((n_p + n_s) * tm, D_IN_PROJ), BF16),
        compiler_params=_cparams("arbitrary"),
        name="in_proj",
    )(xp, xs, mod, pre_mix.reshape(1, D), w_in_bf16)


def _na_bias_table(rpb):
    qc = np.arange(GRID_W)[:, None]
    kc = np.arange(GRID_W)[None, :]
    win_start = np.clip(qc - NA_WIN_C // 2, 0, GRID_W - NA_WIN_C)
    valid = (kc >= win_start) & (kc < win_start + NA_WIN_C)
    dc = np.clip(kc - qc, -(NA_WIN_C - 1), NA_WIN_C - 1) + NA_WIN_C - 1
    full = rpb[:, :, dc]
    full = jnp.where(valid[None, None], full.astype(F32), NEG_INF)
    two = jnp.concatenate([full[:, :-1], full[:, 1:]], axis=-1)
    two = two.reshape(N_HEADS_NA // 2, 2, 2 * NA_WIN_R - 2, GRID_W, 2 * GRID_W)
    return jnp.transpose(two, (0, 2, 1, 3, 4)).reshape(
        N_HEADS_NA // 2, 2 * NA_WIN_R - 2, 2 * GRID_W, 2 * GRID_W)


def _na_kernel(q_ref, kp_ref, kc_ref, kn_ref, vp_ref, vc_ref, vn_ref, bias_ref, gn_ref, o_ref,
               k_scr, v_scr, *, groups):
    s = pl.program_id(0)
    R = NA_ROWS_PER_STEP
    blk = R * GRID_W
    seq_start, seq_blocks = _seq_bounds(s, groups, blk)
    i = s - seq_start
    rows = seq_blocks * R
    k_scr[0:blk] = kp_ref[...]
    k_scr[blk:2 * blk] = kc_ref[...]
    k_scr[2 * blk:3 * blk] = kn_ref[...]
    v_scr[0:blk] = vp_ref[...]
    v_scr[blk:2 * blk] = vc_ref[...]
    v_scr[2 * blk:3 * blk] = vn_ref[...]
    lane = lax.broadcasted_iota(jnp.int32, (1, LANES), 1)
    lo = lane < HEAD_DIM
    gn = gn_ref[...]

    def row_body(j, carry):
        r = i * R + j
        row_start = jnp.clip(r - NA_WIN_R // 2, 0, rows - NA_WIN_R)
        shift = r - row_start
        koff = pl.multiple_of((row_start - i * R + R) * GRID_W, GRID_W)
        qoff = pl.multiple_of(j * GRID_W, GRID_W)
        outs = []
        for p in range(N_HEADS_NA // 2):
            cs = slice(p * LANES, (p + 1) * LANES)
            q2 = q_ref[pl.ds(qoff, GRID_W), cs]
            zero = jnp.zeros_like(q2)
            qq = jnp.concatenate([jnp.where(lo, q2, zero), jnp.where(lo, zero, q2)], axis=0)
            k2 = k_scr[pl.ds(koff, NA_WIN_R * GRID_W), cs]
            v2 = v_scr[pl.ds(koff, NA_WIN_R * GRID_W), cs]
            sc = lax.dot_general(qq, k2, (((1,), (1,)), ((), ())),
                                 preferred_element_type=F32)
            parts = []
            for m in range(NA_WIN_R // 2):
                d = 2 * m - shift + (NA_WIN_R - 1)
                parts.append(sc[:, m * LANES:(m + 1) * LANES] + bias_ref[p, d])
            sc = jnp.concatenate(parts, axis=-1)
            mx = jnp.max(sc, axis=-1, keepdims=True)
            e = jnp.exp(sc - mx)
            l = jnp.sum(e, axis=-1, keepdims=True)
            o2 = jnp.dot(e.astype(BF16), v2, preferred_element_type=F32) / l
            outs.append(jnp.where(lo, o2[:GRID_W], o2[GRID_W:]))
        o = jnp.concatenate(outs, axis=-1)
        o_ref[pl.ds(qoff, GRID_W), :] = _rms(o, gn).astype(BF16)
        return carry

    lax.fori_loop(0, R, row_body, 0)


def _na_attention(proj, bias_tab, gn_na, groups):
    n_tok = proj.shape[0]
    blk = NA_ROWS_PER_STEP * GRID_W

    def kv(col, delta):
        def index_map(s):
            start, length = _seq_bounds(s, groups, blk)
            return (jnp.clip(s + delta, start, start + length - 1), col)
        return pl.BlockSpec((blk, D_NA), index_map)

    return pl.pallas_call(
        functools.partial(_na_kernel, groups=groups),
        grid=(n_tok // blk,),
        in_specs=[pl.BlockSpec((blk, D_NA), lambda s: (s, 0)),
                  kv(1, -1), kv(1, 0), kv(1, 1), kv(2, -1), kv(2, 0), kv(2, 1),
                  pl.BlockSpec(bias_tab.shape, lambda s: (0, 0, 0, 0)),
                  pl.BlockSpec((1, D_NA), lambda s: (0, 0))],
        out_specs=pl.BlockSpec((blk, D_NA), lambda s: (s, 0)),
        out_shape=jax.ShapeDtypeStruct((n_tok, D_NA), BF16),
        scratch_shapes=[pltpu.VMEM((3 * blk, D_NA), BF16), pltpu.VMEM((3 * blk, D_NA), BF16)],
        compiler_params=_cparams("arbitrary"),
        name="na_attn",
    )(proj, proj, proj, proj, proj, proj, proj, bias_tab, gn_na.reshape(1, D_NA))


def _wa_slab_order(a, axis):
    shape = a.shape
    split = shape[:axis] + (N_KV_WA, GQA_GROUP, HEAD_DIM) + shape[axis + 1:]
    return jnp.swapaxes(a.reshape(split), axis, axis + 1).reshape(shape)


def _t5_bucket(rel):
    half = T5_BUCKETS // 2
    ret = (rel > 0).astype(jnp.int32) * half
    n = jnp.abs(rel)
    nf = jnp.maximum(n, 1).astype(F32)
    large = T5_MAX_EXACT + (jnp.log(nf / T5_MAX_EXACT) / math.log(T5_MAX_DIST / T5_MAX_EXACT)
                            * (half - T5_MAX_EXACT)).astype(jnp.int32)
    large = jnp.minimum(large, half - 1)
    return ret + jnp.where(n < T5_MAX_EXACT, n, large)


def _wa_bias_table(t5_table):
    rel = jnp.arange(3 * WA_BLOCK)[None, :] - WA_BLOCK - jnp.arange(WA_BLOCK)[:, None]
    bias = t5_table[_t5_bucket(rel)].astype(F32)
    bias = jnp.where((jnp.abs(rel) <= WA_WINDOW)[..., None], bias, NEG_INF)
    bias = jnp.transpose(bias, (2, 0, 1))
    return jnp.concatenate([bias[:GQA_GROUP], bias[GQA_GROUP:]], axis=1)


def _wa_kernel(sink_ref, q_ref, kp_ref, kc_ref, kn_ref, vp_ref, vc_ref, vn_ref, bias_ref, gn_ref,
               o_ref, *, groups):
    s = pl.program_id(0)
    W = WA_BLOCK
    seq_start, seq_blocks = _seq_bounds(s, groups, W)
    lane = lax.broadcasted_iota(jnp.int32, (1, LANES), 1)
    lo = lane < HEAD_DIM
    row = lax.broadcasted_iota(jnp.int32, (2 * W, 1), 0)
    kcat = jnp.concatenate([kp_ref[...], kc_ref[...], kn_ref[...]], axis=0)
    vcat = jnp.concatenate([vp_ref[...], vc_ref[...], vn_ref[...]], axis=0)
    pen_prev = jnp.where(s == seq_start, NEG_INF, 0.0).astype(F32)
    pen_next = jnp.where(s == seq_start + seq_blocks - 1, NEG_INF, 0.0).astype(F32)
    col = lax.broadcasted_iota(jnp.int32, (1, 3 * W), 1)
    pen = jnp.where(col < W, pen_prev, jnp.where(col >= 2 * W, pen_next, 0.0))
    outs = []
    for p in range(GQA_GROUP):
        q2 = q_ref[:, p * LANES:(p + 1) * LANES]
        zero = jnp.zeros_like(q2)
        qq = jnp.concatenate([jnp.where(lo, q2, zero), jnp.where(lo, zero, q2)], axis=0)
        sc = lax.dot_general(qq, kcat, (((1,), (1,)), ((), ())),
                             preferred_element_type=F32)
        sc = sc + bias_ref[p] + pen
        sink = jnp.where(row < W, sink_ref[p], sink_ref[p + GQA_GROUP])
        mx = jnp.maximum(jnp.max(sc, axis=-1, keepdims=True), sink)
        e = jnp.exp(sc - mx)
        l = jnp.sum(e, axis=-1, keepdims=True) + jnp.exp(sink - mx)
        o2 = jnp.dot(e.astype(BF16), vcat, preferred_element_type=F32) / l
        outs.append(jnp.where(lo, o2[:W], o2[W:]))
    o = jnp.concatenate(outs, axis=-1)
    o_ref[...] = _rms(o, gn_ref[...]).astype(BF16)


def _wa_attention(proj, bias_tab, sink, gn_wa_perm, groups):
    n_tok = proj.shape[0]
    W = WA_BLOCK
    q_col = 3 * D_NA // D_WA
    k_col = (3 * D_NA + D_WA) // D_KV_WA
    v_col = k_col + 1

    def kv(col, delta):
        def index_map(s):
            start, length = _seq_bounds(s, groups, W)
            return (jnp.clip(s + delta, start, start + length - 1), col)
        return pl.BlockSpec((W, D_KV_WA), index_map)

    return pl.pallas_call(
        functools.partial(_wa_kernel, groups=groups),
        grid=(n_tok // W,),
        in_specs=[pl.BlockSpec(memory_space=pltpu.SMEM),
                  pl.BlockSpec((W, D_WA), lambda s: (s, q_col)),
                  kv(k_col, -1), kv(k_col, 0), kv(k_col, 1),
                  kv(v_col, -1), kv(v_col, 0), kv(v_col, 1),
                  pl.BlockSpec(bias_tab.shape, lambda s: (0, 0, 0)),
                  pl.BlockSpec((1, D_WA), lambda s: (0, 0))],
        out_specs=pl.BlockSpec((W, D_WA), lambda s: (s, 0)),
        out_shape=jax.ShapeDtypeStruct((n_tok, D_WA), BF16),
        compiler_params=_cparams("arbitrary"),
        name="wa_attn",
    )(sink, proj, proj, proj, proj, proj, proj, proj, bias_tab, gn_wa_perm.reshape(1, D_WA))


def _postmix_kernel(na_ref, wa_ref, xp_ref, xs_ref, mod_ref, wo_ref, pm_ref, pf_ref, wr_ref, br_ref,
                    tri_ref, x1_ref, h2_ref, idx_ref, gate_ref, rank_ref, cnt_ref, carry, *, n_p):
    s = pl.program_id(0)

    @pl.when(s == 0)
    def _():
        carry[...] = jnp.zeros_like(carry)

    def body(x_ref):
        a = jnp.concatenate([na_ref[...], wa_ref[...]], axis=-1)
        mixed = jnp.dot(a, wo_ref[...], preferred_element_type=F32)
        x1 = x_ref[...] + mod_ref[0, 2:3, :] * _rms(mixed, pm_ref[...])
        x1_ref[...] = x1
        h2 = _rms(x1, pf_ref[...]) * (1.0 + mod_ref[0, 4:5, :]) + mod_ref[0, 3:4, :]
        for c in range(ROW_TILE):
            h2_ref[pl.ds(c, TOKEN_TILE, stride=ROW_TILE), :] = h2[:, c * LANES:(c + 1) * LANES]
        logits = lax.dot_general(wr_ref[...], h2, (((1,), (1,)), ((), ())),
                                 preferred_element_type=F32,
                                 precision=lax.Precision.HIGHEST) + br_ref[...]
        eidx = lax.broadcasted_iota(jnp.int32, logits.shape, 0)
        tops, sels = [], []
        l = logits
        for _ in range(TOP_K):
            m = jnp.max(l, axis=0, keepdims=True)
            sel = jnp.min(jnp.where(l == m, eidx, N_EXPERTS), axis=0, keepdims=True)
            tops.append(m)
            sels.append(sel)
            l = jnp.where(eidx == sel, -jnp.inf, l)
        es = [jnp.exp(t - tops[0]) for t in tops]
        tot = es[0] + es[1] + es[2] + es[3]
        idx_ref[0] = jnp.concatenate(sels, axis=0)
        gate_ref[0] = jnp.concatenate([e / tot for e in es], axis=0)
        hits = [eidx == sel for sel in sels]
        onehot = sum(h.astype(F32) for h in hits)
        prefix = jnp.dot(onehot.astype(BF16), tri_ref[...], preferred_element_type=F32)
        rank_e = carry[...] + prefix - onehot
        rank_ref[0] = jnp.concatenate(
            [jnp.sum(jnp.where(h, rank_e, 0.0), axis=0, keepdims=True) for h in hits],
            axis=0).astype(jnp.int32)
        carry[...] = carry[...] + jnp.sum(onehot, axis=1, keepdims=True)
        cnt_ref[...] = carry[...]

    pl.when(s < n_p)(lambda: body(xp_ref))
    pl.when(s >= n_p)(lambda: body(xs_ref))


def _post_mix(na, wa, xp, xs, mod, w_out_bf16, post_mix, pre_ffn, w_router_t, b_router, groups):
    D = D_MODEL
    tm = TOKEN_TILE
    n_p, n_s = xp.shape[0] // tm, xs.shape[0] // tm
    nt = n_p + n_s
    tri = (np.arange(tm)[:, None] <= np.arange(tm)[None, :]).astype(np.float32)
    tok = lambda s: (s, 0)
    const2 = lambda s: (0, 0)
    tile3 = lambda s: (s, 0, 0)
    return pl.pallas_call(
        functools.partial(_postmix_kernel, n_p=n_p),
        grid=(nt,),
        in_specs=[pl.BlockSpec((tm, D_NA), tok),
                  pl.BlockSpec((tm, D_WA), tok),
                  pl.BlockSpec((tm, D), lambda s: (jnp.minimum(s, n_p - 1), 0)),
                  pl.BlockSpec((tm, D), lambda s: (jnp.maximum(s - n_p, 0), 0)),
                  pl.BlockSpec((1, 6, D), lambda s: (_batch_of_tile(s, groups, tm), 0, 0)),
                  pl.BlockSpec((D, D), const2),
                  pl.BlockSpec((1, D), const2),
                  pl.BlockSpec((1, D), const2),
                  pl.BlockSpec((N_EXPERTS, D), const2),
                  pl.BlockSpec((N_EXPERTS, 1), const2),
                  pl.BlockSpec((tm, tm), const2)],
        out_specs=[pl.BlockSpec((tm, D), tok),
                   pl.BlockSpec((tm * ROW_TILE, LANES), tok),
                   pl.BlockSpec((1, TOP_K, tm), tile3),
                   pl.BlockSpec((1, TOP_K, tm), tile3),
                   pl.BlockSpec((1, TOP_K, tm), tile3),
                   pl.BlockSpec((N_EXPERTS, 1), const2)],
        out_shape=[jax.ShapeDtypeStruct((nt * tm, D), F32),
                   jax.ShapeDtypeStruct((nt * tm * ROW_TILE, LANES), F32),
                   jax.ShapeDtypeStruct((nt, TOP_K, tm), jnp.int32),
                   jax.ShapeDtypeStruct((nt, TOP_K, tm), F32),
                   jax.ShapeDtypeStruct((nt, TOP_K, tm), jnp.int32),
                   jax.ShapeDtypeStruct((N_EXPERTS, 1), F32)],
        scratch_shapes=[pltpu.VMEM((N_EXPERTS, 1), F32)],
        compiler_params=_cparams("arbitrary"),
        name="post_mix",
    )(na, wa, xp, xs, mod, w_out_bf16, post_mix.reshape(1, D), pre_ffn.reshape(1, D),
      w_router_t, b_router.reshape(N_EXPERTS, 1), jnp.asarray(tri, BF16))


def _expert_kernel(blk_e_ref, src_hbm, dst_hbm, h_hbm, wgu_ref, bgu_ref, wd_ref, bd_ref, y_hbm,
                   src_smem, dst_smem, xbuf, xb, act_bf, ob0, ob1, ob2, wgu_bf, wd_bf,
                   idx_sem, g_sem, s_sem, *, n_blocks):
    i = pl.program_id(0)
    BM = EXPERT_BLOCK
    RT = ROW_TILE
    last = n_blocks - 1
    obufs = (ob0, ob1, ob2)

    def idx_copies(blk, slot):
        return (pltpu.make_async_copy(src_hbm.at[blk], src_smem.at[slot], idx_sem.at[slot]),
                pltpu.make_async_copy(dst_hbm.at[blk], dst_smem.at[slot], idx_sem.at[slot]))

    def start_idx(blk, slot):
        for c in idx_copies(blk, slot):
            c.start()

    def wait_idx(blk, slot):
        for c in idx_copies(blk, slot):
            c.wait()

    def paced(row, prev, rounds):
        z = prev
        for _ in range(rounds):
            z = z + lax.shift_right_arithmetic(z, 31)
        return row + lax.shift_right_arithmetic(z, 31)

    def start_gather(islot, xslot, rounds):
        prev = src_smem[islot, 0, 0]
        for r in range(BM):
            prev = paced(src_smem[islot, 0, r], prev, rounds)
            row = pl.multiple_of(prev, RT)
            pltpu.make_async_copy(h_hbm.at[pl.ds(row, RT), :], xbuf.at[xslot, pl.ds(r * RT, RT), :],
                                  g_sem.at[xslot]).start()

    def wait_gather(xslot):
        pltpu.make_async_copy(h_hbm.at[pl.ds(0, BM * RT), :], xbuf.at[xslot], g_sem.at[xslot]).wait()

    def start_scatter(islot, k, rounds):
        prev = dst_smem[islot, 0, 0]
        for r in range(BM):
            prev = paced(dst_smem[islot, 0, r], prev, rounds)
            row = pl.multiple_of(prev, RT)
            pltpu.make_async_copy(obufs[k].at[pl.ds(r * RT, RT), :], y_hbm.at[pl.ds(row, RT), :],
                                  s_sem.at[k]).start(priority=r % 2)

    def wait_scatter(k):
        pltpu.make_async_copy(obufs[k], y_hbm.at[pl.ds(0, BM * RT), :], s_sem.at[k]).wait()

    xs = lax.rem(i, X_SLOTS)

    @pl.when(i == 0)
    def _():
        for b in range(min(3, n_blocks)):
            start_idx(b, b)
        for b in range(min(2, n_blocks)):
            wait_idx(b, b)
            start_gather(b, b, 0)

    @pl.when(i + 3 < n_blocks)
    def _():
        start_idx(i + 3, lax.rem(i + 3, IDX_SLOTS))

    @pl.when(i + 2 < n_blocks)
    def _():
        wait_idx(i + 2, lax.rem(i + 2, IDX_SLOTS))

    wait_gather(xs)

    new_expert = jnp.logical_or(i == 0, blk_e_ref[i] != blk_e_ref[jnp.maximum(i - 1, 0)])

    @pl.when(new_expert)
    def _():
        wgu_bf[...] = wgu_ref[0].astype(BF16)
        wd_bf[...] = wd_ref[0].astype(BF16)

    def up_phase():
        for c in range(RT):
            xb[:, c * LANES:(c + 1) * LANES] = xbuf[xs, pl.ds(c, BM, stride=RT), :].astype(BF16)
        start_gather(lax.rem(jnp.minimum(i + 2, last), IDX_SLOTS), lax.rem(i + 2, X_SLOTS),
                     GATHER_PACING)
        gu = jnp.dot(xb[...], wgu_bf[...], preferred_element_type=F32) + bgu_ref[0]
        glu = jnp.minimum(gu[:, :D_EXPERT], SWIGLU_LIMIT)
        lin = jnp.clip(gu[:, D_EXPERT:], -SWIGLU_LIMIT, SWIGLU_LIMIT)
        act = glu * (1.0 / (1.0 + jnp.exp(-SWIGLU_ALPHA * glu))) * (lin + 1.0)
        act_bf[...] = act.astype(BF16)

    def down_phase(k, scatter_prev):
        if scatter_prev:
            start_scatter(lax.rem(i - 1, IDX_SLOTS), (k - 1) % O_SLOTS, SCATTER_PACING)
        y = jnp.dot(act_bf[...], wd_bf[...], preferred_element_type=F32) + bd_ref[0]
        for c in range(RT):
            obufs[k][pl.ds(c, BM, stride=RT), :] = y[:, c * LANES:(c + 1) * LANES]

    pl.when(i >= 0)(up_phase)
    for k in range(O_SLOTS):
        @pl.when(jnp.logical_and(i >= O_SLOTS, lax.rem(i, O_SLOTS) == k))
        def _():
            wait_scatter(k)
    pl.when(i == 0)(lambda: down_phase(0, False))
    for k in range(O_SLOTS):
        pl.when(jnp.logical_and(i > 0, lax.rem(i, O_SLOTS) == k))(
            functools.partial(down_phase, k, True))

    @pl.when(i == last)
    def _():
        start_scatter(last % IDX_SLOTS, last % O_SLOTS, 0)
        for b in range(max(0, last - O_SLOTS + 1), last + 1):
            wait_scatter(b % O_SLOTS)
        wait_gather(lax.rem(i + 1, X_SLOTS))
        wait_gather(lax.rem(i + 2, X_SLOTS))


def _experts(h2, src_row, dst_row, blk_e, w_gate_up, b_gate_up, w_down, b_down):
    D = D_MODEL
    BM = EXPERT_BLOCK
    n_blocks = blk_e.shape[0]
    wmap = lambda i, be: (be[i], 0, 0)
    grid_spec = pltpu.PrefetchScalarGridSpec(
        num_scalar_prefetch=1,
        grid=(n_blocks,),
        in_specs=[pl.BlockSpec(memory_space=pl.ANY),
                  pl.BlockSpec(memory_space=pl.ANY),
                  pl.BlockSpec(memory_space=pl.ANY),
                  pl.BlockSpec((1, D, 2 * D_EXPERT), wmap),
                  pl.BlockSpec((1, 1, 2 * D_EXPERT), wmap),
                  pl.BlockSpec((1, D_EXPERT, D), wmap),
                  pl.BlockSpec((1, 1, D), wmap)],
        out_specs=pl.BlockSpec(memory_space=pl.ANY),
        scratch_shapes=[pltpu.SMEM((IDX_SLOTS, 1, BM), jnp.int32),
                        pltpu.SMEM((IDX_SLOTS, 1, BM), jnp.int32),
                        pltpu.VMEM((X_SLOTS, BM * ROW_TILE, LANES), F32),
                        pltpu.VMEM((BM, D), BF16),
                        pltpu.VMEM((BM, D_EXPERT), BF16),
                        pltpu.VMEM((BM * ROW_TILE, LANES), F32),
                        pltpu.VMEM((BM * ROW_TILE, LANES), F32),
                        pltpu.VMEM((BM * ROW_TILE, LANES), F32),
                        pltpu.VMEM((D, 2 * D_EXPERT), BF16),
                        pltpu.VMEM((D_EXPERT, D), BF16),
                        pltpu.SemaphoreType.DMA((IDX_SLOTS,)),
                        pltpu.SemaphoreType.DMA((X_SLOTS,)),
                        pltpu.SemaphoreType.DMA((O_SLOTS,))],
    )
    return pl.pallas_call(
        functools.partial(_expert_kernel, n_blocks=n_blocks),
        grid_spec=grid_spec,
        out_shape=jax.ShapeDtypeStruct((n_blocks * BM * ROW_TILE, LANES), F32),
        compiler_params=_cparams("arbitrary"),
        name="experts",
    )(blk_e, src_row.reshape(n_blocks, 1, BM), dst_row.reshape(n_blocks, 1, BM), h2,
      w_gate_up, b_gate_up.reshape(N_EXPERTS, 1, -1), w_down, b_down.reshape(N_EXPERTS, 1, -1))


def _combine_kernel(y0_ref, y1_ref, y2_ref, y3_ref, gate_ref, x1_ref, mod_ref, pf_ref,
                    op_ref, os_ref, *, n_p):
    s = pl.program_id(0)
    g = gate_ref[...]

    def rows(y_ref):
        return jnp.concatenate([y_ref[pl.ds(c, COMBINE_TILE, stride=ROW_TILE), :]
                                for c in range(ROW_TILE)], axis=-1)

    m = rows(y0_ref) * g[:, 0:1]
    for k, y_ref in enumerate((y1_ref, y2_ref, y3_ref), start=1):
        m = m + rows(y_ref) * g[:, k:k + 1]
    res = x1_ref[...] + mod_ref[0, 5:6, :] * _rms(m, pf_ref[...])

    @pl.when(s < n_p)
    def _():
        op_ref[...] = res

    @pl.when(s >= n_p)
    def _():
        os_ref[...] = res


def _combine(y4, gates_tok, x1, mod, post_ffn, n_tok_p, n_tok_s, groups):
    D = D_MODEL
    tc = COMBINE_TILE
    n_p, n_s = n_tok_p // tc, n_tok_s // tc
    nt = n_p + n_s

    def slot(k):
        return pl.BlockSpec((tc * ROW_TILE, LANES), lambda s: (k * nt + s, 0))

    return pl.pallas_call(
        functools.partial(_combine_kernel, n_p=n_p),
        grid=(nt,),
        in_specs=[slot(0), slot(1), slot(2), slot(3),
                  pl.BlockSpec((tc, TOP_K), lambda s: (s, 0)),
                  pl.BlockSpec((tc, D), lambda s: (s, 0)),
                  pl.BlockSpec((1, 6, D), lambda s: (_batch_of_tile(s, groups, tc), 0, 0)),
                  pl.BlockSpec((1, D), lambda s: (0, 0))],
        out_specs=[pl.BlockSpec((tc, D), lambda s: (jnp.minimum(s, n_p - 1), 0)),
                   pl.BlockSpec((tc, D), lambda s: (jnp.maximum(s - n_p, 0), 0))],
        out_shape=[jax.ShapeDtypeStruct((n_tok_p, D), F32),
                   jax.ShapeDtypeStruct((n_tok_s, D), F32)],
        compiler_params=_cparams("arbitrary"),
        name="combine",
    )(y4, y4, y4, y4, gates_tok, x1, mod, post_ffn.reshape(1, D))


def _routing_plan(idx_t, rank_t, counts, n_tok):
    BM = EXPERT_BLOCK
    nt, _, tm = idx_t.shape
    n_blocks = (n_tok * TOP_K) // BM + N_EXPERTS
    padded = ((counts + BM - 1) // BM) * BM
    pad_end = jnp.cumsum(padded)
    pad_start = pad_end - padded
    experts = jnp.arange(N_EXPERTS, dtype=jnp.int32)
    start_of = jnp.sum(jnp.where(idx_t[..., None] == experts, pad_start, 0), axis=-1)
    dest = (start_of + rank_t).astype(jnp.int32)
    t_of = (jnp.arange(nt, dtype=jnp.int32)[:, None, None] * tm
            + jnp.arange(tm, dtype=jnp.int32)[None, None, :])
    slot_row = jnp.arange(TOP_K, dtype=jnp.int32)[None, :, None] * n_tok + t_of
    blk_first = jnp.arange(n_blocks, dtype=jnp.int32) * BM
    blk_e = jnp.minimum(jnp.sum((pad_end[None, :] <= blk_first[:, None]).astype(jnp.int32), axis=1),
                        N_EXPERTS - 1)
    of_blk = blk_e[:, None] == experts
    per_blk = lambda v: jnp.sum(jnp.where(of_blk, v, 0), axis=1)[:, None]
    pos = blk_first[:, None] + jnp.arange(BM, dtype=jnp.int32)[None, :]
    real_before = per_blk(jnp.cumsum(counts) - counts) + jnp.minimum(pos - per_blk(pad_start),
                                                                     per_blk(counts))
    spare = n_tok * TOP_K + pos - real_before
    inv = spare.reshape(-1).astype(jnp.int32).at[dest.reshape(-1)].set(slot_row.reshape(-1))
    src_row = jnp.where(inv < n_tok * TOP_K, inv % n_tok, 0) * ROW_TILE
    dst_row = inv * ROW_TILE
    return src_row, dst_row, blk_e


def kernel(x_prompt, x_sample, c_prompt, c_sample, w_ada, b_ada, pre_mix, post_mix, pre_ffn, post_ffn,
           w_in, na_rel_bias, t5_rel_bias, wa_sink, gn_na, gn_wa, w_out, w_router, b_router,
           w_gate_up, b_gate_up, w_down, b_down):
    depth = w_ada.shape[0]
    D = D_MODEL
    groups = (x_prompt.shape[:2], x_sample.shape[:2])
    n_tok_p = x_prompt.shape[0] * x_prompt.shape[1]
    n_tok_s = x_sample.shape[0] * x_sample.shape[1]
    n_tok = n_tok_p + n_tok_s
    yp = x_prompt.reshape(n_tok_p, D)
    ys = x_sample.reshape(n_tok_s, D)
    c_all = jnp.concatenate([c_prompt, c_sample], axis=0)
    for l in range(depth):
        mod = _ada(c_all, w_ada[l], b_ada[l]).reshape(c_all.shape[0], 6, D)
        qb0 = 3 * D_NA
        scale = HEAD_DIM ** -0.5
        w_in_l = jnp.concatenate([w_in[l][:, :D_NA] * scale, w_in[l][:, D_NA:qb0],
                                  _wa_slab_order(w_in[l][:, qb0:qb0 + D_WA] * scale, axis=1),
                                  w_in[l][:, qb0 + D_WA:]], axis=1).astype(BF16)
        w_out_l = jnp.concatenate([w_out[l][:D_NA], _wa_slab_order(w_out[l][D_NA:], axis=0)],
                                  axis=0).astype(BF16)
        proj = _in_proj(yp, ys, mod, pre_mix[l], w_in_l, groups)
        na = _na_attention(proj, _na_bias_table(na_rel_bias[l]), gn_na[l], groups)
        wa = _wa_attention(proj, _wa_bias_table(t5_rel_bias), wa_sink[l],
                           _wa_slab_order(gn_wa[l], axis=0), groups)
        x1, h2, idx_t, gate_t, rank_t, counts = _post_mix(
            na, wa, yp, ys, mod, w_out_l, post_mix[l], pre_ffn[l], w_router[l].T, b_router[l], groups)
        src_row, dst_row, blk_e = _routing_plan(
            idx_t, rank_t, counts.reshape(N_EXPERTS).astype(jnp.int32), n_tok)
        y4 = _experts(h2, src_row, dst_row, blk_e, w_gate_up[l], b_gate_up[l], w_down[l], b_down[l])
        gates_tok = jnp.transpose(gate_t, (0, 2, 1)).reshape(n_tok, TOP_K)
        yp, ys = _combine(y4, gates_tok, x1, mod, post_ffn[l], n_tok_p, n_tok_s, groups)
    return (yp.reshape(x_prompt.shape), ys.reshape(x_sample.shape))
```

```python
import functools
import math

import jax
import jax.numpy as jnp
import numpy as np
from jax import lax
from jax.experimental import pallas as pl
from jax.experimental.pallas import tpu as pltpu

F32 = jnp.float32
BF16 = jnp.bfloat16

D_MODEL = 1024
HEAD_DIM = 64
D_NA = 512
D_WA = 512
N_HEADS_NA = 8
N_HEADS_WA = 8
N_KV_WA = 2
GQA_GROUP = 4
D_KV_WA = 128
D_IN_PROJ = 3 * D_NA + D_WA + 2 * D_KV_WA
GRID_W = 64
NA_WIN_R = 8
NA_WIN_C = 16
WA_WINDOW = 128
WA_BLOCK = 128
T5_BUCKETS = 32
T5_MAX_EXACT = 8
T5_MAX_DIST = 128
N_EXPERTS = 32
TOP_K = 4
D_EXPERT = D_MODEL
SWIGLU_LIMIT = 7.0
SWIGLU_ALPHA = 1.702
RMS_EPS = 1e-6
NEG_INF = -1e30

VMEM_LIMIT_BYTES = 56 * 1024 * 1024
LANES = 128
ROW_TILE = D_MODEL // LANES

TOKEN_TILE = 512
NA_ROWS_PER_STEP = 8
NA_ROW_UNROLL = 2
EXPERT_BLOCK = 512
X_SLOTS = 3
IDX_SLOTS = X_SLOTS + 2
O_SLOTS = 3
GATHER_PACING = 3
SCATTER_PACING = 0
COMBINE_TILE = 256


def _cparams(*sem):
    return pltpu.CompilerParams(dimension_semantics=sem, vmem_limit_bytes=VMEM_LIMIT_BYTES)


def _rms(x, g):
    return x * lax.rsqrt(jnp.mean(x * x, axis=-1, keepdims=True) + RMS_EPS) * g


def _seq_bounds(s, groups, block):
    base = 0
    start, length = None, None
    for n_seq, seq_len in groups:
        per = seq_len // block
        g_start = base + ((s - base) // per) * per
        if start is None:
            start, length = g_start, per
        else:
            inside = s >= base
            start = jnp.where(inside, g_start, start)
            length = jnp.where(inside, per, length)
        base += n_seq * per
    return start, length


def _batch_of_tile(s, groups, block):
    base_blk, base_seq = 0, 0
    out = None
    for n_seq, seq_len in groups:
        per = seq_len // block
        b = base_seq + (s - base_blk) // per
        out = b if out is None else jnp.where(s >= base_blk, b, out)
        base_blk += n_seq * per
        base_seq += n_seq
    return out


def _ada_kernel(c_ref, w_ref, b_ref, o_ref):
    c = c_ref[...]
    s = c * (1.0 / (1.0 + jnp.exp(-c)))
    o_ref[...] = jnp.dot(s, w_ref[...], preferred_element_type=F32,
                         precision=lax.Precision.HIGHEST) + b_ref[...]


def _ada(c_all, w_ada, b_ada):
    nb = c_all.shape[0]
    return pl.pallas_call(
        _ada_kernel,
        grid=(6,),
        in_specs=[pl.BlockSpec((nb, D_MODEL), lambda j: (0, 0)),
                  pl.BlockSpec((D_MODEL, D_MODEL), lambda j: (0, j)),
                  pl.BlockSpec((1, D_MODEL), lambda j: (0, j))],
        out_specs=pl.BlockSpec((nb, D_MODEL), lambda j: (0, j)),
        out_shape=jax.ShapeDtypeStruct((nb, 6 * D_MODEL), F32),
        compiler_params=_cparams("arbitrary"),
        name="ada",
    )(c_all, w_ada, b_ada.reshape(1, -1))


def _inproj_kernel(xp_ref, xs_ref, mod_ref, g_ref, w_ref, o_ref, *, n_p):
    s = pl.program_id(0)

    def body(x_ref):
        h = _rms(x_ref[...], g_ref[...]) * (1.0 + mod_ref[0, 1:2, :]) + mod_ref[0, 0:1, :]
        o_ref[...] = jnp.dot(h.astype(BF16), w_ref[...], preferred_element_type=F32).astype(BF16)

    pl.when(s < n_p)(lambda: body(xp_ref))
    pl.when(s >= n_p)(lambda: body(xs_ref))


def _in_proj(xp, xs, mod, pre_mix, w_in_bf16, groups):
    D = D_MODEL
    tm = TOKEN_TILE
    n_p, n_s = xp.shape[0] // tm, xs.shape[0] // tm
    return pl.pallas_call(
        functools.partial(_inproj_kernel, n_p=n_p),
        grid=(n_p + n_s,),
        in_specs=[pl.BlockSpec((tm, D), lambda s: (jnp.minimum(s, n_p - 1), 0)),
                  pl.BlockSpec((tm, D), lambda s: (jnp.maximum(s - n_p, 0), 0)),
                  pl.BlockSpec((1, 6, D), lambda s: (_batch_of_tile(s, groups, tm), 0, 0)),
                  pl.BlockSpec((1, D), lambda s: (0, 0)),
                  pl.BlockSpec((D, D_IN_PROJ), lambda s: (0, 0))],
        out_specs=pl.BlockSpec((tm, D_IN_PROJ), lambda s: (s, 0)),
        out_shape=jax.ShapeDtypeStruct(((n_p + n_s) * tm, D_IN_PROJ), BF16),
        compiler_params=_cparams("arbitrary"),
        name="in_proj",
    )(xp, xs, mod, pre_mix.reshape(1, D), w_in_bf16)


def _na_bias_table(rpb):
    qc = np.arange(GRID_W)[:, None]
    kc = np.arange(GRID_W)[None, :]
    win_start = np.clip(qc - NA_WIN_C // 2, 0, GRID_W - NA_WIN_C)
    valid = (kc >= win_start) & (kc < win_start + NA_WIN_C)
    dc = np.clip(kc - qc, -(NA_WIN_C - 1), NA_WIN_C - 1) + NA_WIN_C - 1
    pick = (dc[None] == np.arange(2 * NA_WIN_C - 1)[:, None, None]).astype(np.float32)
    full = jnp.einsum('hrc,cqk->hrqk', rpb.astype(F32), pick,
                      precision=lax.Precision.HIGHEST)
    full = jnp.where(valid[None, None], full.astype(F32), NEG_INF)
    two = jnp.concatenate([full[:, :-1], full[:, 1:]], axis=-1)
    two = two.reshape(N_HEADS_NA // 2, 2, 2 * NA_WIN_R - 2, GRID_W, 2 * GRID_W)
    return jnp.transpose(two, (0, 2, 1, 3, 4)).reshape(
        N_HEADS_NA // 2, 2 * NA_WIN_R - 2, 2 * GRID_W, 2 * GRID_W)


def _na_kernel(q_ref, kp_ref, kc_ref, kn_ref, vp_ref, vc_ref, vn_ref, bias_ref, gn_ref, o_ref,
               k_scr, v_scr, *, groups):
    s = pl.program_id(0)
    R = NA_ROWS_PER_STEP
    blk = R * GRID_W
    seq_start, seq_blocks = _seq_bounds(s, groups, blk)
    i = s - seq_start
    rows = seq_blocks * R
    k_scr[0:blk] = kp_ref[...]
    k_scr[blk:2 * blk] = kc_ref[...]
    k_scr[2 * blk:3 * blk] = kn_ref[...]
    v_scr[0:blk] = vp_ref[...]
    v_scr[blk:2 * blk] = vc_ref[...]
    v_scr[2 * blk:3 * blk] = vn_ref[...]
    lane = lax.broadcasted_iota(jnp.int32, (1, LANES), 1)
    lo = lane < HEAD_DIM
    gn = gn_ref[...]

    def one_row(j):
        r = i * R + j
        row_start = jnp.clip(r - NA_WIN_R // 2, 0, rows - NA_WIN_R)
        shift = r - row_start
        koff = pl.multiple_of((row_start - i * R + R) * GRID_W, GRID_W)
        qoff = pl.multiple_of(j * GRID_W, GRID_W)
        outs = []
        for p in range(N_HEADS_NA // 2):
            cs = slice(p * LANES, (p + 1) * LANES)
            q2 = q_ref[pl.ds(qoff, GRID_W), cs]
            zero = jnp.zeros_like(q2)
            qq = jnp.concatenate([jnp.where(lo, q2, zero), jnp.where(lo, zero, q2)], axis=0)
            k2 = k_scr[pl.ds(koff, NA_WIN_R * GRID_W), cs]
            v2 = v_scr[pl.ds(koff, NA_WIN_R * GRID_W), cs]
            sc = lax.dot_general(qq, k2, (((1,), (1,)), ((), ())),
                                 preferred_element_type=F32)
            parts = []
            for m in range(NA_WIN_R // 2):
                d = 2 * m - shift + (NA_WIN_R - 1)
                parts.append(sc[:, m * LANES:(m + 1) * LANES] + bias_ref[p, d])
            sc = jnp.concatenate(parts, axis=-1)
            mx = jnp.max(sc, axis=-1, keepdims=True)
            e = jnp.exp(sc - mx)
            l = jnp.sum(e, axis=-1, keepdims=True)
            o2 = jnp.dot(e.astype(BF16), v2, preferred_element_type=F32) / l
            outs.append(jnp.where(lo, o2[:GRID_W], o2[GRID_W:]))
        o = jnp.concatenate(outs, axis=-1)
        o_ref[pl.ds(qoff, GRID_W), :] = _rms(o, gn).astype(BF16)

    def rows_body(jj, carry):
        for u in range(NA_ROW_UNROLL):
            one_row(jj * NA_ROW_UNROLL + u)
        return carry

    lax.fori_loop(0, R // NA_ROW_UNROLL, rows_body, 0)


def _na_attention(proj, bias_tab, gn_na, groups):
    n_tok = proj.shape[0]
    blk = NA_ROWS_PER_STEP * GRID_W

    def kv(col, delta):
        def index_map(s):
            start, length = _seq_bounds(s, groups, blk)
            return (jnp.clip(s + delta, start, start + length - 1), col)
        return pl.BlockSpec((blk, D_NA), index_map)

    return pl.pallas_call(
        functools.partial(_na_kernel, groups=groups),
        grid=(n_tok // blk,),
        in_specs=[pl.BlockSpec((blk, D_NA), lambda s: (s, 0)),
                  kv(1, -1), kv(1, 0), kv(1, 1), kv(2, -1), kv(2, 0), kv(2, 1),
                  pl.BlockSpec(bias_tab.shape, lambda s: (0, 0, 0, 0)),
                  pl.BlockSpec((1, D_NA), lambda s: (0, 0))],
        out_specs=pl.BlockSpec((blk, D_NA), lambda s: (s, 0)),
        out_shape=jax.ShapeDtypeStruct((n_tok, D_NA), BF16),
        scratch_shapes=[pltpu.VMEM((3 * blk, D_NA), BF16), pltpu.VMEM((3 * blk, D_NA), BF16)],
        compiler_params=_cparams("arbitrary"),
        name="na_attn",
    )(proj, proj, proj, proj, proj, proj, proj, bias_tab, gn_na.reshape(1, D_NA))


def _wa_slab_order(a, axis):
    shape = a.shape
    split = shape[:axis] + (N_KV_WA, GQA_GROUP, HEAD_DIM) + shape[axis + 1:]
    return jnp.swapaxes(a.reshape(split), axis, axis + 1).reshape(shape)


def _t5_bucket(rel):
    half = T5_BUCKETS // 2
    ret = (rel > 0).astype(jnp.int32) * half
    n = jnp.abs(rel)
    nf = jnp.maximum(n, 1).astype(F32)
    large = T5_MAX_EXACT + (jnp.log(nf / T5_MAX_EXACT) / math.log(T5_MAX_DIST / T5_MAX_EXACT)
                            * (half - T5_MAX_EXACT)).astype(jnp.int32)
    large = jnp.minimum(large, half - 1)
    return ret + jnp.where(n < T5_MAX_EXACT, n, large)


def _wa_bias_table(t5_table):
    rel = jnp.arange(3 * WA_BLOCK)[None, :] - WA_BLOCK - jnp.arange(WA_BLOCK)[:, None]
    pick = (_t5_bucket(rel)[..., None] == jnp.arange(T5_BUCKETS)).astype(F32)
    bias = jnp.einsum('qkb,bh->qkh', pick, t5_table.astype(F32),
                      precision=lax.Precision.HIGHEST)
    bias = jnp.where((jnp.abs(rel) <= WA_WINDOW)[..., None], bias, NEG_INF)
    bias = jnp.transpose(bias, (2, 0, 1))
    return jnp.concatenate([bias[:GQA_GROUP], bias[GQA_GROUP:]], axis=1)


def _wa_kernel(sink_ref, q_ref, kp_ref, kc_ref, kn_ref, vp_ref, vc_ref, vn_ref, bias_ref, gn_ref,
               o_ref, *, groups):
    s = pl.program_id(0)
    W = WA_BLOCK
    seq_start, seq_blocks = _seq_bounds(s, groups, W)
    lane = lax.broadcasted_iota(jnp.int32, (1, LANES), 1)
    lo = lane < HEAD_DIM
    row = lax.broadcasted_iota(jnp.int32, (2 * W, 1), 0)
    kcat = jnp.concatenate([kp_ref[...], kc_ref[...], kn_ref[...]], axis=0)
    vcat = jnp.concatenate([vp_ref[...], vc_ref[...], vn_ref[...]], axis=0)
    pen_prev = jnp.where(s == seq_start, NEG_INF, 0.0).astype(F32)
    pen_next = jnp.where(s == seq_start + seq_blocks - 1, NEG_INF, 0.0).astype(F32)
    col = lax.broadcasted_iota(jnp.int32, (1, 3 * W), 1)
    pen = jnp.where(col < W, pen_prev, jnp.where(col >= 2 * W, pen_next, 0.0))
    outs = []
    for p in range(GQA_GROUP):
        q2 = q_ref[:, p * LANES:(p + 1) * LANES]
        zero = jnp.zeros_like(q2)
        qq = jnp.concatenate([jnp.where(lo, q2, zero), jnp.where(lo, zero, q2)], axis=0)
        sc = lax.dot_general(qq, kcat, (((1,), (1,)), ((), ())),
                             preferred_element_type=F32)
        sc = sc + bias_ref[p] + pen
        sink = jnp.where(row < W, sink_ref[p], sink_ref[p + GQA_GROUP])
        mx = jnp.maximum(jnp.max(sc, axis=-1, keepdims=True), sink)
        e = jnp.exp(sc - mx)
        l = jnp.sum(e, axis=-1, keepdims=True) + jnp.exp(sink - mx)
        o2 = jnp.dot(e.astype(BF16), vcat, preferred_element_type=F32) / l
        outs.append(jnp.where(lo, o2[:W], o2[W:]))
    o = jnp.concatenate(outs, axis=-1)
    o_ref[...] = _rms(o, gn_ref[...]).astype(BF16)


def _wa_attention(proj, bias_tab, sink, gn_wa_perm, groups):
    n_tok = proj.shape[0]
    W = WA_BLOCK
    q_col = 3 * D_NA // D_WA
    k_col = (3 * D_NA + D_WA) // D_KV_WA
    v_col = k_col + 1

    def kv(col, delta):
        def index_map(s):
            start, length = _seq_bounds(s, groups, W)
            return (jnp.clip(s + delta, start, start + length - 1), col)
        return pl.BlockSpec((W, D_KV_WA), index_map)

    return pl.pallas_call(
        functools.partial(_wa_kernel, groups=groups),
        grid=(n_tok // W,),
        in_specs=[pl.BlockSpec(memory_space=pltpu.SMEM),
                  pl.BlockSpec((W, D_WA), lambda s: (s, q_col)),
                  kv(k_col, -1), kv(k_col, 0), kv(k_col, 1),
                  kv(v_col, -1), kv(v_col, 0), kv(v_col, 1),
                  pl.BlockSpec(bias_tab.shape, lambda s: (0, 0, 0)),
                  pl.BlockSpec((1, D_WA), lambda s: (0, 0))],
        out_specs=pl.BlockSpec((W, D_WA), lambda s: (s, 0)),
        out_shape=jax.ShapeDtypeStruct((n_tok, D_WA), BF16),
        compiler_params=_cparams("arbitrary"),
        name="wa_attn",
    )(sink, proj, proj, proj, proj, proj, proj, proj, bias_tab, gn_wa_perm.reshape(1, D_WA))


def _postmix_kernel(na_ref, wa_ref, xp_ref, xs_ref, mod_ref, wo_ref, pm_ref, pf_ref, wr_ref, br_ref,
                    tri_ref, x1_ref, h2_ref, idx_ref, gate_ref, rank_ref, cnt_ref, carry, *, n_p):
    s = pl.program_id(0)

    @pl.when(s == 0)
    def _():
        carry[...] = jnp.zeros_like(carry)

    def body(x_ref):
        a = jnp.concatenate([na_ref[...], wa_ref[...]], axis=-1)
        mixed = jnp.dot(a, wo_ref[...], preferred_element_type=F32)
        x1 = x_ref[...] + mod_ref[0, 2:3, :] * _rms(mixed, pm_ref[...])
        x1_ref[...] = x1
        h2 = _rms(x1, pf_ref[...]) * (1.0 + mod_ref[0, 4:5, :]) + mod_ref[0, 3:4, :]
        for c in range(ROW_TILE):
            h2_ref[pl.ds(c, TOKEN_TILE, stride=ROW_TILE), :] = h2[:, c * LANES:(c + 1) * LANES]
        logits = lax.dot_general(wr_ref[...], h2, (((1,), (1,)), ((), ())),
                                 preferred_element_type=F32,
                                 precision=lax.Precision.HIGHEST) + br_ref[...]
        eidx = lax.broadcasted_iota(jnp.int32, logits.shape, 0)
        tops, sels = [], []
        l = logits
        for _ in range(TOP_K):
            m = jnp.max(l, axis=0, keepdims=True)
            sel = jnp.min(jnp.where(l == m, eidx, N_EXPERTS), axis=0, keepdims=True)
            tops.append(m)
            sels.append(sel)
            l = jnp.where(eidx == sel, -jnp.inf, l)
        es = [jnp.exp(t - tops[0]) for t in tops]
        tot = es[0] + es[1] + es[2] + es[3]
        idx_ref[0] = jnp.concatenate(sels, axis=0)
        gate_ref[0] = jnp.concatenate([e / tot for e in es], axis=0)
        hits = [eidx == sel for sel in sels]
        onehot = sum(h.astype(F32) for h in hits)
        prefix = jnp.dot(onehot.astype(BF16), tri_ref[...], preferred_element_type=F32)
        rank_e = carry[...] + prefix - onehot
        rank_ref[0] = jnp.concatenate(
            [jnp.sum(jnp.where(h, rank_e, 0.0), axis=0, keepdims=True) for h in hits],
            axis=0).astype(jnp.int32)
        carry[...] = carry[...] + jnp.sum(onehot, axis=1, keepdims=True)
        cnt_ref[...] = carry[...]

    pl.when(s < n_p)(lambda: body(xp_ref))
    pl.when(s >= n_p)(lambda: body(xs_ref))


def _post_mix(na, wa, xp, xs, mod, w_out_bf16, post_mix, pre_ffn, w_router_t, b_router, groups):
    D = D_MODEL
    tm = TOKEN_TILE
    n_p, n_s = xp.shape[0] // tm, xs.shape[0] // tm
    nt = n_p + n_s
    tri = (np.arange(tm)[:, None] <= np.arange(tm)[None, :]).astype(np.float32)
    tok = lambda s: (s, 0)
    const2 = lambda s: (0, 0)
    tile3 = lambda s: (s, 0, 0)
    return pl.pallas_call(
        functools.partial(_postmix_kernel, n_p=n_p),
        grid=(nt,),
        in_specs=[pl.BlockSpec((tm, D_NA), tok),
                  pl.BlockSpec((tm, D_WA), tok),
                  pl.BlockSpec((tm, D), lambda s: (jnp.minimum(s, n_p - 1), 0)),
                  pl.BlockSpec((tm, D), lambda s: (jnp.maximum(s - n_p, 0), 0)),
                  pl.BlockSpec((1, 6, D), lambda s: (_batch_of_tile(s, groups, tm), 0, 0)),
                  pl.BlockSpec((D, D), const2),
                  pl.BlockSpec((1, D), const2),
                  pl.BlockSpec((1, D), const2),
                  pl.BlockSpec((N_EXPERTS, D), const2),
                  pl.BlockSpec((N_EXPERTS, 1), const2),
                  pl.BlockSpec((tm, tm), const2)],
        out_specs=[pl.BlockSpec((tm, D), tok),
                   pl.BlockSpec((tm * ROW_TILE, LANES), tok),
                   pl.BlockSpec((1, TOP_K, tm), tile3),
                   pl.BlockSpec((1, TOP_K, tm), tile3),
                   pl.BlockSpec((1, TOP_K, tm), tile3),
                   pl.BlockSpec((N_EXPERTS, 1), const2)],
        out_shape=[jax.ShapeDtypeStruct((nt * tm, D), F32),
                   jax.ShapeDtypeStruct((nt * tm * ROW_TILE, LANES), F32),
                   jax.ShapeDtypeStruct((nt, TOP_K, tm), jnp.int32),
                   jax.ShapeDtypeStruct((nt, TOP_K, tm), F32),
                   jax.ShapeDtypeStruct((nt, TOP_K, tm), jnp.int32),
                   jax.ShapeDtypeStruct((N_EXPERTS, 1), F32)],
        scratch_shapes=[pltpu.VMEM((N_EXPERTS, 1), F32)],
        compiler_params=_cparams("arbitrary"),
        name="post_mix",
    )(na, wa, xp, xs, mod, w_out_bf16, post_mix.reshape(1, D), pre_ffn.reshape(1, D),
      w_router_t, b_router.reshape(N_EXPERTS, 1), jnp.asarray(tri, BF16))


def _expert_kernel(blk_e_ref, src_hbm, dst_hbm, h_hbm, wgu_ref, bgu_ref, wd_ref, bd_ref, y_hbm,
                   src_smem, dst_smem, xbuf, xb, act_bf, ob0, ob1, ob2, wgu_bf, wd_bf,
                   idx_sem, g_sem, s_sem, *, n_blocks):
    i = pl.program_id(0)
    BM = EXPERT_BLOCK
    RT = ROW_TILE
    last = n_blocks - 1
    obufs = (ob0, ob1, ob2)

    def idx_copies(blk, slot):
        return (pltpu.make_async_copy(src_hbm.at[blk], src_smem.at[slot], idx_sem.at[slot]),
                pltpu.make_async_copy(dst_hbm.at[blk], dst_smem.at[slot], idx_sem.at[slot]))

    def start_idx(blk, slot):
        for c in idx_copies(blk, slot):
            c.start()

    def wait_idx(blk, slot):
        for c in idx_copies(blk, slot):
            c.wait()

    def paced(row, prev, rounds):
        z = prev
        for _ in range(rounds):
            z = z + lax.shift_right_arithmetic(z, 31)
        return row + lax.shift_right_arithmetic(z, 31)

    def start_gather(islot, xslot, rounds):
        prev = src_smem[islot, 0, 0]
        for r in range(BM):
            prev = paced(src_smem[islot, 0, r], prev, rounds)
            row = pl.multiple_of(prev, RT)
            pltpu.make_async_copy(h_hbm.at[pl.ds(row, RT), :], xbuf.at[xslot, pl.ds(r * RT, RT), :],
                                  g_sem.at[xslot]).start()

    def wait_gather(xslot):
        pltpu.make_async_copy(h_hbm.at[pl.ds(0, BM * RT), :], xbuf.at[xslot], g_sem.at[xslot]).wait()

    def start_scatter(islot, k, rounds):
        prev = dst_smem[islot, 0, 0]
        for r in range(BM):
            prev = paced(dst_smem[islot, 0, r], prev, rounds)
            row = pl.multiple_of(prev, RT)
            pltpu.make_async_copy(obufs[k].at[pl.ds(r * RT, RT), :], y_hbm.at[pl.ds(row, RT), :],
                                  s_sem.at[k]).start(priority=r % 2)

    def wait_scatter(k):
        pltpu.make_async_copy(obufs[k], y_hbm.at[pl.ds(0, BM * RT), :], s_sem.at[k]).wait()

    xs = lax.rem(i, X_SLOTS)

    @pl.when(i == 0)
    def _():
        for b in range(min(3, n_blocks)):
            start_idx(b, b)
        for b in range(min(2, n_blocks)):
            wait_idx(b, b)
            start_gather(b, b, 0)

    @pl.when(i + 3 < n_blocks)
    def _():
        start_idx(i + 3, lax.rem(i + 3, IDX_SLOTS))

    @pl.when(i + 2 < n_blocks)
    def _():
        wait_idx(i + 2, lax.rem(i + 2, IDX_SLOTS))

    wait_gather(xs)

    new_expert = jnp.logical_or(i == 0, blk_e_ref[i] != blk_e_ref[jnp.maximum(i - 1, 0)])

    @pl.when(new_expert)
    def _():
        wgu_bf[...] = wgu_ref[0].astype(BF16)
        wd_bf[...] = wd_ref[0].astype(BF16)

    def up_phase():
        for c in range(RT):
            xb[:, c * LANES:(c + 1) * LANES] = xbuf[xs, pl.ds(c, BM, stride=RT), :].astype(BF16)
        start_gather(lax.rem(jnp.minimum(i + 2, last), IDX_SLOTS), lax.rem(i + 2, X_SLOTS),
                     GATHER_PACING)
        gu = jnp.dot(xb[...], wgu_bf[...], preferred_element_type=F32) + bgu_ref[0]
        glu = jnp.minimum(gu[:, :D_EXPERT], SWIGLU_LIMIT)
        lin = jnp.clip(gu[:, D_EXPERT:], -SWIGLU_LIMIT, SWIGLU_LIMIT)
        act = glu * (1.0 / (1.0 + jnp.exp(-SWIGLU_ALPHA * glu))) * (lin + 1.0)
        act_bf[...] = act.astype(BF16)

    def down_phase(k, scatter_prev):
        if scatter_prev:
            start_scatter(lax.rem(i - 1, IDX_SLOTS), (k - 1) % O_SLOTS, SCATTER_PACING)
        y = jnp.dot(act_bf[...], wd_bf[...], preferred_element_type=F32) + bd_ref[0]
        for c in range(RT):
            obufs[k][pl.ds(c, BM, stride=RT), :] = y[:, c * LANES:(c + 1) * LANES]

    pl.when(i >= 0)(up_phase)
    for k in range(O_SLOTS):
        @pl.when(jnp.logical_and(i >= O_SLOTS, lax.rem(i, O_SLOTS) == k))
        def _():
            wait_scatter(k)
    pl.when(i == 0)(lambda: down_phase(0, False))
    for k in range(O_SLOTS):
        pl.when(jnp.logical_and(i > 0, lax.rem(i, O_SLOTS) == k))(
            functools.partial(down_phase, k, True))

    @pl.when(i == last)
    def _():
        start_scatter(last % IDX_SLOTS, last % O_SLOTS, 0)
        for b in range(max(0, last - O_SLOTS + 1), last + 1):
            wait_scatter(b % O_SLOTS)
        wait_gather(lax.rem(i + 1, X_SLOTS))
        wait_gather(lax.rem(i + 2, X_SLOTS))


def _experts(h2, src_row, dst_row, blk_e, w_gate_up, b_gate_up, w_down, b_down):
    D = D_MODEL
    BM = EXPERT_BLOCK
    n_blocks = blk_e.shape[0]
    wmap = lambda i, be: (be[i], 0, 0)
    grid_spec = pltpu.PrefetchScalarGridSpec(
        num_scalar_prefetch=1,
        grid=(n_blocks,),
        in_specs=[pl.BlockSpec(memory_space=pl.ANY),
                  pl.BlockSpec(memory_space=pl.ANY),
                  pl.BlockSpec(memory_space=pl.ANY),
                  pl.BlockSpec((1, D, 2 * D_EXPERT), wmap),
                  pl.BlockSpec((1, 1, 2 * D_EXPERT), wmap),
                  pl.BlockSpec((1, D_EXPERT, D), wmap),
                  pl.BlockSpec((1, 1, D), wmap)],
        out_specs=pl.BlockSpec(memory_space=pl.ANY),
        scratch_shapes=[pltpu.SMEM((IDX_SLOTS, 1, BM), jnp.int32),
                        pltpu.SMEM((IDX_SLOTS, 1, BM), jnp.int32),
                        pltpu.VMEM((X_SLOTS, BM * ROW_TILE, LANES), F32),
                        pltpu.VMEM((BM, D), BF16),
                        pltpu.VMEM((BM, D_EXPERT), BF16),
                        pltpu.VMEM((BM * ROW_TILE, LANES), F32),
                        pltpu.VMEM((BM * ROW_TILE, LANES), F32),
                        pltpu.VMEM((BM * ROW_TILE, LANES), F32),
                        pltpu.VMEM((D, 2 * D_EXPERT), BF16),
                        pltpu.VMEM((D_EXPERT, D), BF16),
                        pltpu.SemaphoreType.DMA((IDX_SLOTS,)),
                        pltpu.SemaphoreType.DMA((X_SLOTS,)),
                        pltpu.SemaphoreType.DMA((O_SLOTS,))],
    )
    return pl.pallas_call(
        functools.partial(_expert_kernel, n_blocks=n_blocks),
        grid_spec=grid_spec,
        out_shape=jax.ShapeDtypeStruct((n_blocks * BM * ROW_TILE, LANES), F32),
        compiler_params=_cparams("arbitrary"),
        name="experts",
    )(blk_e, src_row.reshape(n_blocks, 1, BM), dst_row.reshape(n_blocks, 1, BM), h2,
      w_gate_up, b_gate_up.reshape(N_EXPERTS, 1, -1), w_down, b_down.reshape(N_EXPERTS, 1, -1))


def _combine_kernel(y0_ref, y1_ref, y2_ref, y3_ref, gate_ref, x1_ref, mod_ref, pf_ref,
                    op_ref, os_ref, *, n_p):
    s = pl.program_id(0)
    g = gate_ref[...]

    def rows(y_ref):
        return jnp.concatenate([y_ref[pl.ds(c, COMBINE_TILE, stride=ROW_TILE), :]
                                for c in range(ROW_TILE)], axis=-1)

    m = rows(y0_ref) * g[:, 0:1]
    for k, y_ref in enumerate((y1_ref, y2_ref, y3_ref), start=1):
        m = m + rows(y_ref) * g[:, k:k + 1]
    res = x1_ref[...] + mod_ref[0, 5:6, :] * _rms(m, pf_ref[...])

    @pl.when(s < n_p)
    def _():
        op_ref[...] = res

    @pl.when(s >= n_p)
    def _():
        os_ref[...] = res


def _combine(y4, gates_tok, x1, mod, post_ffn, n_tok_p, n_tok_s, groups):
    D = D_MODEL
    tc = COMBINE_TILE
    n_p, n_s = n_tok_p // tc, n_tok_s // tc
    nt = n_p + n_s

    def slot(k):
        return pl.BlockSpec((tc * ROW_TILE, LANES), lambda s: (k * nt + s, 0))

    return pl.pallas_call(
        functools.partial(_combine_kernel, n_p=n_p),
        grid=(nt,),
        in_specs=[slot(0), slot(1), slot(2), slot(3),
                  pl.BlockSpec((tc, TOP_K), lambda s: (s, 0)),
                  pl.BlockSpec((tc, D), lambda s: (s, 0)),
                  pl.BlockSpec((1, 6, D), lambda s: (_batch_of_tile(s, groups, tc), 0, 0)),
                  pl.BlockSpec((1, D), lambda s: (0, 0))],
        out_specs=[pl.BlockSpec((tc, D), lambda s: (jnp.minimum(s, n_p - 1), 0)),
                   pl.BlockSpec((tc, D), lambda s: (jnp.maximum(s - n_p, 0), 0))],
        out_shape=[jax.ShapeDtypeStruct((n_tok_p, D), F32),
                   jax.ShapeDtypeStruct((n_tok_s, D), F32)],
        compiler_params=_cparams("arbitrary"),
        name="combine",
    )(y4, y4, y4, y4, gates_tok, x1, mod, post_ffn.reshape(1, D))


def _routing_plan(idx_t, rank_t, counts, n_tok):
    BM = EXPERT_BLOCK
    nt, _, tm = idx_t.shape
    n_blocks = (n_tok * TOP_K) // BM + N_EXPERTS
    padded = ((counts + BM - 1) // BM) * BM
    pad_end = jnp.cumsum(padded)
    pad_start = pad_end - padded
    experts = jnp.arange(N_EXPERTS, dtype=jnp.int32)
    start_of = jnp.sum(jnp.where(idx_t[..., None] == experts, pad_start, 0), axis=-1)
    dest = (start_of + rank_t).astype(jnp.int32)
    t_of = (jnp.arange(nt, dtype=jnp.int32)[:, None, None] * tm
            + jnp.arange(tm, dtype=jnp.int32)[None, None, :])
    slot_row = jnp.arange(TOP_K, dtype=jnp.int32)[None, :, None] * n_tok + t_of
    blk_first = jnp.arange(n_blocks, dtype=jnp.int32) * BM
    blk_e = jnp.minimum(jnp.sum((pad_end[None, :] <= blk_first[:, None]).astype(jnp.int32), axis=1),
                        N_EXPERTS - 1)
    of_blk = blk_e[:, None] == experts
    per_blk = lambda v: jnp.sum(jnp.where(of_blk, v, 0), axis=1)[:, None]
    pos = blk_first[:, None] + jnp.arange(BM, dtype=jnp.int32)[None, :]
    real_before = per_blk(jnp.cumsum(counts) - counts) + jnp.minimum(pos - per_blk(pad_start),
                                                                     per_blk(counts))
    spare = n_tok * TOP_K + pos - real_before
    inv = spare.reshape(-1).astype(jnp.int32).at[dest.reshape(-1)].set(
        slot_row.reshape(-1), unique_indices=True, mode='promise_in_bounds')
    src_row = jnp.where(inv < n_tok * TOP_K, inv % n_tok, 0) * ROW_TILE
    dst_row = inv * ROW_TILE
    return src_row, dst_row, blk_e


def kernel(x_prompt, x_sample, c_prompt, c_sample, w_ada, b_ada, pre_mix, post_mix, pre_ffn, post_ffn,
           w_in, na_rel_bias, t5_rel_bias, wa_sink, gn_na, gn_wa, w_out, w_router, b_router,
           w_gate_up, b_gate_up, w_down, b_down):
    depth = w_ada.shape[0]
    D = D_MODEL
    groups = (x_prompt.shape[:2], x_sample.shape[:2])
    n_tok_p = x_prompt.shape[0] * x_prompt.shape[1]
    n_tok_s = x_sample.shape[0] * x_sample.shape[1]
    n_tok = n_tok_p + n_tok_s
    yp = x_prompt.reshape(n_tok_p, D)
    ys = x_sample.reshape(n_tok_s, D)
    c_all = jnp.concatenate([c_prompt, c_sample], axis=0)
    for l in range(depth):
        mod = _ada(c_all, w_ada[l], b_ada[l]).reshape(c_all.shape[0], 6, D)
        qb0 = 3 * D_NA
        scale = HEAD_DIM ** -0.5
        w_in_l = jnp.concatenate([w_in[l][:, :D_NA] * scale, w_in[l][:, D_NA:qb0],
                                  _wa_slab_order(w_in[l][:, qb0:qb0 + D_WA] * scale, axis=1),
                                  w_in[l][:, qb0 + D_WA:]], axis=1).astype(BF16)
        w_out_l = jnp.concatenate([w_out[l][:D_NA], _wa_slab_order(w_out[l][D_NA:], axis=0)],
                                  axis=0).astype(BF16)
        proj = _in_proj(yp, ys, mod, pre_mix[l], w_in_l, groups)
        na = _na_attention(proj, _na_bias_table(na_rel_bias[l]), gn_na[l], groups)
        wa = _wa_attention(proj, _wa_bias_table(t5_rel_bias), wa_sink[l],
                           _wa_slab_order(gn_wa[l], axis=0), groups)
        x1, h2, idx_t, gate_t, rank_t, counts = _post_mix(
            na, wa, yp, ys, mod, w_out_l, post_mix[l], pre_ffn[l], w_router[l].T, b_router[l], groups)
        src_row, dst_row, blk_e = _routing_plan(
            idx_t, rank_t, counts.reshape(N_EXPERTS).astype(jnp.int32), n_tok)
        y4 = _experts(h2, src_row, dst_row, blk_e, w_gate_up[l], b_gate_up[l], w_down[l], b_down[l])
        gates_tok = jnp.transpose(gate_t, (0, 2, 1)).reshape(n_tok, TOP_K)
        yp, ys = _combine(y4, gates_tok, x1, mod, post_ffn[l], n_tok_p, n_tok_s, groups)
    return (yp.reshape(x_prompt.shape), ys.reshape(x_sample.shape))
```

```python
import functools
import math

import jax
import jax.numpy as jnp
import numpy as np
from jax import lax
from jax.experimental import pallas as pl
from jax.experimental.pallas import tpu as pltpu

F32 = jnp.float32
BF16 = jnp.bfloat16

D_MODEL = 1024
HEAD_DIM = 64
D_NA = 512
D_WA = 512
N_HEADS_NA = 8
N_HEADS_WA = 8
N_KV_WA = 2
GQA_GROUP = 4
D_KV_WA = 128
D_IN_PROJ = 3 * D_NA + D_WA + 2 * D_KV_WA
GRID_W = 64
NA_WIN_R = 8
NA_WIN_C = 16
WA_WINDOW = 128
WA_BLOCK = 128
T5_BUCKETS = 32
T5_MAX_EXACT = 8
T5_MAX_DIST = 128
N_EXPERTS = 32
TOP_K = 4
D_EXPERT = D_MODEL
SWIGLU_LIMIT = 7.0
SWIGLU_ALPHA = 1.702
RMS_EPS = 1e-6
NEG_INF = -1e30

VMEM_LIMIT_BYTES = 56 * 1024 * 1024
LANES = 128
ROW_TILE = D_MODEL // LANES

TOKEN_TILE = 512
NA_ROWS_PER_STEP = 8
NA_ROW_UNROLL = 2
EXPERT_BLOCK = 256
X_SLOTS = 3
IDX_SLOTS = X_SLOTS + 2
O_SLOTS = 3
GATHER_PACING = 3
SCATTER_PACING = 0
COMBINE_TILE = 256


def _cparams(*sem):
    return pltpu.CompilerParams(dimension_semantics=sem, vmem_limit_bytes=VMEM_LIMIT_BYTES)


def _rms(x, g):
    return x * lax.rsqrt(jnp.mean(x * x, axis=-1, keepdims=True) + RMS_EPS) * g


def _seq_bounds(s, groups, block):
    base = 0
    start, length = None, None
    for n_seq, seq_len in groups:
        per = seq_len // block
        g_start = base + ((s - base) // per) * per
        if start is None:
            start, length = g_start, per
        else:
            inside = s >= base
            start = jnp.where(inside, g_start, start)
            length = jnp.where(inside, per, length)
        base += n_seq * per
    return start, length


def _batch_of_tile(s, groups, block):
    base_blk, base_seq = 0, 0
    out = None
    for n_seq, seq_len in groups:
        per = seq_len // block
        b = base_seq + (s - base_blk) // per
        out = b if out is None else jnp.where(s >= base_blk, b, out)
        base_blk += n_seq * per
        base_seq += n_seq
    return out


def _ada_kernel(c_ref, w_ref, b_ref, o_ref):
    c = c_ref[...]
    s = c * (1.0 / (1.0 + jnp.exp(-c)))
    o_ref[...] = jnp.dot(s, w_ref[...], preferred_element_type=F32,
                         precision=lax.Precision.HIGHEST) + b_ref[...]


def _ada(c_all, w_ada, b_ada):
    nb = c_all.shape[0]
    return pl.pallas_call(
        _ada_kernel,
        grid=(6,),
        in_specs=[pl.BlockSpec((nb, D_MODEL), lambda j: (0, 0)),
                  pl.BlockSpec((D_MODEL, D_MODEL), lambda j: (0, j)),
                  pl.BlockSpec((1, D_MODEL), lambda j: (0, j))],
        out_specs=pl.BlockSpec((nb, D_MODEL), lambda j: (0, j)),
        out_shape=jax.ShapeDtypeStruct((nb, 6 * D_MODEL), F32),
        compiler_params=_cparams("arbitrary"),
        name="ada",
    )(c_all, w_ada, b_ada.reshape(1, -1))


def _inproj_kernel(xp_ref, xs_ref, mod_ref, g_ref, w_ref, o_ref, *, n_p):
    s = pl.program_id(0)

    def body(x_ref):
        h = _rms(x_ref[...], g_ref[...]) * (1.0 + mod_ref[0, 1:2, :]) + mod_ref[0, 0:1, :]
        o_ref[...] = jnp.dot(h.astype(BF16), w_ref[...], preferred_element_type=F32).astype(BF16)

    pl.when(s < n_p)(lambda: body(xp_ref))
    pl.when(s >= n_p)(lambda: body(xs_ref))


def _in_proj(xp, xs, mod, pre_mix, w_in_bf16, groups):
    D = D_MODEL
    tm = TOKEN_TILE
    n_p, n_s = xp.shape[0] // tm, xs.shape[0] // tm
    return pl.pallas_call(
        functools.partial(_inproj_kernel, n_p=n_p),
        grid=(n_p + n_s,),
        in_specs=[pl.BlockSpec((tm, D), lambda s: (jnp.minimum(s, n_p - 1), 0)),
                  pl.BlockSpec((tm, D), lambda s: (jnp.maximum(s - n_p, 0), 0)),
                  pl.BlockSpec((1, 6, D), lambda s: (_batch_of_tile(s, groups, tm), 0, 0)),
                  pl.BlockSpec((1, D), lambda s: (0, 0)),
                  pl.BlockSpec((D, D_IN_PROJ), lambda s: (0, 0))],
        out_specs=pl.BlockSpec((tm, D_IN_PROJ), lambda s: (s, 0)),
        out_shape=jax.ShapeDtypeStruct(((n_p + n_s) * tm, D_IN_PROJ), BF16),
        compiler_params=_cparams("arbitrary"),
        name="in_proj",
    )(xp, xs, mod, pre_mix.reshape(1, D), w_in_bf16)


def _na_bias_table(rpb):
    qc = np.arange(GRID_W)[:, None]
    kc = np.arange(GRID_W)[None, :]
    win_start = np.clip(qc - NA_WIN_C // 2, 0, GRID_W - NA_WIN_C)
    valid = (kc >= win_start) & (kc < win_start + NA_WIN_C)
    dc = np.clip(kc - qc, -(NA_WIN_C - 1), NA_WIN_C - 1) + NA_WIN_C - 1
    pick = (dc[None] == np.arange(2 * NA_WIN_C - 1)[:, None, None]).astype(np.float32)
    full = jnp.einsum('hrc,cqk->hrqk', rpb.astype(F32), pick,
                      precision=lax.Precision.HIGHEST)
    full = jnp.where(valid[None, None], full.astype(F32), NEG_INF)
    two = jnp.concatenate([full[:, :-1], full[:, 1:]], axis=-1)
    two = two.reshape(N_HEADS_NA // 2, 2, 2 * NA_WIN_R - 2, GRID_W, 2 * GRID_W)
    return jnp.transpose(two, (0, 2, 1, 3, 4)).reshape(
        N_HEADS_NA // 2, 2 * NA_WIN_R - 2, 2 * GRID_W, 2 * GRID_W)


def _na_kernel(q_ref, kp_ref, kc_ref, kn_ref, vp_ref, vc_ref, vn_ref, bias_ref, gn_ref, o_ref,
               k_scr, v_scr, *, groups):
    s = pl.program_id(0)
    R = NA_ROWS_PER_STEP
    blk = R * GRID_W
    seq_start, seq_blocks = _seq_bounds(s, groups, blk)
    i = s - seq_start
    rows = seq_blocks * R
    k_scr[0:blk] = kp_ref[...]
    k_scr[blk:2 * blk] = kc_ref[...]
    k_scr[2 * blk:3 * blk] = kn_ref[...]
    v_scr[0:blk] = vp_ref[...]
    v_scr[blk:2 * blk] = vc_ref[...]
    v_scr[2 * blk:3 * blk] = vn_ref[...]
    lane = lax.broadcasted_iota(jnp.int32, (1, LANES), 1)
    lo = lane < HEAD_DIM
    gn = gn_ref[...]

    def one_row(j):
        r = i * R + j
        row_start = jnp.clip(r - NA_WIN_R // 2, 0, rows - NA_WIN_R)
        shift = r - row_start
        koff = pl.multiple_of((row_start - i * R + R) * GRID_W, GRID_W)
        qoff = pl.multiple_of(j * GRID_W, GRID_W)
        outs = []
        for p in range(N_HEADS_NA // 2):
            cs = slice(p * LANES, (p + 1) * LANES)
            q2 = q_ref[pl.ds(qoff, GRID_W), cs]
            zero = jnp.zeros_like(q2)
            qq = jnp.concatenate([jnp.where(lo, q2, zero), jnp.where(lo, zero, q2)], axis=0)
            k2 = k_scr[pl.ds(koff, NA_WIN_R * GRID_W), cs]
            v2 = v_scr[pl.ds(koff, NA_WIN_R * GRID_W), cs]
            sc = lax.dot_general(qq, k2, (((1,), (1,)), ((), ())),
                                 preferred_element_type=F32)
            parts = []
            for m in range(NA_WIN_R // 2):
                d = 2 * m - shift + (NA_WIN_R - 1)
                parts.append(sc[:, m * LANES:(m + 1) * LANES] + bias_ref[p, d])
            sc = jnp.concatenate(parts, axis=-1)
            mx = jnp.max(sc, axis=-1, keepdims=True)
            e = jnp.exp(sc - mx)
            l = jnp.sum(e, axis=-1, keepdims=True)
            o2 = jnp.dot(e.astype(BF16), v2, preferred_element_type=F32) / l
            outs.append(jnp.where(lo, o2[:GRID_W], o2[GRID_W:]))
        o = jnp.concatenate(outs, axis=-1)
        o_ref[pl.ds(qoff, GRID_W), :] = _rms(o, gn).astype(BF16)

    def rows_body(jj, carry):
        for u in range(NA_ROW_UNROLL):
            one_row(jj * NA_ROW_UNROLL + u)
        return carry

    lax.fori_loop(0, R // NA_ROW_UNROLL, rows_body, 0)


def _na_attention(proj, bias_tab, gn_na, groups):
    n_tok = proj.shape[0]
    blk = NA_ROWS_PER_STEP * GRID_W

    def kv(col, delta):
        def index_map(s):
            start, length = _seq_bounds(s, groups, blk)
            return (jnp.clip(s + delta, start, start + length - 1), col)
        return pl.BlockSpec((blk, D_NA), index_map)

    return pl.pallas_call(
        functools.partial(_na_kernel, groups=groups),
        grid=(n_tok // blk,),
        in_specs=[pl.BlockSpec((blk, D_NA), lambda s: (s, 0)),
                  kv(1, -1), kv(1, 0), kv(1, 1), kv(2, -1), kv(2, 0), kv(2, 1),
                  pl.BlockSpec(bias_tab.shape, lambda s: (0, 0, 0, 0)),
                  pl.BlockSpec((1, D_NA), lambda s: (0, 0))],
        out_specs=pl.BlockSpec((blk, D_NA), lambda s: (s, 0)),
        out_shape=jax.ShapeDtypeStruct((n_tok, D_NA), BF16),
        scratch_shapes=[pltpu.VMEM((3 * blk, D_NA), BF16), pltpu.VMEM((3 * blk, D_NA), BF16)],
        compiler_params=_cparams("arbitrary"),
        name="na_attn",
    )(proj, proj, proj, proj, proj, proj, proj, bias_tab, gn_na.reshape(1, D_NA))


def _wa_slab_order(a, axis):
    shape = a.shape
    split = shape[:axis] + (N_KV_WA, GQA_GROUP, HEAD_DIM) + shape[axis + 1:]
    return jnp.swapaxes(a.reshape(split), axis, axis + 1).reshape(shape)


def _t5_bucket(rel):
    half = T5_BUCKETS // 2
    ret = (rel > 0).astype(jnp.int32) * half
    n = jnp.abs(rel)
    nf = jnp.maximum(n, 1).astype(F32)
    large = T5_MAX_EXACT + (jnp.log(nf / T5_MAX_EXACT) / math.log(T5_MAX_DIST / T5_MAX_EXACT)
                            * (half - T5_MAX_EXACT)).astype(jnp.int32)
    large = jnp.minimum(large, half - 1)
    return ret + jnp.where(n < T5_MAX_EXACT, n, large)


def _wa_bias_table(t5_table):
    rel = jnp.arange(3 * WA_BLOCK)[None, :] - WA_BLOCK - jnp.arange(WA_BLOCK)[:, None]
    pick = (_t5_bucket(rel)[..., None] == jnp.arange(T5_BUCKETS)).astype(F32)
    bias = jnp.einsum('qkb,bh->qkh', pick, t5_table.astype(F32),
                      precision=lax.Precision.HIGHEST)
    bias = jnp.where((jnp.abs(rel) <= WA_WINDOW)[..., None], bias, NEG_INF)
    bias = jnp.transpose(bias, (2, 0, 1))
    return jnp.concatenate([bias[:GQA_GROUP], bias[GQA_GROUP:]], axis=1)


def _wa_kernel(sink_ref, q_ref, kp_ref, kc_ref, kn_ref, vp_ref, vc_ref, vn_ref, bias_ref, gn_ref,
               o_ref, *, groups):
    s = pl.program_id(0)
    W = WA_BLOCK
    seq_start, seq_blocks = _seq_bounds(s, groups, W)
    lane = lax.broadcasted_iota(jnp.int32, (1, LANES), 1)
    lo = lane < HEAD_DIM
    row = lax.broadcasted_iota(jnp.int32, (2 * W, 1), 0)
    kcat = jnp.concatenate([kp_ref[...], kc_ref[...], kn_ref[...]], axis=0)
    vcat = jnp.concatenate([vp_ref[...], vc_ref[...], vn_ref[...]], axis=0)
    pen_prev = jnp.where(s == seq_start, NEG_INF, 0.0).astype(F32)
    pen_next = jnp.where(s == seq_start + seq_blocks - 1, NEG_INF, 0.0).astype(F32)
    col = lax.broadcasted_iota(jnp.int32, (1, 3 * W), 1)
    pen = jnp.where(col < W, pen_prev, jnp.where(col >= 2 * W, pen_next, 0.0))
    outs = []
    for p in range(GQA_GROUP):
        q2 = q_ref[:, p * LANES:(p + 1) * LANES]
        zero = jnp.zeros_like(q2)
        qq = jnp.concatenate([jnp.where(lo, q2, zero), jnp.where(lo, zero, q2)], axis=0)
        sc = lax.dot_general(qq, kcat, (((1,), (1,)), ((), ())),
                             preferred_element_type=F32)
        sc = sc + bias_ref[p] + pen
        sink = jnp.where(row < W, sink_ref[p], sink_ref[p + GQA_GROUP])
        mx = jnp.maximum(jnp.max(sc, axis=-1, keepdims=True), sink)
        e = jnp.exp(sc - mx)
        l = jnp.sum(e, axis=-1, keepdims=True) + jnp.exp(sink - mx)
        o2 = jnp.dot(e.astype(BF16), vcat, preferred_element_type=F32) / l
        outs.append(jnp.where(lo, o2[:W], o2[W:]))
    o = jnp.concatenate(outs, axis=-1)
    o_ref[...] = _rms(o, gn_ref[...]).astype(BF16)


def _wa_attention(proj, bias_tab, sink, gn_wa_perm, groups):
    n_tok = proj.shape[0]
    W = WA_BLOCK
    q_col = 3 * D_NA // D_WA
    k_col = (3 * D_NA + D_WA) // D_KV_WA
    v_col = k_col + 1

    def kv(col, delta):
        def index_map(s):
            start, length = _seq_bounds(s, groups, W)
            return (jnp.clip(s + delta, start, start + length - 1), col)
        return pl.BlockSpec((W, D_KV_WA), index_map)

    return pl.pallas_call(
        functools.partial(_wa_kernel, groups=groups),
        grid=(n_tok // W,),
        in_specs=[pl.BlockSpec(memory_space=pltpu.SMEM),
                  pl.BlockSpec((W, D_WA), lambda s: (s, q_col)),
                  kv(k_col, -1), kv(k_col, 0), kv(k_col, 1),
                  kv(v_col, -1), kv(v_col, 0), kv(v_col, 1),
                  pl.BlockSpec(bias_tab.shape, lambda s: (0, 0, 0)),
                  pl.BlockSpec((1, D_WA), lambda s: (0, 0))],
        out_specs=pl.BlockSpec((W, D_WA), lambda s: (s, 0)),
        out_shape=jax.ShapeDtypeStruct((n_tok, D_WA), BF16),
        compiler_params=_cparams("arbitrary"),
        name="wa_attn",
    )(sink, proj, proj, proj, proj, proj, proj, proj, bias_tab, gn_wa_perm.reshape(1, D_WA))


def _postmix_kernel(na_ref, wa_ref, xp_ref, xs_ref, mod_ref, wo_ref, pm_ref, pf_ref, wr_ref, br_ref,
                    tri_ref, x1_ref, h2_ref, idx_ref, gate_ref, rank_ref, cnt_ref, carry, *, n_p):
    s = pl.program_id(0)

    @pl.when(s == 0)
    def _():
        carry[...] = jnp.zeros_like(carry)

    def body(x_ref):
        a = jnp.concatenate([na_ref[...], wa_ref[...]], axis=-1)
        mixed = jnp.dot(a, wo_ref[...], preferred_element_type=F32)
        x1 = x_ref[...] + mod_ref[0, 2:3, :] * _rms(mixed, pm_ref[...])
        x1_ref[...] = x1
        h2 = _rms(x1, pf_ref[...]) * (1.0 + mod_ref[0, 4:5, :]) + mod_ref[0, 3:4, :]
        for c in range(ROW_TILE):
            h2_ref[pl.ds(c, TOKEN_TILE, stride=ROW_TILE), :] = h2[:, c * LANES:(c + 1) * LANES]
        logits = lax.dot_general(wr_ref[...], h2, (((1,), (1,)), ((), ())),
                                 preferred_element_type=F32,
                                 precision=lax.Precision.HIGHEST) + br_ref[...]
        eidx = lax.broadcasted_iota(jnp.int32, logits.shape, 0)
        tops, sels = [], []
        l = logits
        for _ in range(TOP_K):
            m = jnp.max(l, axis=0, keepdims=True)
            sel = jnp.min(jnp.where(l == m, eidx, N_EXPERTS), axis=0, keepdims=True)
            tops.append(m)
            sels.append(sel)
            l = jnp.where(eidx == sel, -jnp.inf, l)
        es = [jnp.exp(t - tops[0]) for t in tops]
        tot = es[0] + es[1] + es[2] + es[3]
        idx_ref[0] = jnp.concatenate(sels, axis=0)
        gate_ref[0] = jnp.concatenate([e / tot for e in es], axis=0)
        hits = [eidx == sel for sel in sels]
        onehot = sum(h.astype(F32) for h in hits)
        prefix = jnp.dot(onehot.astype(BF16), tri_ref[...], preferred_element_type=F32)
        rank_e = carry[...] + prefix - onehot
        rank_ref[0] = jnp.concatenate(
            [jnp.sum(jnp.where(h, rank_e, 0.0), axis=0, keepdims=True) for h in hits],
            axis=0).astype(jnp.int32)
        carry[...] = carry[...] + jnp.sum(onehot, axis=1, keepdims=True)
        cnt_ref[...] = carry[...]

    pl.when(s < n_p)(lambda: body(xp_ref))
    pl.when(s >= n_p)(lambda: body(xs_ref))


def _post_mix(na, wa, xp, xs, mod, w_out_bf16, post_mix, pre_ffn, w_router_t, b_router, groups):
    D = D_MODEL
    tm = TOKEN_TILE
    n_p, n_s = xp.shape[0] // tm, xs.shape[0] // tm
    nt = n_p + n_s
    tri = (np.arange(tm)[:, None] <= np.arange(tm)[None, :]).astype(np.float32)
    tok = lambda s: (s, 0)
    const2 = lambda s: (0, 0)
    tile3 = lambda s: (s, 0, 0)
    return pl.pallas_call(
        functools.partial(_postmix_kernel, n_p=n_p),
        grid=(nt,),
        in_specs=[pl.BlockSpec((tm, D_NA), tok),
                  pl.BlockSpec((tm, D_WA), tok),
                  pl.BlockSpec((tm, D), lambda s: (jnp.minimum(s, n_p - 1), 0)),
                  pl.BlockSpec((tm, D), lambda s: (jnp.maximum(s - n_p, 0), 0)),
                  pl.BlockSpec((1, 6, D), lambda s: (_batch_of_tile(s, groups, tm), 0, 0)),
                  pl.BlockSpec((D, D), const2),
                  pl.BlockSpec((1, D), const2),
                  pl.BlockSpec((1, D), const2),
                  pl.BlockSpec((N_EXPERTS, D), const2),
                  pl.BlockSpec((N_EXPERTS, 1), const2),
                  pl.BlockSpec((tm, tm), const2)],
        out_specs=[pl.BlockSpec((tm, D), tok),
                   pl.BlockSpec((tm * ROW_TILE, LANES), tok),
                   pl.BlockSpec((1, TOP_K, tm), tile3),
                   pl.BlockSpec((1, TOP_K, tm), tile3),
                   pl.BlockSpec((1, TOP_K, tm), tile3),
                   pl.BlockSpec((N_EXPERTS, 1), const2)],
        out_shape=[jax.ShapeDtypeStruct((nt * tm, D), F32),
                   jax.ShapeDtypeStruct((nt * tm * ROW_TILE, LANES), F32),
                   jax.ShapeDtypeStruct((nt, TOP_K, tm), jnp.int32),
                   jax.ShapeDtypeStruct((nt, TOP_K, tm), F32),
                   jax.ShapeDtypeStruct((nt, TOP_K, tm), jnp.int32),
                   jax.ShapeDtypeStruct((N_EXPERTS, 1), F32)],
        scratch_shapes=[pltpu.VMEM((N_EXPERTS, 1), F32)],
        compiler_params=_cparams("arbitrary"),
        name="post_mix",
    )(na, wa, xp, xs, mod, w_out_bf16, post_mix.reshape(1, D), pre_ffn.reshape(1, D),
      w_router_t, b_router.reshape(N_EXPERTS, 1), jnp.asarray(tri, BF16))


def _expert_kernel(blk_e_ref, src_hbm, dst_hbm, h_hbm, wgu_ref, bgu_ref, wd_ref, bd_ref, y_hbm,
                   src_smem, dst_smem, xbuf, xb, act_bf, ob0, ob1, ob2, wgu_bf, wd_bf,
                   idx_sem, g_sem, s_sem, *, n_blocks):
    i = pl.program_id(0)
    BM = EXPERT_BLOCK
    RT = ROW_TILE
    last = n_blocks - 1
    obufs = (ob0, ob1, ob2)

    def idx_copies(blk, slot):
        return (pltpu.make_async_copy(src_hbm.at[blk], src_smem.at[slot], idx_sem.at[slot]),
                pltpu.make_async_copy(dst_hbm.at[blk], dst_smem.at[slot], idx_sem.at[slot]))

    def start_idx(blk, slot):
        for c in idx_copies(blk, slot):
            c.start()

    def wait_idx(blk, slot):
        for c in idx_copies(blk, slot):
            c.wait()

    def paced(row, prev, rounds):
        z = prev
        for _ in range(rounds):
            z = z + lax.shift_right_arithmetic(z, 31)
        return row + lax.shift_right_arithmetic(z, 31)

    def start_gather(islot, xslot, rounds):
        prev = src_smem[islot, 0, 0]
        for r in range(BM):
            prev = paced(src_smem[islot, 0, r], prev, rounds)
            row = pl.multiple_of(prev, RT)
            pltpu.make_async_copy(h_hbm.at[pl.ds(row, RT), :], xbuf.at[xslot, pl.ds(r * RT, RT), :],
                                  g_sem.at[xslot]).start()

    def wait_gather(xslot):
        pltpu.make_async_copy(h_hbm.at[pl.ds(0, BM * RT), :], xbuf.at[xslot], g_sem.at[xslot]).wait()

    def start_scatter(islot, k, rounds):
        prev = dst_smem[islot, 0, 0]
        for r in range(BM):
            prev = paced(dst_smem[islot, 0, r], prev, rounds)
            row = pl.multiple_of(prev, RT)
            pltpu.make_async_copy(obufs[k].at[pl.ds(r * RT, RT), :], y_hbm.at[pl.ds(row, RT), :],
                                  s_sem.at[k]).start(priority=r % 2)

    def wait_scatter(k):
        pltpu.make_async_copy(obufs[k], y_hbm.at[pl.ds(0, BM * RT), :], s_sem.at[k]).wait()

    xs = lax.rem(i, X_SLOTS)

    @pl.when(i == 0)
    def _():
        for b in range(min(3, n_blocks)):
            start_idx(b, b)
        for b in range(min(2, n_blocks)):
            wait_idx(b, b)
            start_gather(b, b, 0)

    @pl.when(i + 3 < n_blocks)
    def _():
        start_idx(i + 3, lax.rem(i + 3, IDX_SLOTS))

    @pl.when(i + 2 < n_blocks)
    def _():
        wait_idx(i + 2, lax.rem(i + 2, IDX_SLOTS))

    wait_gather(xs)

    new_expert = jnp.logical_or(i == 0, blk_e_ref[i] != blk_e_ref[jnp.maximum(i - 1, 0)])

    @pl.when(new_expert)
    def _():
        wgu_bf[...] = wgu_ref[0].astype(BF16)
        wd_bf[...] = wd_ref[0].astype(BF16)

    def up_phase():
        for c in range(RT):
            xb[:, c * LANES:(c + 1) * LANES] = xbuf[xs, pl.ds(c, BM, stride=RT), :].astype(BF16)
        start_gather(lax.rem(jnp.minimum(i + 2, last), IDX_SLOTS), lax.rem(i + 2, X_SLOTS),
                     GATHER_PACING)
        gu = jnp.dot(xb[...], wgu_bf[...], preferred_element_type=F32) + bgu_ref[0]
        glu = jnp.minimum(gu[:, :D_EXPERT], SWIGLU_LIMIT)
        lin = jnp.clip(gu[:, D_EXPERT:], -SWIGLU_LIMIT, SWIGLU_LIMIT)
        act = glu * (1.0 / (1.0 + jnp.exp(-SWIGLU_ALPHA * glu))) * (lin + 1.0)
        act_bf[...] = act.astype(BF16)

    def down_phase(k, scatter_prev):
        if scatter_prev:
            start_scatter(lax.rem(i - 1, IDX_SLOTS), (k - 1) % O_SLOTS, SCATTER_PACING)
        y = jnp.dot(act_bf[...], wd_bf[...], preferred_element_type=F32) + bd_ref[0]
        for c in range(RT):
            obufs[k][pl.ds(c, BM, stride=RT), :] = y[:, c * LANES:(c + 1) * LANES]

    pl.when(i >= 0)(up_phase)
    for k in range(O_SLOTS):
        @pl.when(jnp.logical_and(i >= O_SLOTS, lax.rem(i, O_SLOTS) == k))
        def _():
            wait_scatter(k)
    pl.when(i == 0)(lambda: down_phase(0, False))
    for k in range(O_SLOTS):
        pl.when(jnp.logical_and(i > 0, lax.rem(i, O_SLOTS) == k))(
            functools.partial(down_phase, k, True))

    @pl.when(i == last)
    def _():
        start_scatter(last % IDX_SLOTS, last % O_SLOTS, 0)
        for b in range(max(0, last - O_SLOTS + 1), last + 1):
            wait_scatter(b % O_SLOTS)
        wait_gather(lax.rem(i + 1, X_SLOTS))
        wait_gather(lax.rem(i + 2, X_SLOTS))


def _experts(h2, src_row, dst_row, blk_e, w_gate_up, b_gate_up, w_down, b_down):
    D = D_MODEL
    BM = EXPERT_BLOCK
    n_blocks = blk_e.shape[0]
    wmap = lambda i, be: (be[i], 0, 0)
    grid_spec = pltpu.PrefetchScalarGridSpec(
        num_scalar_prefetch=1,
        grid=(n_blocks,),
        in_specs=[pl.BlockSpec(memory_space=pl.ANY),
                  pl.BlockSpec(memory_space=pl.ANY),
                  pl.BlockSpec(memory_space=pl.ANY),
                  pl.BlockSpec((1, D, 2 * D_EXPERT), wmap),
                  pl.BlockSpec((1, 1, 2 * D_EXPERT), wmap),
                  pl.BlockSpec((1, D_EXPERT, D), wmap),
                  pl.BlockSpec((1, 1, D), wmap)],
        out_specs=pl.BlockSpec(memory_space=pl.ANY),
        scratch_shapes=[pltpu.SMEM((IDX_SLOTS, 1, BM), jnp.int32),
                        pltpu.SMEM((IDX_SLOTS, 1, BM), jnp.int32),
                        pltpu.VMEM((X_SLOTS, BM * ROW_TILE, LANES), F32),
                        pltpu.VMEM((BM, D), BF16),
                        pltpu.VMEM((BM, D_EXPERT), BF16),
                        pltpu.VMEM((BM * ROW_TILE, LANES), F32),
                        pltpu.VMEM((BM * ROW_TILE, LANES), F32),
                        pltpu.VMEM((BM * ROW_TILE, LANES), F32),
                        pltpu.VMEM((D, 2 * D_EXPERT), BF16),
                        pltpu.VMEM((D_EXPERT, D), BF16),
                        pltpu.SemaphoreType.DMA((IDX_SLOTS,)),
                        pltpu.SemaphoreType.DMA((X_SLOTS,)),
                        pltpu.SemaphoreType.DMA((O_SLOTS,))],
    )
    return pl.pallas_call(
        functools.partial(_expert_kernel, n_blocks=n_blocks),
        grid_spec=grid_spec,
        out_shape=jax.ShapeDtypeStruct((n_blocks * BM * ROW_TILE, LANES), F32),
        compiler_params=_cparams("arbitrary"),
        name="experts",
    )(blk_e, src_row.reshape(n_blocks, 1, BM), dst_row.reshape(n_blocks, 1, BM), h2,
      w_gate_up, b_gate_up.reshape(N_EXPERTS, 1, -1), w_down, b_down.reshape(N_EXPERTS, 1, -1))


def _combine_kernel(y0_ref, y1_ref, y2_ref, y3_ref, gate_ref, x1_ref, mod_ref, pf_ref,
                    op_ref, os_ref, *, n_p):
    s = pl.program_id(0)
    g = gate_ref[...]

    def rows(y_ref):
        return jnp.concatenate([y_ref[pl.ds(c, COMBINE_TILE, stride=ROW_TILE), :]
                                for c in range(ROW_TILE)], axis=-1)

    m = rows(y0_ref) * g[:, 0:1]
    for k, y_ref in enumerate((y1_ref, y2_ref, y3_ref), start=1):
        m = m + rows(y_ref) * g[:, k:k + 1]
    res = x1_ref[...] + mod_ref[0, 5:6, :] * _rms(m, pf_ref[...])

    @pl.when(s < n_p)
    def _():
        op_ref[...] = res

    @pl.when(s >= n_p)
    def _():
        os_ref[...] = res


def _combine(y4, gates_tok, x1, mod, post_ffn, n_tok_p, n_tok_s, groups):
    D = D_MODEL
    tc = COMBINE_TILE
    n_p, n_s = n_tok_p // tc, n_tok_s // tc
    nt = n_p + n_s

    def slot(k):
        return pl.BlockSpec((tc * ROW_TILE, LANES), lambda s: (k * nt + s, 0))

    return pl.pallas_call(
        functools.partial(_combine_kernel, n_p=n_p),
        grid=(nt,),
        in_specs=[slot(0), slot(1), slot(2), slot(3),
                  pl.BlockSpec((tc, TOP_K), lambda s: (s, 0)),
                  pl.BlockSpec((tc, D), lambda s: (s, 0)),
                  pl.BlockSpec((1, 6, D), lambda s: (_batch_of_tile(s, groups, tc), 0, 0)),
                  pl.BlockSpec((1, D), lambda s: (0, 0))],
        out_specs=[pl.BlockSpec((tc, D), lambda s: (jnp.minimum(s, n_p - 1), 0)),
                   pl.BlockSpec((tc, D), lambda s: (jnp.maximum(s - n_p, 0), 0))],
        out_shape=[jax.ShapeDtypeStruct((n_tok_p, D), F32),
                   jax.ShapeDtypeStruct((n_tok_s, D), F32)],
        compiler_params=_cparams("arbitrary"),
        name="combine",
    )(y4, y4, y4, y4, gates_tok, x1, mod, post_ffn.reshape(1, D))


def _dispatch_kernel(npad_ref, dest_hbm, pad_hbm, h_ref, xs_hbm, idx_smem, pad_smem, zeros, idx_sem, sem,
                     *, n_steps):
    s = pl.program_id(0)
    tm = TOKEN_TILE
    RT = ROW_TILE
    slot = lax.rem(s, 2)

    def idx_copy(step, sl):
        return pltpu.make_async_copy(dest_hbm.at[step], idx_smem.at[sl], idx_sem.at[sl])

    @pl.when(s == 0)
    def _():
        idx_copy(0, 0).start()
        zeros[...] = jnp.zeros_like(zeros)
        cp = pltpu.make_async_copy(pad_hbm, pad_smem, idx_sem.at[1])
        cp.start()
        cp.wait()

        def pad_row(r):
            row = pl.multiple_of(pad_smem[0, r], RT)
            return pltpu.make_async_copy(zeros, xs_hbm.at[pl.ds(row, RT), :], sem)

        def issue(r, c):
            pad_row(r).start()
            return c

        def drain(r, c):
            pad_row(r).wait()
            return c

        lax.fori_loop(0, npad_ref[0], issue, 0)
        lax.fori_loop(0, npad_ref[0], drain, 0)

    idx_copy(s, slot).wait()
    if n_steps > 1:
        @pl.when(s + 1 < n_steps)
        def _():
            idx_copy(s + 1, 1 - slot).start()

    for k in range(TOP_K):
        for j in range(tm):
            row = pl.multiple_of(idx_smem[slot, 0, k * tm + j], RT)
            pltpu.make_async_copy(h_ref.at[pl.ds(j * RT, RT), :], xs_hbm.at[pl.ds(row, RT), :],
                                  sem).start(priority=j % 2)
    for k in range(TOP_K):
        pltpu.make_async_copy(h_ref, xs_hbm.at[pl.ds(0, tm * RT), :], sem).wait()


def _dispatch(h2, dest_rows, pad_rows, n_pad, n_rows):
    tm = TOKEN_TILE
    n_steps = h2.shape[0] // (tm * ROW_TILE)
    grid_spec = pltpu.PrefetchScalarGridSpec(
        num_scalar_prefetch=1,
        grid=(n_steps,),
        in_specs=[pl.BlockSpec(memory_space=pl.ANY),
                  pl.BlockSpec(memory_space=pl.ANY),
                  pl.BlockSpec((tm * ROW_TILE, LANES), lambda s, npad: (s, 0))],
        out_specs=pl.BlockSpec(memory_space=pl.ANY),
        scratch_shapes=[pltpu.SMEM((2, 1, TOP_K * tm), jnp.int32),
                        pltpu.SMEM((1, pad_rows.shape[0]), jnp.int32),
                        pltpu.VMEM((ROW_TILE, LANES), F32),
                        pltpu.SemaphoreType.DMA((2,)),
                        pltpu.SemaphoreType.DMA],
    )
    return pl.pallas_call(
        functools.partial(_dispatch_kernel, n_steps=n_steps),
        grid_spec=grid_spec,
        out_shape=jax.ShapeDtypeStruct((n_rows * ROW_TILE, LANES), F32),
        compiler_params=_cparams("arbitrary"),
        name="dispatch",
    )(n_pad, dest_rows.reshape(n_steps, 1, TOP_K * tm), pad_rows.reshape(1, -1), h2)


def _mlp_kernel(blk_e_ref, nused_ref, x_ref, wgu_ref, bgu_ref, wd_ref, bd_ref, o_ref, wgu_bf, wd_bf):
    i = pl.program_id(0)
    BM = EXPERT_BLOCK
    RT = ROW_TILE

    @pl.when(i < nused_ref[0])
    def _():
        new_expert = jnp.logical_or(i == 0, blk_e_ref[i] != blk_e_ref[jnp.maximum(i - 1, 0)])

        @pl.when(new_expert)
        def _():
            wgu_bf[...] = wgu_ref[0].astype(BF16)
            wd_bf[...] = wd_ref[0].astype(BF16)

        x = jnp.concatenate([x_ref[pl.ds(c, BM, stride=RT), :] for c in range(RT)],
                            axis=-1).astype(BF16)
        gu = jnp.dot(x, wgu_bf[...], preferred_element_type=F32) + bgu_ref[0]
        glu = jnp.minimum(gu[:, :D_EXPERT], SWIGLU_LIMIT)
        lin = jnp.clip(gu[:, D_EXPERT:], -SWIGLU_LIMIT, SWIGLU_LIMIT)
        act = glu * (1.0 / (1.0 + jnp.exp(-SWIGLU_ALPHA * glu))) * (lin + 1.0)
        y = jnp.dot(act.astype(BF16), wd_bf[...], preferred_element_type=F32) + bd_ref[0]
        for c in range(RT):
            o_ref[pl.ds(c, BM, stride=RT), :] = y[:, c * LANES:(c + 1) * LANES]

    @pl.when(i >= nused_ref[0])
    def _():
        o_ref[...] = jnp.zeros_like(o_ref)


def _expert_mlp(xs, blk_e, n_used, w_gate_up, b_gate_up, w_down, b_down):
    D = D_MODEL
    BM = EXPERT_BLOCK
    n_blocks = blk_e.shape[0]
    wmap = lambda i, be, nu: (be[i], 0, 0)
    grid_spec = pltpu.PrefetchScalarGridSpec(
        num_scalar_prefetch=2,
        grid=(n_blocks,),
        in_specs=[pl.BlockSpec((BM * ROW_TILE, LANES), lambda i, be, nu: (jnp.minimum(i, nu[0] - 1), 0)),
                  pl.BlockSpec((1, D, 2 * D_EXPERT), wmap),
                  pl.BlockSpec((1, 1, 2 * D_EXPERT), wmap),
                  pl.BlockSpec((1, D_EXPERT, D), wmap),
                  pl.BlockSpec((1, 1, D), wmap)],
        out_specs=pl.BlockSpec((BM * ROW_TILE, LANES), lambda i, be, nu: (i, 0)),
        scratch_shapes=[pltpu.VMEM((D, 2 * D_EXPERT), BF16),
                        pltpu.VMEM((D_EXPERT, D), BF16)],
    )
    return pl.pallas_call(
        _mlp_kernel,
        grid_spec=grid_spec,
        out_shape=jax.ShapeDtypeStruct(xs.shape, F32),
        compiler_params=_cparams("arbitrary"),
        name="expert_mlp",
    )(blk_e, n_used, xs, w_gate_up, b_gate_up.reshape(N_EXPERTS, 1, -1), w_down,
      b_down.reshape(N_EXPERTS, 1, -1))


def _gather_combine_kernel(dest_hbm, ys_hbm, gate_ref, x1_ref, mod_ref, pf_ref, op_ref, os_ref,
                           idx_smem, ybuf, idx_sem, row_sem, *, n_steps, n_p):
    s = pl.program_id(0)
    tc = COMBINE_TILE
    RT = ROW_TILE
    n_rows = TOP_K * tc
    slot = lax.rem(s, 2)

    def idx_copy(step, sl):
        return pltpu.make_async_copy(dest_hbm.at[step], idx_smem.at[sl], idx_sem.at[sl])

    def start_rows(sl):
        for r in range(n_rows):
            row = pl.multiple_of(idx_smem[sl, 0, r], RT)
            pltpu.make_async_copy(ys_hbm.at[pl.ds(row, RT), :], ybuf.at[sl, pl.ds(r * RT, RT), :],
                                  row_sem.at[sl]).start(priority=r % 2)

    def wait_rows(sl):
        pltpu.make_async_copy(ys_hbm.at[pl.ds(0, n_rows * RT), :], ybuf.at[sl], row_sem.at[sl]).wait()

    @pl.when(s == 0)
    def _():
        idx_copy(0, 0).start()
        idx_copy(0, 0).wait()
        start_rows(0)
        if n_steps > 1:
            idx_copy(1, 1).start()

    @pl.when(s + 1 < n_steps)
    def _():
        idx_copy(s + 1, 1 - slot).wait()
        start_rows(1 - slot)

    wait_rows(slot)

    @pl.when(s + 2 < n_steps)
    def _():
        idx_copy(s + 2, slot).start()

    g = gate_ref[...]

    def rows(k):
        return jnp.concatenate([ybuf[slot, pl.ds(k * tc * RT + c, tc, stride=RT), :]
                                for c in range(RT)], axis=-1)

    m = rows(0) * g[:, 0:1]
    for k in range(1, TOP_K):
        m = m + rows(k) * g[:, k:k + 1]
    res = x1_ref[...] + mod_ref[0, 5:6, :] * _rms(m, pf_ref[...])

    @pl.when(s < n_p)
    def _():
        op_ref[...] = res

    @pl.when(s >= n_p)
    def _():
        os_ref[...] = res


def _gather_combine(dest_rows, ys, gates_tok, x1, mod, post_ffn, n_tok_p, n_tok_s, groups):
    D = D_MODEL
    tc = COMBINE_TILE
    n_p, n_s = n_tok_p // tc, n_tok_s // tc
    nt = n_p + n_s
    return pl.pallas_call(
        functools.partial(_gather_combine_kernel, n_steps=nt, n_p=n_p),
        grid=(nt,),
        in_specs=[pl.BlockSpec(memory_space=pl.ANY),
                  pl.BlockSpec(memory_space=pl.ANY),
                  pl.BlockSpec((tc, TOP_K), lambda s: (s, 0)),
                  pl.BlockSpec((tc, D), lambda s: (s, 0)),
                  pl.BlockSpec((1, 6, D), lambda s: (_batch_of_tile(s, groups, tc), 0, 0)),
                  pl.BlockSpec((1, D), lambda s: (0, 0))],
        out_specs=[pl.BlockSpec((tc, D), lambda s: (jnp.minimum(s, n_p - 1), 0)),
                   pl.BlockSpec((tc, D), lambda s: (jnp.maximum(s - n_p, 0), 0))],
        out_shape=[jax.ShapeDtypeStruct((n_tok_p, D), F32),
                   jax.ShapeDtypeStruct((n_tok_s, D), F32)],
        scratch_shapes=[pltpu.SMEM((2, 1, TOP_K * tc), jnp.int32),
                        pltpu.VMEM((2, TOP_K * tc * ROW_TILE, LANES), F32),
                        pltpu.SemaphoreType.DMA((2,)),
                        pltpu.SemaphoreType.DMA((2,))],
        compiler_params=_cparams("arbitrary"),
        name="combine",
    )(dest_rows.reshape(nt, 1, TOP_K * tc), ys, gates_tok, x1, mod, post_ffn.reshape(1, D))


def _routing_dest(idx_t, rank_t, counts):
    BM = EXPERT_BLOCK
    nt, _, tm = idx_t.shape
    n_blocks = (nt * tm * TOP_K) // BM + N_EXPERTS
    padded = ((counts + BM - 1) // BM) * BM
    pad_end = jnp.cumsum(padded)
    pad_start = pad_end - padded
    experts = jnp.arange(N_EXPERTS, dtype=jnp.int32)
    start_of = jnp.sum(jnp.where(idx_t[..., None] == experts, pad_start, 0), axis=-1)
    dest = (start_of + rank_t).astype(jnp.int32)
    blk_first = jnp.arange(n_blocks, dtype=jnp.int32) * BM
    blk_e = jnp.minimum(jnp.sum((pad_end[None, :] <= blk_first[:, None]).astype(jnp.int32), axis=1),
                        N_EXPERTS - 1)
    n_used = (pad_end[-1:] // BM).astype(jnp.int32)
    pads = padded - counts
    pads_end = jnp.cumsum(pads)
    pads_before = pads_end - pads
    i = jnp.arange(N_EXPERTS * BM, dtype=jnp.int32)[:, None]
    mine = jnp.logical_and(i >= pads_before, i < pads_end)
    pad_pos = jnp.sum(jnp.where(mine, pad_start + counts + i - pads_before, 0), axis=1).astype(jnp.int32)
    n_pad = pads_end[-1:].astype(jnp.int32)
    return dest, blk_e, n_used, pad_pos, n_pad, n_blocks


def _routing_plan(idx_t, rank_t, counts, n_tok):
    BM = EXPERT_BLOCK
    nt, _, tm = idx_t.shape
    n_blocks = (n_tok * TOP_K) // BM + N_EXPERTS
    padded = ((counts + BM - 1) // BM) * BM
    pad_end = jnp.cumsum(padded)
    pad_start = pad_end - padded
    experts = jnp.arange(N_EXPERTS, dtype=jnp.int32)
    start_of = jnp.sum(jnp.where(idx_t[..., None] == experts, pad_start, 0), axis=-1)
    dest = (start_of + rank_t).astype(jnp.int32)
    t_of = (jnp.arange(nt, dtype=jnp.int32)[:, None, None] * tm
            + jnp.arange(tm, dtype=jnp.int32)[None, None, :])
    slot_row = jnp.arange(TOP_K, dtype=jnp.int32)[None, :, None] * n_tok + t_of
    blk_first = jnp.arange(n_blocks, dtype=jnp.int32) * BM
    blk_e = jnp.minimum(jnp.sum((pad_end[None, :] <= blk_first[:, None]).astype(jnp.int32), axis=1),
                        N_EXPERTS - 1)
    of_blk = blk_e[:, None] == experts
    per_blk = lambda v: jnp.sum(jnp.where(of_blk, v, 0), axis=1)[:, None]
    pos = blk_first[:, None] + jnp.arange(BM, dtype=jnp.int32)[None, :]
    real_before = per_blk(jnp.cumsum(counts) - counts) + jnp.minimum(pos - per_blk(pad_start),
                                                                     per_blk(counts))
    spare = n_tok * TOP_K + pos - real_before
    inv = spare.reshape(-1).astype(jnp.int32).at[dest.reshape(-1)].set(
        slot_row.reshape(-1), unique_indices=True, mode='promise_in_bounds')
    src_row = jnp.where(inv < n_tok * TOP_K, inv % n_tok, 0) * ROW_TILE
    dst_row = inv * ROW_TILE
    return src_row, dst_row, blk_e


def kernel(x_prompt, x_sample, c_prompt, c_sample, w_ada, b_ada, pre_mix, post_mix, pre_ffn, post_ffn,
           w_in, na_rel_bias, t5_rel_bias, wa_sink, gn_na, gn_wa, w_out, w_router, b_router,
           w_gate_up, b_gate_up, w_down, b_down):
    depth = w_ada.shape[0]
    D = D_MODEL
    groups = (x_prompt.shape[:2], x_sample.shape[:2])
    n_tok_p = x_prompt.shape[0] * x_prompt.shape[1]
    n_tok_s = x_sample.shape[0] * x_sample.shape[1]
    n_tok = n_tok_p + n_tok_s
    yp = x_prompt.reshape(n_tok_p, D)
    ys = x_sample.reshape(n_tok_s, D)
    c_all = jnp.concatenate([c_prompt, c_sample], axis=0)
    for l in range(depth):
        mod = _ada(c_all, w_ada[l], b_ada[l]).reshape(c_all.shape[0], 6, D)
        qb0 = 3 * D_NA
        scale = HEAD_DIM ** -0.5
        w_in_l = jnp.concatenate([w_in[l][:, :D_NA] * scale, w_in[l][:, D_NA:qb0],
                                  _wa_slab_order(w_in[l][:, qb0:qb0 + D_WA] * scale, axis=1),
                                  w_in[l][:, qb0 + D_WA:]], axis=1).astype(BF16)
        w_out_l = jnp.concatenate([w_out[l][:D_NA], _wa_slab_order(w_out[l][D_NA:], axis=0)],
                                  axis=0).astype(BF16)
        proj = _in_proj(yp, ys, mod, pre_mix[l], w_in_l, groups)
        na = _na_attention(proj, _na_bias_table(na_rel_bias[l]), gn_na[l], groups)
        wa = _wa_attention(proj, _wa_bias_table(t5_rel_bias), wa_sink[l],
                           _wa_slab_order(gn_wa[l], axis=0), groups)
        x1, h2, idx_t, gate_t, rank_t, counts = _post_mix(
            na, wa, yp, ys, mod, w_out_l, post_mix[l], pre_ffn[l], w_router[l].T, b_router[l], groups)
        dest, blk_e, n_used, pad_pos, n_pad, n_blocks = _routing_dest(
            idx_t, rank_t, counts.reshape(N_EXPERTS).astype(jnp.int32))
        dest_rows = dest * ROW_TILE
        xs = _dispatch(h2, dest_rows, pad_pos * ROW_TILE, n_pad, n_blocks * EXPERT_BLOCK)
        ys_rows = _expert_mlp(xs, blk_e, n_used, w_gate_up[l], b_gate_up[l], w_down[l], b_down[l])
        gates_tok = jnp.transpose(gate_t, (0, 2, 1)).reshape(n_tok, TOP_K)
        halves = TOKEN_TILE // COMBINE_TILE
        dest_steps = jnp.transpose(dest_rows.reshape(-1, TOP_K, halves, COMBINE_TILE), (0, 2, 1, 3))
        yp, ys = _gather_combine(dest_steps, ys_rows, gates_tok, x1, mod, post_ffn[l], n_tok_p, n_tok_s,
                                 groups)
    return (yp.reshape(x_prompt.shape), ys.reshape(x_sample.shape))
```

```python
import functools
import math

import jax
import jax.numpy as jnp
import numpy as np
from jax import lax
from jax.experimental import pallas as pl
from jax.experimental.pallas import tpu as pltpu

F32 = jnp.float32
BF16 = jnp.bfloat16

D_MODEL = 1024
HEAD_DIM = 64
D_NA = 512
D_WA = 512
N_HEADS_NA = 8
N_HEADS_WA = 8
N_KV_WA = 2
GQA_GROUP = 4
D_KV_WA = 128
D_IN_PROJ = 3 * D_NA + D_WA + 2 * D_KV_WA
GRID_W = 64
NA_WIN_R = 8
NA_WIN_C = 16
WA_WINDOW = 128
WA_BLOCK = 128
T5_BUCKETS = 32
T5_MAX_EXACT = 8
T5_MAX_DIST = 128
N_EXPERTS = 32
TOP_K = 4
D_EXPERT = D_MODEL
SWIGLU_LIMIT = 7.0
SWIGLU_ALPHA = 1.702
RMS_EPS = 1e-6
NEG_INF = -1e30

VMEM_LIMIT_BYTES = 56 * 1024 * 1024
LANES = 128
ROW_TILE = D_MODEL // LANES

TOKEN_TILE = 512
NA_ROWS_PER_STEP = 8
NA_ROW_UNROLL = 4
EXPERT_BLOCK = 256
X_SLOTS = 3
IDX_SLOTS = X_SLOTS + 2
O_SLOTS = 3
GATHER_PACING = 3
SCATTER_PACING = 0
COMBINE_TILE = 256


def _cparams(*sem):
    return pltpu.CompilerParams(dimension_semantics=sem, vmem_limit_bytes=VMEM_LIMIT_BYTES)


def _rms(x, g):
    return x * lax.rsqrt(jnp.mean(x * x, axis=-1, keepdims=True) + RMS_EPS) * g


def _seq_bounds(s, groups, block):
    base = 0
    start, length = None, None
    for n_seq, seq_len in groups:
        per = seq_len // block
        g_start = base + ((s - base) // per) * per
        if start is None:
            start, length = g_start, per
        else:
            inside = s >= base
            start = jnp.where(inside, g_start, start)
            length = jnp.where(inside, per, length)
        base += n_seq * per
    return start, length


def _batch_of_tile(s, groups, block):
    base_blk, base_seq = 0, 0
    out = None
    for n_seq, seq_len in groups:
        per = seq_len // block
        b = base_seq + (s - base_blk) // per
        out = b if out is None else jnp.where(s >= base_blk, b, out)
        base_blk += n_seq * per
        base_seq += n_seq
    return out


def _ada_kernel(c_ref, w_ref, b_ref, o_ref):
    c = c_ref[...]
    s = c * (1.0 / (1.0 + jnp.exp(-c)))
    o_ref[...] = jnp.dot(s, w_ref[...], preferred_element_type=F32,
                         precision=lax.Precision.HIGHEST) + b_ref[...]


def _ada(c_all, w_ada, b_ada):
    nb = c_all.shape[0]
    return pl.pallas_call(
        _ada_kernel,
        grid=(6,),
        in_specs=[pl.BlockSpec((nb, D_MODEL), lambda j: (0, 0)),
                  pl.BlockSpec((D_MODEL, D_MODEL), lambda j: (0, j)),
                  pl.BlockSpec((1, D_MODEL), lambda j: (0, j))],
        out_specs=pl.BlockSpec((nb, D_MODEL), lambda j: (0, j)),
        out_shape=jax.ShapeDtypeStruct((nb, 6 * D_MODEL), F32),
        compiler_params=_cparams("arbitrary"),
        name="ada",
    )(c_all, w_ada, b_ada.reshape(1, -1))


def _inproj_kernel(xp_ref, xs_ref, mod_ref, g_ref, w_ref, o_ref, *, n_p):
    s = pl.program_id(0)

    def body(x_ref):
        h = _rms(x_ref[...], g_ref[...]) * (1.0 + mod_ref[0, 1:2, :]) + mod_ref[0, 0:1, :]
        o_ref[...] = jnp.dot(h.astype(BF16), w_ref[...], preferred_element_type=F32).astype(BF16)

    pl.when(s < n_p)(lambda: body(xp_ref))
    pl.when(s >= n_p)(lambda: body(xs_ref))


def _in_proj(xp, xs, mod, pre_mix, w_in_bf16, groups):
    D = D_MODEL
    tm = TOKEN_TILE
    n_p, n_s = xp.shape[0] // tm, xs.shape[0] // tm
    return pl.pallas_call(
        functools.partial(_inproj_kernel, n_p=n_p),
        grid=(n_p + n_s,),
        in_specs=[pl.BlockSpec((tm, D), lambda s: (jnp.minimum(s, n_p - 1), 0)),
                  pl.BlockSpec((tm, D), lambda s: (jnp.maximum(s - n_p, 0), 0)),
                  pl.BlockSpec((1, 6, D), lambda s: (_batch_of_tile(s, groups, tm), 0, 0)),
                  pl.BlockSpec((1, D), lambda s: (0, 0)),
                  pl.BlockSpec((D, D_IN_PROJ), lambda s: (0, 0))],
        out_specs=pl.BlockSpec((tm, D_IN_PROJ), lambda s: (s, 0)),
        out_shape=jax.ShapeDtypeStruct(((n_p + n_s) * tm, D_IN_PROJ), BF16),
        compiler_params=_cparams("arbitrary"),
        name="in_proj",
    )(xp, xs, mod, pre_mix.reshape(1, D), w_in_bf16)


def _na_bias_table(rpb):
    qc = np.arange(GRID_W)[:, None]
    kc = np.arange(GRID_W)[None, :]
    win_start = np.clip(qc - NA_WIN_C // 2, 0, GRID_W - NA_WIN_C)
    valid = (kc >= win_start) & (kc < win_start + NA_WIN_C)
    dc = np.clip(kc - qc, -(NA_WIN_C - 1), NA_WIN_C - 1) + NA_WIN_C - 1
    pick = (dc[None] == np.arange(2 * NA_WIN_C - 1)[:, None, None]).astype(np.float32)
    full = jnp.einsum('hrc,cqk->hrqk', rpb.astype(F32), pick,
                      precision=lax.Precision.HIGHEST)
    full = jnp.where(valid[None, None], full.astype(F32), NEG_INF)
    two = jnp.concatenate([full[:, :-1], full[:, 1:]], axis=-1)
    two = two.reshape(N_HEADS_NA // 2, 2, 2 * NA_WIN_R - 2, GRID_W, 2 * GRID_W)
    return jnp.transpose(two, (0, 2, 1, 3, 4)).reshape(
        N_HEADS_NA // 2, 2 * NA_WIN_R - 2, 2 * GRID_W, 2 * GRID_W)


def _na_kernel(q_ref, kp_ref, kc_ref, kn_ref, vp_ref, vc_ref, vn_ref, bias_ref, gn_ref, o_ref,
               k_scr, v_scr, *, groups):
    s = pl.program_id(0)
    R = NA_ROWS_PER_STEP
    blk = R * GRID_W
    seq_start, seq_blocks = _seq_bounds(s, groups, blk)
    i = s - seq_start
    rows = seq_blocks * R
    k_scr[0:blk] = kp_ref[...]
    k_scr[blk:2 * blk] = kc_ref[...]
    k_scr[2 * blk:3 * blk] = kn_ref[...]
    v_scr[0:blk] = vp_ref[...]
    v_scr[blk:2 * blk] = vc_ref[...]
    v_scr[2 * blk:3 * blk] = vn_ref[...]
    lane = lax.broadcasted_iota(jnp.int32, (1, LANES), 1)
    lo = lane < HEAD_DIM
    gn = gn_ref[...]

    n_pairs = N_HEADS_NA // 2
    n_keys = NA_WIN_R * GRID_W

    def rows_body(jj, carry):
        js = [jj * NA_ROW_UNROLL + u for u in range(NA_ROW_UNROLL)]
        geo = []
        for j in js:
            r = i * R + j
            row_start = jnp.clip(r - NA_WIN_R // 2, 0, rows - NA_WIN_R)
            geo.append((r - row_start,
                        pl.multiple_of((row_start - i * R + R) * GRID_W, GRID_W),
                        pl.multiple_of(j * GRID_W, GRID_W)))
        scores = []
        for shift, koff, qoff in geo:
            for p in range(n_pairs):
                cs = slice(p * LANES, (p + 1) * LANES)
                q2 = q_ref[pl.ds(qoff, GRID_W), cs]
                zero = jnp.zeros_like(q2)
                qq = jnp.concatenate([jnp.where(lo, q2, zero), jnp.where(lo, zero, q2)], axis=0)
                k2 = k_scr[pl.ds(koff, n_keys), cs]
                scores.append(lax.dot_general(qq, k2, (((1,), (1,)), ((), ())),
                                              preferred_element_type=F32))
        probs = []
        for u, (shift, koff, qoff) in enumerate(geo):
            for p in range(n_pairs):
                sc = scores[u * n_pairs + p]
                parts = []
                for m in range(NA_WIN_R // 2):
                    d = 2 * m - shift + (NA_WIN_R - 1)
                    parts.append(sc[:, m * LANES:(m + 1) * LANES] + bias_ref[p, d])
                sc = jnp.concatenate(parts, axis=-1)
                e = jnp.exp(sc - jnp.max(sc, axis=-1, keepdims=True))
                probs.append((e.astype(BF16), jnp.sum(e, axis=-1, keepdims=True)))
        for u, (shift, koff, qoff) in enumerate(geo):
            outs = []
            for p in range(n_pairs):
                e, l = probs[u * n_pairs + p]
                v2 = v_scr[pl.ds(koff, n_keys), p * LANES:(p + 1) * LANES]
                o2 = jnp.dot(e, v2, preferred_element_type=F32) / l
                outs.append(jnp.where(lo, o2[:GRID_W], o2[GRID_W:]))
            o = jnp.concatenate(outs, axis=-1)
            o_ref[pl.ds(qoff, GRID_W), :] = _rms(o, gn).astype(BF16)
        return carry

    lax.fori_loop(0, R // NA_ROW_UNROLL, rows_body, 0)


def _na_attention(proj, bias_tab, gn_na, groups):
    n_tok = proj.shape[0]
    blk = NA_ROWS_PER_STEP * GRID_W

    def kv(col, delta):
        def index_map(s):
            start, length = _seq_bounds(s, groups, blk)
            return (jnp.clip(s + delta, start, start + length - 1), col)
        return pl.BlockSpec((blk, D_NA), index_map)

    return pl.pallas_call(
        functools.partial(_na_kernel, groups=groups),
        grid=(n_tok // blk,),
        in_specs=[pl.BlockSpec((blk, D_NA), lambda s: (s, 0)),
                  kv(1, -1), kv(1, 0), kv(1, 1), kv(2, -1), kv(2, 0), kv(2, 1),
                  pl.BlockSpec(bias_tab.shape, lambda s: (0, 0, 0, 0)),
                  pl.BlockSpec((1, D_NA), lambda s: (0, 0))],
        out_specs=pl.BlockSpec((blk, D_NA), lambda s: (s, 0)),
        out_shape=jax.ShapeDtypeStruct((n_tok, D_NA), BF16),
        scratch_shapes=[pltpu.VMEM((3 * blk, D_NA), BF16), pltpu.VMEM((3 * blk, D_NA), BF16)],
        compiler_params=_cparams("arbitrary"),
        name="na_attn",
    )(proj, proj, proj, proj, proj, proj, proj, bias_tab, gn_na.reshape(1, D_NA))


def _wa_slab_order(a, axis):
    shape = a.shape
    split = shape[:axis] + (N_KV_WA, GQA_GROUP, HEAD_DIM) + shape[axis + 1:]
    return jnp.swapaxes(a.reshape(split), axis, axis + 1).reshape(shape)


def _t5_bucket(rel):
    half = T5_BUCKETS // 2
    ret = (rel > 0).astype(jnp.int32) * half
    n = jnp.abs(rel)
    nf = jnp.maximum(n, 1).astype(F32)
    large = T5_MAX_EXACT + (jnp.log(nf / T5_MAX_EXACT) / math.log(T5_MAX_DIST / T5_MAX_EXACT)
                            * (half - T5_MAX_EXACT)).astype(jnp.int32)
    large = jnp.minimum(large, half - 1)
    return ret + jnp.where(n < T5_MAX_EXACT, n, large)


def _wa_bias_table(t5_table):
    rel = jnp.arange(3 * WA_BLOCK)[None, :] - WA_BLOCK - jnp.arange(WA_BLOCK)[:, None]
    pick = (_t5_bucket(rel)[..., None] == jnp.arange(T5_BUCKETS)).astype(F32)
    bias = jnp.einsum('qkb,bh->qkh', pick, t5_table.astype(F32),
                      precision=lax.Precision.HIGHEST)
    bias = jnp.where((jnp.abs(rel) <= WA_WINDOW)[..., None], bias, NEG_INF)
    bias = jnp.transpose(bias, (2, 0, 1))
    return jnp.concatenate([bias[:GQA_GROUP], bias[GQA_GROUP:]], axis=1)


def _wa_kernel(sink_ref, q_ref, kp_ref, kc_ref, kn_ref, vp_ref, vc_ref, vn_ref, bias_ref, gn_ref,
               o_ref, *, groups):
    s = pl.program_id(0)
    W = WA_BLOCK
    seq_start, seq_blocks = _seq_bounds(s, groups, W)
    lane = lax.broadcasted_iota(jnp.int32, (1, LANES), 1)
    lo = lane < HEAD_DIM
    row = lax.broadcasted_iota(jnp.int32, (2 * W, 1), 0)
    kcat = jnp.concatenate([kp_ref[...], kc_ref[...], kn_ref[...]], axis=0)
    vcat = jnp.concatenate([vp_ref[...], vc_ref[...], vn_ref[...]], axis=0)
    pen_prev = jnp.where(s == seq_start, NEG_INF, 0.0).astype(F32)
    pen_next = jnp.where(s == seq_start + seq_blocks - 1, NEG_INF, 0.0).astype(F32)
    col = lax.broadcasted_iota(jnp.int32, (1, 3 * W), 1)
    pen = jnp.where(col < W, pen_prev, jnp.where(col >= 2 * W, pen_next, 0.0))
    scores = []
    for p in range(GQA_GROUP):
        q2 = q_ref[:, p * LANES:(p + 1) * LANES]
        zero = jnp.zeros_like(q2)
        qq = jnp.concatenate([jnp.where(lo, q2, zero), jnp.where(lo, zero, q2)], axis=0)
        scores.append(lax.dot_general(qq, kcat, (((1,), (1,)), ((), ())),
                                      preferred_element_type=F32))
    probs = []
    for p in range(GQA_GROUP):
        sc = scores[p] + bias_ref[p] + pen
        sink = jnp.where(row < W, sink_ref[p], sink_ref[p + GQA_GROUP])
        mx = jnp.maximum(jnp.max(sc, axis=-1, keepdims=True), sink)
        e = jnp.exp(sc - mx)
        probs.append((e.astype(BF16), jnp.sum(e, axis=-1, keepdims=True) + jnp.exp(sink - mx)))
    outs = []
    for e, l in probs:
        o2 = jnp.dot(e, vcat, preferred_element_type=F32) / l
        outs.append(jnp.where(lo, o2[:W], o2[W:]))
    o = jnp.concatenate(outs, axis=-1)
    o_ref[...] = _rms(o, gn_ref[...]).astype(BF16)


def _wa_attention(proj, bias_tab, sink, gn_wa_perm, groups):
    n_tok = proj.shape[0]
    W = WA_BLOCK
    q_col = 3 * D_NA // D_WA
    k_col = (3 * D_NA + D_WA) // D_KV_WA
    v_col = k_col + 1

    def kv(col, delta):
        def index_map(s):
            start, length = _seq_bounds(s, groups, W)
            return (jnp.clip(s + delta, start, start + length - 1), col)
        return pl.BlockSpec((W, D_KV_WA), index_map)

    return pl.pallas_call(
        functools.partial(_wa_kernel, groups=groups),
        grid=(n_tok // W,),
        in_specs=[pl.BlockSpec(memory_space=pltpu.SMEM),
                  pl.BlockSpec((W, D_WA), lambda s: (s, q_col)),
                  kv(k_col, -1), kv(k_col, 0), kv(k_col, 1),
                  kv(v_col, -1), kv(v_col, 0), kv(v_col, 1),
                  pl.BlockSpec(bias_tab.shape, lambda s: (0, 0, 0)),
                  pl.BlockSpec((1, D_WA), lambda s: (0, 0))],
        out_specs=pl.BlockSpec((W, D_WA), lambda s: (s, 0)),
        out_shape=jax.ShapeDtypeStruct((n_tok, D_WA), BF16),
        compiler_params=_cparams("arbitrary"),
        name="wa_attn",
    )(sink, proj, proj, proj, proj, proj, proj, proj, bias_tab, gn_wa_perm.reshape(1, D_WA))


def _postmix_kernel(na_ref, wa_ref, xp_ref, xs_ref, mod_ref, wo_ref, pm_ref, pf_ref, wr_ref, br_ref,
                    tri_ref, x1_ref, h2_ref, idx_ref, gate_ref, rank_ref, cnt_ref, carry, *, n_p):
    s = pl.program_id(0)

    @pl.when(s == 0)
    def _():
        carry[...] = jnp.zeros_like(carry)

    def body(x_ref):
        a = jnp.concatenate([na_ref[...], wa_ref[...]], axis=-1)
        mixed = jnp.dot(a, wo_ref[...], preferred_element_type=F32)
        x1 = x_ref[...] + mod_ref[0, 2:3, :] * _rms(mixed, pm_ref[...])
        x1_ref[...] = x1
        h2 = _rms(x1, pf_ref[...]) * (1.0 + mod_ref[0, 4:5, :]) + mod_ref[0, 3:4, :]
        for c in range(ROW_TILE):
            h2_ref[pl.ds(c, TOKEN_TILE, stride=ROW_TILE), :] = h2[:, c * LANES:(c + 1) * LANES]
        logits = lax.dot_general(wr_ref[...], h2, (((1,), (1,)), ((), ())),
                                 preferred_element_type=F32,
                                 precision=lax.Precision.HIGHEST) + br_ref[...]
        eidx = lax.broadcasted_iota(jnp.int32, logits.shape, 0)
        tops, sels = [], []
        l = logits
        for _ in range(TOP_K):
            m = jnp.max(l, axis=0, keepdims=True)
            sel = jnp.min(jnp.where(l == m, eidx, N_EXPERTS), axis=0, keepdims=True)
            tops.append(m)
            sels.append(sel)
            l = jnp.where(eidx == sel, -jnp.inf, l)
        es = [jnp.exp(t - tops[0]) for t in tops]
        tot = es[0] + es[1] + es[2] + es[3]
        idx_ref[0] = jnp.concatenate(sels, axis=0)
        gate_ref[0] = jnp.concatenate([e / tot for e in es], axis=0)
        hits = [eidx == sel for sel in sels]
        onehot = sum(h.astype(F32) for h in hits)
        prefix = jnp.dot(onehot.astype(BF16), tri_ref[...], preferred_element_type=F32)
        rank_e = carry[...] + prefix - onehot
        rank_ref[0] = jnp.concatenate(
            [jnp.sum(jnp.where(h, rank_e, 0.0), axis=0, keepdims=True) for h in hits],
            axis=0).astype(jnp.int32)
        carry[...] = carry[...] + jnp.sum(onehot, axis=1, keepdims=True)
        cnt_ref[...] = carry[...]

    pl.when(s < n_p)(lambda: body(xp_ref))
    pl.when(s >= n_p)(lambda: body(xs_ref))


def _post_mix(na, wa, xp, xs, mod, w_out_bf16, post_mix, pre_ffn, w_router_t, b_router, groups):
    D = D_MODEL
    tm = TOKEN_TILE
    n_p, n_s = xp.shape[0] // tm, xs.shape[0] // tm
    nt = n_p + n_s
    tri = (np.arange(tm)[:, None] <= np.arange(tm)[None, :]).astype(np.float32)
    tok = lambda s: (s, 0)
    const2 = lambda s: (0, 0)
    tile3 = lambda s: (s, 0, 0)
    return pl.pallas_call(
        functools.partial(_postmix_kernel, n_p=n_p),
        grid=(nt,),
        in_specs=[pl.BlockSpec((tm, D_NA), tok),
                  pl.BlockSpec((tm, D_WA), tok),
                  pl.BlockSpec((tm, D), lambda s: (jnp.minimum(s, n_p - 1), 0)),
                  pl.BlockSpec((tm, D), lambda s: (jnp.maximum(s - n_p, 0), 0)),
                  pl.BlockSpec((1, 6, D), lambda s: (_batch_of_tile(s, groups, tm), 0, 0)),
                  pl.BlockSpec((D, D), const2),
                  pl.BlockSpec((1, D), const2),
                  pl.BlockSpec((1, D), const2),
                  pl.BlockSpec((N_EXPERTS, D), const2),
                  pl.BlockSpec((N_EXPERTS, 1), const2),
                  pl.BlockSpec((tm, tm), const2)],
        out_specs=[pl.BlockSpec((tm, D), tok),
                   pl.BlockSpec((tm * ROW_TILE, LANES), tok),
                   pl.BlockSpec((1, TOP_K, tm), tile3),
                   pl.BlockSpec((1, TOP_K, tm), tile3),
                   pl.BlockSpec((1, TOP_K, tm), tile3),
                   pl.BlockSpec((N_EXPERTS, 1), const2)],
        out_shape=[jax.ShapeDtypeStruct((nt * tm, D), F32),
                   jax.ShapeDtypeStruct((nt * tm * ROW_TILE, LANES), F32),
                   jax.ShapeDtypeStruct((nt, TOP_K, tm), jnp.int32),
                   jax.ShapeDtypeStruct((nt, TOP_K, tm), F32),
                   jax.ShapeDtypeStruct((nt, TOP_K, tm), jnp.int32),
                   jax.ShapeDtypeStruct((N_EXPERTS, 1), F32)],
        scratch_shapes=[pltpu.VMEM((N_EXPERTS, 1), F32)],
        compiler_params=_cparams("arbitrary"),
        name="post_mix",
    )(na, wa, xp, xs, mod, w_out_bf16, post_mix.reshape(1, D), pre_ffn.reshape(1, D),
      w_router_t, b_router.reshape(N_EXPERTS, 1), jnp.asarray(tri, BF16))


def _expert_kernel(blk_e_ref, src_hbm, dst_hbm, h_hbm, wgu_ref, bgu_ref, wd_ref, bd_ref, y_hbm,
                   src_smem, dst_smem, xbuf, xb, act_bf, ob0, ob1, ob2, wgu_bf, wd_bf,
                   idx_sem, g_sem, s_sem, *, n_blocks):
    i = pl.program_id(0)
    BM = EXPERT_BLOCK
    RT = ROW_TILE
    last = n_blocks - 1
    obufs = (ob0, ob1, ob2)

    def idx_copies(blk, slot):
        return (pltpu.make_async_copy(src_hbm.at[blk], src_smem.at[slot], idx_sem.at[slot]),
                pltpu.make_async_copy(dst_hbm.at[blk], dst_smem.at[slot], idx_sem.at[slot]))

    def start_idx(blk, slot):
        for c in idx_copies(blk, slot):
            c.start()

    def wait_idx(blk, slot):
        for c in idx_copies(blk, slot):
            c.wait()

    def paced(row, prev, rounds):
        z = prev
        for _ in range(rounds):
            z = z + lax.shift_right_arithmetic(z, 31)
        return row + lax.shift_right_arithmetic(z, 31)

    def start_gather(islot, xslot, rounds):
        prev = src_smem[islot, 0, 0]
        for r in range(BM):
            prev = paced(src_smem[islot, 0, r], prev, rounds)
            row = pl.multiple_of(prev, RT)
            pltpu.make_async_copy(h_hbm.at[pl.ds(row, RT), :], xbuf.at[xslot, pl.ds(r * RT, RT), :],
                                  g_sem.at[xslot]).start()

    def wait_gather(xslot):
        pltpu.make_async_copy(h_hbm.at[pl.ds(0, BM * RT), :], xbuf.at[xslot], g_sem.at[xslot]).wait()

    def start_scatter(islot, k, rounds):
        prev = dst_smem[islot, 0, 0]
        for r in range(BM):
            prev = paced(dst_smem[islot, 0, r], prev, rounds)
            row = pl.multiple_of(prev, RT)
            pltpu.make_async_copy(obufs[k].at[pl.ds(r * RT, RT), :], y_hbm.at[pl.ds(row, RT), :],
                                  s_sem.at[k]).start(priority=r % 2)

    def wait_scatter(k):
        pltpu.make_async_copy(obufs[k], y_hbm.at[pl.ds(0, BM * RT), :], s_sem.at[k]).wait()

    xs = lax.rem(i, X_SLOTS)

    @pl.when(i == 0)
    def _():
        for b in range(min(3, n_blocks)):
            start_idx(b, b)
        for b in range(min(2, n_blocks)):
            wait_idx(b, b)
            start_gather(b, b, 0)

    @pl.when(i + 3 < n_blocks)
    def _():
        start_idx(i + 3, lax.rem(i + 3, IDX_SLOTS))

    @pl.when(i + 2 < n_blocks)
    def _():
        wait_idx(i + 2, lax.rem(i + 2, IDX_SLOTS))

    wait_gather(xs)

    new_expert = jnp.logical_or(i == 0, blk_e_ref[i] != blk_e_ref[jnp.maximum(i - 1, 0)])

    @pl.when(new_expert)
    def _():
        wgu_bf[...] = wgu_ref[0].astype(BF16)
        wd_bf[...] = wd_ref[0].astype(BF16)

    def up_phase():
        for c in range(RT):
            xb[:, c * LANES:(c + 1) * LANES] = xbuf[xs, pl.ds(c, BM, stride=RT), :].astype(BF16)
        start_gather(lax.rem(jnp.minimum(i + 2, last), IDX_SLOTS), lax.rem(i + 2, X_SLOTS),
                     GATHER_PACING)
        gu = jnp.dot(xb[...], wgu_bf[...], preferred_element_type=F32) + bgu_ref[0]
        glu = jnp.minimum(gu[:, :D_EXPERT], SWIGLU_LIMIT)
        lin = jnp.clip(gu[:, D_EXPERT:], -SWIGLU_LIMIT, SWIGLU_LIMIT)
        act = glu * (1.0 / (1.0 + jnp.exp(-SWIGLU_ALPHA * glu))) * (lin + 1.0)
        act_bf[...] = act.astype(BF16)

    def down_phase(k, scatter_prev):
        if scatter_prev:
            start_scatter(lax.rem(i - 1, IDX_SLOTS), (k - 1) % O_SLOTS, SCATTER_PACING)
        y = jnp.dot(act_bf[...], wd_bf[...], preferred_element_type=F32) + bd_ref[0]
        for c in range(RT):
            obufs[k][pl.ds(c, BM, stride=RT), :] = y[:, c * LANES:(c + 1) * LANES]

    pl.when(i >= 0)(up_phase)
    for k in range(O_SLOTS):
        @pl.when(jnp.logical_and(i >= O_SLOTS, lax.rem(i, O_SLOTS) == k))
        def _():
            wait_scatter(k)
    pl.when(i == 0)(lambda: down_phase(0, False))
    for k in range(O_SLOTS):
        pl.when(jnp.logical_and(i > 0, lax.rem(i, O_SLOTS) == k))(
            functools.partial(down_phase, k, True))

    @pl.when(i == last)
    def _():
        start_scatter(last % IDX_SLOTS, last % O_SLOTS, 0)
        for b in range(max(0, last - O_SLOTS + 1), last + 1):
            wait_scatter(b % O_SLOTS)
        wait_gather(lax.rem(i + 1, X_SLOTS))
        wait_gather(lax.rem(i + 2, X_SLOTS))


def _experts(h2, src_row, dst_row, blk_e, w_gate_up, b_gate_up, w_down, b_down):
    D = D_MODEL
    BM = EXPERT_BLOCK
    n_blocks = blk_e.shape[0]
    wmap = lambda i, be: (be[i], 0, 0)
    grid_spec = pltpu.PrefetchScalarGridSpec(
        num_scalar_prefetch=1,
        grid=(n_blocks,),
        in_specs=[pl.BlockSpec(memory_space=pl.ANY),
                  pl.BlockSpec(memory_space=pl.ANY),
                  pl.BlockSpec(memory_space=pl.ANY),
                  pl.BlockSpec((1, D, 2 * D_EXPERT), wmap),
                  pl.BlockSpec((1, 1, 2 * D_EXPERT), wmap),
                  pl.BlockSpec((1, D_EXPERT, D), wmap),
                  pl.BlockSpec((1, 1, D), wmap)],
        out_specs=pl.BlockSpec(memory_space=pl.ANY),
        scratch_shapes=[pltpu.SMEM((IDX_SLOTS, 1, BM), jnp.int32),
                        pltpu.SMEM((IDX_SLOTS, 1, BM), jnp.int32),
                        pltpu.VMEM((X_SLOTS, BM * ROW_TILE, LANES), F32),
                        pltpu.VMEM((BM, D), BF16),
                        pltpu.VMEM((BM, D_EXPERT), BF16),
                        pltpu.VMEM((BM * ROW_TILE, LANES), F32),
                        pltpu.VMEM((BM * ROW_TILE, LANES), F32),
                        pltpu.VMEM((BM * ROW_TILE, LANES), F32),
                        pltpu.VMEM((D, 2 * D_EXPERT), BF16),
                        pltpu.VMEM((D_EXPERT, D), BF16),
                        pltpu.SemaphoreType.DMA((IDX_SLOTS,)),
                        pltpu.SemaphoreType.DMA((X_SLOTS,)),
                        pltpu.SemaphoreType.DMA((O_SLOTS,))],
    )
    return pl.pallas_call(
        functools.partial(_expert_kernel, n_blocks=n_blocks),
        grid_spec=grid_spec,
        out_shape=jax.ShapeDtypeStruct((n_blocks * BM * ROW_TILE, LANES), F32),
        compiler_params=_cparams("arbitrary"),
        name="experts",
    )(blk_e, src_row.reshape(n_blocks, 1, BM), dst_row.reshape(n_blocks, 1, BM), h2,
      w_gate_up, b_gate_up.reshape(N_EXPERTS, 1, -1), w_down, b_down.reshape(N_EXPERTS, 1, -1))


def _combine_kernel(y0_ref, y1_ref, y2_ref, y3_ref, gate_ref, x1_ref, mod_ref, pf_ref,
                    op_ref, os_ref, *, n_p):
    s = pl.program_id(0)
    g = gate_ref[...]

    def rows(y_ref):
        return jnp.concatenate([y_ref[pl.ds(c, COMBINE_TILE, stride=ROW_TILE), :]
                                for c in range(ROW_TILE)], axis=-1)

    m = rows(y0_ref) * g[:, 0:1]
    for k, y_ref in enumerate((y1_ref, y2_ref, y3_ref), start=1):
        m = m + rows(y_ref) * g[:, k:k + 1]
    res = x1_ref[...] + mod_ref[0, 5:6, :] * _rms(m, pf_ref[...])

    @pl.when(s < n_p)
    def _():
        op_ref[...] = res

    @pl.when(s >= n_p)
    def _():
        os_ref[...] = res


def _combine(y4, gates_tok, x1, mod, post_ffn, n_tok_p, n_tok_s, groups):
    D = D_MODEL
    tc = COMBINE_TILE
    n_p, n_s = n_tok_p // tc, n_tok_s // tc
    nt = n_p + n_s

    def slot(k):
        return pl.BlockSpec((tc * ROW_TILE, LANES), lambda s: (k * nt + s, 0))

    return pl.pallas_call(
        functools.partial(_combine_kernel, n_p=n_p),
        grid=(nt,),
        in_specs=[slot(0), slot(1), slot(2), slot(3),
                  pl.BlockSpec((tc, TOP_K), lambda s: (s, 0)),
                  pl.BlockSpec((tc, D), lambda s: (s, 0)),
                  pl.BlockSpec((1, 6, D), lambda s: (_batch_of_tile(s, groups, tc), 0, 0)),
                  pl.BlockSpec((1, D), lambda s: (0, 0))],
        out_specs=[pl.BlockSpec((tc, D), lambda s: (jnp.minimum(s, n_p - 1), 0)),
                   pl.BlockSpec((tc, D), lambda s: (jnp.maximum(s - n_p, 0), 0))],
        out_shape=[jax.ShapeDtypeStruct((n_tok_p, D), F32),
                   jax.ShapeDtypeStruct((n_tok_s, D), F32)],
        compiler_params=_cparams("arbitrary"),
        name="combine",
    )(y4, y4, y4, y4, gates_tok, x1, mod, post_ffn.reshape(1, D))


def _dispatch_kernel(npad_ref, nused_ref, dest_hbm, pad_hbm, h_ref, xs_hbm, idx_smem, pad_smem, zeros,
                     zero_blk, idx_sem, sem, *, n_steps, n_blocks):
    s = pl.program_id(0)
    tm = TOKEN_TILE
    RT = ROW_TILE
    slot = lax.rem(s, 2)

    def idx_copy(step, sl):
        return pltpu.make_async_copy(dest_hbm.at[step], idx_smem.at[sl], idx_sem.at[sl])

    @pl.when(s == 0)
    def _():
        idx_copy(0, 0).start()
        zeros[...] = jnp.zeros_like(zeros)
        cp = pltpu.make_async_copy(pad_hbm, pad_smem, idx_sem.at[1])
        cp.start()
        cp.wait()

        def pad_row(r):
            row = pl.multiple_of(pad_smem[0, r], RT)
            return pltpu.make_async_copy(zeros, xs_hbm.at[pl.ds(row, RT), :], sem)

        def issue(r, c):
            pad_row(r).start()
            return c

        def drain(r, c):
            pad_row(r).wait()
            return c

        lax.fori_loop(0, npad_ref[0], issue, 0)
        lax.fori_loop(0, npad_ref[0], drain, 0)

        zero_blk[...] = jnp.zeros_like(zero_blk)

        def blk_copy(b):
            start = pl.multiple_of(b * (EXPERT_BLOCK * RT), EXPERT_BLOCK * RT)
            return pltpu.make_async_copy(zero_blk, xs_hbm.at[pl.ds(start, EXPERT_BLOCK * RT), :], sem)

        def issue_blk(b, c):
            blk_copy(b).start()
            return c

        def drain_blk(b, c):
            blk_copy(b).wait()
            return c

        lax.fori_loop(nused_ref[0], n_blocks, issue_blk, 0)
        lax.fori_loop(nused_ref[0], n_blocks, drain_blk, 0)

    idx_copy(s, slot).wait()
    if n_steps > 1:
        @pl.when(s + 1 < n_steps)
        def _():
            idx_copy(s + 1, 1 - slot).start()

    for k in range(TOP_K):
        for j in range(tm):
            row = pl.multiple_of(idx_smem[slot, 0, k * tm + j], RT)
            pltpu.make_async_copy(h_ref.at[pl.ds(j * RT, RT), :], xs_hbm.at[pl.ds(row, RT), :],
                                  sem).start(priority=j % 2)
    for k in range(TOP_K):
        pltpu.make_async_copy(h_ref, xs_hbm.at[pl.ds(0, tm * RT), :], sem).wait()


def _dispatch(h2, dest_rows, pad_rows, n_pad, n_used, n_blocks):
    tm = TOKEN_TILE
    n_steps = h2.shape[0] // (tm * ROW_TILE)
    grid_spec = pltpu.PrefetchScalarGridSpec(
        num_scalar_prefetch=2,
        grid=(n_steps,),
        in_specs=[pl.BlockSpec(memory_space=pl.ANY),
                  pl.BlockSpec(memory_space=pl.ANY),
                  pl.BlockSpec((tm * ROW_TILE, LANES), lambda s, npad, nused: (s, 0))],
        out_specs=pl.BlockSpec(memory_space=pl.ANY),
        scratch_shapes=[pltpu.SMEM((2, 1, TOP_K * tm), jnp.int32),
                        pltpu.SMEM((1, pad_rows.shape[0]), jnp.int32),
                        pltpu.VMEM((ROW_TILE, LANES), F32),
                        pltpu.VMEM((EXPERT_BLOCK * ROW_TILE, LANES), F32),
                        pltpu.SemaphoreType.DMA((2,)),
                        pltpu.SemaphoreType.DMA],
    )
    return pl.pallas_call(
        functools.partial(_dispatch_kernel, n_steps=n_steps, n_blocks=n_blocks),
        grid_spec=grid_spec,
        out_shape=jax.ShapeDtypeStruct((n_blocks * EXPERT_BLOCK * ROW_TILE, LANES), F32),
        compiler_params=_cparams("arbitrary"),
        name="dispatch",
    )(n_pad, n_used, dest_rows.reshape(n_steps, 1, TOP_K * tm), pad_rows.reshape(1, -1), h2)


def _mlp_kernel(blk_e_ref, nused_ref, x_ref, wgu_ref, bgu_ref, wd_ref, bd_ref, o_ref, wgu_bf, wd_bf):
    i = pl.program_id(0)
    BM = EXPERT_BLOCK
    RT = ROW_TILE

    @pl.when(i < nused_ref[0])
    def _():
        new_expert = jnp.logical_or(i == 0, blk_e_ref[i] != blk_e_ref[jnp.maximum(i - 1, 0)])

        @pl.when(new_expert)
        def _():
            wgu_bf[...] = wgu_ref[0].astype(BF16)
            wd_bf[...] = wd_ref[0].astype(BF16)

        x = jnp.concatenate([x_ref[pl.ds(c, BM, stride=RT), :] for c in range(RT)],
                            axis=-1).astype(BF16)
        gu = jnp.dot(x, wgu_bf[...], preferred_element_type=F32) + bgu_ref[0]
        glu = jnp.minimum(gu[:, :D_EXPERT], SWIGLU_LIMIT)
        lin = jnp.clip(gu[:, D_EXPERT:], -SWIGLU_LIMIT, SWIGLU_LIMIT)
        act = glu * (1.0 / (1.0 + jnp.exp(-SWIGLU_ALPHA * glu))) * (lin + 1.0)
        y = jnp.dot(act.astype(BF16), wd_bf[...], preferred_element_type=F32) + bd_ref[0]
        for c in range(RT):
            o_ref[pl.ds(c, BM, stride=RT), :] = y[:, c * LANES:(c + 1) * LANES]

    @pl.when(i >= nused_ref[0])
    def _():
        o_ref[...] = jnp.zeros_like(o_ref)


def _expert_mlp(xs, blk_e, n_used, w_gate_up, b_gate_up, w_down, b_down):
    D = D_MODEL
    BM = EXPERT_BLOCK
    n_blocks = blk_e.shape[0]
    wmap = lambda i, be, nu: (be[i], 0, 0)
    grid_spec = pltpu.PrefetchScalarGridSpec(
        num_scalar_prefetch=2,
        grid=(n_blocks,),
        in_specs=[pl.BlockSpec((BM * ROW_TILE, LANES), lambda i, be, nu: (jnp.minimum(i, nu[0] - 1), 0)),
                  pl.BlockSpec((1, D, 2 * D_EXPERT), wmap),
                  pl.BlockSpec((1, 1, 2 * D_EXPERT), wmap),
                  pl.BlockSpec((1, D_EXPERT, D), wmap),
                  pl.BlockSpec((1, 1, D), wmap)],
        out_specs=pl.BlockSpec((BM * ROW_TILE, LANES), lambda i, be, nu: (i, 0)),
        scratch_shapes=[pltpu.VMEM((D, 2 * D_EXPERT), BF16),
                        pltpu.VMEM((D_EXPERT, D), BF16)],
    )
    return pl.pallas_call(
        _mlp_kernel,
        grid_spec=grid_spec,
        out_shape=jax.ShapeDtypeStruct(xs.shape, F32),
        compiler_params=_cparams("arbitrary"),
        name="expert_mlp",
    )(blk_e, n_used, xs, w_gate_up, b_gate_up.reshape(N_EXPERTS, 1, -1), w_down,
      b_down.reshape(N_EXPERTS, 1, -1))


def _gather_combine_kernel(dest_hbm, ys_hbm, gate_ref, x1_ref, mod_ref, pf_ref, op_ref, os_ref,
                           idx_smem, ybuf, idx_sem, row_sem, *, n_steps, n_p):
    s = pl.program_id(0)
    tc = COMBINE_TILE
    RT = ROW_TILE
    n_rows = TOP_K * tc
    slot = lax.rem(s, 2)

    def idx_copy(step, sl):
        return pltpu.make_async_copy(dest_hbm.at[step], idx_smem.at[sl], idx_sem.at[sl])

    def start_rows(sl):
        for r in range(n_rows):
            row = pl.multiple_of(idx_smem[sl, 0, r], RT)
            pltpu.make_async_copy(ys_hbm.at[pl.ds(row, RT), :], ybuf.at[sl, pl.ds(r * RT, RT), :],
                                  row_sem.at[sl]).start(priority=r % 2)

    def wait_rows(sl):
        pltpu.make_async_copy(ys_hbm.at[pl.ds(0, n_rows * RT), :], ybuf.at[sl], row_sem.at[sl]).wait()

    @pl.when(s == 0)
    def _():
        idx_copy(0, 0).start()
        idx_copy(0, 0).wait()
        start_rows(0)
        if n_steps > 1:
            idx_copy(1, 1).start()

    @pl.when(s + 1 < n_steps)
    def _():
        idx_copy(s + 1, 1 - slot).wait()
        start_rows(1 - slot)

    wait_rows(slot)

    @pl.when(s + 2 < n_steps)
    def _():
        idx_copy(s + 2, slot).start()

    g = gate_ref[...]

    def rows(k):
        return jnp.concatenate([ybuf[slot, pl.ds(k * tc * RT + c, tc, stride=RT), :]
                                for c in range(RT)], axis=-1)

    m = rows(0) * g[:, 0:1]
    for k in range(1, TOP_K):
        m = m + rows(k) * g[:, k:k + 1]
    res = x1_ref[...] + mod_ref[0, 5:6, :] * _rms(m, pf_ref[...])

    @pl.when(s < n_p)
    def _():
        op_ref[...] = res

    @pl.when(s >= n_p)
    def _():
        os_ref[...] = res


def _gather_combine(dest_rows, ys, gates_tok, x1, mod, post_ffn, n_tok_p, n_tok_s, groups):
    D = D_MODEL
    tc = COMBINE_TILE
    n_p, n_s = n_tok_p // tc, n_tok_s // tc
    nt = n_p + n_s
    return pl.pallas_call(
        functools.partial(_gather_combine_kernel, n_steps=nt, n_p=n_p),
        grid=(nt,),
        in_specs=[pl.BlockSpec(memory_space=pl.ANY),
                  pl.BlockSpec(memory_space=pl.ANY),
                  pl.BlockSpec((tc, TOP_K), lambda s: (s, 0)),
                  pl.BlockSpec((tc, D), lambda s: (s, 0)),
                  pl.BlockSpec((1, 6, D), lambda s: (_batch_of_tile(s, groups, tc), 0, 0)),
                  pl.BlockSpec((1, D), lambda s: (0, 0))],
        out_specs=[pl.BlockSpec((tc, D), lambda s: (jnp.minimum(s, n_p - 1), 0)),
                   pl.BlockSpec((tc, D), lambda s: (jnp.maximum(s - n_p, 0), 0))],
        out_shape=[jax.ShapeDtypeStruct((n_tok_p, D), F32),
                   jax.ShapeDtypeStruct((n_tok_s, D), F32)],
        scratch_shapes=[pltpu.SMEM((2, 1, TOP_K * tc), jnp.int32),
                        pltpu.VMEM((2, TOP_K * tc * ROW_TILE, LANES), F32),
                        pltpu.SemaphoreType.DMA((2,)),
                        pltpu.SemaphoreType.DMA((2,))],
        compiler_params=_cparams("arbitrary"),
        name="combine",
    )(dest_rows.reshape(nt, 1, TOP_K * tc), ys, gates_tok, x1, mod, post_ffn.reshape(1, D))


def _routing_dest(idx_t, rank_t, counts):
    BM = EXPERT_BLOCK
    nt, _, tm = idx_t.shape
    n_blocks = (nt * tm * TOP_K) // BM + N_EXPERTS
    padded = ((counts + BM - 1) // BM) * BM
    pad_end = jnp.cumsum(padded)
    pad_start = pad_end - padded
    experts = jnp.arange(N_EXPERTS, dtype=jnp.int32)
    start_of = jnp.sum(jnp.where(idx_t[..., None] == experts, pad_start, 0), axis=-1)
    dest = (start_of + rank_t).astype(jnp.int32)
    blk_first = jnp.arange(n_blocks, dtype=jnp.int32) * BM
    blk_e = jnp.minimum(jnp.sum((pad_end[None, :] <= blk_first[:, None]).astype(jnp.int32), axis=1),
                        N_EXPERTS - 1)
    n_used = (pad_end[-1:] // BM).astype(jnp.int32)
    pads = padded - counts
    pads_end = jnp.cumsum(pads)
    pads_before = pads_end - pads
    i = jnp.arange(N_EXPERTS * BM, dtype=jnp.int32)[:, None]
    mine = jnp.logical_and(i >= pads_before, i < pads_end)
    pad_pos = jnp.sum(jnp.where(mine, pad_start + counts + i - pads_before, 0), axis=1).astype(jnp.int32)
    n_pad = pads_end[-1:].astype(jnp.int32)
    return dest, blk_e, n_used, pad_pos, n_pad, n_blocks


def _routing_plan(idx_t, rank_t, counts, n_tok):
    BM = EXPERT_BLOCK
    nt, _, tm = idx_t.shape
    n_blocks = (n_tok * TOP_K) // BM + N_EXPERTS
    padded = ((counts + BM - 1) // BM) * BM
    pad_end = jnp.cumsum(padded)
    pad_start = pad_end - padded
    experts = jnp.arange(N_EXPERTS, dtype=jnp.int32)
    start_of = jnp.sum(jnp.where(idx_t[..., None] == experts, pad_start, 0), axis=-1)
    dest = (start_of + rank_t).astype(jnp.int32)
    t_of = (jnp.arange(nt, dtype=jnp.int32)[:, None, None] * tm
            + jnp.arange(tm, dtype=jnp.int32)[None, None, :])
    slot_row = jnp.arange(TOP_K, dtype=jnp.int32)[None, :, None] * n_tok + t_of
    blk_first = jnp.arange(n_blocks, dtype=jnp.int32) * BM
    blk_e = jnp.minimum(jnp.sum((pad_end[None, :] <= blk_first[:, None]).astype(jnp.int32), axis=1),
                        N_EXPERTS - 1)
    of_blk = blk_e[:, None] == experts
    per_blk = lambda v: jnp.sum(jnp.where(of_blk, v, 0), axis=1)[:, None]
    pos = blk_first[:, None] + jnp.arange(BM, dtype=jnp.int32)[None, :]
    real_before = per_blk(jnp.cumsum(counts) - counts) + jnp.minimum(pos - per_blk(pad_start),
                                                                     per_blk(counts))
    spare = n_tok * TOP_K + pos - real_before
    inv = spare.reshape(-1).astype(jnp.int32).at[dest.reshape(-1)].set(
        slot_row.reshape(-1), unique_indices=True, mode='promise_in_bounds')
    src_row = jnp.where(inv < n_tok * TOP_K, inv % n_tok, 0) * ROW_TILE
    dst_row = inv * ROW_TILE
    return src_row, dst_row, blk_e


def kernel(x_prompt, x_sample, c_prompt, c_sample, w_ada, b_ada, pre_mix, post_mix, pre_ffn, post_ffn,
           w_in, na_rel_bias, t5_rel_bias, wa_sink, gn_na, gn_wa, w_out, w_router, b_router,
           w_gate_up, b_gate_up, w_down, b_down):
    depth = w_ada.shape[0]
    D = D_MODEL
    groups = (x_prompt.shape[:2], x_sample.shape[:2])
    n_tok_p = x_prompt.shape[0] * x_prompt.shape[1]
    n_tok_s = x_sample.shape[0] * x_sample.shape[1]
    n_tok = n_tok_p + n_tok_s
    yp = x_prompt.reshape(n_tok_p, D)
    ys = x_sample.reshape(n_tok_s, D)
    c_all = jnp.concatenate([c_prompt, c_sample], axis=0)
    for l in range(depth):
        mod = _ada(c_all, w_ada[l], b_ada[l]).reshape(c_all.shape[0], 6, D)
        qb0 = 3 * D_NA
        scale = HEAD_DIM ** -0.5
        w_in_l = jnp.concatenate([w_in[l][:, :D_NA] * scale, w_in[l][:, D_NA:qb0],
                                  _wa_slab_order(w_in[l][:, qb0:qb0 + D_WA] * scale, axis=1),
                                  w_in[l][:, qb0 + D_WA:]], axis=1).astype(BF16)
        w_out_l = jnp.concatenate([w_out[l][:D_NA], _wa_slab_order(w_out[l][D_NA:], axis=0)],
                                  axis=0).astype(BF16)
        proj = _in_proj(yp, ys, mod, pre_mix[l], w_in_l, groups)
        na = _na_attention(proj, _na_bias_table(na_rel_bias[l]), gn_na[l], groups)
        wa = _wa_attention(proj, _wa_bias_table(t5_rel_bias), wa_sink[l],
                           _wa_slab_order(gn_wa[l], axis=0), groups)
        x1, h2, idx_t, gate_t, rank_t, counts = _post_mix(
            na, wa, yp, ys, mod, w_out_l, post_mix[l], pre_ffn[l], w_router[l].T, b_router[l], groups)
        dest, blk_e, n_used, pad_pos, n_pad, n_blocks = _routing_dest(
            idx_t, rank_t, counts.reshape(N_EXPERTS).astype(jnp.int32))
        dest_rows = dest * ROW_TILE
        xs = _dispatch(h2, dest_rows, pad_pos * ROW_TILE, n_pad, n_used, n_blocks)
        ys_rows = _expert_mlp(xs, blk_e, n_used, w_gate_up[l], b_gate_up[l], w_down[l], b_down[l])
        gates_tok = jnp.transpose(gate_t, (0, 2, 1)).reshape(n_tok, TOP_K)
        halves = TOKEN_TILE // COMBINE_TILE
        dest_steps = jnp.transpose(dest_rows.reshape(-1, TOP_K, halves, COMBINE_TILE), (0, 2, 1, 3))
        yp, ys = _gather_combine(dest_steps, ys_rows, gates_tok, x1, mod, post_ffn[l], n_tok_p, n_tok_s,
                                 groups)
    return (yp.reshape(x_prompt.shape), ys.reshape(x_sample.shape))
```

```python
import functools
import math

import jax
import jax.numpy as jnp
import numpy as np
from jax import lax
from jax.experimental import pallas as pl
from jax.experimental.pallas import tpu as pltpu

F32 = jnp.float32
BF16 = jnp.bfloat16

D_MODEL = 1024
HEAD_DIM = 64
D_NA = 512
D_WA = 512
N_HEADS_NA = 8
N_HEADS_WA = 8
N_KV_WA = 2
GQA_GROUP = 4
D_KV_WA = 128
D_IN_PROJ = 3 * D_NA + D_WA + 2 * D_KV_WA
GRID_W = 64
NA_WIN_R = 8
NA_WIN_C = 16
WA_WINDOW = 128
WA_BLOCK = 128
T5_BUCKETS = 32
T5_MAX_EXACT = 8
T5_MAX_DIST = 128
N_EXPERTS = 32
TOP_K = 4
D_EXPERT = D_MODEL
SWIGLU_LIMIT = 7.0
SWIGLU_ALPHA = 1.702
RMS_EPS = 1e-6
NEG_INF = -1e30

VMEM_LIMIT_BYTES = 56 * 1024 * 1024
LANES = 128
ROW_TILE = D_MODEL // LANES

TOKEN_TILE = 512
NA_ROWS_PER_STEP = 8
NA_ROW_UNROLL = 4
EXPERT_BLOCK = 512
MLP_CHUNKS = 2
X_SLOTS = 3
IDX_SLOTS = X_SLOTS + 2
O_SLOTS = 3
GATHER_PACING = 3
SCATTER_PACING = 0
COMBINE_TILE = 256


def _cparams(*sem):
    return pltpu.CompilerParams(dimension_semantics=sem, vmem_limit_bytes=VMEM_LIMIT_BYTES)


def _rms(x, g):
    return x * lax.rsqrt(jnp.mean(x * x, axis=-1, keepdims=True) + RMS_EPS) * g


def _seq_bounds(s, groups, block):
    base = 0
    start, length = None, None
    for n_seq, seq_len in groups:
        per = seq_len // block
        g_start = base + ((s - base) // per) * per
        if start is None:
            start, length = g_start, per
        else:
            inside = s >= base
            start = jnp.where(inside, g_start, start)
            length = jnp.where(inside, per, length)
        base += n_seq * per
    return start, length


def _batch_of_tile(s, groups, block):
    base_blk, base_seq = 0, 0
    out = None
    for n_seq, seq_len in groups:
        per = seq_len // block
        b = base_seq + (s - base_blk) // per
        out = b if out is None else jnp.where(s >= base_blk, b, out)
        base_blk += n_seq * per
        base_seq += n_seq
    return out


def _ada_kernel(c_ref, w_ref, b_ref, o_ref):
    c = c_ref[...]
    s = c * (1.0 / (1.0 + jnp.exp(-c)))
    o_ref[...] = jnp.dot(s, w_ref[...], preferred_element_type=F32,
                         precision=lax.Precision.HIGHEST) + b_ref[...]


def _ada(c_all, w_ada, b_ada):
    nb = c_all.shape[0]
    return pl.pallas_call(
        _ada_kernel,
        grid=(6,),
        in_specs=[pl.BlockSpec((nb, D_MODEL), lambda j: (0, 0)),
                  pl.BlockSpec((D_MODEL, D_MODEL), lambda j: (0, j)),
                  pl.BlockSpec((1, D_MODEL), lambda j: (0, j))],
        out_specs=pl.BlockSpec((nb, D_MODEL), lambda j: (0, j)),
        out_shape=jax.ShapeDtypeStruct((nb, 6 * D_MODEL), F32),
        compiler_params=_cparams("arbitrary"),
        name="ada",
    )(c_all, w_ada, b_ada.reshape(1, -1))


def _inproj_kernel(xp_ref, xs_ref, mod_ref, g_ref, w_ref, o_ref, *, n_p):
    s = pl.program_id(0)

    def body(x_ref):
        h = _rms(x_ref[...], g_ref[...]) * (1.0 + mod_ref[0, 1:2, :]) + mod_ref[0, 0:1, :]
        o_ref[...] = jnp.dot(h.astype(BF16), w_ref[...], preferred_element_type=F32).astype(BF16)

    pl.when(s < n_p)(lambda: body(xp_ref))
    pl.when(s >= n_p)(lambda: body(xs_ref))


def _in_proj(xp, xs, mod, pre_mix, w_in_bf16, groups):
    D = D_MODEL
    tm = TOKEN_TILE
    n_p, n_s = xp.shape[0] // tm, xs.shape[0] // tm
    return pl.pallas_call(
        functools.partial(_inproj_kernel, n_p=n_p),
        grid=(n_p + n_s,),
        in_specs=[pl.BlockSpec((tm, D), lambda s: (jnp.minimum(s, n_p - 1), 0)),
                  pl.BlockSpec((tm, D), lambda s: (jnp.maximum(s - n_p, 0), 0)),
                  pl.BlockSpec((1, 6, D), lambda s: (_batch_of_tile(s, groups, tm), 0, 0)),
                  pl.BlockSpec((1, D), lambda s: (0, 0)),
                  pl.BlockSpec((D, D_IN_PROJ), lambda s: (0, 0))],
        out_specs=pl.BlockSpec((tm, D_IN_PROJ), lambda s: (s, 0)),
        out_shape=jax.ShapeDtypeStruct(((n_p + n_s) * tm, D_IN_PROJ), BF16),
        compiler_params=_cparams("arbitrary"),
        name="in_proj",
    )(xp, xs, mod, pre_mix.reshape(1, D), w_in_bf16)


def _na_bias_table(rpb):
    qc = np.arange(GRID_W)[:, None]
    kc = np.arange(GRID_W)[None, :]
    win_start = np.clip(qc - NA_WIN_C // 2, 0, GRID_W - NA_WIN_C)
    valid = (kc >= win_start) & (kc < win_start + NA_WIN_C)
    dc = np.clip(kc - qc, -(NA_WIN_C - 1), NA_WIN_C - 1) + NA_WIN_C - 1
    pick = (dc[None] == np.arange(2 * NA_WIN_C - 1)[:, None, None]).astype(np.float32)
    full = jnp.einsum('hrc,cqk->hrqk', rpb.astype(F32), pick,
                      precision=lax.Precision.HIGHEST)
    full = jnp.where(valid[None, None], full.astype(F32), NEG_INF)
    two = jnp.concatenate([full[:, :-1], full[:, 1:]], axis=-1)
    two = two.reshape(N_HEADS_NA // 2, 2, 2 * NA_WIN_R - 2, GRID_W, 2 * GRID_W)
    return jnp.transpose(two, (0, 2, 1, 3, 4)).reshape(
        N_HEADS_NA // 2, 2 * NA_WIN_R - 2, 2 * GRID_W, 2 * GRID_W)


def _na_kernel(q_ref, kp_ref, kc_ref, kn_ref, vp_ref, vc_ref, vn_ref, bias_ref, gn_ref, o_ref,
               k_scr, v_scr, *, groups):
    s = pl.program_id(0)
    R = NA_ROWS_PER_STEP
    blk = R * GRID_W
    seq_start, seq_blocks = _seq_bounds(s, groups, blk)
    i = s - seq_start
    rows = seq_blocks * R
    k_scr[0:blk] = kp_ref[...]
    k_scr[blk:2 * blk] = kc_ref[...]
    k_scr[2 * blk:3 * blk] = kn_ref[...]
    v_scr[0:blk] = vp_ref[...]
    v_scr[blk:2 * blk] = vc_ref[...]
    v_scr[2 * blk:3 * blk] = vn_ref[...]
    lane = lax.broadcasted_iota(jnp.int32, (1, LANES), 1)
    lo = lane < HEAD_DIM
    gn = gn_ref[...]

    n_pairs = N_HEADS_NA // 2
    n_keys = NA_WIN_R * GRID_W

    def rows_body(jj, carry):
        js = [jj * NA_ROW_UNROLL + u for u in range(NA_ROW_UNROLL)]
        geo = []
        for j in js:
            r = i * R + j
            row_start = jnp.clip(r - NA_WIN_R // 2, 0, rows - NA_WIN_R)
            geo.append((r - row_start,
                        pl.multiple_of((row_start - i * R + R) * GRID_W, GRID_W),
                        pl.multiple_of(j * GRID_W, GRID_W)))
        scores = []
        for shift, koff, qoff in geo:
            for p in range(n_pairs):
                cs = slice(p * LANES, (p + 1) * LANES)
                q2 = q_ref[pl.ds(qoff, GRID_W), cs]
                zero = jnp.zeros_like(q2)
                qq = jnp.concatenate([jnp.where(lo, q2, zero), jnp.where(lo, zero, q2)], axis=0)
                k2 = k_scr[pl.ds(koff, n_keys), cs]
                scores.append(lax.dot_general(qq, k2, (((1,), (1,)), ((), ())),
                                              preferred_element_type=F32))
        probs = []
        for u, (shift, koff, qoff) in enumerate(geo):
            for p in range(n_pairs):
                sc = scores[u * n_pairs + p]
                parts = []
                for m in range(NA_WIN_R // 2):
                    d = 2 * m - shift + (NA_WIN_R - 1)
                    parts.append(sc[:, m * LANES:(m + 1) * LANES] + bias_ref[p, d])
                sc = jnp.concatenate(parts, axis=-1)
                e = jnp.exp(sc - jnp.max(sc, axis=-1, keepdims=True))
                probs.append((e.astype(BF16), jnp.sum(e, axis=-1, keepdims=True)))
        for u, (shift, koff, qoff) in enumerate(geo):
            outs = []
            for p in range(n_pairs):
                e, l = probs[u * n_pairs + p]
                v2 = v_scr[pl.ds(koff, n_keys), p * LANES:(p + 1) * LANES]
                o2 = jnp.dot(e, v2, preferred_element_type=F32) / l
                outs.append(jnp.where(lo, o2[:GRID_W], o2[GRID_W:]))
            o = jnp.concatenate(outs, axis=-1)
            o_ref[pl.ds(qoff, GRID_W), :] = _rms(o, gn).astype(BF16)
        return carry

    lax.fori_loop(0, R // NA_ROW_UNROLL, rows_body, 0)


def _na_attention(proj, bias_tab, gn_na, groups):
    n_tok = proj.shape[0]
    blk = NA_ROWS_PER_STEP * GRID_W

    def kv(col, delta):
        def index_map(s):
            start, length = _seq_bounds(s, groups, blk)
            return (jnp.clip(s + delta, start, start + length - 1), col)
        return pl.BlockSpec((blk, D_NA), index_map)

    return pl.pallas_call(
        functools.partial(_na_kernel, groups=groups),
        grid=(n_tok // blk,),
        in_specs=[pl.BlockSpec((blk, D_NA), lambda s: (s, 0)),
                  kv(1, -1), kv(1, 0), kv(1, 1), kv(2, -1), kv(2, 0), kv(2, 1),
                  pl.BlockSpec(bias_tab.shape, lambda s: (0, 0, 0, 0)),
                  pl.BlockSpec((1, D_NA), lambda s: (0, 0))],
        out_specs=pl.BlockSpec((blk, D_NA), lambda s: (s, 0)),
        out_shape=jax.ShapeDtypeStruct((n_tok, D_NA), BF16),
        scratch_shapes=[pltpu.VMEM((3 * blk, D_NA), BF16), pltpu.VMEM((3 * blk, D_NA), BF16)],
        compiler_params=_cparams("arbitrary"),
        name="na_attn",
    )(proj, proj, proj, proj, proj, proj, proj, bias_tab, gn_na.reshape(1, D_NA))


def _wa_slab_order(a, axis):
    shape = a.shape
    split = shape[:axis] + (N_KV_WA, GQA_GROUP, HEAD_DIM) + shape[axis + 1:]
    return jnp.swapaxes(a.reshape(split), axis, axis + 1).reshape(shape)


def _t5_bucket(rel):
    half = T5_BUCKETS // 2
    ret = (rel > 0).astype(jnp.int32) * half
    n = jnp.abs(rel)
    nf = jnp.maximum(n, 1).astype(F32)
    large = T5_MAX_EXACT + (jnp.log(nf / T5_MAX_EXACT) / math.log(T5_MAX_DIST / T5_MAX_EXACT)
                            * (half - T5_MAX_EXACT)).astype(jnp.int32)
    large = jnp.minimum(large, half - 1)
    return ret + jnp.where(n < T5_MAX_EXACT, n, large)


def _wa_bias_table(t5_table):
    rel = jnp.arange(3 * WA_BLOCK)[None, :] - WA_BLOCK - jnp.arange(WA_BLOCK)[:, None]
    pick = (_t5_bucket(rel)[..., None] == jnp.arange(T5_BUCKETS)).astype(F32)
    bias = jnp.einsum('qkb,bh->qkh', pick, t5_table.astype(F32),
                      precision=lax.Precision.HIGHEST)
    bias = jnp.where((jnp.abs(rel) <= WA_WINDOW)[..., None], bias, NEG_INF)
    bias = jnp.transpose(bias, (2, 0, 1))
    return jnp.concatenate([bias[:GQA_GROUP], bias[GQA_GROUP:]], axis=1)


def _wa_kernel(sink_ref, q_ref, kp_ref, kc_ref, kn_ref, vp_ref, vc_ref, vn_ref, bias_ref, gn_ref,
               o_ref, *, groups):
    s = pl.program_id(0)
    W = WA_BLOCK
    seq_start, seq_blocks = _seq_bounds(s, groups, W)
    lane = lax.broadcasted_iota(jnp.int32, (1, LANES), 1)
    lo = lane < HEAD_DIM
    row = lax.broadcasted_iota(jnp.int32, (2 * W, 1), 0)
    kcat = jnp.concatenate([kp_ref[...], kc_ref[...], kn_ref[...]], axis=0)
    vcat = jnp.concatenate([vp_ref[...], vc_ref[...], vn_ref[...]], axis=0)
    pen_prev = jnp.where(s == seq_start, NEG_INF, 0.0).astype(F32)
    pen_next = jnp.where(s == seq_start + seq_blocks - 1, NEG_INF, 0.0).astype(F32)
    col = lax.broadcasted_iota(jnp.int32, (1, 3 * W), 1)
    pen = jnp.where(col < W, pen_prev, jnp.where(col >= 2 * W, pen_next, 0.0))
    scores = []
    for p in range(GQA_GROUP):
        q2 = q_ref[:, p * LANES:(p + 1) * LANES]
        zero = jnp.zeros_like(q2)
        qq = jnp.concatenate([jnp.where(lo, q2, zero), jnp.where(lo, zero, q2)], axis=0)
        scores.append(lax.dot_general(qq, kcat, (((1,), (1,)), ((), ())),
                                      preferred_element_type=F32))
    probs = []
    for p in range(GQA_GROUP):
        sc = scores[p] + bias_ref[p] + pen
        sink = jnp.where(row < W, sink_ref[p], sink_ref[p + GQA_GROUP])
        mx = jnp.maximum(jnp.max(sc, axis=-1, keepdims=True), sink)
        e = jnp.exp(sc - mx)
        probs.append((e.astype(BF16), jnp.sum(e, axis=-1, keepdims=True) + jnp.exp(sink - mx)))
    outs = []
    for e, l in probs:
        o2 = jnp.dot(e, vcat, preferred_element_type=F32) / l
        outs.append(jnp.where(lo, o2[:W], o2[W:]))
    o = jnp.concatenate(outs, axis=-1)
    o_ref[...] = _rms(o, gn_ref[...]).astype(BF16)


def _wa_attention(proj, bias_tab, sink, gn_wa_perm, groups):
    n_tok = proj.shape[0]
    W = WA_BLOCK
    q_col = 3 * D_NA // D_WA
    k_col = (3 * D_NA + D_WA) // D_KV_WA
    v_col = k_col + 1

    def kv(col, delta):
        def index_map(s):
            start, length = _seq_bounds(s, groups, W)
            return (jnp.clip(s + delta, start, start + length - 1), col)
        return pl.BlockSpec((W, D_KV_WA), index_map)

    return pl.pallas_call(
        functools.partial(_wa_kernel, groups=groups),
        grid=(n_tok // W,),
        in_specs=[pl.BlockSpec(memory_space=pltpu.SMEM),
                  pl.BlockSpec((W, D_WA), lambda s: (s, q_col)),
                  kv(k_col, -1), kv(k_col, 0), kv(k_col, 1),
                  kv(v_col, -1), kv(v_col, 0), kv(v_col, 1),
                  pl.BlockSpec(bias_tab.shape, lambda s: (0, 0, 0)),
                  pl.BlockSpec((1, D_WA), lambda s: (0, 0))],
        out_specs=pl.BlockSpec((W, D_WA), lambda s: (s, 0)),
        out_shape=jax.ShapeDtypeStruct((n_tok, D_WA), BF16),
        compiler_params=_cparams("arbitrary"),
        name="wa_attn",
    )(sink, proj, proj, proj, proj, proj, proj, proj, bias_tab, gn_wa_perm.reshape(1, D_WA))


def _postmix_kernel(na_ref, wa_ref, xp_ref, xs_ref, mod_ref, wo_ref, pm_ref, pf_ref, wr_ref, br_ref,
                    tri_ref, x1_ref, h2_ref, idx_ref, gate_ref, rank_ref, cnt_ref, carry, *, n_p):
    s = pl.program_id(0)

    @pl.when(s == 0)
    def _():
        carry[...] = jnp.zeros_like(carry)

    def body(x_ref):
        a = jnp.concatenate([na_ref[...], wa_ref[...]], axis=-1)
        mixed = jnp.dot(a, wo_ref[...], preferred_element_type=F32)
        x1 = x_ref[...] + mod_ref[0, 2:3, :] * _rms(mixed, pm_ref[...])
        x1_ref[...] = x1
        h2 = _rms(x1, pf_ref[...]) * (1.0 + mod_ref[0, 4:5, :]) + mod_ref[0, 3:4, :]
        for c in range(ROW_TILE):
            h2_ref[pl.ds(c, TOKEN_TILE, stride=ROW_TILE), :] = h2[:, c * LANES:(c + 1) * LANES]
        logits = lax.dot_general(wr_ref[...], h2, (((1,), (1,)), ((), ())),
                                 preferred_element_type=F32,
                                 precision=lax.Precision.HIGHEST) + br_ref[...]
        eidx = lax.broadcasted_iota(jnp.int32, logits.shape, 0)
        tops, sels = [], []
        l = logits
        for _ in range(TOP_K):
            m = jnp.max(l, axis=0, keepdims=True)
            sel = jnp.min(jnp.where(l == m, eidx, N_EXPERTS), axis=0, keepdims=True)
            tops.append(m)
            sels.append(sel)
            l = jnp.where(eidx == sel, -jnp.inf, l)
        es = [jnp.exp(t - tops[0]) for t in tops]
        tot = es[0] + es[1] + es[2] + es[3]
        idx_ref[0] = jnp.concatenate(sels, axis=0)
        gate_ref[0] = jnp.concatenate([e / tot for e in es], axis=0)
        hits = [eidx == sel for sel in sels]
        onehot = sum(h.astype(F32) for h in hits)
        prefix = jnp.dot(onehot.astype(BF16), tri_ref[...], preferred_element_type=F32)
        rank_e = carry[...] + prefix - onehot
        rank_ref[0] = jnp.concatenate(
            [jnp.sum(jnp.where(h, rank_e, 0.0), axis=0, keepdims=True) for h in hits],
            axis=0).astype(jnp.int32)
        carry[...] = carry[...] + jnp.sum(onehot, axis=1, keepdims=True)
        cnt_ref[...] = carry[...]

    pl.when(s < n_p)(lambda: body(xp_ref))
    pl.when(s >= n_p)(lambda: body(xs_ref))


def _post_mix(na, wa, xp, xs, mod, w_out_bf16, post_mix, pre_ffn, w_router_t, b_router, groups):
    D = D_MODEL
    tm = TOKEN_TILE
    n_p, n_s = xp.shape[0] // tm, xs.shape[0] // tm
    nt = n_p + n_s
    tri = (np.arange(tm)[:, None] <= np.arange(tm)[None, :]).astype(np.float32)
    tok = lambda s: (s, 0)
    const2 = lambda s: (0, 0)
    tile3 = lambda s: (s, 0, 0)
    return pl.pallas_call(
        functools.partial(_postmix_kernel, n_p=n_p),
        grid=(nt,),
        in_specs=[pl.BlockSpec((tm, D_NA), tok),
                  pl.BlockSpec((tm, D_WA), tok),
                  pl.BlockSpec((tm, D), lambda s: (jnp.minimum(s, n_p - 1), 0)),
                  pl.BlockSpec((tm, D), lambda s: (jnp.maximum(s - n_p, 0), 0)),
                  pl.BlockSpec((1, 6, D), lambda s: (_batch_of_tile(s, groups, tm), 0, 0)),
                  pl.BlockSpec((D, D), const2),
                  pl.BlockSpec((1, D), const2),
                  pl.BlockSpec((1, D), const2),
                  pl.BlockSpec((N_EXPERTS, D), const2),
                  pl.BlockSpec((N_EXPERTS, 1), const2),
                  pl.BlockSpec((tm, tm), const2)],
        out_specs=[pl.BlockSpec((tm, D), tok),
                   pl.BlockSpec((tm * ROW_TILE, LANES), tok),
                   pl.BlockSpec((1, TOP_K, tm), tile3),
                   pl.BlockSpec((1, TOP_K, tm), tile3),
                   pl.BlockSpec((1, TOP_K, tm), tile3),
                   pl.BlockSpec((N_EXPERTS, 1), const2)],
        out_shape=[jax.ShapeDtypeStruct((nt * tm, D), F32),
                   jax.ShapeDtypeStruct((nt * tm * ROW_TILE, LANES), F32),
                   jax.ShapeDtypeStruct((nt, TOP_K, tm), jnp.int32),
                   jax.ShapeDtypeStruct((nt, TOP_K, tm), F32),
                   jax.ShapeDtypeStruct((nt, TOP_K, tm), jnp.int32),
                   jax.ShapeDtypeStruct((N_EXPERTS, 1), F32)],
        scratch_shapes=[pltpu.VMEM((N_EXPERTS, 1), F32)],
        compiler_params=_cparams("arbitrary"),
        name="post_mix",
    )(na, wa, xp, xs, mod, w_out_bf16, post_mix.reshape(1, D), pre_ffn.reshape(1, D),
      w_router_t, b_router.reshape(N_EXPERTS, 1), jnp.asarray(tri, BF16))


def _expert_kernel(blk_e_ref, src_hbm, dst_hbm, h_hbm, wgu_ref, bgu_ref, wd_ref, bd_ref, y_hbm,
                   src_smem, dst_smem, xbuf, xb, act_bf, ob0, ob1, ob2, wgu_bf, wd_bf,
                   idx_sem, g_sem, s_sem, *, n_blocks):
    i = pl.program_id(0)
    BM = EXPERT_BLOCK
    RT = ROW_TILE
    last = n_blocks - 1
    obufs = (ob0, ob1, ob2)

    def idx_copies(blk, slot):
        return (pltpu.make_async_copy(src_hbm.at[blk], src_smem.at[slot], idx_sem.at[slot]),
                pltpu.make_async_copy(dst_hbm.at[blk], dst_smem.at[slot], idx_sem.at[slot]))

    def start_idx(blk, slot):
        for c in idx_copies(blk, slot):
            c.start()

    def wait_idx(blk, slot):
        for c in idx_copies(blk, slot):
            c.wait()

    def paced(row, prev, rounds):
        z = prev
        for _ in range(rounds):
            z = z + lax.shift_right_arithmetic(z, 31)
        return row + lax.shift_right_arithmetic(z, 31)

    def start_gather(islot, xslot, rounds):
        prev = src_smem[islot, 0, 0]
        for r in range(BM):
            prev = paced(src_smem[islot, 0, r], prev, rounds)
            row = pl.multiple_of(prev, RT)
            pltpu.make_async_copy(h_hbm.at[pl.ds(row, RT), :], xbuf.at[xslot, pl.ds(r * RT, RT), :],
                                  g_sem.at[xslot]).start()

    def wait_gather(xslot):
        pltpu.make_async_copy(h_hbm.at[pl.ds(0, BM * RT), :], xbuf.at[xslot], g_sem.at[xslot]).wait()

    def start_scatter(islot, k, rounds):
        prev = dst_smem[islot, 0, 0]
        for r in range(BM):
            prev = paced(dst_smem[islot, 0, r], prev, rounds)
            row = pl.multiple_of(prev, RT)
            pltpu.make_async_copy(obufs[k].at[pl.ds(r * RT, RT), :], y_hbm.at[pl.ds(row, RT), :],
                                  s_sem.at[k]).start(priority=r % 2)

    def wait_scatter(k):
        pltpu.make_async_copy(obufs[k], y_hbm.at[pl.ds(0, BM * RT), :], s_sem.at[k]).wait()

    xs = lax.rem(i, X_SLOTS)

    @pl.when(i == 0)
    def _():
        for b in range(min(3, n_blocks)):
            start_idx(b, b)
        for b in range(min(2, n_blocks)):
            wait_idx(b, b)
            start_gather(b, b, 0)

    @pl.when(i + 3 < n_blocks)
    def _():
        start_idx(i + 3, lax.rem(i + 3, IDX_SLOTS))

    @pl.when(i + 2 < n_blocks)
    def _():
        wait_idx(i + 2, lax.rem(i + 2, IDX_SLOTS))

    wait_gather(xs)

    new_expert = jnp.logical_or(i == 0, blk_e_ref[i] != blk_e_ref[jnp.maximum(i - 1, 0)])

    @pl.when(new_expert)
    def _():
        wgu_bf[...] = wgu_ref[0].astype(BF16)
        wd_bf[...] = wd_ref[0].astype(BF16)

    def up_phase():
        for c in range(RT):
            xb[:, c * LANES:(c + 1) * LANES] = xbuf[xs, pl.ds(c, BM, stride=RT), :].astype(BF16)
        start_gather(lax.rem(jnp.minimum(i + 2, last), IDX_SLOTS), lax.rem(i + 2, X_SLOTS),
                     GATHER_PACING)
        gu = jnp.dot(xb[...], wgu_bf[...], preferred_element_type=F32) + bgu_ref[0]
        glu = jnp.minimum(gu[:, :D_EXPERT], SWIGLU_LIMIT)
        lin = jnp.clip(gu[:, D_EXPERT:], -SWIGLU_LIMIT, SWIGLU_LIMIT)
        act = glu * (1.0 / (1.0 + jnp.exp(-SWIGLU_ALPHA * glu))) * (lin + 1.0)
        act_bf[...] = act.astype(BF16)

    def down_phase(k, scatter_prev):
        if scatter_prev:
            start_scatter(lax.rem(i - 1, IDX_SLOTS), (k - 1) % O_SLOTS, SCATTER_PACING)
        y = jnp.dot(act_bf[...], wd_bf[...], preferred_element_type=F32) + bd_ref[0]
        for c in range(RT):
            obufs[k][pl.ds(c, BM, stride=RT), :] = y[:, c * LANES:(c + 1) * LANES]

    pl.when(i >= 0)(up_phase)
    for k in range(O_SLOTS):
        @pl.when(jnp.logical_and(i >= O_SLOTS, lax.rem(i, O_SLOTS) == k))
        def _():
            wait_scatter(k)
    pl.when(i == 0)(lambda: down_phase(0, False))
    for k in range(O_SLOTS):
        pl.when(jnp.logical_and(i > 0, lax.rem(i, O_SLOTS) == k))(
            functools.partial(down_phase, k, True))

    @pl.when(i == last)
    def _():
        start_scatter(last % IDX_SLOTS, last % O_SLOTS, 0)
        for b in range(max(0, last - O_SLOTS + 1), last + 1):
            wait_scatter(b % O_SLOTS)
        wait_gather(lax.rem(i + 1, X_SLOTS))
        wait_gather(lax.rem(i + 2, X_SLOTS))


def _experts(h2, src_row, dst_row, blk_e, w_gate_up, b_gate_up, w_down, b_down):
    D = D_MODEL
    BM = EXPERT_BLOCK
    n_blocks = blk_e.shape[0]
    wmap = lambda i, be: (be[i], 0, 0)
    grid_spec = pltpu.PrefetchScalarGridSpec(
        num_scalar_prefetch=1,
        grid=(n_blocks,),
        in_specs=[pl.BlockSpec(memory_space=pl.ANY),
                  pl.BlockSpec(memory_space=pl.ANY),
                  pl.BlockSpec(memory_space=pl.ANY),
                  pl.BlockSpec((1, D, 2 * D_EXPERT), wmap),
                  pl.BlockSpec((1, 1, 2 * D_EXPERT), wmap),
                  pl.BlockSpec((1, D_EXPERT, D), wmap),
                  pl.BlockSpec((1, 1, D), wmap)],
        out_specs=pl.BlockSpec(memory_space=pl.ANY),
        scratch_shapes=[pltpu.SMEM((IDX_SLOTS, 1, BM), jnp.int32),
                        pltpu.SMEM((IDX_SLOTS, 1, BM), jnp.int32),
                        pltpu.VMEM((X_SLOTS, BM * ROW_TILE, LANES), F32),
                        pltpu.VMEM((BM, D), BF16),
                        pltpu.VMEM((BM, D_EXPERT), BF16),
                        pltpu.VMEM((BM * ROW_TILE, LANES), F32),
                        pltpu.VMEM((BM * ROW_TILE, LANES), F32),
                        pltpu.VMEM((BM * ROW_TILE, LANES), F32),
                        pltpu.VMEM((D, 2 * D_EXPERT), BF16),
                        pltpu.VMEM((D_EXPERT, D), BF16),
                        pltpu.SemaphoreType.DMA((IDX_SLOTS,)),
                        pltpu.SemaphoreType.DMA((X_SLOTS,)),
                        pltpu.SemaphoreType.DMA((O_SLOTS,))],
    )
    return pl.pallas_call(
        functools.partial(_expert_kernel, n_blocks=n_blocks),
        grid_spec=grid_spec,
        out_shape=jax.ShapeDtypeStruct((n_blocks * BM * ROW_TILE, LANES), F32),
        compiler_params=_cparams("arbitrary"),
        name="experts",
    )(blk_e, src_row.reshape(n_blocks, 1, BM), dst_row.reshape(n_blocks, 1, BM), h2,
      w_gate_up, b_gate_up.reshape(N_EXPERTS, 1, -1), w_down, b_down.reshape(N_EXPERTS, 1, -1))


def _combine_kernel(y0_ref, y1_ref, y2_ref, y3_ref, gate_ref, x1_ref, mod_ref, pf_ref,
                    op_ref, os_ref, *, n_p):
    s = pl.program_id(0)
    g = gate_ref[...]

    def rows(y_ref):
        return jnp.concatenate([y_ref[pl.ds(c, COMBINE_TILE, stride=ROW_TILE), :]
                                for c in range(ROW_TILE)], axis=-1)

    m = rows(y0_ref) * g[:, 0:1]
    for k, y_ref in enumerate((y1_ref, y2_ref, y3_ref), start=1):
        m = m + rows(y_ref) * g[:, k:k + 1]
    res = x1_ref[...] + mod_ref[0, 5:6, :] * _rms(m, pf_ref[...])

    @pl.when(s < n_p)
    def _():
        op_ref[...] = res

    @pl.when(s >= n_p)
    def _():
        os_ref[...] = res


def _combine(y4, gates_tok, x1, mod, post_ffn, n_tok_p, n_tok_s, groups):
    D = D_MODEL
    tc = COMBINE_TILE
    n_p, n_s = n_tok_p // tc, n_tok_s // tc
    nt = n_p + n_s

    def slot(k):
        return pl.BlockSpec((tc * ROW_TILE, LANES), lambda s: (k * nt + s, 0))

    return pl.pallas_call(
        functools.partial(_combine_kernel, n_p=n_p),
        grid=(nt,),
        in_specs=[slot(0), slot(1), slot(2), slot(3),
                  pl.BlockSpec((tc, TOP_K), lambda s: (s, 0)),
                  pl.BlockSpec((tc, D), lambda s: (s, 0)),
                  pl.BlockSpec((1, 6, D), lambda s: (_batch_of_tile(s, groups, tc), 0, 0)),
                  pl.BlockSpec((1, D), lambda s: (0, 0))],
        out_specs=[pl.BlockSpec((tc, D), lambda s: (jnp.minimum(s, n_p - 1), 0)),
                   pl.BlockSpec((tc, D), lambda s: (jnp.maximum(s - n_p, 0), 0))],
        out_shape=[jax.ShapeDtypeStruct((n_tok_p, D), F32),
                   jax.ShapeDtypeStruct((n_tok_s, D), F32)],
        compiler_params=_cparams("arbitrary"),
        name="combine",
    )(y4, y4, y4, y4, gates_tok, x1, mod, post_ffn.reshape(1, D))


def _dispatch_kernel(npad_ref, nused_ref, dest_hbm, pad_hbm, h_ref, xs_hbm, idx_smem, pad_smem, zeros,
                     zero_blk, idx_sem, sem, *, n_steps, n_blocks):
    s = pl.program_id(0)
    tm = TOKEN_TILE
    RT = ROW_TILE
    slot = lax.rem(s, 2)

    def idx_copy(step, sl):
        return pltpu.make_async_copy(dest_hbm.at[step], idx_smem.at[sl], idx_sem.at[sl])

    @pl.when(s == 0)
    def _():
        idx_copy(0, 0).start()
        zeros[...] = jnp.zeros_like(zeros)
        cp = pltpu.make_async_copy(pad_hbm, pad_smem, idx_sem.at[1])
        cp.start()
        cp.wait()

        def pad_row(r):
            row = pl.multiple_of(pad_smem[0, r], RT)
            return pltpu.make_async_copy(zeros, xs_hbm.at[pl.ds(row, RT), :], sem)

        def issue(r, c):
            pad_row(r).start()
            return c

        def drain(r, c):
            pad_row(r).wait()
            return c

        lax.fori_loop(0, npad_ref[0], issue, 0)
        lax.fori_loop(0, npad_ref[0], drain, 0)

        zero_blk[...] = jnp.zeros_like(zero_blk)

        def blk_copy(b):
            start = pl.multiple_of(b * (EXPERT_BLOCK * RT), EXPERT_BLOCK * RT)
            return pltpu.make_async_copy(zero_blk, xs_hbm.at[pl.ds(start, EXPERT_BLOCK * RT), :], sem)

        def issue_blk(b, c):
            blk_copy(b).start()
            return c

        def drain_blk(b, c):
            blk_copy(b).wait()
            return c

        lax.fori_loop(nused_ref[0], n_blocks, issue_blk, 0)
        lax.fori_loop(nused_ref[0], n_blocks, drain_blk, 0)

    idx_copy(s, slot).wait()
    if n_steps > 1:
        @pl.when(s + 1 < n_steps)
        def _():
            idx_copy(s + 1, 1 - slot).start()

    for k in range(TOP_K):
        for j in range(tm):
            row = pl.multiple_of(idx_smem[slot, 0, k * tm + j], RT)
            pltpu.make_async_copy(h_ref.at[pl.ds(j * RT, RT), :], xs_hbm.at[pl.ds(row, RT), :],
                                  sem).start(priority=j % 2)
    for k in range(TOP_K):
        pltpu.make_async_copy(h_ref, xs_hbm.at[pl.ds(0, tm * RT), :], sem).wait()


def _dispatch(h2, dest_rows, pad_rows, n_pad, n_used, n_blocks):
    tm = TOKEN_TILE
    n_steps = h2.shape[0] // (tm * ROW_TILE)
    grid_spec = pltpu.PrefetchScalarGridSpec(
        num_scalar_prefetch=2,
        grid=(n_steps,),
        in_specs=[pl.BlockSpec(memory_space=pl.ANY),
                  pl.BlockSpec(memory_space=pl.ANY),
                  pl.BlockSpec((tm * ROW_TILE, LANES), lambda s, npad, nused: (s, 0))],
        out_specs=pl.BlockSpec(memory_space=pl.ANY),
        scratch_shapes=[pltpu.SMEM((2, 1, TOP_K * tm), jnp.int32),
                        pltpu.SMEM((1, pad_rows.shape[0]), jnp.int32),
                        pltpu.VMEM((ROW_TILE, LANES), F32),
                        pltpu.VMEM((EXPERT_BLOCK * ROW_TILE, LANES), F32),
                        pltpu.SemaphoreType.DMA((2,)),
                        pltpu.SemaphoreType.DMA],
    )
    return pl.pallas_call(
        functools.partial(_dispatch_kernel, n_steps=n_steps, n_blocks=n_blocks),
        grid_spec=grid_spec,
        out_shape=jax.ShapeDtypeStruct((n_blocks * EXPERT_BLOCK * ROW_TILE, LANES), F32),
        compiler_params=_cparams("arbitrary"),
        name="dispatch",
    )(n_pad, n_used, dest_rows.reshape(n_steps, 1, TOP_K * tm), pad_rows.reshape(1, -1), h2)


def _mlp_kernel(blk_e_ref, nused_ref, x_ref, wgu_ref, bgu_ref, wd_ref, bd_ref, o_ref, wgu_bf, wd_bf):
    i = pl.program_id(0)
    BM = EXPERT_BLOCK
    RT = ROW_TILE

    @pl.when(i < nused_ref[0])
    def _():
        new_expert = jnp.logical_or(i == 0, blk_e_ref[i] != blk_e_ref[jnp.maximum(i - 1, 0)])

        @pl.when(new_expert)
        def _():
            wgu_bf[...] = wgu_ref[0].astype(BF16)
            wd_bf[...] = wd_ref[0].astype(BF16)

        x = jnp.concatenate([x_ref[pl.ds(c, BM, stride=RT), :] for c in range(RT)],
                            axis=-1).astype(BF16)
        H = D_EXPERT // MLP_CHUNKS
        ups = []
        for h in range(MLP_CHUNKS):
            g = jnp.dot(x, wgu_bf[:, h * H:(h + 1) * H], preferred_element_type=F32)
            u = jnp.dot(x, wgu_bf[:, D_EXPERT + h * H:D_EXPERT + (h + 1) * H],
                        preferred_element_type=F32)
            ups.append((g + bgu_ref[0, :, h * H:(h + 1) * H],
                        u + bgu_ref[0, :, D_EXPERT + h * H:D_EXPERT + (h + 1) * H]))
        y = bd_ref[0]
        for h, (g, u) in enumerate(ups):
            glu = jnp.minimum(g, SWIGLU_LIMIT)
            lin = jnp.clip(u, -SWIGLU_LIMIT, SWIGLU_LIMIT)
            act = glu * (1.0 / (1.0 + jnp.exp(-SWIGLU_ALPHA * glu))) * (lin + 1.0)
            y = y + jnp.dot(act.astype(BF16), wd_bf[h * H:(h + 1) * H, :], preferred_element_type=F32)
        for c in range(RT):
            o_ref[pl.ds(c, BM, stride=RT), :] = y[:, c * LANES:(c + 1) * LANES]

    @pl.when(i >= nused_ref[0])
    def _():
        o_ref[...] = jnp.zeros_like(o_ref)


def _expert_mlp(xs, blk_e, n_used, w_gate_up, b_gate_up, w_down, b_down):
    D = D_MODEL
    BM = EXPERT_BLOCK
    n_blocks = blk_e.shape[0]
    wmap = lambda i, be, nu: (be[i], 0, 0)
    grid_spec = pltpu.PrefetchScalarGridSpec(
        num_scalar_prefetch=2,
        grid=(n_blocks,),
        in_specs=[pl.BlockSpec((BM * ROW_TILE, LANES), lambda i, be, nu: (jnp.minimum(i, nu[0] - 1), 0)),
                  pl.BlockSpec((1, D, 2 * D_EXPERT), wmap),
                  pl.BlockSpec((1, 1, 2 * D_EXPERT), wmap),
                  pl.BlockSpec((1, D_EXPERT, D), wmap),
                  pl.BlockSpec((1, 1, D), wmap)],
        out_specs=pl.BlockSpec((BM * ROW_TILE, LANES), lambda i, be, nu: (i, 0)),
        scratch_shapes=[pltpu.VMEM((D, 2 * D_EXPERT), BF16),
                        pltpu.VMEM((D_EXPERT, D), BF16)],
    )
    return pl.pallas_call(
        _mlp_kernel,
        grid_spec=grid_spec,
        out_shape=jax.ShapeDtypeStruct(xs.shape, F32),
        compiler_params=_cparams("arbitrary"),
        name="expert_mlp",
    )(blk_e, n_used, xs, w_gate_up, b_gate_up.reshape(N_EXPERTS, 1, -1), w_down,
      b_down.reshape(N_EXPERTS, 1, -1))


def _gather_combine_kernel(dest_hbm, ys_hbm, gate_ref, x1_ref, mod_ref, pf_ref, op_ref, os_ref,
                           idx_smem, yb0, yb1, idx_sem, row_sem, *, n_steps, n_p):
    s = pl.program_id(0)
    tc = COMBINE_TILE
    RT = ROW_TILE
    n_rows = TOP_K * tc
    last = n_steps - 1
    ybufs = (yb0, yb1)
    n_idx = 3

    def idx_copy(step, sl):
        return pltpu.make_async_copy(dest_hbm.at[step], idx_smem.at[sl], idx_sem.at[sl])

    def start_rows(islot, b):
        for r in range(n_rows):
            row = pl.multiple_of(idx_smem[islot, 0, r], RT)
            pltpu.make_async_copy(ys_hbm.at[pl.ds(row, RT), :], ybufs[b].at[pl.ds(r * RT, RT), :],
                                  row_sem.at[b]).start(priority=r % 2)

    def wait_rows(b):
        pltpu.make_async_copy(ys_hbm.at[pl.ds(0, n_rows * RT), :], ybufs[b], row_sem.at[b]).wait()

    @pl.when(s == 0)
    def _():
        for t in range(min(2, n_steps)):
            idx_copy(t, t).start()
        idx_copy(0, 0).wait()
        start_rows(0, 0)

    @pl.when(s + 2 < n_steps)
    def _():
        idx_copy(s + 2, lax.rem(s + 2, n_idx)).start()

    @pl.when(s + 1 < n_steps)
    def _():
        idx_copy(s + 1, lax.rem(s + 1, n_idx)).wait()

    def body(cur):
        wait_rows(cur)
        start_rows(lax.rem(jnp.minimum(s + 1, last), n_idx), 1 - cur)
        g = gate_ref[...]

        def rows(k):
            return jnp.concatenate([ybufs[cur][pl.ds(k * tc * RT + c, tc, stride=RT), :]
                                    for c in range(RT)], axis=-1)

        m = rows(0) * g[:, 0:1]
        for k in range(1, TOP_K):
            m = m + rows(k) * g[:, k:k + 1]
        res = x1_ref[...] + mod_ref[0, 5:6, :] * _rms(m, pf_ref[...])

        @pl.when(s < n_p)
        def _():
            op_ref[...] = res

        @pl.when(s >= n_p)
        def _():
            os_ref[...] = res

    for cur in range(2):
        pl.when(lax.rem(s, 2) == cur)(functools.partial(body, cur))

    @pl.when(s == last)
    def _():
        wait_rows(1 - last % 2)


def _gather_combine(dest_rows, ys, gates_tok, x1, mod, post_ffn, n_tok_p, n_tok_s, groups):
    D = D_MODEL
    tc = COMBINE_TILE
    n_p, n_s = n_tok_p // tc, n_tok_s // tc
    nt = n_p + n_s
    return pl.pallas_call(
        functools.partial(_gather_combine_kernel, n_steps=nt, n_p=n_p),
        grid=(nt,),
        in_specs=[pl.BlockSpec(memory_space=pl.ANY),
                  pl.BlockSpec(memory_space=pl.ANY),
                  pl.BlockSpec((tc, TOP_K), lambda s: (s, 0)),
                  pl.BlockSpec((tc, D), lambda s: (s, 0)),
                  pl.BlockSpec((1, 6, D), lambda s: (_batch_of_tile(s, groups, tc), 0, 0)),
                  pl.BlockSpec((1, D), lambda s: (0, 0))],
        out_specs=[pl.BlockSpec((tc, D), lambda s: (jnp.minimum(s, n_p - 1), 0)),
                   pl.BlockSpec((tc, D), lambda s: (jnp.maximum(s - n_p, 0), 0))],
        out_shape=[jax.ShapeDtypeStruct((n_tok_p, D), F32),
                   jax.ShapeDtypeStruct((n_tok_s, D), F32)],
        scratch_shapes=[pltpu.SMEM((3, 1, TOP_K * tc), jnp.int32),
                        pltpu.VMEM((TOP_K * tc * ROW_TILE, LANES), F32),
                        pltpu.VMEM((TOP_K * tc * ROW_TILE, LANES), F32),
                        pltpu.SemaphoreType.DMA((3,)),
                        pltpu.SemaphoreType.DMA((2,))],
        compiler_params=_cparams("arbitrary"),
        name="combine",
    )(dest_rows.reshape(nt, 1, TOP_K * tc), ys, gates_tok, x1, mod, post_ffn.reshape(1, D))


def _routing_dest(idx_t, rank_t, counts):
    BM = EXPERT_BLOCK
    nt, _, tm = idx_t.shape
    n_blocks = (nt * tm * TOP_K) // BM + N_EXPERTS
    padded = ((counts + BM - 1) // BM) * BM
    pad_end = jnp.cumsum(padded)
    pad_start = pad_end - padded
    experts = jnp.arange(N_EXPERTS, dtype=jnp.int32)
    start_of = jnp.sum(jnp.where(idx_t[..., None] == experts, pad_start, 0), axis=-1)
    dest = (start_of + rank_t).astype(jnp.int32)
    blk_first = jnp.arange(n_blocks, dtype=jnp.int32) * BM
    blk_e = jnp.minimum(jnp.sum((pad_end[None, :] <= blk_first[:, None]).astype(jnp.int32), axis=1),
                        N_EXPERTS - 1)
    n_used = (pad_end[-1:] // BM).astype(jnp.int32)
    pads = padded - counts
    pads_end = jnp.cumsum(pads)
    pads_before = pads_end - pads
    i = jnp.arange(N_EXPERTS * BM, dtype=jnp.int32)[:, None]
    mine = jnp.logical_and(i >= pads_before, i < pads_end)
    pad_pos = jnp.sum(jnp.where(mine, pad_start + counts + i - pads_before, 0), axis=1).astype(jnp.int32)
    n_pad = pads_end[-1:].astype(jnp.int32)
    return dest, blk_e, n_used, pad_pos, n_pad, n_blocks


def _routing_plan(idx_t, rank_t, counts, n_tok):
    BM = EXPERT_BLOCK
    nt, _, tm = idx_t.shape
    n_blocks = (n_tok * TOP_K) // BM + N_EXPERTS
    padded = ((counts + BM - 1) // BM) * BM
    pad_end = jnp.cumsum(padded)
    pad_start = pad_end - padded
    experts = jnp.arange(N_EXPERTS, dtype=jnp.int32)
    start_of = jnp.sum(jnp.where(idx_t[..., None] == experts, pad_start, 0), axis=-1)
    dest = (start_of + rank_t).astype(jnp.int32)
    t_of = (jnp.arange(nt, dtype=jnp.int32)[:, None, None] * tm
            + jnp.arange(tm, dtype=jnp.int32)[None, None, :])
    slot_row = jnp.arange(TOP_K, dtype=jnp.int32)[None, :, None] * n_tok + t_of
    blk_first = jnp.arange(n_blocks, dtype=jnp.int32) * BM
    blk_e = jnp.minimum(jnp.sum((pad_end[None, :] <= blk_first[:, None]).astype(jnp.int32), axis=1),
                        N_EXPERTS - 1)
    of_blk = blk_e[:, None] == experts
    per_blk = lambda v: jnp.sum(jnp.where(of_blk, v, 0), axis=1)[:, None]
    pos = blk_first[:, None] + jnp.arange(BM, dtype=jnp.int32)[None, :]
    real_before = per_blk(jnp.cumsum(counts) - counts) + jnp.minimum(pos - per_blk(pad_start),
                                                                     per_blk(counts))
    spare = n_tok * TOP_K + pos - real_before
    inv = spare.reshape(-1).astype(jnp.int32).at[dest.reshape(-1)].set(
        slot_row.reshape(-1), unique_indices=True, mode='promise_in_bounds')
    src_row = jnp.where(inv < n_tok * TOP_K, inv % n_tok, 0) * ROW_TILE
    dst_row = inv * ROW_TILE
    return src_row, dst_row, blk_e


def kernel(x_prompt, x_sample, c_prompt, c_sample, w_ada, b_ada, pre_mix, post_mix, pre_ffn, post_ffn,
           w_in, na_rel_bias, t5_rel_bias, wa_sink, gn_na, gn_wa, w_out, w_router, b_router,
           w_gate_up, b_gate_up, w_down, b_down):
    depth = w_ada.shape[0]
    D = D_MODEL
    groups = (x_prompt.shape[:2], x_sample.shape[:2])
    n_tok_p = x_prompt.shape[0] * x_prompt.shape[1]
    n_tok_s = x_sample.shape[0] * x_sample.shape[1]
    n_tok = n_tok_p + n_tok_s
    yp = x_prompt.reshape(n_tok_p, D)
    ys = x_sample.reshape(n_tok_s, D)
    c_all = jnp.concatenate([c_prompt, c_sample], axis=0)
    for l in range(depth):
        mod = _ada(c_all, w_ada[l], b_ada[l]).reshape(c_all.shape[0], 6, D)
        qb0 = 3 * D_NA
        scale = HEAD_DIM ** -0.5
        w_in_l = jnp.concatenate([w_in[l][:, :D_NA] * scale, w_in[l][:, D_NA:qb0],
                                  _wa_slab_order(w_in[l][:, qb0:qb0 + D_WA] * scale, axis=1),
                                  w_in[l][:, qb0 + D_WA:]], axis=1).astype(BF16)
        w_out_l = jnp.concatenate([w_out[l][:D_NA], _wa_slab_order(w_out[l][D_NA:], axis=0)],
                                  axis=0).astype(BF16)
        proj = _in_proj(yp, ys, mod, pre_mix[l], w_in_l, groups)
        na = _na_attention(proj, _na_bias_table(na_rel_bias[l]), gn_na[l], groups)
        wa = _wa_attention(proj, _wa_bias_table(t5_rel_bias), wa_sink[l],
                           _wa_slab_order(gn_wa[l], axis=0), groups)
        x1, h2, idx_t, gate_t, rank_t, counts = _post_mix(
            na, wa, yp, ys, mod, w_out_l, post_mix[l], pre_ffn[l], w_router[l].T, b_router[l], groups)
        dest, blk_e, n_used, pad_pos, n_pad, n_blocks = _routing_dest(
            idx_t, rank_t, counts.reshape(N_EXPERTS).astype(jnp.int32))
        dest_rows = dest * ROW_TILE
        xs = _dispatch(h2, dest_rows, pad_pos * ROW_TILE, n_pad, n_used, n_blocks)
        ys_rows = _expert_mlp(xs, blk_e, n_used, w_gate_up[l], b_gate_up[l], w_down[l], b_down[l])
        gates_tok = jnp.transpose(gate_t, (0, 2, 1)).reshape(n_tok, TOP_K)
        halves = TOKEN_TILE // COMBINE_TILE
        dest_steps = jnp.transpose(dest_rows.reshape(-1, TOP_K, halves, COMBINE_TILE), (0, 2, 1, 3))
        yp, ys = _gather_combine(dest_steps, ys_rows, gates_tok, x1, mod, post_ffn[l], n_tok_p, n_tok_s,
                                 groups)
    return (yp.reshape(x_prompt.shape), ys.reshape(x_sample.shape))
```

```python
import functools
import math

import jax
import jax.numpy as jnp
import numpy as np
from jax import lax
from jax.experimental import pallas as pl
from jax.experimental.pallas import tpu as pltpu

F32 = jnp.float32
BF16 = jnp.bfloat16

D_MODEL = 1024
HEAD_DIM = 64
D_NA = 512
D_WA = 512
N_HEADS_NA = 8
N_HEADS_WA = 8
N_KV_WA = 2
GQA_GROUP = 4
D_KV_WA = 128
D_IN_PROJ = 3 * D_NA + D_WA + 2 * D_KV_WA
GRID_W = 64
NA_WIN_R = 8
NA_WIN_C = 16
WA_WINDOW = 128
WA_BLOCK = 128
T5_BUCKETS = 32
T5_MAX_EXACT = 8
T5_MAX_DIST = 128
N_EXPERTS = 32
TOP_K = 4
D_EXPERT = D_MODEL
SWIGLU_LIMIT = 7.0
SWIGLU_ALPHA = 1.702
RMS_EPS = 1e-6
NEG_INF = -1e30

VMEM_LIMIT_BYTES = 56 * 1024 * 1024
LANES = 128
ROW_TILE = D_MODEL // LANES

TOKEN_TILE = 512
POST_MIX_PARTS = 4
IN_PROJ_PARTS = 2
NA_ROWS_PER_STEP = 8
NA_ROW_UNROLL = 4
EXPERT_BLOCK = 512
MLP_CHUNKS = 2
X_SLOTS = 3
IDX_SLOTS = X_SLOTS + 2
O_SLOTS = 3
GATHER_PACING = 3
SCATTER_PACING = 0
COMBINE_TILE = 256


def _cparams(*sem):
    return pltpu.CompilerParams(dimension_semantics=sem, vmem_limit_bytes=VMEM_LIMIT_BYTES)


def _rms(x, g):
    return x * lax.rsqrt(jnp.mean(x * x, axis=-1, keepdims=True) + RMS_EPS) * g


def _seq_bounds(s, groups, block):
    base = 0
    start, length = None, None
    for n_seq, seq_len in groups:
        per = seq_len // block
        g_start = base + ((s - base) // per) * per
        if start is None:
            start, length = g_start, per
        else:
            inside = s >= base
            start = jnp.where(inside, g_start, start)
            length = jnp.where(inside, per, length)
        base += n_seq * per
    return start, length


def _batch_of_tile(s, groups, block):
    base_blk, base_seq = 0, 0
    out = None
    for n_seq, seq_len in groups:
        per = seq_len // block
        b = base_seq + (s - base_blk) // per
        out = b if out is None else jnp.where(s >= base_blk, b, out)
        base_blk += n_seq * per
        base_seq += n_seq
    return out


def _ada_kernel(c_ref, w_ref, b_ref, o_ref):
    c = c_ref[...]
    s = c * (1.0 / (1.0 + jnp.exp(-c)))
    o_ref[...] = jnp.dot(s, w_ref[...], preferred_element_type=F32,
                         precision=lax.Precision.HIGHEST) + b_ref[...]


def _ada(c_all, w_ada, b_ada):
    nb = c_all.shape[0]
    return pl.pallas_call(
        _ada_kernel,
        grid=(6,),
        in_specs=[pl.BlockSpec((nb, D_MODEL), lambda j: (0, 0)),
                  pl.BlockSpec((D_MODEL, D_MODEL), lambda j: (0, j)),
                  pl.BlockSpec((1, D_MODEL), lambda j: (0, j))],
        out_specs=pl.BlockSpec((nb, D_MODEL), lambda j: (0, j)),
        out_shape=jax.ShapeDtypeStruct((nb, 6 * D_MODEL), F32),
        compiler_params=_cparams("arbitrary"),
        name="ada",
    )(c_all, w_ada, b_ada.reshape(1, -1))


def _inproj_kernel(xp_ref, xs_ref, mod_ref, g_ref, w_ref, o_ref, *, n_p):
    s = pl.program_id(0)

    def body(x_ref):
        H = TOKEN_TILE // IN_PROJ_PARTS
        hs = [(_rms(x_ref[p * H:(p + 1) * H], g_ref[...]) * (1.0 + mod_ref[0, 1:2, :])
               + mod_ref[0, 0:1, :]).astype(BF16) for p in range(IN_PROJ_PARTS)]
        for p, h in enumerate(hs):
            o_ref[p * H:(p + 1) * H] = jnp.dot(h, w_ref[...], preferred_element_type=F32).astype(BF16)

    pl.when(s < n_p)(lambda: body(xp_ref))
    pl.when(s >= n_p)(lambda: body(xs_ref))


def _in_proj(xp, xs, mod, pre_mix, w_in_bf16, groups):
    D = D_MODEL
    tm = TOKEN_TILE
    n_p, n_s = xp.shape[0] // tm, xs.shape[0] // tm
    return pl.pallas_call(
        functools.partial(_inproj_kernel, n_p=n_p),
        grid=(n_p + n_s,),
        in_specs=[pl.BlockSpec((tm, D), lambda s: (jnp.minimum(s, n_p - 1), 0)),
                  pl.BlockSpec((tm, D), lambda s: (jnp.maximum(s - n_p, 0), 0)),
                  pl.BlockSpec((1, 6, D), lambda s: (_batch_of_tile(s, groups, tm), 0, 0)),
                  pl.BlockSpec((1, D), lambda s: (0, 0)),
                  pl.BlockSpec((D, D_IN_PROJ), lambda s: (0, 0))],
        out_specs=pl.BlockSpec((tm, D_IN_PROJ), lambda s: (s, 0)),
        out_shape=jax.ShapeDtypeStruct(((n_p + n_s) * tm, D_IN_PROJ), BF16),
        compiler_params=_cparams("arbitrary"),
        name="in_proj",
    )(xp, xs, mod, pre_mix.reshape(1, D), w_in_bf16)


def _na_bias_table(rpb):
    qc = np.arange(GRID_W)[:, None]
    kc = np.arange(GRID_W)[None, :]
    win_start = np.clip(qc - NA_WIN_C // 2, 0, GRID_W - NA_WIN_C)
    valid = (kc >= win_start) & (kc < win_start + NA_WIN_C)
    dc = np.clip(kc - qc, -(NA_WIN_C - 1), NA_WIN_C - 1) + NA_WIN_C - 1
    pick = (dc[None] == np.arange(2 * NA_WIN_C - 1)[:, None, None]).astype(np.float32)
    full = jnp.einsum('hrc,cqk->hrqk', rpb.astype(F32), pick,
                      precision=lax.Precision.HIGHEST)
    full = jnp.where(valid[None, None], full.astype(F32), NEG_INF)
    two = jnp.concatenate([full[:, :-1], full[:, 1:]], axis=-1)
    two = two.reshape(N_HEADS_NA // 2, 2, 2 * NA_WIN_R - 2, GRID_W, 2 * GRID_W)
    return jnp.transpose(two, (0, 2, 1, 3, 4)).reshape(
        N_HEADS_NA // 2, 2 * NA_WIN_R - 2, 2 * GRID_W, 2 * GRID_W)


def _na_kernel(q_ref, kp_ref, kc_ref, kn_ref, vp_ref, vc_ref, vn_ref, bias_ref, gn_ref, o_ref,
               k_scr, v_scr, *, groups):
    s = pl.program_id(0)
    R = NA_ROWS_PER_STEP
    blk = R * GRID_W
    seq_start, seq_blocks = _seq_bounds(s, groups, blk)
    i = s - seq_start
    rows = seq_blocks * R
    k_scr[0:blk] = kp_ref[...]
    k_scr[blk:2 * blk] = kc_ref[...]
    k_scr[2 * blk:3 * blk] = kn_ref[...]
    v_scr[0:blk] = vp_ref[...]
    v_scr[blk:2 * blk] = vc_ref[...]
    v_scr[2 * blk:3 * blk] = vn_ref[...]
    lane = lax.broadcasted_iota(jnp.int32, (1, LANES), 1)
    lo = lane < HEAD_DIM
    gn = gn_ref[...]

    n_pairs = N_HEADS_NA // 2
    n_keys = NA_WIN_R * GRID_W

    def rows_body(jj, carry):
        js = [jj * NA_ROW_UNROLL + u for u in range(NA_ROW_UNROLL)]
        geo = []
        for j in js:
            r = i * R + j
            row_start = jnp.clip(r - NA_WIN_R // 2, 0, rows - NA_WIN_R)
            geo.append((r - row_start,
                        pl.multiple_of((row_start - i * R + R) * GRID_W, GRID_W),
                        pl.multiple_of(j * GRID_W, GRID_W)))
        scores = []
        for shift, koff, qoff in geo:
            for p in range(n_pairs):
                cs = slice(p * LANES, (p + 1) * LANES)
                q2 = q_ref[pl.ds(qoff, GRID_W), cs]
                zero = jnp.zeros_like(q2)
                qq = jnp.concatenate([jnp.where(lo, q2, zero), jnp.where(lo, zero, q2)], axis=0)
                k2 = k_scr[pl.ds(koff, n_keys), cs]
                scores.append(lax.dot_general(qq, k2, (((1,), (1,)), ((), ())),
                                              preferred_element_type=F32))
        probs = []
        for u, (shift, koff, qoff) in enumerate(geo):
            for p in range(n_pairs):
                sc = scores[u * n_pairs + p]
                parts = []
                for m in range(NA_WIN_R // 2):
                    d = 2 * m - shift + (NA_WIN_R - 1)
                    parts.append(sc[:, m * LANES:(m + 1) * LANES] + bias_ref[p, d])
                sc = jnp.concatenate(parts, axis=-1)
                e = jnp.exp(sc - jnp.max(sc, axis=-1, keepdims=True))
                probs.append((e.astype(BF16), jnp.sum(e, axis=-1, keepdims=True)))
        for u, (shift, koff, qoff) in enumerate(geo):
            outs = []
            for p in range(n_pairs):
                e, l = probs[u * n_pairs + p]
                v2 = v_scr[pl.ds(koff, n_keys), p * LANES:(p + 1) * LANES]
                o2 = jnp.dot(e, v2, preferred_element_type=F32) / l
                outs.append(jnp.where(lo, o2[:GRID_W], o2[GRID_W:]))
            o = jnp.concatenate(outs, axis=-1)
            o_ref[pl.ds(qoff, GRID_W), :] = _rms(o, gn).astype(BF16)
        return carry

    lax.fori_loop(0, R // NA_ROW_UNROLL, rows_body, 0)


def _na_attention(proj, bias_tab, gn_na, groups):
    n_tok = proj.shape[0]
    blk = NA_ROWS_PER_STEP * GRID_W

    def kv(col, delta):
        def index_map(s):
            start, length = _seq_bounds(s, groups, blk)
            return (jnp.clip(s + delta, start, start + length - 1), col)
        return pl.BlockSpec((blk, D_NA), index_map)

    return pl.pallas_call(
        functools.partial(_na_kernel, groups=groups),
        grid=(n_tok // blk,),
        in_specs=[pl.BlockSpec((blk, D_NA), lambda s: (s, 0)),
                  kv(1, -1), kv(1, 0), kv(1, 1), kv(2, -1), kv(2, 0), kv(2, 1),
                  pl.BlockSpec(bias_tab.shape, lambda s: (0, 0, 0, 0)),
                  pl.BlockSpec((1, D_NA), lambda s: (0, 0))],
        out_specs=pl.BlockSpec((blk, D_NA), lambda s: (s, 0)),
        out_shape=jax.ShapeDtypeStruct((n_tok, D_NA), BF16),
        scratch_shapes=[pltpu.VMEM((3 * blk, D_NA), BF16), pltpu.VMEM((3 * blk, D_NA), BF16)],
        compiler_params=_cparams("arbitrary"),
        name="na_attn",
    )(proj, proj, proj, proj, proj, proj, proj, bias_tab, gn_na.reshape(1, D_NA))


def _wa_slab_order(a, axis):
    shape = a.shape
    split = shape[:axis] + (N_KV_WA, GQA_GROUP, HEAD_DIM) + shape[axis + 1:]
    return jnp.swapaxes(a.reshape(split), axis, axis + 1).reshape(shape)


def _t5_bucket(rel):
    half = T5_BUCKETS // 2
    ret = (rel > 0).astype(jnp.int32) * half
    n = jnp.abs(rel)
    nf = jnp.maximum(n, 1).astype(F32)
    large = T5_MAX_EXACT + (jnp.log(nf / T5_MAX_EXACT) / math.log(T5_MAX_DIST / T5_MAX_EXACT)
                            * (half - T5_MAX_EXACT)).astype(jnp.int32)
    large = jnp.minimum(large, half - 1)
    return ret + jnp.where(n < T5_MAX_EXACT, n, large)


def _wa_bias_table(t5_table):
    rel = jnp.arange(3 * WA_BLOCK)[None, :] - WA_BLOCK - jnp.arange(WA_BLOCK)[:, None]
    pick = (_t5_bucket(rel)[..., None] == jnp.arange(T5_BUCKETS)).astype(F32)
    bias = jnp.einsum('qkb,bh->qkh', pick, t5_table.astype(F32),
                      precision=lax.Precision.HIGHEST)
    bias = jnp.where((jnp.abs(rel) <= WA_WINDOW)[..., None], bias, NEG_INF)
    bias = jnp.transpose(bias, (2, 0, 1))
    return jnp.concatenate([bias[:GQA_GROUP], bias[GQA_GROUP:]], axis=1)


def _wa_kernel(sink_ref, q_ref, kp_ref, kc_ref, kn_ref, vp_ref, vc_ref, vn_ref, bias_ref, gn_ref,
               o_ref, *, groups):
    s = pl.program_id(0)
    W = WA_BLOCK
    seq_start, seq_blocks = _seq_bounds(s, groups, W)
    lane = lax.broadcasted_iota(jnp.int32, (1, LANES), 1)
    lo = lane < HEAD_DIM
    row = lax.broadcasted_iota(jnp.int32, (2 * W, 1), 0)
    kcat = jnp.concatenate([kp_ref[...], kc_ref[...], kn_ref[...]], axis=0)
    vcat = jnp.concatenate([vp_ref[...], vc_ref[...], vn_ref[...]], axis=0)
    pen_prev = jnp.where(s == seq_start, NEG_INF, 0.0).astype(F32)
    pen_next = jnp.where(s == seq_start + seq_blocks - 1, NEG_INF, 0.0).astype(F32)
    col = lax.broadcasted_iota(jnp.int32, (1, 3 * W), 1)
    pen = jnp.where(col < W, pen_prev, jnp.where(col >= 2 * W, pen_next, 0.0))
    scores = []
    for p in range(GQA_GROUP):
        q2 = q_ref[:, p * LANES:(p + 1) * LANES]
        zero = jnp.zeros_like(q2)
        qq = jnp.concatenate([jnp.where(lo, q2, zero), jnp.where(lo, zero, q2)], axis=0)
        scores.append(lax.dot_general(qq, kcat, (((1,), (1,)), ((), ())),
                                      preferred_element_type=F32))
    probs = []
    for p in range(GQA_GROUP):
        sc = scores[p] + bias_ref[p] + pen
        sink = jnp.where(row < W, sink_ref[p], sink_ref[p + GQA_GROUP])
        mx = jnp.maximum(jnp.max(sc, axis=-1, keepdims=True), sink)
        e = jnp.exp(sc - mx)
        probs.append((e.astype(BF16), jnp.sum(e, axis=-1, keepdims=True) + jnp.exp(sink - mx)))
    outs = []
    for e, l in probs:
        o2 = jnp.dot(e, vcat, preferred_element_type=F32) / l
        outs.append(jnp.where(lo, o2[:W], o2[W:]))
    o = jnp.concatenate(outs, axis=-1)
    o_ref[...] = _rms(o, gn_ref[...]).astype(BF16)


def _wa_attention(proj, bias_tab, sink, gn_wa_perm, groups):
    n_tok = proj.shape[0]
    W = WA_BLOCK
    q_col = 3 * D_NA // D_WA
    k_col = (3 * D_NA + D_WA) // D_KV_WA
    v_col = k_col + 1

    def kv(col, delta):
        def index_map(s):
            start, length = _seq_bounds(s, groups, W)
            return (jnp.clip(s + delta, start, start + length - 1), col)
        return pl.BlockSpec((W, D_KV_WA), index_map)

    return pl.pallas_call(
        functools.partial(_wa_kernel, groups=groups),
        grid=(n_tok // W,),
        in_specs=[pl.BlockSpec(memory_space=pltpu.SMEM),
                  pl.BlockSpec((W, D_WA), lambda s: (s, q_col)),
                  kv(k_col, -1), kv(k_col, 0), kv(k_col, 1),
                  kv(v_col, -1), kv(v_col, 0), kv(v_col, 1),
                  pl.BlockSpec(bias_tab.shape, lambda s: (0, 0, 0)),
                  pl.BlockSpec((1, D_WA), lambda s: (0, 0))],
        out_specs=pl.BlockSpec((W, D_WA), lambda s: (s, 0)),
        out_shape=jax.ShapeDtypeStruct((n_tok, D_WA), BF16),
        compiler_params=_cparams("arbitrary"),
        name="wa_attn",
    )(sink, proj, proj, proj, proj, proj, proj, proj, bias_tab, gn_wa_perm.reshape(1, D_WA))


def _postmix_kernel(na_ref, wa_ref, xp_ref, xs_ref, mod_ref, wo_ref, pm_ref, pf_ref, wr_ref, br_ref,
                    tri_ref, x1_ref, h2_ref, idx_ref, gate_ref, rank_ref, cnt_ref, carry, *, n_p):
    s = pl.program_id(0)

    @pl.when(s == 0)
    def _():
        carry[...] = jnp.zeros_like(carry)

    def body(x_ref):
        P = POST_MIX_PARTS
        H = TOKEN_TILE // P
        mixed = [jnp.dot(jnp.concatenate([na_ref[h * H:(h + 1) * H], wa_ref[h * H:(h + 1) * H]], axis=-1),
                         wo_ref[...], preferred_element_type=F32) for h in range(P)]
        h2s = []
        for h in range(P):
            x1 = x_ref[h * H:(h + 1) * H] + mod_ref[0, 2:3, :] * _rms(mixed[h], pm_ref[...])
            x1_ref[h * H:(h + 1) * H] = x1
            h2 = _rms(x1, pf_ref[...]) * (1.0 + mod_ref[0, 4:5, :]) + mod_ref[0, 3:4, :]
            for c in range(ROW_TILE):
                h2_ref[pl.ds(h * H * ROW_TILE + c, H, stride=ROW_TILE), :] = h2[:, c * LANES:(c + 1) * LANES]
            h2s.append(h2.astype(BF16))
        wr = wr_ref[...].astype(BF16)
        logits = jnp.concatenate(
            [lax.dot_general(wr, h2, (((1,), (1,)), ((), ())), preferred_element_type=F32)
             for h2 in h2s], axis=-1) + br_ref[...]
        eidx = lax.broadcasted_iota(jnp.int32, logits.shape, 0)
        tops, sels = [], []
        l = logits
        for _ in range(TOP_K):
            m = jnp.max(l, axis=0, keepdims=True)
            sel = jnp.min(jnp.where(l == m, eidx, N_EXPERTS), axis=0, keepdims=True)
            tops.append(m)
            sels.append(sel)
            l = jnp.where(eidx == sel, -jnp.inf, l)
        es = [jnp.exp(t - tops[0]) for t in tops]
        tot = es[0] + es[1] + es[2] + es[3]
        idx_ref[0] = jnp.concatenate(sels, axis=0)
        gate_ref[0] = jnp.concatenate([e / tot for e in es], axis=0)
        hits = [eidx == sel for sel in sels]
        onehot = sum(h.astype(F32) for h in hits)
        prefix = jnp.dot(onehot.astype(BF16), tri_ref[...], preferred_element_type=F32)
        rank_e = carry[...] + prefix - onehot
        rank_ref[0] = jnp.concatenate(
            [jnp.sum(jnp.where(h, rank_e, 0.0), axis=0, keepdims=True) for h in hits],
            axis=0).astype(jnp.int32)
        carry[...] = carry[...] + jnp.sum(onehot, axis=1, keepdims=True)
        cnt_ref[...] = carry[...]

    pl.when(s < n_p)(lambda: body(xp_ref))
    pl.when(s >= n_p)(lambda: body(xs_ref))


def _post_mix(na, wa, xp, xs, mod, w_out_bf16, post_mix, pre_ffn, w_router_t, b_router, groups):
    D = D_MODEL
    tm = TOKEN_TILE
    n_p, n_s = xp.shape[0] // tm, xs.shape[0] // tm
    nt = n_p + n_s
    tri = (np.arange(tm)[:, None] <= np.arange(tm)[None, :]).astype(np.float32)
    tok = lambda s: (s, 0)
    const2 = lambda s: (0, 0)
    tile3 = lambda s: (s, 0, 0)
    return pl.pallas_call(
        functools.partial(_postmix_kernel, n_p=n_p),
        grid=(nt,),
        in_specs=[pl.BlockSpec((tm, D_NA), tok),
                  pl.BlockSpec((tm, D_WA), tok),
                  pl.BlockSpec((tm, D), lambda s: (jnp.minimum(s, n_p - 1), 0)),
                  pl.BlockSpec((tm, D), lambda s: (jnp.maximum(s - n_p, 0), 0)),
                  pl.BlockSpec((1, 6, D), lambda s: (_batch_of_tile(s, groups, tm), 0, 0)),
                  pl.BlockSpec((D, D), const2),
                  pl.BlockSpec((1, D), const2),
                  pl.BlockSpec((1, D), const2),
                  pl.BlockSpec((N_EXPERTS, D), const2),
                  pl.BlockSpec((N_EXPERTS, 1), const2),
                  pl.BlockSpec((tm, tm), const2)],
        out_specs=[pl.BlockSpec((tm, D), tok),
                   pl.BlockSpec((tm * ROW_TILE, LANES), tok),
                   pl.BlockSpec((1, TOP_K, tm), tile3),
                   pl.BlockSpec((1, TOP_K, tm), tile3),
                   pl.BlockSpec((1, TOP_K, tm), tile3),
                   pl.BlockSpec((N_EXPERTS, 1), const2)],
        out_shape=[jax.ShapeDtypeStruct((nt * tm, D), F32),
                   jax.ShapeDtypeStruct((nt * tm * ROW_TILE, LANES), F32),
                   jax.ShapeDtypeStruct((nt, TOP_K, tm), jnp.int32),
                   jax.ShapeDtypeStruct((nt, TOP_K, tm), F32),
                   jax.ShapeDtypeStruct((nt, TOP_K, tm), jnp.int32),
                   jax.ShapeDtypeStruct((N_EXPERTS, 1), F32)],
        scratch_shapes=[pltpu.VMEM((N_EXPERTS, 1), F32)],
        compiler_params=_cparams("arbitrary"),
        name="post_mix",
    )(na, wa, xp, xs, mod, w_out_bf16, post_mix.reshape(1, D), pre_ffn.reshape(1, D),
      w_router_t, b_router.reshape(N_EXPERTS, 1), jnp.asarray(tri, BF16))


def _expert_kernel(blk_e_ref, src_hbm, dst_hbm, h_hbm, wgu_ref, bgu_ref, wd_ref, bd_ref, y_hbm,
                   src_smem, dst_smem, xbuf, xb, act_bf, ob0, ob1, ob2, wgu_bf, wd_bf,
                   idx_sem, g_sem, s_sem, *, n_blocks):
    i = pl.program_id(0)
    BM = EXPERT_BLOCK
    RT = ROW_TILE
    last = n_blocks - 1
    obufs = (ob0, ob1, ob2)

    def idx_copies(blk, slot):
        return (pltpu.make_async_copy(src_hbm.at[blk], src_smem.at[slot], idx_sem.at[slot]),
                pltpu.make_async_copy(dst_hbm.at[blk], dst_smem.at[slot], idx_sem.at[slot]))

    def start_idx(blk, slot):
        for c in idx_copies(blk, slot):
            c.start()

    def wait_idx(blk, slot):
        for c in idx_copies(blk, slot):
            c.wait()

    def paced(row, prev, rounds):
        z = prev
        for _ in range(rounds):
            z = z + lax.shift_right_arithmetic(z, 31)
        return row + lax.shift_right_arithmetic(z, 31)

    def start_gather(islot, xslot, rounds):
        prev = src_smem[islot, 0, 0]
        for r in range(BM):
            prev = paced(src_smem[islot, 0, r], prev, rounds)
            row = pl.multiple_of(prev, RT)
            pltpu.make_async_copy(h_hbm.at[pl.ds(row, RT), :], xbuf.at[xslot, pl.ds(r * RT, RT), :],
                                  g_sem.at[xslot]).start()

    def wait_gather(xslot):
        pltpu.make_async_copy(h_hbm.at[pl.ds(0, BM * RT), :], xbuf.at[xslot], g_sem.at[xslot]).wait()

    def start_scatter(islot, k, rounds):
        prev = dst_smem[islot, 0, 0]
        for r in range(BM):
            prev = paced(dst_smem[islot, 0, r], prev, rounds)
            row = pl.multiple_of(prev, RT)
            pltpu.make_async_copy(obufs[k].at[pl.ds(r * RT, RT), :], y_hbm.at[pl.ds(row, RT), :],
                                  s_sem.at[k]).start(priority=r % 2)

    def wait_scatter(k):
        pltpu.make_async_copy(obufs[k], y_hbm.at[pl.ds(0, BM * RT), :], s_sem.at[k]).wait()

    xs = lax.rem(i, X_SLOTS)

    @pl.when(i == 0)
    def _():
        for b in range(min(3, n_blocks)):
            start_idx(b, b)
        for b in range(min(2, n_blocks)):
            wait_idx(b, b)
            start_gather(b, b, 0)

    @pl.when(i + 3 < n_blocks)
    def _():
        start_idx(i + 3, lax.rem(i + 3, IDX_SLOTS))

    @pl.when(i + 2 < n_blocks)
    def _():
        wait_idx(i + 2, lax.rem(i + 2, IDX_SLOTS))

    wait_gather(xs)

    new_expert = jnp.logical_or(i == 0, blk_e_ref[i] != blk_e_ref[jnp.maximum(i - 1, 0)])

    @pl.when(new_expert)
    def _():
        wgu_bf[...] = wgu_ref[0].astype(BF16)
        wd_bf[...] = wd_ref[0].astype(BF16)

    def up_phase():
        for c in range(RT):
            xb[:, c * LANES:(c + 1) * LANES] = xbuf[xs, pl.ds(c, BM, stride=RT), :].astype(BF16)
        start_gather(lax.rem(jnp.minimum(i + 2, last), IDX_SLOTS), lax.rem(i + 2, X_SLOTS),
                     GATHER_PACING)
        gu = jnp.dot(xb[...], wgu_bf[...], preferred_element_type=F32) + bgu_ref[0]
        glu = jnp.minimum(gu[:, :D_EXPERT], SWIGLU_LIMIT)
        lin = jnp.clip(gu[:, D_EXPERT:], -SWIGLU_LIMIT, SWIGLU_LIMIT)
        act = glu * (1.0 / (1.0 + jnp.exp(-SWIGLU_ALPHA * glu))) * (lin + 1.0)
        act_bf[...] = act.astype(BF16)

    def down_phase(k, scatter_prev):
        if scatter_prev:
            start_scatter(lax.rem(i - 1, IDX_SLOTS), (k - 1) % O_SLOTS, SCATTER_PACING)
        y = jnp.dot(act_bf[...], wd_bf[...], preferred_element_type=F32) + bd_ref[0]
        for c in range(RT):
            obufs[k][pl.ds(c, BM, stride=RT), :] = y[:, c * LANES:(c + 1) * LANES]

    pl.when(i >= 0)(up_phase)
    for k in range(O_SLOTS):
        @pl.when(jnp.logical_and(i >= O_SLOTS, lax.rem(i, O_SLOTS) == k))
        def _():
            wait_scatter(k)
    pl.when(i == 0)(lambda: down_phase(0, False))
    for k in range(O_SLOTS):
        pl.when(jnp.logical_and(i > 0, lax.rem(i, O_SLOTS) == k))(
            functools.partial(down_phase, k, True))

    @pl.when(i == last)
    def _():
        start_scatter(last % IDX_SLOTS, last % O_SLOTS, 0)
        for b in range(max(0, last - O_SLOTS + 1), last + 1):
            wait_scatter(b % O_SLOTS)
        wait_gather(lax.rem(i + 1, X_SLOTS))
        wait_gather(lax.rem(i + 2, X_SLOTS))


def _experts(h2, src_row, dst_row, blk_e, w_gate_up, b_gate_up, w_down, b_down):
    D = D_MODEL
    BM = EXPERT_BLOCK
    n_blocks = blk_e.shape[0]
    wmap = lambda i, be: (be[i], 0, 0)
    grid_spec = pltpu.PrefetchScalarGridSpec(
        num_scalar_prefetch=1,
        grid=(n_blocks,),
        in_specs=[pl.BlockSpec(memory_space=pl.ANY),
                  pl.BlockSpec(memory_space=pl.ANY),
                  pl.BlockSpec(memory_space=pl.ANY),
                  pl.BlockSpec((1, D, 2 * D_EXPERT), wmap),
                  pl.BlockSpec((1, 1, 2 * D_EXPERT), wmap),
                  pl.BlockSpec((1, D_EXPERT, D), wmap),
                  pl.BlockSpec((1, 1, D), wmap)],
        out_specs=pl.BlockSpec(memory_space=pl.ANY),
        scratch_shapes=[pltpu.SMEM((IDX_SLOTS, 1, BM), jnp.int32),
                        pltpu.SMEM((IDX_SLOTS, 1, BM), jnp.int32),
                        pltpu.VMEM((X_SLOTS, BM * ROW_TILE, LANES), F32),
                        pltpu.VMEM((BM, D), BF16),
                        pltpu.VMEM((BM, D_EXPERT), BF16),
                        pltpu.VMEM((BM * ROW_TILE, LANES), F32),
                        pltpu.VMEM((BM * ROW_TILE, LANES), F32),
                        pltpu.VMEM((BM * ROW_TILE, LANES), F32),
                        pltpu.VMEM((D, 2 * D_EXPERT), BF16),
                        pltpu.VMEM((D_EXPERT, D), BF16),
                        pltpu.SemaphoreType.DMA((IDX_SLOTS,)),
                        pltpu.SemaphoreType.DMA((X_SLOTS,)),
                        pltpu.SemaphoreType.DMA((O_SLOTS,))],
    )
    return pl.pallas_call(
        functools.partial(_expert_kernel, n_blocks=n_blocks),
        grid_spec=grid_spec,
        out_shape=jax.ShapeDtypeStruct((n_blocks * BM * ROW_TILE, LANES), F32),
        compiler_params=_cparams("arbitrary"),
        name="experts",
    )(blk_e, src_row.reshape(n_blocks, 1, BM), dst_row.reshape(n_blocks, 1, BM), h2,
      w_gate_up, b_gate_up.reshape(N_EXPERTS, 1, -1), w_down, b_down.reshape(N_EXPERTS, 1, -1))


def _combine_kernel(y0_ref, y1_ref, y2_ref, y3_ref, gate_ref, x1_ref, mod_ref, pf_ref,
                    op_ref, os_ref, *, n_p):
    s = pl.program_id(0)
    g = gate_ref[...]

    def rows(y_ref):
        return jnp.concatenate([y_ref[pl.ds(c, COMBINE_TILE, stride=ROW_TILE), :]
                                for c in range(ROW_TILE)], axis=-1)

    m = rows(y0_ref) * g[:, 0:1]
    for k, y_ref in enumerate((y1_ref, y2_ref, y3_ref), start=1):
        m = m + rows(y_ref) * g[:, k:k + 1]
    res = x1_ref[...] + mod_ref[0, 5:6, :] * _rms(m, pf_ref[...])

    @pl.when(s < n_p)
    def _():
        op_ref[...] = res

    @pl.when(s >= n_p)
    def _():
        os_ref[...] = res


def _combine(y4, gates_tok, x1, mod, post_ffn, n_tok_p, n_tok_s, groups):
    D = D_MODEL
    tc = COMBINE_TILE
    n_p, n_s = n_tok_p // tc, n_tok_s // tc
    nt = n_p + n_s

    def slot(k):
        return pl.BlockSpec((tc * ROW_TILE, LANES), lambda s: (k * nt + s, 0))

    return pl.pallas_call(
        functools.partial(_combine_kernel, n_p=n_p),
        grid=(nt,),
        in_specs=[slot(0), slot(1), slot(2), slot(3),
                  pl.BlockSpec((tc, TOP_K), lambda s: (s, 0)),
                  pl.BlockSpec((tc, D), lambda s: (s, 0)),
                  pl.BlockSpec((1, 6, D), lambda s: (_batch_of_tile(s, groups, tc), 0, 0)),
                  pl.BlockSpec((1, D), lambda s: (0, 0))],
        out_specs=[pl.BlockSpec((tc, D), lambda s: (jnp.minimum(s, n_p - 1), 0)),
                   pl.BlockSpec((tc, D), lambda s: (jnp.maximum(s - n_p, 0), 0))],
        out_shape=[jax.ShapeDtypeStruct((n_tok_p, D), F32),
                   jax.ShapeDtypeStruct((n_tok_s, D), F32)],
        compiler_params=_cparams("arbitrary"),
        name="combine",
    )(y4, y4, y4, y4, gates_tok, x1, mod, post_ffn.reshape(1, D))


def _dispatch_kernel(padfirst_ref, padcnt_ref, nused_ref, dest_hbm, h_ref, xs_hbm, idx_smem, zero_blk,
                     idx_sem, sem, *, n_steps, n_blocks):
    s = pl.program_id(0)
    tm = TOKEN_TILE
    RT = ROW_TILE
    slot = lax.rem(s, 2)

    def idx_copy(step, sl):
        return pltpu.make_async_copy(dest_hbm.at[step], idx_smem.at[sl], idx_sem.at[sl])

    @pl.when(s == 0)
    def _():
        idx_copy(0, 0).start()
        zero_blk[...] = jnp.zeros_like(zero_blk)

        def pad_copies(e, wait):
            first = padfirst_ref[e]
            cnt = padcnt_ref[e]
            for b in reversed(range(EXPERT_BLOCK.bit_length() - 1)):
                size = (1 << b) * RT
                bit = lax.shift_right_logical(cnt, b) & 1

                @pl.when(bit == 1)
                def _():
                    cp = pltpu.make_async_copy(
                        zero_blk.at[pl.ds(0, size), :],
                        xs_hbm.at[pl.ds(pl.multiple_of(first * RT, RT), size), :], sem)
                    cp.wait() if wait else cp.start()

                first = first + bit * (1 << b)

        def issue(e, c):
            pad_copies(e, False)
            return c

        def drain(e, c):
            pad_copies(e, True)
            return c

        lax.fori_loop(0, N_EXPERTS, issue, 0)
        lax.fori_loop(0, N_EXPERTS, drain, 0)

        def blk_copy(b):
            start = pl.multiple_of(b * (EXPERT_BLOCK * RT), EXPERT_BLOCK * RT)
            return pltpu.make_async_copy(zero_blk, xs_hbm.at[pl.ds(start, EXPERT_BLOCK * RT), :], sem)

        def issue_blk(b, c):
            blk_copy(b).start()
            return c

        def drain_blk(b, c):
            blk_copy(b).wait()
            return c

        lax.fori_loop(nused_ref[0], n_blocks, issue_blk, 0)
        lax.fori_loop(nused_ref[0], n_blocks, drain_blk, 0)

    idx_copy(s, slot).wait()
    if n_steps > 1:
        @pl.when(s + 1 < n_steps)
        def _():
            idx_copy(s + 1, 1 - slot).start()

    for k in range(TOP_K):
        for j in range(tm):
            row = pl.multiple_of(idx_smem[slot, 0, k * tm + j], RT)
            pltpu.make_async_copy(h_ref.at[pl.ds(j * RT, RT), :], xs_hbm.at[pl.ds(row, RT), :],
                                  sem).start(priority=j % 2)
    for k in range(TOP_K):
        pltpu.make_async_copy(h_ref, xs_hbm.at[pl.ds(0, tm * RT), :], sem).wait()


def _dispatch(h2, dest_rows, pad_first, pad_cnt, n_used, n_blocks):
    tm = TOKEN_TILE
    n_steps = h2.shape[0] // (tm * ROW_TILE)
    grid_spec = pltpu.PrefetchScalarGridSpec(
        num_scalar_prefetch=3,
        grid=(n_steps,),
        in_specs=[pl.BlockSpec(memory_space=pl.ANY),
                  pl.BlockSpec((tm * ROW_TILE, LANES), lambda s, pf, pc, nused: (s, 0))],
        out_specs=pl.BlockSpec(memory_space=pl.ANY),
        scratch_shapes=[pltpu.SMEM((2, 1, TOP_K * tm), jnp.int32),
                        pltpu.VMEM((EXPERT_BLOCK * ROW_TILE, LANES), F32),
                        pltpu.SemaphoreType.DMA((2,)),
                        pltpu.SemaphoreType.DMA],
    )
    return pl.pallas_call(
        functools.partial(_dispatch_kernel, n_steps=n_steps, n_blocks=n_blocks),
        grid_spec=grid_spec,
        out_shape=jax.ShapeDtypeStruct((n_blocks * EXPERT_BLOCK * ROW_TILE, LANES), F32),
        compiler_params=_cparams("arbitrary"),
        name="dispatch",
    )(pad_first, pad_cnt, n_used, dest_rows.reshape(n_steps, 1, TOP_K * tm), h2)


def _mlp_kernel(blk_e_ref, nused_ref, x_ref, wgu_ref, bgu_ref, wd_ref, bd_ref, o_ref, wgu_bf, wd_bf):
    i = pl.program_id(0)
    BM = EXPERT_BLOCK
    RT = ROW_TILE

    @pl.when(i < nused_ref[0])
    def _():
        new_expert = jnp.logical_or(i == 0, blk_e_ref[i] != blk_e_ref[jnp.maximum(i - 1, 0)])

        @pl.when(new_expert)
        def _():
            wgu_bf[...] = wgu_ref[0].astype(BF16)
            wd_bf[...] = wd_ref[0].astype(BF16)

        x = jnp.concatenate([x_ref[pl.ds(c, BM, stride=RT), :] for c in range(RT)],
                            axis=-1).astype(BF16)
        H = D_EXPERT // MLP_CHUNKS
        ups = []
        for h in range(MLP_CHUNKS):
            g = jnp.dot(x, wgu_bf[:, h * H:(h + 1) * H], preferred_element_type=F32)
            u = jnp.dot(x, wgu_bf[:, D_EXPERT + h * H:D_EXPERT + (h + 1) * H],
                        preferred_element_type=F32)
            ups.append((g + bgu_ref[0, :, h * H:(h + 1) * H],
                        u + bgu_ref[0, :, D_EXPERT + h * H:D_EXPERT + (h + 1) * H]))
        y = bd_ref[0]
        for h, (g, u) in enumerate(ups):
            glu = jnp.minimum(g, SWIGLU_LIMIT)
            lin = jnp.clip(u, -SWIGLU_LIMIT, SWIGLU_LIMIT)
            act = glu * (1.0 / (1.0 + jnp.exp(-SWIGLU_ALPHA * glu))) * (lin + 1.0)
            y = y + jnp.dot(act.astype(BF16), wd_bf[h * H:(h + 1) * H, :], preferred_element_type=F32)
        for c in range(RT):
            o_ref[pl.ds(c, BM, stride=RT), :] = y[:, c * LANES:(c + 1) * LANES]

    @pl.when(i >= nused_ref[0])
    def _():
        o_ref[...] = jnp.zeros_like(o_ref)


def _expert_mlp(xs, blk_e, n_used, w_gate_up, b_gate_up, w_down, b_down):
    D = D_MODEL
    BM = EXPERT_BLOCK
    n_blocks = blk_e.shape[0]
    wmap = lambda i, be, nu: (be[i], 0, 0)
    grid_spec = pltpu.PrefetchScalarGridSpec(
        num_scalar_prefetch=2,
        grid=(n_blocks,),
        in_specs=[pl.BlockSpec((BM * ROW_TILE, LANES), lambda i, be, nu: (jnp.minimum(i, nu[0] - 1), 0)),
                  pl.BlockSpec((1, D, 2 * D_EXPERT), wmap),
                  pl.BlockSpec((1, 1, 2 * D_EXPERT), wmap),
                  pl.BlockSpec((1, D_EXPERT, D), wmap),
                  pl.BlockSpec((1, 1, D), wmap)],
        out_specs=pl.BlockSpec((BM * ROW_TILE, LANES), lambda i, be, nu: (i, 0)),
        scratch_shapes=[pltpu.VMEM((D, 2 * D_EXPERT), BF16),
                        pltpu.VMEM((D_EXPERT, D), BF16)],
    )
    return pl.pallas_call(
        _mlp_kernel,
        grid_spec=grid_spec,
        out_shape=jax.ShapeDtypeStruct(xs.shape, F32),
        compiler_params=_cparams("arbitrary"),
        name="expert_mlp",
    )(blk_e, n_used, xs, w_gate_up, b_gate_up.reshape(N_EXPERTS, 1, -1), w_down,
      b_down.reshape(N_EXPERTS, 1, -1))


def _gather_combine_kernel(dest_hbm, ys_hbm, gate_ref, x1_ref, mod_ref, pf_ref, op_ref, os_ref,
                           idx_smem, yb0, yb1, idx_sem, row_sem, *, n_steps, n_p):
    s = pl.program_id(0)
    tc = COMBINE_TILE
    RT = ROW_TILE
    n_rows = TOP_K * tc
    last = n_steps - 1
    ybufs = (yb0, yb1)
    n_idx = 3

    def idx_copy(step, sl):
        return pltpu.make_async_copy(dest_hbm.at[step], idx_smem.at[sl], idx_sem.at[sl])

    def start_rows(islot, b):
        for r in range(n_rows):
            row = pl.multiple_of(idx_smem[islot, 0, r], RT)
            pltpu.make_async_copy(ys_hbm.at[pl.ds(row, RT), :], ybufs[b].at[pl.ds(r * RT, RT), :],
                                  row_sem.at[b]).start(priority=r % 2)

    def wait_rows(b):
        pltpu.make_async_copy(ys_hbm.at[pl.ds(0, n_rows * RT), :], ybufs[b], row_sem.at[b]).wait()

    @pl.when(s == 0)
    def _():
        for t in range(min(2, n_steps)):
            idx_copy(t, t).start()
        idx_copy(0, 0).wait()
        start_rows(0, 0)

    @pl.when(s + 2 < n_steps)
    def _():
        idx_copy(s + 2, lax.rem(s + 2, n_idx)).start()

    @pl.when(s + 1 < n_steps)
    def _():
        idx_copy(s + 1, lax.rem(s + 1, n_idx)).wait()

    def body(cur):
        wait_rows(cur)
        start_rows(lax.rem(jnp.minimum(s + 1, last), n_idx), 1 - cur)
        g = gate_ref[...]

        def rows(k):
            return jnp.concatenate([ybufs[cur][pl.ds(k * tc * RT + c, tc, stride=RT), :]
                                    for c in range(RT)], axis=-1)

        m = rows(0) * g[:, 0:1]
        for k in range(1, TOP_K):
            m = m + rows(k) * g[:, k:k + 1]
        res = x1_ref[...] + mod_ref[0, 5:6, :] * _rms(m, pf_ref[...])

        @pl.when(s < n_p)
        def _():
            op_ref[...] = res

        @pl.when(s >= n_p)
        def _():
            os_ref[...] = res

    for cur in range(2):
        pl.when(lax.rem(s, 2) == cur)(functools.partial(body, cur))

    @pl.when(s == last)
    def _():
        wait_rows(1 - last % 2)


def _gather_combine(dest_rows, ys, gates_tok, x1, mod, post_ffn, n_tok_p, n_tok_s, groups):
    D = D_MODEL
    tc = COMBINE_TILE
    n_p, n_s = n_tok_p // tc, n_tok_s // tc
    nt = n_p + n_s
    return pl.pallas_call(
        functools.partial(_gather_combine_kernel, n_steps=nt, n_p=n_p),
        grid=(nt,),
        in_specs=[pl.BlockSpec(memory_space=pl.ANY),
                  pl.BlockSpec(memory_space=pl.ANY),
                  pl.BlockSpec((tc, TOP_K), lambda s: (s, 0)),
                  pl.BlockSpec((tc, D), lambda s: (s, 0)),
                  pl.BlockSpec((1, 6, D), lambda s: (_batch_of_tile(s, groups, tc), 0, 0)),
                  pl.BlockSpec((1, D), lambda s: (0, 0))],
        out_specs=[pl.BlockSpec((tc, D), lambda s: (jnp.minimum(s, n_p - 1), 0)),
                   pl.BlockSpec((tc, D), lambda s: (jnp.maximum(s - n_p, 0), 0))],
        out_shape=[jax.ShapeDtypeStruct((n_tok_p, D), F32),
                   jax.ShapeDtypeStruct((n_tok_s, D), F32)],
        scratch_shapes=[pltpu.SMEM((3, 1, TOP_K * tc), jnp.int32),
                        pltpu.VMEM((TOP_K * tc * ROW_TILE, LANES), F32),
                        pltpu.VMEM((TOP_K * tc * ROW_TILE, LANES), F32),
                        pltpu.SemaphoreType.DMA((3,)),
                        pltpu.SemaphoreType.DMA((2,))],
        compiler_params=_cparams("arbitrary"),
        name="combine",
    )(dest_rows.reshape(nt, 1, TOP_K * tc), ys, gates_tok, x1, mod, post_ffn.reshape(1, D))


def _routing_dest(idx_t, rank_t, counts):
    BM = EXPERT_BLOCK
    nt, _, tm = idx_t.shape
    n_blocks = (nt * tm * TOP_K) // BM + N_EXPERTS
    padded = ((counts + BM - 1) // BM) * BM
    pad_end = jnp.cumsum(padded)
    pad_start = pad_end - padded
    experts = jnp.arange(N_EXPERTS, dtype=jnp.int32)
    start_of = jnp.sum(jnp.where(idx_t[..., None] == experts, pad_start, 0), axis=-1)
    dest = (start_of + rank_t).astype(jnp.int32)
    blk_first = jnp.arange(n_blocks, dtype=jnp.int32) * BM
    blk_e = jnp.minimum(jnp.sum((pad_end[None, :] <= blk_first[:, None]).astype(jnp.int32), axis=1),
                        N_EXPERTS - 1)
    n_used = (pad_end[-1:] // BM).astype(jnp.int32)
    pad_first = (pad_start + counts).astype(jnp.int32)
    pad_cnt = (padded - counts).astype(jnp.int32)
    return dest, blk_e, n_used, pad_first, pad_cnt, n_blocks


def _routing_plan(idx_t, rank_t, counts, n_tok):
    BM = EXPERT_BLOCK
    nt, _, tm = idx_t.shape
    n_blocks = (n_tok * TOP_K) // BM + N_EXPERTS
    padded = ((counts + BM - 1) // BM) * BM
    pad_end = jnp.cumsum(padded)
    pad_start = pad_end - padded
    experts = jnp.arange(N_EXPERTS, dtype=jnp.int32)
    start_of = jnp.sum(jnp.where(idx_t[..., None] == experts, pad_start, 0), axis=-1)
    dest = (start_of + rank_t).astype(jnp.int32)
    t_of = (jnp.arange(nt, dtype=jnp.int32)[:, None, None] * tm
            + jnp.arange(tm, dtype=jnp.int32)[None, None, :])
    slot_row = jnp.arange(TOP_K, dtype=jnp.int32)[None, :, None] * n_tok + t_of
    blk_first = jnp.arange(n_blocks, dtype=jnp.int32) * BM
    blk_e = jnp.minimum(jnp.sum((pad_end[None, :] <= blk_first[:, None]).astype(jnp.int32), axis=1),
                        N_EXPERTS - 1)
    of_blk = blk_e[:, None] == experts
    per_blk = lambda v: jnp.sum(jnp.where(of_blk, v, 0), axis=1)[:, None]
    pos = blk_first[:, None] + jnp.arange(BM, dtype=jnp.int32)[None, :]
    real_before = per_blk(jnp.cumsum(counts) - counts) + jnp.minimum(pos - per_blk(pad_start),
                                                                     per_blk(counts))
    spare = n_tok * TOP_K + pos - real_before
    inv = spare.reshape(-1).astype(jnp.int32).at[dest.reshape(-1)].set(
        slot_row.reshape(-1), unique_indices=True, mode='promise_in_bounds')
    src_row = jnp.where(inv < n_tok * TOP_K, inv % n_tok, 0) * ROW_TILE
    dst_row = inv * ROW_TILE
    return src_row, dst_row, blk_e


def kernel(x_prompt, x_sample, c_prompt, c_sample, w_ada, b_ada, pre_mix, post_mix, pre_ffn, post_ffn,
           w_in, na_rel_bias, t5_rel_bias, wa_sink, gn_na, gn_wa, w_out, w_router, b_router,
           w_gate_up, b_gate_up, w_down, b_down):
    depth = w_ada.shape[0]
    D = D_MODEL
    groups = (x_prompt.shape[:2], x_sample.shape[:2])
    n_tok_p = x_prompt.shape[0] * x_prompt.shape[1]
    n_tok_s = x_sample.shape[0] * x_sample.shape[1]
    n_tok = n_tok_p + n_tok_s
    yp = x_prompt.reshape(n_tok_p, D)
    ys = x_sample.reshape(n_tok_s, D)
    c_all = jnp.concatenate([c_prompt, c_sample], axis=0)
    for l in range(depth):
        mod = _ada(c_all, w_ada[l], b_ada[l]).reshape(c_all.shape[0], 6, D)
        qb0 = 3 * D_NA
        scale = HEAD_DIM ** -0.5
        w_in_l = jnp.concatenate([w_in[l][:, :D_NA] * scale, w_in[l][:, D_NA:qb0],
                                  _wa_slab_order(w_in[l][:, qb0:qb0 + D_WA] * scale, axis=1),
                                  w_in[l][:, qb0 + D_WA:]], axis=1).astype(BF16)
        w_out_l = jnp.concatenate([w_out[l][:D_NA], _wa_slab_order(w_out[l][D_NA:], axis=0)],
                                  axis=0).astype(BF16)
        proj = _in_proj(yp, ys, mod, pre_mix[l], w_in_l, groups)
        na = _na_attention(proj, _na_bias_table(na_rel_bias[l]), gn_na[l], groups)
        wa = _wa_attention(proj, _wa_bias_table(t5_rel_bias), wa_sink[l],
                           _wa_slab_order(gn_wa[l], axis=0), groups)
        x1, h2, idx_t, gate_t, rank_t, counts = _post_mix(
            na, wa, yp, ys, mod, w_out_l, post_mix[l], pre_ffn[l], w_router[l].T, b_router[l], groups)
        dest, blk_e, n_used, pad_first, pad_cnt, n_blocks = _routing_dest(
            idx_t, rank_t, counts.reshape(N_EXPERTS).astype(jnp.int32))
        dest_rows = dest * ROW_TILE
        xs = _dispatch(h2, dest_rows, pad_first, pad_cnt, n_used, n_blocks)
        ys_rows = _expert_mlp(xs, blk_e, n_used, w_gate_up[l], b_gate_up[l], w_down[l], b_down[l])
        gates_tok = jnp.transpose(gate_t, (0, 2, 1)).reshape(n_tok, TOP_K)
        halves = TOKEN_TILE // COMBINE_TILE
        dest_steps = jnp.transpose(dest_rows.reshape(-1, TOP_K, halves, COMBINE_TILE), (0, 2, 1, 3))
        yp, ys = _gather_combine(dest_steps, ys_rows, gates_tok, x1, mod, post_ffn[l], n_tok_p, n_tok_s,
                                 groups)
    return (yp.reshape(x_prompt.shape), ys.reshape(x_sample.shape))
```

```python
import functools
import math

import jax
import jax.numpy as jnp
import numpy as np
from jax import lax
from jax.experimental import pallas as pl
from jax.experimental.pallas import tpu as pltpu

F32 = jnp.float32
BF16 = jnp.bfloat16

D_MODEL = 1024
HEAD_DIM = 64
D_NA = 512
D_WA = 512
N_HEADS_NA = 8
N_HEADS_WA = 8
N_KV_WA = 2
GQA_GROUP = 4
D_KV_WA = 128
D_IN_PROJ = 3 * D_NA + D_WA + 2 * D_KV_WA
GRID_W = 64
NA_WIN_R = 8
NA_WIN_C = 16
WA_WINDOW = 128
WA_BLOCK = 128
T5_BUCKETS = 32
T5_MAX_EXACT = 8
T5_MAX_DIST = 128
N_EXPERTS = 32
TOP_K = 4
D_EXPERT = D_MODEL
SWIGLU_LIMIT = 7.0
SWIGLU_ALPHA = 1.702
RMS_EPS = 1e-6
NEG_INF = -1e30
LOG2E = math.log2(math.e)

VMEM_LIMIT_BYTES = 56 * 1024 * 1024
LANES = 128
ROW_TILE = D_MODEL // LANES

TOKEN_TILE = 512
POST_MIX_PARTS = 4
IN_PROJ_PARTS = 2
NA_ROWS_PER_STEP = 8
NA_ROW_UNROLL = 8
WA_BLOCKS_PER_STEP = 2
EXPERT_BLOCK = 512
MLP_CHUNKS = 2
X_SLOTS = 3
IDX_SLOTS = X_SLOTS + 2
O_SLOTS = 3
GATHER_PACING = 3
SCATTER_PACING = 0
COMBINE_TILE = 256


def _cparams(*sem):
    return pltpu.CompilerParams(dimension_semantics=sem, vmem_limit_bytes=VMEM_LIMIT_BYTES)


def _rms(x, g):
    return x * lax.rsqrt(jnp.mean(x * x, axis=-1, keepdims=True) + RMS_EPS) * g


def _seq_bounds(s, groups, block):
    base = 0
    start, length = None, None
    for n_seq, seq_len in groups:
        per = seq_len // block
        g_start = base + ((s - base) // per) * per
        if start is None:
            start, length = g_start, per
        else:
            inside = s >= base
            start = jnp.where(inside, g_start, start)
            length = jnp.where(inside, per, length)
        base += n_seq * per
    return start, length


def _batch_of_tile(s, groups, block):
    base_blk, base_seq = 0, 0
    out = None
    for n_seq, seq_len in groups:
        per = seq_len // block
        b = base_seq + (s - base_blk) // per
        out = b if out is None else jnp.where(s >= base_blk, b, out)
        base_blk += n_seq * per
        base_seq += n_seq
    return out


def _ada_kernel(c_ref, w_ref, b_ref, o_ref):
    c = c_ref[...]
    s = c * (1.0 / (1.0 + jnp.exp(-c)))
    o_ref[...] = jnp.dot(s, w_ref[...], preferred_element_type=F32,
                         precision=lax.Precision.HIGHEST) + b_ref[...]


def _ada(c_all, w_ada, b_ada):
    nb = c_all.shape[0]
    return pl.pallas_call(
        _ada_kernel,
        grid=(6,),
        in_specs=[pl.BlockSpec((nb, D_MODEL), lambda j: (0, 0)),
                  pl.BlockSpec((D_MODEL, D_MODEL), lambda j: (0, j)),
                  pl.BlockSpec((1, D_MODEL), lambda j: (0, j))],
        out_specs=pl.BlockSpec((nb, D_MODEL), lambda j: (0, j)),
        out_shape=jax.ShapeDtypeStruct((nb, 6 * D_MODEL), F32),
        compiler_params=_cparams("arbitrary"),
        name="ada",
    )(c_all, w_ada, b_ada.reshape(1, -1))


def _inproj_kernel(xp_ref, xs_ref, mod_ref, g_ref, w_ref, o_ref, *, n_p):
    s = pl.program_id(0)

    def body(x_ref):
        H = TOKEN_TILE // IN_PROJ_PARTS
        hs = [(_rms(x_ref[p * H:(p + 1) * H], g_ref[...]) * (1.0 + mod_ref[0, 1:2, :])
               + mod_ref[0, 0:1, :]).astype(BF16) for p in range(IN_PROJ_PARTS)]
        for p, h in enumerate(hs):
            o_ref[p * H:(p + 1) * H] = jnp.dot(h, w_ref[...], preferred_element_type=F32).astype(BF16)

    pl.when(s < n_p)(lambda: body(xp_ref))
    pl.when(s >= n_p)(lambda: body(xs_ref))


def _in_proj(xp, xs, mod, pre_mix, w_in_bf16, groups):
    D = D_MODEL
    tm = TOKEN_TILE
    n_p, n_s = xp.shape[0] // tm, xs.shape[0] // tm
    return pl.pallas_call(
        functools.partial(_inproj_kernel, n_p=n_p),
        grid=(n_p + n_s,),
        in_specs=[pl.BlockSpec((tm, D), lambda s: (jnp.minimum(s, n_p - 1), 0)),
                  pl.BlockSpec((tm, D), lambda s: (jnp.maximum(s - n_p, 0), 0)),
                  pl.BlockSpec((1, 6, D), lambda s: (_batch_of_tile(s, groups, tm), 0, 0)),
                  pl.BlockSpec((1, D), lambda s: (0, 0)),
                  pl.BlockSpec((D, D_IN_PROJ), lambda s: (0, 0))],
        out_specs=pl.BlockSpec((tm, D_IN_PROJ), lambda s: (s, 0)),
        out_shape=jax.ShapeDtypeStruct(((n_p + n_s) * tm, D_IN_PROJ), BF16),
        compiler_params=_cparams("arbitrary"),
        name="in_proj",
    )(xp, xs, mod, pre_mix.reshape(1, D), w_in_bf16)


def _na_bias_table(rpb):
    qc = np.arange(GRID_W)[:, None]
    kc = np.arange(GRID_W)[None, :]
    win_start = np.clip(qc - NA_WIN_C // 2, 0, GRID_W - NA_WIN_C)
    valid = (kc >= win_start) & (kc < win_start + NA_WIN_C)
    dc = np.clip(kc - qc, -(NA_WIN_C - 1), NA_WIN_C - 1) + NA_WIN_C - 1
    pick = (dc[None] == np.arange(2 * NA_WIN_C - 1)[:, None, None]).astype(np.float32)
    full = jnp.einsum('hrc,cqk->hrqk', rpb.astype(F32), pick,
                      precision=lax.Precision.HIGHEST)
    full = jnp.where(valid[None, None], full.astype(F32), NEG_INF)
    two = jnp.concatenate([full[:, :-1], full[:, 1:]], axis=-1)
    two = two.reshape(N_HEADS_NA // 2, 2, 2 * NA_WIN_R - 2, GRID_W, 2 * GRID_W)
    return jnp.transpose(two, (0, 2, 1, 3, 4)).reshape(
        N_HEADS_NA // 2, 2 * NA_WIN_R - 2, 2 * GRID_W, 2 * GRID_W)


def _na_kernel(q_ref, kp_ref, kc_ref, kn_ref, vp_ref, vc_ref, vn_ref, bias_ref, gn_ref, o_ref,
               k_scr, v_scr, *, groups):
    s = pl.program_id(0)
    R = NA_ROWS_PER_STEP
    blk = R * GRID_W
    seq_start, seq_blocks = _seq_bounds(s, groups, blk)
    i = s - seq_start
    rows = seq_blocks * R
    k_scr[0:blk] = kp_ref[...]
    k_scr[blk:2 * blk] = kc_ref[...]
    k_scr[2 * blk:3 * blk] = kn_ref[...]
    v_scr[0:blk] = vp_ref[...]
    v_scr[blk:2 * blk] = vc_ref[...]
    v_scr[2 * blk:3 * blk] = vn_ref[...]
    lane = lax.broadcasted_iota(jnp.int32, (1, LANES), 1)
    lo = lane < HEAD_DIM
    gn = gn_ref[...]

    n_pairs = N_HEADS_NA // 2
    n_keys = NA_WIN_R * GRID_W

    def rows_body(jj, carry):
        js = [jj * NA_ROW_UNROLL + u for u in range(NA_ROW_UNROLL)]
        geo = []
        for j in js:
            r = i * R + j
            row_start = jnp.clip(r - NA_WIN_R // 2, 0, rows - NA_WIN_R)
            geo.append((r - row_start,
                        pl.multiple_of((row_start - i * R + R) * GRID_W, GRID_W),
                        pl.multiple_of(j * GRID_W, GRID_W)))
        scores = []
        for shift, koff, qoff in geo:
            for p in range(n_pairs):
                cs = slice(p * LANES, (p + 1) * LANES)
                q2 = q_ref[pl.ds(qoff, GRID_W), cs]
                zero = jnp.zeros_like(q2)
                qq = jnp.concatenate([jnp.where(lo, q2, zero), jnp.where(lo, zero, q2)], axis=0)
                k2 = k_scr[pl.ds(koff, n_keys), cs]
                scores.append(lax.dot_general(qq, k2, (((1,), (1,)), ((), ())),
                                              preferred_element_type=F32))
        probs = []
        for u, (shift, koff, qoff) in enumerate(geo):
            for p in range(n_pairs):
                sc = scores[u * n_pairs + p]
                parts = []
                for m in range(NA_WIN_R // 2):
                    d = 2 * m - shift + (NA_WIN_R - 1)
                    parts.append(sc[:, m * LANES:(m + 1) * LANES] + bias_ref[p, d])
                sc = jnp.concatenate(parts, axis=-1)
                e = jnp.exp2(sc - jnp.max(sc, axis=-1, keepdims=True))
                probs.append((e.astype(BF16), jnp.sum(e, axis=-1, keepdims=True)))
        for u, (shift, koff, qoff) in enumerate(geo):
            outs = []
            for p in range(n_pairs):
                e, l = probs[u * n_pairs + p]
                v2 = v_scr[pl.ds(koff, n_keys), p * LANES:(p + 1) * LANES]
                o2 = jnp.dot(e, v2, preferred_element_type=F32) / l
                outs.append(jnp.where(lo, o2[:GRID_W], o2[GRID_W:]))
            o = jnp.concatenate(outs, axis=-1)
            o_ref[pl.ds(qoff, GRID_W), :] = _rms(o, gn).astype(BF16)
        return carry

    lax.fori_loop(0, R // NA_ROW_UNROLL, rows_body, 0)


def _na_attention(proj, bias_tab, gn_na, groups):
    n_tok = proj.shape[0]
    blk = NA_ROWS_PER_STEP * GRID_W

    def kv(col, delta):
        def index_map(s):
            start, length = _seq_bounds(s, groups, blk)
            return (jnp.clip(s + delta, start, start + length - 1), col)
        return pl.BlockSpec((blk, D_NA), index_map)

    return pl.pallas_call(
        functools.partial(_na_kernel, groups=groups),
        grid=(n_tok // blk,),
        in_specs=[pl.BlockSpec((blk, D_NA), lambda s: (s, 0)),
                  kv(1, -1), kv(1, 0), kv(1, 1), kv(2, -1), kv(2, 0), kv(2, 1),
                  pl.BlockSpec(bias_tab.shape, lambda s: (0, 0, 0, 0)),
                  pl.BlockSpec((1, D_NA), lambda s: (0, 0))],
        out_specs=pl.BlockSpec((blk, D_NA), lambda s: (s, 0)),
        out_shape=jax.ShapeDtypeStruct((n_tok, D_NA), BF16),
        scratch_shapes=[pltpu.VMEM((3 * blk, D_NA), BF16), pltpu.VMEM((3 * blk, D_NA), BF16)],
        compiler_params=_cparams("arbitrary"),
        name="na_attn",
    )(proj, proj, proj, proj, proj, proj, proj, bias_tab, gn_na.reshape(1, D_NA))


def _wa_slab_order(a, axis):
    shape = a.shape
    split = shape[:axis] + (N_KV_WA, GQA_GROUP, HEAD_DIM) + shape[axis + 1:]
    return jnp.swapaxes(a.reshape(split), axis, axis + 1).reshape(shape)


def _t5_bucket(rel):
    half = T5_BUCKETS // 2
    ret = (rel > 0).astype(jnp.int32) * half
    n = jnp.abs(rel)
    nf = jnp.maximum(n, 1).astype(F32)
    large = T5_MAX_EXACT + (jnp.log(nf / T5_MAX_EXACT) / math.log(T5_MAX_DIST / T5_MAX_EXACT)
                            * (half - T5_MAX_EXACT)).astype(jnp.int32)
    large = jnp.minimum(large, half - 1)
    return ret + jnp.where(n < T5_MAX_EXACT, n, large)


def _wa_bias_table(t5_table):
    rel = jnp.arange(3 * WA_BLOCK)[None, :] - WA_BLOCK - jnp.arange(WA_BLOCK)[:, None]
    pick = (_t5_bucket(rel)[..., None] == jnp.arange(T5_BUCKETS)).astype(F32)
    bias = jnp.einsum('qkb,bh->qkh', pick, t5_table.astype(F32),
                      precision=lax.Precision.HIGHEST)
    bias = jnp.where((jnp.abs(rel) <= WA_WINDOW)[..., None], bias, NEG_INF)
    bias = jnp.transpose(bias, (2, 0, 1))
    return jnp.concatenate([bias[:GQA_GROUP], bias[GQA_GROUP:]], axis=1)


def _wa_kernel(sink_ref, q_ref, *refs, groups):
    s = pl.program_id(0)
    W = WA_BLOCK
    NB = WA_BLOCKS_PER_STEP
    k_refs, v_refs = refs[:NB + 2], refs[NB + 2:2 * NB + 4]
    bias_ref, gn_ref, o_ref = refs[2 * NB + 4:]
    first = s * NB
    seq_start, seq_blocks = _seq_bounds(first, groups, W)
    lane = lax.broadcasted_iota(jnp.int32, (1, LANES), 1)
    lo = lane < HEAD_DIM
    row = lax.broadcasted_iota(jnp.int32, (2 * W, 1), 0)
    col = lax.broadcasted_iota(jnp.int32, (1, 3 * W), 1)
    ks = [r[...] for r in k_refs]
    vs = [r[...] for r in v_refs]
    scores = []
    for b in range(NB):
        kcat = jnp.concatenate(ks[b:b + 3], axis=0)
        for p in range(GQA_GROUP):
            q2 = q_ref[b * W:(b + 1) * W, p * LANES:(p + 1) * LANES]
            zero = jnp.zeros_like(q2)
            qq = jnp.concatenate([jnp.where(lo, q2, zero), jnp.where(lo, zero, q2)], axis=0)
            scores.append(lax.dot_general(qq, kcat, (((1,), (1,)), ((), ())),
                                          preferred_element_type=F32))
    probs = []
    for b in range(NB):
        pen_prev = jnp.where(first + b == seq_start, NEG_INF, 0.0).astype(F32)
        pen_next = jnp.where(first + b == seq_start + seq_blocks - 1, NEG_INF, 0.0).astype(F32)
        pen = jnp.where(col < W, pen_prev, jnp.where(col >= 2 * W, pen_next, 0.0))
        for p in range(GQA_GROUP):
            sc = scores[b * GQA_GROUP + p] + bias_ref[p] + pen
            sink = jnp.where(row < W, sink_ref[p], sink_ref[p + GQA_GROUP])
            mx = jnp.maximum(jnp.max(sc, axis=-1, keepdims=True), sink)
            e = jnp.exp2(sc - mx)
            probs.append((e.astype(BF16),
                          jnp.sum(e, axis=-1, keepdims=True) + jnp.exp2(sink - mx)))
    for b in range(NB):
        vcat = jnp.concatenate(vs[b:b + 3], axis=0)
        outs = []
        for p in range(GQA_GROUP):
            e, l = probs[b * GQA_GROUP + p]
            o2 = jnp.dot(e, vcat, preferred_element_type=F32) / l
            outs.append(jnp.where(lo, o2[:W], o2[W:]))
        o = jnp.concatenate(outs, axis=-1)
        o_ref[b * W:(b + 1) * W] = _rms(o, gn_ref[...]).astype(BF16)


def _wa_attention(proj, bias_tab, sink, gn_wa_perm, groups):
    n_tok = proj.shape[0]
    W = WA_BLOCK
    q_col = 3 * D_NA // D_WA
    k_col = (3 * D_NA + D_WA) // D_KV_WA
    v_col = k_col + 1

    NB = WA_BLOCKS_PER_STEP

    def kv(col, delta):
        def index_map(s):
            start, length = _seq_bounds(s * NB, groups, W)
            return (jnp.clip(s * NB + delta, start, start + length - 1), col)
        return pl.BlockSpec((W, D_KV_WA), index_map)

    deltas = range(-1, NB + 1)
    return pl.pallas_call(
        functools.partial(_wa_kernel, groups=groups),
        grid=(n_tok // (NB * W),),
        in_specs=[pl.BlockSpec(memory_space=pltpu.SMEM),
                  pl.BlockSpec((NB * W, D_WA), lambda s: (s, q_col)),
                  *[kv(k_col, d) for d in deltas],
                  *[kv(v_col, d) for d in deltas],
                  pl.BlockSpec(bias_tab.shape, lambda s: (0, 0, 0)),
                  pl.BlockSpec((1, D_WA), lambda s: (0, 0))],
        out_specs=pl.BlockSpec((NB * W, D_WA), lambda s: (s, 0)),
        out_shape=jax.ShapeDtypeStruct((n_tok, D_WA), BF16),
        compiler_params=_cparams("arbitrary"),
        name="wa_attn",
    )(sink, *([proj] * (2 * NB + 5)), bias_tab, gn_wa_perm.reshape(1, D_WA))


def _postmix_kernel(na_ref, wa_ref, xp_ref, xs_ref, mod_ref, wo_ref, pm_ref, pf_ref, wr_ref, br_ref,
                    tri_ref, x1_ref, h2_ref, idx_ref, gate_ref, rank_ref, cnt_ref, carry, *, n_p):
    s = pl.program_id(0)

    @pl.when(s == 0)
    def _():
        carry[...] = jnp.zeros_like(carry)

    def body(x_ref):
        P = POST_MIX_PARTS
        H = TOKEN_TILE // P
        mixed = [jnp.dot(jnp.concatenate([na_ref[h * H:(h + 1) * H], wa_ref[h * H:(h + 1) * H]], axis=-1),
                         wo_ref[...], preferred_element_type=F32) for h in range(P)]
        h2s = []
        for h in range(P):
            x1 = x_ref[h * H:(h + 1) * H] + mod_ref[0, 2:3, :] * _rms(mixed[h], pm_ref[...])
            x1_ref[h * H:(h + 1) * H] = x1
            h2 = _rms(x1, pf_ref[...]) * (1.0 + mod_ref[0, 4:5, :]) + mod_ref[0, 3:4, :]
            for c in range(ROW_TILE):
                h2_ref[pl.ds(h * H * ROW_TILE + c, H, stride=ROW_TILE), :] = h2[:, c * LANES:(c + 1) * LANES]
            h2s.append(h2.astype(BF16))
        wr = wr_ref[...].astype(BF16)
        logits = jnp.concatenate(
            [lax.dot_general(wr, h2, (((1,), (1,)), ((), ())), preferred_element_type=F32)
             for h2 in h2s], axis=-1) + br_ref[...]
        eidx = lax.broadcasted_iota(jnp.int32, logits.shape, 0)
        tops, sels = [], []
        l = logits
        for _ in range(TOP_K):
            m = jnp.max(l, axis=0, keepdims=True)
            sel = jnp.min(jnp.where(l == m, eidx, N_EXPERTS), axis=0, keepdims=True)
            tops.append(m)
            sels.append(sel)
            l = jnp.where(eidx == sel, -jnp.inf, l)
        es = [jnp.exp(t - tops[0]) for t in tops]
        tot = es[0] + es[1] + es[2] + es[3]
        idx_ref[0] = jnp.concatenate(sels, axis=0)
        gate_ref[0] = jnp.concatenate([e / tot for e in es], axis=0)
        hits = [eidx == sel for sel in sels]
        onehot = sum(h.astype(F32) for h in hits)
        prefix = jnp.dot(onehot.astype(BF16), tri_ref[...], preferred_element_type=F32)
        rank_e = carry[...] + prefix - onehot
        rank_ref[0] = jnp.concatenate(
            [jnp.sum(jnp.where(h, rank_e, 0.0), axis=0, keepdims=True) for h in hits],
            axis=0).astype(jnp.int32)
        carry[...] = carry[...] + jnp.sum(onehot, axis=1, keepdims=True)
        cnt_ref[...] = carry[...]

    pl.when(s < n_p)(lambda: body(xp_ref))
    pl.when(s >= n_p)(lambda: body(xs_ref))


def _post_mix(na, wa, xp, xs, mod, w_out_bf16, post_mix, pre_ffn, w_router_t, b_router, groups):
    D = D_MODEL
    tm = TOKEN_TILE
    n_p, n_s = xp.shape[0] // tm, xs.shape[0] // tm
    nt = n_p + n_s
    tri = (np.arange(tm)[:, None] <= np.arange(tm)[None, :]).astype(np.float32)
    tok = lambda s: (s, 0)
    const2 = lambda s: (0, 0)
    tile3 = lambda s: (s, 0, 0)
    return pl.pallas_call(
        functools.partial(_postmix_kernel, n_p=n_p),
        grid=(nt,),
        in_specs=[pl.BlockSpec((tm, D_NA), tok),
                  pl.BlockSpec((tm, D_WA), tok),
                  pl.BlockSpec((tm, D), lambda s: (jnp.minimum(s, n_p - 1), 0)),
                  pl.BlockSpec((tm, D), lambda s: (jnp.maximum(s - n_p, 0), 0)),
                  pl.BlockSpec((1, 6, D), lambda s: (_batch_of_tile(s, groups, tm), 0, 0)),
                  pl.BlockSpec((D, D), const2),
                  pl.BlockSpec((1, D), const2),
                  pl.BlockSpec((1, D), const2),
                  pl.BlockSpec((N_EXPERTS, D), const2),
                  pl.BlockSpec((N_EXPERTS, 1), const2),
                  pl.BlockSpec((tm, tm), const2)],
        out_specs=[pl.BlockSpec((tm, D), tok),
                   pl.BlockSpec((tm * ROW_TILE, LANES), tok),
                   pl.BlockSpec((1, TOP_K, tm), tile3),
                   pl.BlockSpec((1, TOP_K, tm), tile3),
                   pl.BlockSpec((1, TOP_K, tm), tile3),
                   pl.BlockSpec((N_EXPERTS, 1), const2)],
        out_shape=[jax.ShapeDtypeStruct((nt * tm, D), F32),
                   jax.ShapeDtypeStruct((nt * tm * ROW_TILE, LANES), F32),
                   jax.ShapeDtypeStruct((nt, TOP_K, tm), jnp.int32),
                   jax.ShapeDtypeStruct((nt, TOP_K, tm), F32),
                   jax.ShapeDtypeStruct((nt, TOP_K, tm), jnp.int32),
                   jax.ShapeDtypeStruct((N_EXPERTS, 1), F32)],
        scratch_shapes=[pltpu.VMEM((N_EXPERTS, 1), F32)],
        compiler_params=_cparams("arbitrary"),
        name="post_mix",
    )(na, wa, xp, xs, mod, w_out_bf16, post_mix.reshape(1, D), pre_ffn.reshape(1, D),
      w_router_t, b_router.reshape(N_EXPERTS, 1), jnp.asarray(tri, BF16))


def _expert_kernel(blk_e_ref, src_hbm, dst_hbm, h_hbm, wgu_ref, bgu_ref, wd_ref, bd_ref, y_hbm,
                   src_smem, dst_smem, xbuf, xb, act_bf, ob0, ob1, ob2, wgu_bf, wd_bf,
                   idx_sem, g_sem, s_sem, *, n_blocks):
    i = pl.program_id(0)
    BM = EXPERT_BLOCK
    RT = ROW_TILE
    last = n_blocks - 1
    obufs = (ob0, ob1, ob2)

    def idx_copies(blk, slot):
        return (pltpu.make_async_copy(src_hbm.at[blk], src_smem.at[slot], idx_sem.at[slot]),
                pltpu.make_async_copy(dst_hbm.at[blk], dst_smem.at[slot], idx_sem.at[slot]))

    def start_idx(blk, slot):
        for c in idx_copies(blk, slot):
            c.start()

    def wait_idx(blk, slot):
        for c in idx_copies(blk, slot):
            c.wait()

    def paced(row, prev, rounds):
        z = prev
        for _ in range(rounds):
            z = z + lax.shift_right_arithmetic(z, 31)
        return row + lax.shift_right_arithmetic(z, 31)

    def start_gather(islot, xslot, rounds):
        prev = src_smem[islot, 0, 0]
        for r in range(BM):
            prev = paced(src_smem[islot, 0, r], prev, rounds)
            row = pl.multiple_of(prev, RT)
            pltpu.make_async_copy(h_hbm.at[pl.ds(row, RT), :], xbuf.at[xslot, pl.ds(r * RT, RT), :],
                                  g_sem.at[xslot]).start()

    def wait_gather(xslot):
        pltpu.make_async_copy(h_hbm.at[pl.ds(0, BM * RT), :], xbuf.at[xslot], g_sem.at[xslot]).wait()

    def start_scatter(islot, k, rounds):
        prev = dst_smem[islot, 0, 0]
        for r in range(BM):
            prev = paced(dst_smem[islot, 0, r], prev, rounds)
            row = pl.multiple_of(prev, RT)
            pltpu.make_async_copy(obufs[k].at[pl.ds(r * RT, RT), :], y_hbm.at[pl.ds(row, RT), :],
                                  s_sem.at[k]).start(priority=r % 2)

    def wait_scatter(k):
        pltpu.make_async_copy(obufs[k], y_hbm.at[pl.ds(0, BM * RT), :], s_sem.at[k]).wait()

    xs = lax.rem(i, X_SLOTS)

    @pl.when(i == 0)
    def _():
        for b in range(min(3, n_blocks)):
            start_idx(b, b)
        for b in range(min(2, n_blocks)):
            wait_idx(b, b)
            start_gather(b, b, 0)

    @pl.when(i + 3 < n_blocks)
    def _():
        start_idx(i + 3, lax.rem(i + 3, IDX_SLOTS))

    @pl.when(i + 2 < n_blocks)
    def _():
        wait_idx(i + 2, lax.rem(i + 2, IDX_SLOTS))

    wait_gather(xs)

    new_expert = jnp.logical_or(i == 0, blk_e_ref[i] != blk_e_ref[jnp.maximum(i - 1, 0)])

    @pl.when(new_expert)
    def _():
        wgu_bf[...] = wgu_ref[0].astype(BF16)
        wd_bf[...] = wd_ref[0].astype(BF16)

    def up_phase():
        for c in range(RT):
            xb[:, c * LANES:(c + 1) * LANES] = xbuf[xs, pl.ds(c, BM, stride=RT), :].astype(BF16)
        start_gather(lax.rem(jnp.minimum(i + 2, last), IDX_SLOTS), lax.rem(i + 2, X_SLOTS),
                     GATHER_PACING)
        gu = jnp.dot(xb[...], wgu_bf[...], preferred_element_type=F32) + bgu_ref[0]
        glu = jnp.minimum(gu[:, :D_EXPERT], SWIGLU_LIMIT)
        lin = jnp.clip(gu[:, D_EXPERT:], -SWIGLU_LIMIT, SWIGLU_LIMIT)
        act = glu * (1.0 / (1.0 + jnp.exp(-SWIGLU_ALPHA * glu))) * (lin + 1.0)
        act_bf[...] = act.astype(BF16)

    def down_phase(k, scatter_prev):
        if scatter_prev:
            start_scatter(lax.rem(i - 1, IDX_SLOTS), (k - 1) % O_SLOTS, SCATTER_PACING)
        y = jnp.dot(act_bf[...], wd_bf[...], preferred_element_type=F32) + bd_ref[0]
        for c in range(RT):
            obufs[k][pl.ds(c, BM, stride=RT), :] = y[:, c * LANES:(c + 1) * LANES]

    pl.when(i >= 0)(up_phase)
    for k in range(O_SLOTS):
        @pl.when(jnp.logical_and(i >= O_SLOTS, lax.rem(i, O_SLOTS) == k))
        def _():
            wait_scatter(k)
    pl.when(i == 0)(lambda: down_phase(0, False))
    for k in range(O_SLOTS):
        pl.when(jnp.logical_and(i > 0, lax.rem(i, O_SLOTS) == k))(
            functools.partial(down_phase, k, True))

    @pl.when(i == last)
    def _():
        start_scatter(last % IDX_SLOTS, last % O_SLOTS, 0)
        for b in range(max(0, last - O_SLOTS + 1), last + 1):
            wait_scatter(b % O_SLOTS)
        wait_gather(lax.rem(i + 1, X_SLOTS))
        wait_gather(lax.rem(i + 2, X_SLOTS))


def _experts(h2, src_row, dst_row, blk_e, w_gate_up, b_gate_up, w_down, b_down):
    D = D_MODEL
    BM = EXPERT_BLOCK
    n_blocks = blk_e.shape[0]
    wmap = lambda i, be: (be[i], 0, 0)
    grid_spec = pltpu.PrefetchScalarGridSpec(
        num_scalar_prefetch=1,
        grid=(n_blocks,),
        in_specs=[pl.BlockSpec(memory_space=pl.ANY),
                  pl.BlockSpec(memory_space=pl.ANY),
                  pl.BlockSpec(memory_space=pl.ANY),
                  pl.BlockSpec((1, D, 2 * D_EXPERT), wmap),
                  pl.BlockSpec((1, 1, 2 * D_EXPERT), wmap),
                  pl.BlockSpec((1, D_EXPERT, D), wmap),
                  pl.BlockSpec((1, 1, D), wmap)],
        out_specs=pl.BlockSpec(memory_space=pl.ANY),
        scratch_shapes=[pltpu.SMEM((IDX_SLOTS, 1, BM), jnp.int32),
                        pltpu.SMEM((IDX_SLOTS, 1, BM), jnp.int32),
                        pltpu.VMEM((X_SLOTS, BM * ROW_TILE, LANES), F32),
                        pltpu.VMEM((BM, D), BF16),
                        pltpu.VMEM((BM, D_EXPERT), BF16),
                        pltpu.VMEM((BM * ROW_TILE, LANES), F32),
                        pltpu.VMEM((BM * ROW_TILE, LANES), F32),
                        pltpu.VMEM((BM * ROW_TILE, LANES), F32),
                        pltpu.VMEM((D, 2 * D_EXPERT), BF16),
                        pltpu.VMEM((D_EXPERT, D), BF16),
                        pltpu.SemaphoreType.DMA((IDX_SLOTS,)),
                        pltpu.SemaphoreType.DMA((X_SLOTS,)),
                        pltpu.SemaphoreType.DMA((O_SLOTS,))],
    )
    return pl.pallas_call(
        functools.partial(_expert_kernel, n_blocks=n_blocks),
        grid_spec=grid_spec,
        out_shape=jax.ShapeDtypeStruct((n_blocks * BM * ROW_TILE, LANES), F32),
        compiler_params=_cparams("arbitrary"),
        name="experts",
    )(blk_e, src_row.reshape(n_blocks, 1, BM), dst_row.reshape(n_blocks, 1, BM), h2,
      w_gate_up, b_gate_up.reshape(N_EXPERTS, 1, -1), w_down, b_down.reshape(N_EXPERTS, 1, -1))


def _combine_kernel(y0_ref, y1_ref, y2_ref, y3_ref, gate_ref, x1_ref, mod_ref, pf_ref,
                    op_ref, os_ref, *, n_p):
    s = pl.program_id(0)
    g = gate_ref[...]

    def rows(y_ref):
        return jnp.concatenate([y_ref[pl.ds(c, COMBINE_TILE, stride=ROW_TILE), :]
                                for c in range(ROW_TILE)], axis=-1)

    m = rows(y0_ref) * g[:, 0:1]
    for k, y_ref in enumerate((y1_ref, y2_ref, y3_ref), start=1):
        m = m + rows(y_ref) * g[:, k:k + 1]
    res = x1_ref[...] + mod_ref[0, 5:6, :] * _rms(m, pf_ref[...])

    @pl.when(s < n_p)
    def _():
        op_ref[...] = res

    @pl.when(s >= n_p)
    def _():
        os_ref[...] = res


def _combine(y4, gates_tok, x1, mod, post_ffn, n_tok_p, n_tok_s, groups):
    D = D_MODEL
    tc = COMBINE_TILE
    n_p, n_s = n_tok_p // tc, n_tok_s // tc
    nt = n_p + n_s

    def slot(k):
        return pl.BlockSpec((tc * ROW_TILE, LANES), lambda s: (k * nt + s, 0))

    return pl.pallas_call(
        functools.partial(_combine_kernel, n_p=n_p),
        grid=(nt,),
        in_specs=[slot(0), slot(1), slot(2), slot(3),
                  pl.BlockSpec((tc, TOP_K), lambda s: (s, 0)),
                  pl.BlockSpec((tc, D), lambda s: (s, 0)),
                  pl.BlockSpec((1, 6, D), lambda s: (_batch_of_tile(s, groups, tc), 0, 0)),
                  pl.BlockSpec((1, D), lambda s: (0, 0))],
        out_specs=[pl.BlockSpec((tc, D), lambda s: (jnp.minimum(s, n_p - 1), 0)),
                   pl.BlockSpec((tc, D), lambda s: (jnp.maximum(s - n_p, 0), 0))],
        out_shape=[jax.ShapeDtypeStruct((n_tok_p, D), F32),
                   jax.ShapeDtypeStruct((n_tok_s, D), F32)],
        compiler_params=_cparams("arbitrary"),
        name="combine",
    )(y4, y4, y4, y4, gates_tok, x1, mod, post_ffn.reshape(1, D))


def _dispatch_kernel(padfirst_ref, padcnt_ref, nused_ref, dest_hbm, h_ref, xs_hbm, idx_smem, zero_blk,
                     idx_sem, sem, *, n_steps, n_blocks):
    s = pl.program_id(0)
    tm = TOKEN_TILE
    RT = ROW_TILE
    slot = lax.rem(s, 2)

    def idx_copy(step, sl):
        return pltpu.make_async_copy(dest_hbm.at[step], idx_smem.at[sl], idx_sem.at[sl])

    @pl.when(s == 0)
    def _():
        idx_copy(0, 0).start()
        zero_blk[...] = jnp.zeros_like(zero_blk)

        def pad_copies(e, wait):
            first = padfirst_ref[e]
            cnt = padcnt_ref[e]
            for b in reversed(range(EXPERT_BLOCK.bit_length() - 1)):
                size = (1 << b) * RT
                bit = lax.shift_right_logical(cnt, b) & 1

                @pl.when(bit == 1)
                def _():
                    cp = pltpu.make_async_copy(
                        zero_blk.at[pl.ds(0, size), :],
                        xs_hbm.at[pl.ds(pl.multiple_of(first * RT, RT), size), :], sem)
                    cp.wait() if wait else cp.start()

                first = first + bit * (1 << b)

        def issue(e, c):
            pad_copies(e, False)
            return c

        def drain(e, c):
            pad_copies(e, True)
            return c

        lax.fori_loop(0, N_EXPERTS, issue, 0)
        lax.fori_loop(0, N_EXPERTS, drain, 0)

        def blk_copy(b):
            start = pl.multiple_of(b * (EXPERT_BLOCK * RT), EXPERT_BLOCK * RT)
            return pltpu.make_async_copy(zero_blk, xs_hbm.at[pl.ds(start, EXPERT_BLOCK * RT), :], sem)

        def issue_blk(b, c):
            blk_copy(b).start()
            return c

        def drain_blk(b, c):
            blk_copy(b).wait()
            return c

        lax.fori_loop(nused_ref[0], n_blocks, issue_blk, 0)
        lax.fori_loop(nused_ref[0], n_blocks, drain_blk, 0)

    idx_copy(s, slot).wait()
    if n_steps > 1:
        @pl.when(s + 1 < n_steps)
        def _():
            idx_copy(s + 1, 1 - slot).start()

    for k in range(TOP_K):
        for j in range(tm):
            row = pl.multiple_of(idx_smem[slot, 0, k * tm + j], RT)
            pltpu.make_async_copy(h_ref.at[pl.ds(j * RT, RT), :], xs_hbm.at[pl.ds(row, RT), :],
                                  sem).start(priority=j % 2)
    for k in range(TOP_K):
        pltpu.make_async_copy(h_ref, xs_hbm.at[pl.ds(0, tm * RT), :], sem).wait()


def _dispatch(h2, dest_rows, pad_first, pad_cnt, n_used, n_blocks):
    tm = TOKEN_TILE
    n_steps = h2.shape[0] // (tm * ROW_TILE)
    grid_spec = pltpu.PrefetchScalarGridSpec(
        num_scalar_prefetch=3,
        grid=(n_steps,),
        in_specs=[pl.BlockSpec(memory_space=pl.ANY),
                  pl.BlockSpec((tm * ROW_TILE, LANES), lambda s, pf, pc, nused: (s, 0))],
        out_specs=pl.BlockSpec(memory_space=pl.ANY),
        scratch_shapes=[pltpu.SMEM((2, 1, TOP_K * tm), jnp.int32),
                        pltpu.VMEM((EXPERT_BLOCK * ROW_TILE, LANES), F32),
                        pltpu.SemaphoreType.DMA((2,)),
                        pltpu.SemaphoreType.DMA],
    )
    return pl.pallas_call(
        functools.partial(_dispatch_kernel, n_steps=n_steps, n_blocks=n_blocks),
        grid_spec=grid_spec,
        out_shape=jax.ShapeDtypeStruct((n_blocks * EXPERT_BLOCK * ROW_TILE, LANES), F32),
        compiler_params=_cparams("arbitrary"),
        name="dispatch",
    )(pad_first, pad_cnt, n_used, dest_rows.reshape(n_steps, 1, TOP_K * tm), h2)


def _mlp_kernel(blk_e_ref, nused_ref, x_ref, wgu_ref, bgu_ref, wd_ref, bd_ref, o_ref, wgu_bf, wd_bf):
    i = pl.program_id(0)
    BM = EXPERT_BLOCK
    RT = ROW_TILE

    @pl.when(i < nused_ref[0])
    def _():
        new_expert = jnp.logical_or(i == 0, blk_e_ref[i] != blk_e_ref[jnp.maximum(i - 1, 0)])

        @pl.when(new_expert)
        def _():
            wgu_bf[...] = wgu_ref[0].astype(BF16)
            wd_bf[...] = wd_ref[0].astype(BF16)

        x = jnp.concatenate([x_ref[pl.ds(c, BM, stride=RT), :] for c in range(RT)],
                            axis=-1).astype(BF16)
        H = D_EXPERT // MLP_CHUNKS
        ups = []
        for h in range(MLP_CHUNKS):
            g = jnp.dot(x, wgu_bf[:, h * H:(h + 1) * H], preferred_element_type=F32)
            u = jnp.dot(x, wgu_bf[:, D_EXPERT + h * H:D_EXPERT + (h + 1) * H],
                        preferred_element_type=F32)
            ups.append((g + bgu_ref[0, :, h * H:(h + 1) * H],
                        u + bgu_ref[0, :, D_EXPERT + h * H:D_EXPERT + (h + 1) * H]))
        y = bd_ref[0]
        for h, (g, u) in enumerate(ups):
            glu = jnp.minimum(g, SWIGLU_LIMIT)
            lin = jnp.clip(u, -SWIGLU_LIMIT, SWIGLU_LIMIT)
            act = glu * (1.0 / (1.0 + jnp.exp(-SWIGLU_ALPHA * glu))) * (lin + 1.0)
            y = y + jnp.dot(act.astype(BF16), wd_bf[h * H:(h + 1) * H, :], preferred_element_type=F32)
        for c in range(RT):
            o_ref[pl.ds(c, BM, stride=RT), :] = y[:, c * LANES:(c + 1) * LANES]

    @pl.when(i >= nused_ref[0])
    def _():
        o_ref[...] = jnp.zeros_like(o_ref)


def _expert_mlp(xs, blk_e, n_used, w_gate_up, b_gate_up, w_down, b_down):
    D = D_MODEL
    BM = EXPERT_BLOCK
    n_blocks = blk_e.shape[0]
    wmap = lambda i, be, nu: (be[i], 0, 0)
    grid_spec = pltpu.PrefetchScalarGridSpec(
        num_scalar_prefetch=2,
        grid=(n_blocks,),
        in_specs=[pl.BlockSpec((BM * ROW_TILE, LANES), lambda i, be, nu: (jnp.minimum(i, nu[0] - 1), 0)),
                  pl.BlockSpec((1, D, 2 * D_EXPERT), wmap),
                  pl.BlockSpec((1, 1, 2 * D_EXPERT), wmap),
                  pl.BlockSpec((1, D_EXPERT, D), wmap),
                  pl.BlockSpec((1, 1, D), wmap)],
        out_specs=pl.BlockSpec((BM * ROW_TILE, LANES), lambda i, be, nu: (i, 0)),
        scratch_shapes=[pltpu.VMEM((D, 2 * D_EXPERT), BF16),
                        pltpu.VMEM((D_EXPERT, D), BF16)],
    )
    return pl.pallas_call(
        _mlp_kernel,
        grid_spec=grid_spec,
        out_shape=jax.ShapeDtypeStruct(xs.shape, F32),
        compiler_params=_cparams("arbitrary"),
        name="expert_mlp",
    )(blk_e, n_used, xs, w_gate_up, b_gate_up.reshape(N_EXPERTS, 1, -1), w_down,
      b_down.reshape(N_EXPERTS, 1, -1))


def _gather_combine_kernel(dest_hbm, ys_hbm, gate_ref, x1_ref, mod_ref, pf_ref, op_ref, os_ref,
                           idx_smem, yb0, yb1, idx_sem, row_sem, *, n_steps, n_p):
    s = pl.program_id(0)
    tc = COMBINE_TILE
    RT = ROW_TILE
    n_rows = TOP_K * tc
    last = n_steps - 1
    ybufs = (yb0, yb1)
    n_idx = 3

    def idx_copy(step, sl):
        return pltpu.make_async_copy(dest_hbm.at[step], idx_smem.at[sl], idx_sem.at[sl])

    def start_rows(islot, b):
        for r in range(n_rows):
            row = pl.multiple_of(idx_smem[islot, 0, r], RT)
            pltpu.make_async_copy(ys_hbm.at[pl.ds(row, RT), :], ybufs[b].at[pl.ds(r * RT, RT), :],
                                  row_sem.at[b]).start(priority=r % 2)

    def wait_rows(b):
        pltpu.make_async_copy(ys_hbm.at[pl.ds(0, n_rows * RT), :], ybufs[b], row_sem.at[b]).wait()

    @pl.when(s == 0)
    def _():
        for t in range(min(2, n_steps)):
            idx_copy(t, t).start()
        idx_copy(0, 0).wait()
        start_rows(0, 0)

    @pl.when(s + 2 < n_steps)
    def _():
        idx_copy(s + 2, lax.rem(s + 2, n_idx)).start()

    @pl.when(s + 1 < n_steps)
    def _():
        idx_copy(s + 1, lax.rem(s + 1, n_idx)).wait()

    def body(cur):
        wait_rows(cur)
        start_rows(lax.rem(jnp.minimum(s + 1, last), n_idx), 1 - cur)
        g = gate_ref[...]

        def rows(k):
            return jnp.concatenate([ybufs[cur][pl.ds(k * tc * RT + c, tc, stride=RT), :]
                                    for c in range(RT)], axis=-1)

        m = rows(0) * g[:, 0:1]
        for k in range(1, TOP_K):
            m = m + rows(k) * g[:, k:k + 1]
        res = x1_ref[...] + mod_ref[0, 5:6, :] * _rms(m, pf_ref[...])

        @pl.when(s < n_p)
        def _():
            op_ref[...] = res

        @pl.when(s >= n_p)
        def _():
            os_ref[...] = res

    for cur in range(2):
        pl.when(lax.rem(s, 2) == cur)(functools.partial(body, cur))

    @pl.when(s == last)
    def _():
        wait_rows(1 - last % 2)


def _gather_combine(dest_rows, ys, gates_tok, x1, mod, post_ffn, n_tok_p, n_tok_s, groups):
    D = D_MODEL
    tc = COMBINE_TILE
    n_p, n_s = n_tok_p // tc, n_tok_s // tc
    nt = n_p + n_s
    return pl.pallas_call(
        functools.partial(_gather_combine_kernel, n_steps=nt, n_p=n_p),
        grid=(nt,),
        in_specs=[pl.BlockSpec(memory_space=pl.ANY),
                  pl.BlockSpec(memory_space=pl.ANY),
                  pl.BlockSpec((tc, TOP_K), lambda s: (s, 0)),
                  pl.BlockSpec((tc, D), lambda s: (s, 0)),
                  pl.BlockSpec((1, 6, D), lambda s: (_batch_of_tile(s, groups, tc), 0, 0)),
                  pl.BlockSpec((1, D), lambda s: (0, 0))],
        out_specs=[pl.BlockSpec((tc, D), lambda s: (jnp.minimum(s, n_p - 1), 0)),
                   pl.BlockSpec((tc, D), lambda s: (jnp.maximum(s - n_p, 0), 0))],
        out_shape=[jax.ShapeDtypeStruct((n_tok_p, D), F32),
                   jax.ShapeDtypeStruct((n_tok_s, D), F32)],
        scratch_shapes=[pltpu.SMEM((3, 1, TOP_K * tc), jnp.int32),
                        pltpu.VMEM((TOP_K * tc * ROW_TILE, LANES), F32),
                        pltpu.VMEM((TOP_K * tc * ROW_TILE, LANES), F32),
                        pltpu.SemaphoreType.DMA((3,)),
                        pltpu.SemaphoreType.DMA((2,))],
        compiler_params=_cparams("arbitrary"),
        name="combine",
    )(dest_rows.reshape(nt, 1, TOP_K * tc), ys, gates_tok, x1, mod, post_ffn.reshape(1, D))


def _routing_dest(idx_t, rank_t, counts):
    BM = EXPERT_BLOCK
    nt, _, tm = idx_t.shape
    n_blocks = (nt * tm * TOP_K) // BM + N_EXPERTS
    padded = ((counts + BM - 1) // BM) * BM
    pad_end = jnp.cumsum(padded)
    pad_start = pad_end - padded
    experts = jnp.arange(N_EXPERTS, dtype=jnp.int32)
    start_of = jnp.sum(jnp.where(idx_t[..., None] == experts, pad_start, 0), axis=-1)
    dest = (start_of + rank_t).astype(jnp.int32)
    blk_first = jnp.arange(n_blocks, dtype=jnp.int32) * BM
    blk_e = jnp.minimum(jnp.sum((pad_end[None, :] <= blk_first[:, None]).astype(jnp.int32), axis=1),
                        N_EXPERTS - 1)
    n_used = (pad_end[-1:] // BM).astype(jnp.int32)
    pad_first = (pad_start + counts).astype(jnp.int32)
    pad_cnt = (padded - counts).astype(jnp.int32)
    return dest, blk_e, n_used, pad_first, pad_cnt, n_blocks


def _routing_plan(idx_t, rank_t, counts, n_tok):
    BM = EXPERT_BLOCK
    nt, _, tm = idx_t.shape
    n_blocks = (n_tok * TOP_K) // BM + N_EXPERTS
    padded = ((counts + BM - 1) // BM) * BM
    pad_end = jnp.cumsum(padded)
    pad_start = pad_end - padded
    experts = jnp.arange(N_EXPERTS, dtype=jnp.int32)
    start_of = jnp.sum(jnp.where(idx_t[..., None] == experts, pad_start, 0), axis=-1)
    dest = (start_of + rank_t).astype(jnp.int32)
    t_of = (jnp.arange(nt, dtype=jnp.int32)[:, None, None] * tm
            + jnp.arange(tm, dtype=jnp.int32)[None, None, :])
    slot_row = jnp.arange(TOP_K, dtype=jnp.int32)[None, :, None] * n_tok + t_of
    blk_first = jnp.arange(n_blocks, dtype=jnp.int32) * BM
    blk_e = jnp.minimum(jnp.sum((pad_end[None, :] <= blk_first[:, None]).astype(jnp.int32), axis=1),
                        N_EXPERTS - 1)
    of_blk = blk_e[:, None] == experts
    per_blk = lambda v: jnp.sum(jnp.where(of_blk, v, 0), axis=1)[:, None]
    pos = blk_first[:, None] + jnp.arange(BM, dtype=jnp.int32)[None, :]
    real_before = per_blk(jnp.cumsum(counts) - counts) + jnp.minimum(pos - per_blk(pad_start),
                                                                     per_blk(counts))
    spare = n_tok * TOP_K + pos - real_before
    inv = spare.reshape(-1).astype(jnp.int32).at[dest.reshape(-1)].set(
        slot_row.reshape(-1), unique_indices=True, mode='promise_in_bounds')
    src_row = jnp.where(inv < n_tok * TOP_K, inv % n_tok, 0) * ROW_TILE
    dst_row = inv * ROW_TILE
    return src_row, dst_row, blk_e


def kernel(x_prompt, x_sample, c_prompt, c_sample, w_ada, b_ada, pre_mix, post_mix, pre_ffn, post_ffn,
           w_in, na_rel_bias, t5_rel_bias, wa_sink, gn_na, gn_wa, w_out, w_router, b_router,
           w_gate_up, b_gate_up, w_down, b_down):
    depth = w_ada.shape[0]
    D = D_MODEL
    groups = (x_prompt.shape[:2], x_sample.shape[:2])
    n_tok_p = x_prompt.shape[0] * x_prompt.shape[1]
    n_tok_s = x_sample.shape[0] * x_sample.shape[1]
    n_tok = n_tok_p + n_tok_s
    yp = x_prompt.reshape(n_tok_p, D)
    ys = x_sample.reshape(n_tok_s, D)
    c_all = jnp.concatenate([c_prompt, c_sample], axis=0)
    for l in range(depth):
        mod = _ada(c_all, w_ada[l], b_ada[l]).reshape(c_all.shape[0], 6, D)
        qb0 = 3 * D_NA
        scale = HEAD_DIM ** -0.5 * LOG2E
        w_in_l = jnp.concatenate([w_in[l][:, :D_NA] * scale, w_in[l][:, D_NA:qb0],
                                  _wa_slab_order(w_in[l][:, qb0:qb0 + D_WA] * scale, axis=1),
                                  w_in[l][:, qb0 + D_WA:]], axis=1).astype(BF16)
        w_out_l = jnp.concatenate([w_out[l][:D_NA], _wa_slab_order(w_out[l][D_NA:], axis=0)],
                                  axis=0).astype(BF16)
        proj = _in_proj(yp, ys, mod, pre_mix[l], w_in_l, groups)
        na = _na_attention(proj, _na_bias_table(na_rel_bias[l] * LOG2E), gn_na[l], groups)
        wa = _wa_attention(proj, _wa_bias_table(t5_rel_bias * LOG2E), wa_sink[l] * LOG2E,
                           _wa_slab_order(gn_wa[l], axis=0), groups)
        x1, h2, idx_t, gate_t, rank_t, counts = _post_mix(
            na, wa, yp, ys, mod, w_out_l, post_mix[l], pre_ffn[l], w_router[l].T, b_router[l], groups)
        dest, blk_e, n_used, pad_first, pad_cnt, n_blocks = _routing_dest(
            idx_t, rank_t, counts.reshape(N_EXPERTS).astype(jnp.int32))
        dest_rows = dest * ROW_TILE
        xs = _dispatch(h2, dest_rows, pad_first, pad_cnt, n_used, n_blocks)
        ys_rows = _expert_mlp(xs, blk_e, n_used, w_gate_up[l], b_gate_up[l], w_down[l], b_down[l])
        gates_tok = jnp.transpose(gate_t, (0, 2, 1)).reshape(n_tok, TOP_K)
        halves = TOKEN_TILE // COMBINE_TILE
        dest_steps = jnp.transpose(dest_rows.reshape(-1, TOP_K, halves, COMBINE_TILE), (0, 2, 1, 3))
        yp, ys = _gather_combine(dest_steps, ys_rows, gates_tok, x1, mod, post_ffn[l], n_tok_p, n_tok_s,
                                 groups)
    return (yp.reshape(x_prompt.shape), ys.reshape(x_sample.shape))
```

```python
import functools
import math

import jax
import jax.numpy as jnp
import numpy as np
from jax import lax
from jax.experimental import pallas as pl
from jax.experimental.pallas import tpu as pltpu

F32 = jnp.float32
BF16 = jnp.bfloat16

D_MODEL = 1024
HEAD_DIM = 64
D_NA = 512
D_WA = 512
N_HEADS_NA = 8
N_HEADS_WA = 8
N_KV_WA = 2
GQA_GROUP = 4
D_KV_WA = 128
D_IN_PROJ = 3 * D_NA + D_WA + 2 * D_KV_WA
GRID_W = 64
NA_WIN_R = 8
NA_WIN_C = 16
WA_WINDOW = 128
WA_BLOCK = 128
T5_BUCKETS = 32
T5_MAX_EXACT = 8
T5_MAX_DIST = 128
N_EXPERTS = 32
TOP_K = 4
D_EXPERT = D_MODEL
SWIGLU_LIMIT = 7.0
SWIGLU_ALPHA = 1.702
RMS_EPS = 1e-6
NEG_INF = -1e30
LOG2E = math.log2(math.e)

VMEM_LIMIT_BYTES = 56 * 1024 * 1024
LANES = 128
ROW_TILE = D_MODEL // LANES

TOKEN_TILE = 512
IN_PROJ_PARTS = 2
POST_MIX_PARTS = 4
NA_ROWS_PER_STEP = 8
NA_ROW_UNROLL = 8
WA_BLOCKS_PER_STEP = 2
EXPERT_BLOCK = 512
MLP_CHUNKS = 2
DISPATCH_TILES = 2
COMBINE_TILE = 256


def _cparams(*sem):
    return pltpu.CompilerParams(dimension_semantics=sem, vmem_limit_bytes=VMEM_LIMIT_BYTES)


def _rms(x, g):
    return x * lax.rsqrt(jnp.mean(x * x, axis=-1, keepdims=True) + RMS_EPS) * g


def _seq_bounds(s, groups, block):
    base = 0
    start, length = None, None
    for n_seq, seq_len in groups:
        per = seq_len // block
        g_start = base + ((s - base) // per) * per
        if start is None:
            start, length = g_start, per
        else:
            inside = s >= base
            start = jnp.where(inside, g_start, start)
            length = jnp.where(inside, per, length)
        base += n_seq * per
    return start, length


def _batch_of_tile(s, groups, block):
    base_blk, base_seq = 0, 0
    out = None
    for n_seq, seq_len in groups:
        per = seq_len // block
        b = base_seq + (s - base_blk) // per
        out = b if out is None else jnp.where(s >= base_blk, b, out)
        base_blk += n_seq * per
        base_seq += n_seq
    return out


def _ada_kernel(c_ref, w_ref, b_ref, o_ref):
    c = c_ref[...]
    s = c * (1.0 / (1.0 + jnp.exp(-c)))
    o_ref[...] = jnp.dot(s, w_ref[...], preferred_element_type=F32,
                         precision=lax.Precision.HIGHEST) + b_ref[...]


def _ada(c_all, w_ada, b_ada):
    nb = c_all.shape[0]
    return pl.pallas_call(
        _ada_kernel,
        grid=(6,),
        in_specs=[pl.BlockSpec((nb, D_MODEL), lambda j: (0, 0)),
                  pl.BlockSpec((D_MODEL, D_MODEL), lambda j: (0, j)),
                  pl.BlockSpec((1, D_MODEL), lambda j: (0, j))],
        out_specs=pl.BlockSpec((nb, D_MODEL), lambda j: (0, j)),
        out_shape=jax.ShapeDtypeStruct((nb, 6 * D_MODEL), F32),
        compiler_params=_cparams("arbitrary"),
        name="ada",
    )(c_all, w_ada, b_ada.reshape(1, -1))


def _inproj_kernel(xp_ref, xs_ref, mod_ref, g_ref, w_ref, o_ref, *, n_p):
    s = pl.program_id(0)

    def body(x_ref):
        H = TOKEN_TILE // IN_PROJ_PARTS
        hs = [(_rms(x_ref[p * H:(p + 1) * H], g_ref[...]) * (1.0 + mod_ref[0, 1:2, :])
               + mod_ref[0, 0:1, :]).astype(BF16) for p in range(IN_PROJ_PARTS)]
        for p, h in enumerate(hs):
            o_ref[p * H:(p + 1) * H] = jnp.dot(h, w_ref[...], preferred_element_type=F32).astype(BF16)

    pl.when(s < n_p)(lambda: body(xp_ref))
    pl.when(s >= n_p)(lambda: body(xs_ref))


def _in_proj(xp, xs, mod, pre_mix, w_in_bf16, groups):
    D = D_MODEL
    tm = TOKEN_TILE
    n_p, n_s = xp.shape[0] // tm, xs.shape[0] // tm
    return pl.pallas_call(
        functools.partial(_inproj_kernel, n_p=n_p),
        grid=(n_p + n_s,),
        in_specs=[pl.BlockSpec((tm, D), lambda s: (jnp.minimum(s, n_p - 1), 0)),
                  pl.BlockSpec((tm, D), lambda s: (jnp.maximum(s - n_p, 0), 0)),
                  pl.BlockSpec((1, 6, D), lambda s: (_batch_of_tile(s, groups, tm), 0, 0)),
                  pl.BlockSpec((1, D), lambda s: (0, 0)),
                  pl.BlockSpec((D, D_IN_PROJ), lambda s: (0, 0))],
        out_specs=pl.BlockSpec((tm, D_IN_PROJ), lambda s: (s, 0)),
        out_shape=jax.ShapeDtypeStruct(((n_p + n_s) * tm, D_IN_PROJ), BF16),
        compiler_params=_cparams("arbitrary"),
        name="in_proj",
    )(xp, xs, mod, pre_mix.reshape(1, D), w_in_bf16)


def _na_bias_table(rpb):
    qc = np.arange(GRID_W)[:, None]
    kc = np.arange(GRID_W)[None, :]
    win_start = np.clip(qc - NA_WIN_C // 2, 0, GRID_W - NA_WIN_C)
    valid = (kc >= win_start) & (kc < win_start + NA_WIN_C)
    dc = np.clip(kc - qc, -(NA_WIN_C - 1), NA_WIN_C - 1) + NA_WIN_C - 1
    pick = (dc[None] == np.arange(2 * NA_WIN_C - 1)[:, None, None]).astype(np.float32)
    full = jnp.einsum('hrc,cqk->hrqk', rpb.astype(F32), pick,
                      precision=lax.Precision.HIGHEST)
    full = jnp.where(valid[None, None], full.astype(F32), NEG_INF)
    two = jnp.concatenate([full[:, :-1], full[:, 1:]], axis=-1)
    two = two.reshape(N_HEADS_NA // 2, 2, 2 * NA_WIN_R - 2, GRID_W, 2 * GRID_W)
    return jnp.transpose(two, (0, 2, 1, 3, 4)).reshape(
        N_HEADS_NA // 2, 2 * NA_WIN_R - 2, 2 * GRID_W, 2 * GRID_W)


def _na_kernel(q_ref, kp_ref, kc_ref, kn_ref, vp_ref, vc_ref, vn_ref, bias_ref, gn_ref, o_ref,
               k_scr, v_scr, *, groups):
    s = pl.program_id(0)
    R = NA_ROWS_PER_STEP
    blk = R * GRID_W
    seq_start, seq_blocks = _seq_bounds(s, groups, blk)
    i = s - seq_start
    rows = seq_blocks * R
    k_scr[0:blk] = kp_ref[...]
    k_scr[blk:2 * blk] = kc_ref[...]
    k_scr[2 * blk:3 * blk] = kn_ref[...]
    v_scr[0:blk] = vp_ref[...]
    v_scr[blk:2 * blk] = vc_ref[...]
    v_scr[2 * blk:3 * blk] = vn_ref[...]
    lane = lax.broadcasted_iota(jnp.int32, (1, LANES), 1)
    lo = lane < HEAD_DIM
    gn = gn_ref[...]

    n_pairs = N_HEADS_NA // 2
    n_keys = NA_WIN_R * GRID_W

    def rows_body(jj, carry):
        js = [jj * NA_ROW_UNROLL + u for u in range(NA_ROW_UNROLL)]
        geo = []
        for j in js:
            r = i * R + j
            row_start = jnp.clip(r - NA_WIN_R // 2, 0, rows - NA_WIN_R)
            geo.append((r - row_start,
                        pl.multiple_of((row_start - i * R + R) * GRID_W, GRID_W),
                        pl.multiple_of(j * GRID_W, GRID_W)))
        scores = []
        for shift, koff, qoff in geo:
            for p in range(n_pairs):
                cs = slice(p * LANES, (p + 1) * LANES)
                q2 = q_ref[pl.ds(qoff, GRID_W), cs]
                zero = jnp.zeros_like(q2)
                qq = jnp.concatenate([jnp.where(lo, q2, zero), jnp.where(lo, zero, q2)], axis=0)
                k2 = k_scr[pl.ds(koff, n_keys), cs]
                scores.append(lax.dot_general(qq, k2, (((1,), (1,)), ((), ())),
                                              preferred_element_type=F32))
        probs = []
        for u, (shift, koff, qoff) in enumerate(geo):
            for p in range(n_pairs):
                sc = scores[u * n_pairs + p]
                parts = []
                for m in range(NA_WIN_R // 2):
                    d = 2 * m - shift + (NA_WIN_R - 1)
                    parts.append(sc[:, m * LANES:(m + 1) * LANES] + bias_ref[p, d])
                sc = jnp.concatenate(parts, axis=-1)
                e = jnp.exp2(sc - jnp.max(sc, axis=-1, keepdims=True))
                probs.append((e.astype(BF16), jnp.sum(e, axis=-1, keepdims=True)))
        for u, (shift, koff, qoff) in enumerate(geo):
            outs = []
            for p in range(n_pairs):
                e, l = probs[u * n_pairs + p]
                v2 = v_scr[pl.ds(koff, n_keys), p * LANES:(p + 1) * LANES]
                o2 = jnp.dot(e, v2, preferred_element_type=F32) / l
                outs.append(jnp.where(lo, o2[:GRID_W], o2[GRID_W:]))
            o = jnp.concatenate(outs, axis=-1)
            o_ref[pl.ds(qoff, GRID_W), :] = _rms(o, gn).astype(BF16)
        return carry

    lax.fori_loop(0, R // NA_ROW_UNROLL, rows_body, 0)


def _na_attention(proj, bias_tab, gn_na, groups):
    n_tok = proj.shape[0]
    blk = NA_ROWS_PER_STEP * GRID_W

    def kv(col, delta):
        def index_map(s):
            start, length = _seq_bounds(s, groups, blk)
            return (jnp.clip(s + delta, start, start + length - 1), col)
        return pl.BlockSpec((blk, D_NA), index_map)

    return pl.pallas_call(
        functools.partial(_na_kernel, groups=groups),
        grid=(n_tok // blk,),
        in_specs=[pl.BlockSpec((blk, D_NA), lambda s: (s, 0)),
                  kv(1, -1), kv(1, 0), kv(1, 1), kv(2, -1), kv(2, 0), kv(2, 1),
                  pl.BlockSpec(bias_tab.shape, lambda s: (0, 0, 0, 0)),
                  pl.BlockSpec((1, D_NA), lambda s: (0, 0))],
        out_specs=pl.BlockSpec((blk, D_NA), lambda s: (s, 0)),
        out_shape=jax.ShapeDtypeStruct((n_tok, D_NA), BF16),
        scratch_shapes=[pltpu.VMEM((3 * blk, D_NA), BF16), pltpu.VMEM((3 * blk, D_NA), BF16)],
        compiler_params=_cparams("arbitrary"),
        name="na_attn",
    )(proj, proj, proj, proj, proj, proj, proj, bias_tab, gn_na.reshape(1, D_NA))


def _wa_slab_order(a, axis):
    shape = a.shape
    split = shape[:axis] + (N_KV_WA, GQA_GROUP, HEAD_DIM) + shape[axis + 1:]
    return jnp.swapaxes(a.reshape(split), axis, axis + 1).reshape(shape)


def _t5_bucket(rel):
    half = T5_BUCKETS // 2
    ret = (rel > 0).astype(jnp.int32) * half
    n = jnp.abs(rel)
    nf = jnp.maximum(n, 1).astype(F32)
    large = T5_MAX_EXACT + (jnp.log(nf / T5_MAX_EXACT) / math.log(T5_MAX_DIST / T5_MAX_EXACT)
                            * (half - T5_MAX_EXACT)).astype(jnp.int32)
    large = jnp.minimum(large, half - 1)
    return ret + jnp.where(n < T5_MAX_EXACT, n, large)


def _wa_bias_table(t5_table):
    rel = jnp.arange(3 * WA_BLOCK)[None, :] - WA_BLOCK - jnp.arange(WA_BLOCK)[:, None]
    pick = (_t5_bucket(rel)[..., None] == jnp.arange(T5_BUCKETS)).astype(F32)
    bias = jnp.einsum('qkb,bh->qkh', pick, t5_table.astype(F32),
                      precision=lax.Precision.HIGHEST)
    bias = jnp.where((jnp.abs(rel) <= WA_WINDOW)[..., None], bias, NEG_INF)
    bias = jnp.transpose(bias, (2, 0, 1))
    return jnp.concatenate([bias[:GQA_GROUP], bias[GQA_GROUP:]], axis=1)


def _wa_kernel(sink_ref, q_ref, *refs, groups):
    s = pl.program_id(0)
    W = WA_BLOCK
    NB = WA_BLOCKS_PER_STEP
    k_refs, v_refs = refs[:NB + 2], refs[NB + 2:2 * NB + 4]
    bias_ref, gn_ref, o_ref = refs[2 * NB + 4:]
    first = s * NB
    seq_start, seq_blocks = _seq_bounds(first, groups, W)
    lane = lax.broadcasted_iota(jnp.int32, (1, LANES), 1)
    lo = lane < HEAD_DIM
    row = lax.broadcasted_iota(jnp.int32, (2 * W, 1), 0)
    col = lax.broadcasted_iota(jnp.int32, (1, 3 * W), 1)
    ks = [r[...] for r in k_refs]
    vs = [r[...] for r in v_refs]
    scores = []
    for b in range(NB):
        kcat = jnp.concatenate(ks[b:b + 3], axis=0)
        for p in range(GQA_GROUP):
            q2 = q_ref[b * W:(b + 1) * W, p * LANES:(p + 1) * LANES]
            zero = jnp.zeros_like(q2)
            qq = jnp.concatenate([jnp.where(lo, q2, zero), jnp.where(lo, zero, q2)], axis=0)
            scores.append(lax.dot_general(qq, kcat, (((1,), (1,)), ((), ())),
                                          preferred_element_type=F32))
    probs = []
    for b in range(NB):
        pen_prev = jnp.where(first + b == seq_start, NEG_INF, 0.0).astype(F32)
        pen_next = jnp.where(first + b == seq_start + seq_blocks - 1, NEG_INF, 0.0).astype(F32)
        pen = jnp.where(col < W, pen_prev, jnp.where(col >= 2 * W, pen_next, 0.0))
        for p in range(GQA_GROUP):
            sc = scores[b * GQA_GROUP + p] + bias_ref[p] + pen
            sink = jnp.where(row < W, sink_ref[p], sink_ref[p + GQA_GROUP])
            mx = jnp.maximum(jnp.max(sc, axis=-1, keepdims=True), sink)
            e = jnp.exp2(sc - mx)
            probs.append((e.astype(BF16),
                          jnp.sum(e, axis=-1, keepdims=True) + jnp.exp2(sink - mx)))
    for b in range(NB):
        vcat = jnp.concatenate(vs[b:b + 3], axis=0)
        outs = []
        for p in range(GQA_GROUP):
            e, l = probs[b * GQA_GROUP + p]
            o2 = jnp.dot(e, vcat, preferred_element_type=F32) / l
            outs.append(jnp.where(lo, o2[:W], o2[W:]))
        o = jnp.concatenate(outs, axis=-1)
        o_ref[b * W:(b + 1) * W] = _rms(o, gn_ref[...]).astype(BF16)


def _wa_attention(proj, bias_tab, sink, gn_wa_perm, groups):
    n_tok = proj.shape[0]
    W = WA_BLOCK
    q_col = 3 * D_NA // D_WA
    k_col = (3 * D_NA + D_WA) // D_KV_WA
    v_col = k_col + 1

    NB = WA_BLOCKS_PER_STEP

    def kv(col, delta):
        def index_map(s):
            start, length = _seq_bounds(s * NB, groups, W)
            return (jnp.clip(s * NB + delta, start, start + length - 1), col)
        return pl.BlockSpec((W, D_KV_WA), index_map)

    deltas = range(-1, NB + 1)
    return pl.pallas_call(
        functools.partial(_wa_kernel, groups=groups),
        grid=(n_tok // (NB * W),),
        in_specs=[pl.BlockSpec(memory_space=pltpu.SMEM),
                  pl.BlockSpec((NB * W, D_WA), lambda s: (s, q_col)),
                  *[kv(k_col, d) for d in deltas],
                  *[kv(v_col, d) for d in deltas],
                  pl.BlockSpec(bias_tab.shape, lambda s: (0, 0, 0)),
                  pl.BlockSpec((1, D_WA), lambda s: (0, 0))],
        out_specs=pl.BlockSpec((NB * W, D_WA), lambda s: (s, 0)),
        out_shape=jax.ShapeDtypeStruct((n_tok, D_WA), BF16),
        compiler_params=_cparams("arbitrary"),
        name="wa_attn",
    )(sink, *([proj] * (2 * NB + 5)), bias_tab, gn_wa_perm.reshape(1, D_WA))


def _postmix_kernel(na_ref, wa_ref, xp_ref, xs_ref, mod_ref, wo_ref, pm_ref, pf_ref, wr_ref, br_ref,
                    tri_ref, x1_ref, h2_ref, idx_ref, gate_ref, rank_ref, cnt_ref, carry, *, n_p):
    s = pl.program_id(0)

    @pl.when(s == 0)
    def _():
        carry[...] = jnp.zeros_like(carry)

    def body(x_ref):
        P = POST_MIX_PARTS
        H = TOKEN_TILE // P
        mixed = [jnp.dot(jnp.concatenate([na_ref[h * H:(h + 1) * H], wa_ref[h * H:(h + 1) * H]], axis=-1),
                         wo_ref[...], preferred_element_type=F32) for h in range(P)]
        h2s = []
        for h in range(P):
            x1 = x_ref[h * H:(h + 1) * H] + mod_ref[0, 2:3, :] * _rms(mixed[h], pm_ref[...])
            x1_ref[h * H:(h + 1) * H] = x1
            h2 = _rms(x1, pf_ref[...]) * (1.0 + mod_ref[0, 4:5, :]) + mod_ref[0, 3:4, :]
            for c in range(ROW_TILE):
                h2_ref[pl.ds(h * H * ROW_TILE + c, H, stride=ROW_TILE), :] = h2[:, c * LANES:(c + 1) * LANES]
            h2s.append(h2.astype(BF16))
        wr = wr_ref[...].astype(BF16)
        logits = jnp.concatenate(
            [lax.dot_general(wr, h2, (((1,), (1,)), ((), ())), preferred_element_type=F32)
             for h2 in h2s], axis=-1) + br_ref[...]
        eidx = lax.broadcasted_iota(jnp.int32, logits.shape, 0)
        tops, sels = [], []
        l = logits
        for _ in range(TOP_K):
            m = jnp.max(l, axis=0, keepdims=True)
            sel = jnp.min(jnp.where(l == m, eidx, N_EXPERTS), axis=0, keepdims=True)
            tops.append(m)
            sels.append(sel)
            l = jnp.where(eidx == sel, -jnp.inf, l)
        es = [jnp.exp(t - tops[0]) for t in tops]
        tot = es[0] + es[1] + es[2] + es[3]
        idx_ref[0] = jnp.concatenate(sels, axis=0)
        gate_ref[0] = jnp.concatenate([e / tot for e in es], axis=0)
        hits = [eidx == sel for sel in sels]
        onehot = sum(h.astype(F32) for h in hits)
        prefix = jnp.dot(onehot.astype(BF16), tri_ref[...], preferred_element_type=F32)
        rank_e = carry[...] + prefix - onehot
        rank_ref[0] = jnp.concatenate(
            [jnp.sum(jnp.where(h, rank_e, 0.0), axis=0, keepdims=True) for h in hits],
            axis=0).astype(jnp.int32)
        carry[...] = carry[...] + jnp.sum(onehot, axis=1, keepdims=True)
        cnt_ref[...] = carry[...]

    pl.when(s < n_p)(lambda: body(xp_ref))
    pl.when(s >= n_p)(lambda: body(xs_ref))


def _post_mix(na, wa, xp, xs, mod, w_out_bf16, post_mix, pre_ffn, w_router_t, b_router, groups):
    D = D_MODEL
    tm = TOKEN_TILE
    n_p, n_s = xp.shape[0] // tm, xs.shape[0] // tm
    nt = n_p + n_s
    tri = (np.arange(tm)[:, None] <= np.arange(tm)[None, :]).astype(np.float32)
    tok = lambda s: (s, 0)
    const2 = lambda s: (0, 0)
    tile3 = lambda s: (s, 0, 0)
    return pl.pallas_call(
        functools.partial(_postmix_kernel, n_p=n_p),
        grid=(nt,),
        in_specs=[pl.BlockSpec((tm, D_NA), tok),
                  pl.BlockSpec((tm, D_WA), tok),
                  pl.BlockSpec((tm, D), lambda s: (jnp.minimum(s, n_p - 1), 0)),
                  pl.BlockSpec((tm, D), lambda s: (jnp.maximum(s - n_p, 0), 0)),
                  pl.BlockSpec((1, 6, D), lambda s: (_batch_of_tile(s, groups, tm), 0, 0)),
                  pl.BlockSpec((D, D), const2),
                  pl.BlockSpec((1, D), const2),
                  pl.BlockSpec((1, D), const2),
                  pl.BlockSpec((N_EXPERTS, D), const2),
                  pl.BlockSpec((N_EXPERTS, 1), const2),
                  pl.BlockSpec((tm, tm), const2)],
        out_specs=[pl.BlockSpec((tm, D), tok),
                   pl.BlockSpec((tm * ROW_TILE, LANES), tok),
                   pl.BlockSpec((1, TOP_K, tm), tile3),
                   pl.BlockSpec((1, TOP_K, tm), tile3),
                   pl.BlockSpec((1, TOP_K, tm), tile3),
                   pl.BlockSpec((N_EXPERTS, 1), const2)],
        out_shape=[jax.ShapeDtypeStruct((nt * tm, D), F32),
                   jax.ShapeDtypeStruct((nt * tm * ROW_TILE, LANES), F32),
                   jax.ShapeDtypeStruct((nt, TOP_K, tm), jnp.int32),
                   jax.ShapeDtypeStruct((nt, TOP_K, tm), F32),
                   jax.ShapeDtypeStruct((nt, TOP_K, tm), jnp.int32),
                   jax.ShapeDtypeStruct((N_EXPERTS, 1), F32)],
        scratch_shapes=[pltpu.VMEM((N_EXPERTS, 1), F32)],
        compiler_params=_cparams("arbitrary"),
        name="post_mix",
    )(na, wa, xp, xs, mod, w_out_bf16, post_mix.reshape(1, D), pre_ffn.reshape(1, D),
      w_router_t, b_router.reshape(N_EXPERTS, 1), jnp.asarray(tri, BF16))


def _dispatch_kernel(padfirst_ref, padcnt_ref, nused_ref, dest_hbm, h_ref, xs_hbm, idx_smem, zero_blk,
                     idx_sem, sem, *, n_steps, n_blocks):
    s = pl.program_id(0)
    tm = TOKEN_TILE
    RT = ROW_TILE
    slot = lax.rem(s, 2)

    def idx_copy(step, sl):
        return pltpu.make_async_copy(dest_hbm.at[step], idx_smem.at[sl], idx_sem.at[sl])

    @pl.when(s == 0)
    def _():
        idx_copy(0, 0).start()
        zero_blk[...] = jnp.zeros_like(zero_blk)

        def pad_copies(e, wait):
            first = padfirst_ref[e]
            cnt = padcnt_ref[e]
            for b in reversed(range(EXPERT_BLOCK.bit_length() - 1)):
                size = (1 << b) * RT
                bit = lax.shift_right_logical(cnt, b) & 1

                @pl.when(bit == 1)
                def _():
                    cp = pltpu.make_async_copy(
                        zero_blk.at[pl.ds(0, size), :],
                        xs_hbm.at[pl.ds(pl.multiple_of(first * RT, RT), size), :], sem)
                    cp.wait() if wait else cp.start()

                first = first + bit * (1 << b)

        def issue(e, c):
            pad_copies(e, False)
            return c

        def drain(e, c):
            pad_copies(e, True)
            return c

        lax.fori_loop(0, N_EXPERTS, issue, 0)
        lax.fori_loop(0, N_EXPERTS, drain, 0)

        def blk_copy(b):
            start = pl.multiple_of(b * (EXPERT_BLOCK * RT), EXPERT_BLOCK * RT)
            return pltpu.make_async_copy(zero_blk, xs_hbm.at[pl.ds(start, EXPERT_BLOCK * RT), :], sem)

        def issue_blk(b, c):
            blk_copy(b).start()
            return c

        def drain_blk(b, c):
            blk_copy(b).wait()
            return c

        lax.fori_loop(nused_ref[0], n_blocks, issue_blk, 0)
        lax.fori_loop(nused_ref[0], n_blocks, drain_blk, 0)

    idx_copy(s, slot).wait()
    if n_steps > 1:
        @pl.when(s + 1 < n_steps)
        def _():
            idx_copy(s + 1, 1 - slot).start()

    for u in range(DISPATCH_TILES):
        for k in range(TOP_K):
            for j in range(tm):
                row = pl.multiple_of(idx_smem[slot, 0, (u * TOP_K + k) * tm + j], RT)
                pltpu.make_async_copy(h_ref.at[pl.ds((u * tm + j) * RT, RT), :],
                                      xs_hbm.at[pl.ds(row, RT), :], sem).start(priority=j % 2)
    for k in range(TOP_K):
        pltpu.make_async_copy(h_ref, xs_hbm.at[pl.ds(0, DISPATCH_TILES * tm * RT), :], sem).wait()


def _dispatch(h2, dest_rows, pad_first, pad_cnt, n_used, n_blocks):
    tm = DISPATCH_TILES * TOKEN_TILE
    n_steps = h2.shape[0] // (tm * ROW_TILE)
    grid_spec = pltpu.PrefetchScalarGridSpec(
        num_scalar_prefetch=3,
        grid=(n_steps,),
        in_specs=[pl.BlockSpec(memory_space=pl.ANY),
                  pl.BlockSpec((tm * ROW_TILE, LANES), lambda s, pf, pc, nused: (s, 0))],
        out_specs=pl.BlockSpec(memory_space=pl.ANY),
        scratch_shapes=[pltpu.SMEM((2, 1, TOP_K * tm), jnp.int32),
                        pltpu.VMEM((EXPERT_BLOCK * ROW_TILE, LANES), F32),
                        pltpu.SemaphoreType.DMA((2,)),
                        pltpu.SemaphoreType.DMA],
    )
    return pl.pallas_call(
        functools.partial(_dispatch_kernel, n_steps=n_steps, n_blocks=n_blocks),
        grid_spec=grid_spec,
        out_shape=jax.ShapeDtypeStruct((n_blocks * EXPERT_BLOCK * ROW_TILE, LANES), F32),
        compiler_params=_cparams("arbitrary"),
        name="dispatch",
    )(pad_first, pad_cnt, n_used, dest_rows.reshape(n_steps, 1, TOP_K * tm), h2)


def _mlp_kernel(blk_e_ref, nused_ref, x_ref, wgu_ref, bgu_ref, wd_ref, bd_ref, o_ref, wgu_bf, wd_bf):
    i = pl.program_id(0)
    BM = EXPERT_BLOCK
    RT = ROW_TILE

    @pl.when(i < nused_ref[0])
    def _():
        new_expert = jnp.logical_or(i == 0, blk_e_ref[i] != blk_e_ref[jnp.maximum(i - 1, 0)])

        @pl.when(new_expert)
        def _():
            wgu_bf[...] = wgu_ref[0].astype(BF16)
            wd_bf[...] = wd_ref[0].astype(BF16)

        x = jnp.concatenate([x_ref[pl.ds(c, BM, stride=RT), :] for c in range(RT)],
                            axis=-1).astype(BF16)
        H = D_EXPERT // MLP_CHUNKS
        ups = []
        for h in range(MLP_CHUNKS):
            g = jnp.dot(x, wgu_bf[:, h * H:(h + 1) * H], preferred_element_type=F32)
            u = jnp.dot(x, wgu_bf[:, D_EXPERT + h * H:D_EXPERT + (h + 1) * H],
                        preferred_element_type=F32)
            ups.append((g + bgu_ref[0, :, h * H:(h + 1) * H],
                        u + bgu_ref[0, :, D_EXPERT + h * H:D_EXPERT + (h + 1) * H]))
        y = bd_ref[0]
        for h, (g, u) in enumerate(ups):
            glu = jnp.minimum(g, SWIGLU_LIMIT)
            lin = jnp.clip(u, -SWIGLU_LIMIT, SWIGLU_LIMIT)
            act = glu * (1.0 / (1.0 + jnp.exp(-SWIGLU_ALPHA * glu))) * (lin + 1.0)
            y = y + jnp.dot(act.astype(BF16), wd_bf[h * H:(h + 1) * H, :], preferred_element_type=F32)
        for c in range(RT):
            o_ref[pl.ds(c, BM, stride=RT), :] = y[:, c * LANES:(c + 1) * LANES]

    @pl.when(i >= nused_ref[0])
    def _():
        o_ref[...] = jnp.zeros_like(o_ref)


def _expert_mlp(xs, blk_e, n_used, w_gate_up, b_gate_up, w_down, b_down):
    D = D_MODEL
    BM = EXPERT_BLOCK
    n_blocks = blk_e.shape[0]
    wmap = lambda i, be, nu: (be[i], 0, 0)
    grid_spec = pltpu.PrefetchScalarGridSpec(
        num_scalar_prefetch=2,
        grid=(n_blocks,),
        in_specs=[pl.BlockSpec((BM * ROW_TILE, LANES), lambda i, be, nu: (jnp.minimum(i, nu[0] - 1), 0)),
                  pl.BlockSpec((1, D, 2 * D_EXPERT), wmap),
                  pl.BlockSpec((1, 1, 2 * D_EXPERT), wmap),
                  pl.BlockSpec((1, D_EXPERT, D), wmap),
                  pl.BlockSpec((1, 1, D), wmap)],
        out_specs=pl.BlockSpec((BM * ROW_TILE, LANES), lambda i, be, nu: (i, 0)),
        scratch_shapes=[pltpu.VMEM((D, 2 * D_EXPERT), BF16),
                        pltpu.VMEM((D_EXPERT, D), BF16)],
    )
    return pl.pallas_call(
        _mlp_kernel,
        grid_spec=grid_spec,
        out_shape=jax.ShapeDtypeStruct(xs.shape, F32),
        compiler_params=_cparams("arbitrary"),
        name="expert_mlp",
    )(blk_e, n_used, xs, w_gate_up, b_gate_up.reshape(N_EXPERTS, 1, -1), w_down,
      b_down.reshape(N_EXPERTS, 1, -1))


def _gather_combine_kernel(dest_hbm, ys_hbm, gate_ref, x1_ref, mod_ref, pf_ref, op_ref, os_ref,
                           idx_smem, yb0, yb1, idx_sem, row_sem, *, n_steps, n_p):
    s = pl.program_id(0)
    tc = COMBINE_TILE
    RT = ROW_TILE
    n_rows = TOP_K * tc
    last = n_steps - 1
    ybufs = (yb0, yb1)
    n_idx = 3

    def idx_copy(step, sl):
        return pltpu.make_async_copy(dest_hbm.at[step], idx_smem.at[sl], idx_sem.at[sl])

    def start_rows(islot, b):
        for r in range(n_rows):
            row = pl.multiple_of(idx_smem[islot, 0, r], RT)
            pltpu.make_async_copy(ys_hbm.at[pl.ds(row, RT), :], ybufs[b].at[pl.ds(r * RT, RT), :],
                                  row_sem.at[b]).start(priority=r % 2)

    def wait_rows(b):
        pltpu.make_async_copy(ys_hbm.at[pl.ds(0, n_rows * RT), :], ybufs[b], row_sem.at[b]).wait()

    @pl.when(s == 0)
    def _():
        for t in range(min(2, n_steps)):
            idx_copy(t, t).start()
        idx_copy(0, 0).wait()
        start_rows(0, 0)

    @pl.when(s + 2 < n_steps)
    def _():
        idx_copy(s + 2, lax.rem(s + 2, n_idx)).start()

    @pl.when(s + 1 < n_steps)
    def _():
        idx_copy(s + 1, lax.rem(s + 1, n_idx)).wait()

    def body(cur):
        wait_rows(cur)
        start_rows(lax.rem(jnp.minimum(s + 1, last), n_idx), 1 - cur)
        g = gate_ref[...]

        def rows(k):
            return jnp.concatenate([ybufs[cur][pl.ds(k * tc * RT + c, tc, stride=RT), :]
                                    for c in range(RT)], axis=-1)

        m = rows(0) * g[:, 0:1]
        for k in range(1, TOP_K):
            m = m + rows(k) * g[:, k:k + 1]
        res = x1_ref[...] + mod_ref[0, 5:6, :] * _rms(m, pf_ref[...])

        @pl.when(s < n_p)
        def _():
            op_ref[...] = res

        @pl.when(s >= n_p)
        def _():
            os_ref[...] = res

    for cur in range(2):
        pl.when(lax.rem(s, 2) == cur)(functools.partial(body, cur))

    @pl.when(s == last)
    def _():
        wait_rows(1 - last % 2)


def _gather_combine(dest_rows, ys, gates_tok, x1, mod, post_ffn, n_tok_p, n_tok_s, groups):
    D = D_MODEL
    tc = COMBINE_TILE
    n_p, n_s = n_tok_p // tc, n_tok_s // tc
    nt = n_p + n_s
    return pl.pallas_call(
        functools.partial(_gather_combine_kernel, n_steps=nt, n_p=n_p),
        grid=(nt,),
        in_specs=[pl.BlockSpec(memory_space=pl.ANY),
                  pl.BlockSpec(memory_space=pl.ANY),
                  pl.BlockSpec((tc, TOP_K), lambda s: (s, 0)),
                  pl.BlockSpec((tc, D), lambda s: (s, 0)),
                  pl.BlockSpec((1, 6, D), lambda s: (_batch_of_tile(s, groups, tc), 0, 0)),
                  pl.BlockSpec((1, D), lambda s: (0, 0))],
        out_specs=[pl.BlockSpec((tc, D), lambda s: (jnp.minimum(s, n_p - 1), 0)),
                   pl.BlockSpec((tc, D), lambda s: (jnp.maximum(s - n_p, 0), 0))],
        out_shape=[jax.ShapeDtypeStruct((n_tok_p, D), F32),
                   jax.ShapeDtypeStruct((n_tok_s, D), F32)],
        scratch_shapes=[pltpu.SMEM((3, 1, TOP_K * tc), jnp.int32),
                        pltpu.VMEM((TOP_K * tc * ROW_TILE, LANES), F32),
                        pltpu.VMEM((TOP_K * tc * ROW_TILE, LANES), F32),
                        pltpu.SemaphoreType.DMA((3,)),
                        pltpu.SemaphoreType.DMA((2,))],
        compiler_params=_cparams("arbitrary"),
        name="combine",
    )(dest_rows.reshape(nt, 1, TOP_K * tc), ys, gates_tok, x1, mod, post_ffn.reshape(1, D))


def _routing_dest(idx_t, rank_t, counts):
    BM = EXPERT_BLOCK
    nt, _, tm = idx_t.shape
    n_blocks = (nt * tm * TOP_K) // BM + N_EXPERTS
    padded = ((counts + BM - 1) // BM) * BM
    pad_end = jnp.cumsum(padded)
    pad_start = pad_end - padded
    experts = jnp.arange(N_EXPERTS, dtype=jnp.int32)
    start_of = jnp.sum(jnp.where(idx_t[..., None] == experts, pad_start, 0), axis=-1)
    dest = (start_of + rank_t).astype(jnp.int32)
    blk_first = jnp.arange(n_blocks, dtype=jnp.int32) * BM
    blk_e = jnp.minimum(jnp.sum((pad_end[None, :] <= blk_first[:, None]).astype(jnp.int32), axis=1),
                        N_EXPERTS - 1)
    n_used = (pad_end[-1:] // BM).astype(jnp.int32)
    pad_first = (pad_start + counts).astype(jnp.int32)
    pad_cnt = (padded - counts).astype(jnp.int32)
    return dest, blk_e, n_used, pad_first, pad_cnt, n_blocks


def kernel(x_prompt, x_sample, c_prompt, c_sample, w_ada, b_ada, pre_mix, post_mix, pre_ffn, post_ffn,
           w_in, na_rel_bias, t5_rel_bias, wa_sink, gn_na, gn_wa, w_out, w_router, b_router,
           w_gate_up, b_gate_up, w_down, b_down):
    depth = w_ada.shape[0]
    D = D_MODEL
    groups = (x_prompt.shape[:2], x_sample.shape[:2])
    n_tok_p = x_prompt.shape[0] * x_prompt.shape[1]
    n_tok_s = x_sample.shape[0] * x_sample.shape[1]
    n_tok = n_tok_p + n_tok_s
    yp = x_prompt.reshape(n_tok_p, D)
    ys = x_sample.reshape(n_tok_s, D)
    c_all = jnp.concatenate([c_prompt, c_sample], axis=0)
    for l in range(depth):
        mod = _ada(c_all, w_ada[l], b_ada[l]).reshape(c_all.shape[0], 6, D)
        qb0 = 3 * D_NA
        scale = HEAD_DIM ** -0.5 * LOG2E
        w_in_l = jnp.concatenate([w_in[l][:, :D_NA] * scale, w_in[l][:, D_NA:qb0],
                                  _wa_slab_order(w_in[l][:, qb0:qb0 + D_WA] * scale, axis=1),
                                  w_in[l][:, qb0 + D_WA:]], axis=1).astype(BF16)
        w_out_l = jnp.concatenate([w_out[l][:D_NA], _wa_slab_order(w_out[l][D_NA:], axis=0)],
                                  axis=0).astype(BF16)
        proj = _in_proj(yp, ys, mod, pre_mix[l], w_in_l, groups)
        na = _na_attention(proj, _na_bias_table(na_rel_bias[l] * LOG2E), gn_na[l], groups)
        wa = _wa_attention(proj, _wa_bias_table(t5_rel_bias * LOG2E), wa_sink[l] * LOG2E,
                           _wa_slab_order(gn_wa[l], axis=0), groups)
        x1, h2, idx_t, gate_t, rank_t, counts = _post_mix(
            na, wa, yp, ys, mod, w_out_l, post_mix[l], pre_ffn[l], w_router[l].T, b_router[l], groups)
        dest, blk_e, n_used, pad_first, pad_cnt, n_blocks = _routing_dest(
            idx_t, rank_t, counts.reshape(N_EXPERTS).astype(jnp.int32))
        dest_rows = dest * ROW_TILE
        xs = _dispatch(h2, dest_rows, pad_first, pad_cnt, n_used, n_blocks)
        ys_rows = _expert_mlp(xs, blk_e, n_used, w_gate_up[l], b_gate_up[l], w_down[l], b_down[l])
        gates_tok = jnp.transpose(gate_t, (0, 2, 1)).reshape(n_tok, TOP_K)
        halves = TOKEN_TILE // COMBINE_TILE
        dest_steps = jnp.transpose(dest_rows.reshape(-1, TOP_K, halves, COMBINE_TILE), (0, 2, 1, 3))
        yp, ys = _gather_combine(dest_steps, ys_rows, gates_tok, x1, mod, post_ffn[l], n_tok_p, n_tok_s,
                                 groups)
    return (yp.reshape(x_prompt.shape), ys.reshape(x_sample.shape))
```

```python
import functools
import math

import jax
import jax.numpy as jnp
import numpy as np
from jax import lax
from jax.experimental import pallas as pl
from jax.experimental.pallas import tpu as pltpu

F32 = jnp.float32
BF16 = jnp.bfloat16

D_MODEL = 1024
HEAD_DIM = 64
D_NA = 512
D_WA = 512
N_HEADS_NA = 8
N_HEADS_WA = 8
N_KV_WA = 2
GQA_GROUP = 4
D_KV_WA = 128
D_IN_PROJ = 3 * D_NA + D_WA + 2 * D_KV_WA
GRID_W = 64
NA_WIN_R = 8
NA_WIN_C = 16
WA_WINDOW = 128
WA_BLOCK = 128
T5_BUCKETS = 32
T5_MAX_EXACT = 8
T5_MAX_DIST = 128
N_EXPERTS = 32
TOP_K = 4
D_EXPERT = D_MODEL
SWIGLU_LIMIT = 7.0
SWIGLU_ALPHA = 1.702
RMS_EPS = 1e-6
NEG_INF = -1e30
LOG2E = math.log2(math.e)

VMEM_LIMIT_BYTES = 56 * 1024 * 1024
LANES = 128
ROW_TILE = D_MODEL // LANES

TOKEN_TILE = 512
IN_PROJ_PARTS = 2
POST_MIX_PARTS = 4
NA_ROWS_PER_STEP = 8
NA_ROW_UNROLL = 8
WA_BLOCKS_PER_STEP = 4
EXPERT_BLOCK = 512
MLP_CHUNKS = 2
DISPATCH_TILES = 2
COMBINE_TILE = 512


def _cparams(*sem):
    return pltpu.CompilerParams(dimension_semantics=sem, vmem_limit_bytes=VMEM_LIMIT_BYTES)


def _rms(x, g):
    return x * lax.rsqrt(jnp.mean(x * x, axis=-1, keepdims=True) + RMS_EPS) * g


def _seq_bounds(s, groups, block):
    base = 0
    start, length = None, None
    for n_seq, seq_len in groups:
        per = seq_len // block
        g_start = base + ((s - base) // per) * per
        if start is None:
            start, length = g_start, per
        else:
            inside = s >= base
            start = jnp.where(inside, g_start, start)
            length = jnp.where(inside, per, length)
        base += n_seq * per
    return start, length


def _batch_of_tile(s, groups, block):
    base_blk, base_seq = 0, 0
    out = None
    for n_seq, seq_len in groups:
        per = seq_len // block
        b = base_seq + (s - base_blk) // per
        out = b if out is None else jnp.where(s >= base_blk, b, out)
        base_blk += n_seq * per
        base_seq += n_seq
    return out


def _ada_kernel(c_ref, w_ref, b_ref, o_ref):
    c = c_ref[...]
    s = c * (1.0 / (1.0 + jnp.exp(-c)))
    o_ref[...] = jnp.dot(s, w_ref[...], preferred_element_type=F32,
                         precision=lax.Precision.HIGHEST) + b_ref[...]


def _ada(c_all, w_ada, b_ada):
    nb = c_all.shape[0]
    return pl.pallas_call(
        _ada_kernel,
        grid=(6,),
        in_specs=[pl.BlockSpec((nb, D_MODEL), lambda j: (0, 0)),
                  pl.BlockSpec((D_MODEL, D_MODEL), lambda j: (0, j)),
                  pl.BlockSpec((1, D_MODEL), lambda j: (0, j))],
        out_specs=pl.BlockSpec((nb, D_MODEL), lambda j: (0, j)),
        out_shape=jax.ShapeDtypeStruct((nb, 6 * D_MODEL), F32),
        compiler_params=_cparams("arbitrary"),
        name="ada",
    )(c_all, w_ada, b_ada.reshape(1, -1))


def _inproj_kernel(xp_ref, xs_ref, mod_ref, g_ref, w_ref, o_ref, *, n_p):
    s = pl.program_id(0)

    def body(x_ref):
        H = TOKEN_TILE // IN_PROJ_PARTS
        hs = [(_rms(x_ref[p * H:(p + 1) * H], g_ref[...]) * (1.0 + mod_ref[0, 1:2, :])
               + mod_ref[0, 0:1, :]).astype(BF16) for p in range(IN_PROJ_PARTS)]
        for p, h in enumerate(hs):
            o_ref[p * H:(p + 1) * H] = jnp.dot(h, w_ref[...], preferred_element_type=F32).astype(BF16)

    pl.when(s < n_p)(lambda: body(xp_ref))
    pl.when(s >= n_p)(lambda: body(xs_ref))


def _in_proj(xp, xs, mod, pre_mix, w_in_bf16, groups):
    D = D_MODEL
    tm = TOKEN_TILE
    n_p, n_s = xp.shape[0] // tm, xs.shape[0] // tm
    return pl.pallas_call(
        functools.partial(_inproj_kernel, n_p=n_p),
        grid=(n_p + n_s,),
        in_specs=[pl.BlockSpec((tm, D), lambda s: (jnp.minimum(s, n_p - 1), 0)),
                  pl.BlockSpec((tm, D), lambda s: (jnp.maximum(s - n_p, 0), 0)),
                  pl.BlockSpec((1, 6, D), lambda s: (_batch_of_tile(s, groups, tm), 0, 0)),
                  pl.BlockSpec((1, D), lambda s: (0, 0)),
                  pl.BlockSpec((D, D_IN_PROJ), lambda s: (0, 0))],
        out_specs=pl.BlockSpec((tm, D_IN_PROJ), lambda s: (s, 0)),
        out_shape=jax.ShapeDtypeStruct(((n_p + n_s) * tm, D_IN_PROJ), BF16),
        compiler_params=_cparams("arbitrary"),
        name="in_proj",
    )(xp, xs, mod, pre_mix.reshape(1, D), w_in_bf16)


def _na_bias_table(rpb):
    qc = np.arange(GRID_W)[:, None]
    kc = np.arange(GRID_W)[None, :]
    win_start = np.clip(qc - NA_WIN_C // 2, 0, GRID_W - NA_WIN_C)
    valid = (kc >= win_start) & (kc < win_start + NA_WIN_C)
    dc = np.clip(kc - qc, -(NA_WIN_C - 1), NA_WIN_C - 1) + NA_WIN_C - 1
    pick = (dc[None] == np.arange(2 * NA_WIN_C - 1)[:, None, None]).astype(np.float32)
    full = jnp.einsum('hrc,cqk->hrqk', rpb.astype(F32), pick,
                      precision=lax.Precision.HIGHEST)
    full = jnp.where(valid[None, None], full.astype(F32), NEG_INF)
    two = jnp.concatenate([full[:, :-1], full[:, 1:]], axis=-1)
    two = two.reshape(N_HEADS_NA // 2, 2, 2 * NA_WIN_R - 2, GRID_W, 2 * GRID_W)
    return jnp.transpose(two, (0, 2, 1, 3, 4)).reshape(
        N_HEADS_NA // 2, 2 * NA_WIN_R - 2, 2 * GRID_W, 2 * GRID_W)


def _na_kernel(q_ref, kp_ref, kc_ref, kn_ref, vp_ref, vc_ref, vn_ref, bias_ref, gn_ref, o_ref,
               k_scr, v_scr, *, groups):
    s = pl.program_id(0)
    R = NA_ROWS_PER_STEP
    blk = R * GRID_W
    seq_start, seq_blocks = _seq_bounds(s, groups, blk)
    i = s - seq_start
    rows = seq_blocks * R
    k_scr[0:blk] = kp_ref[...]
    k_scr[blk:2 * blk] = kc_ref[...]
    k_scr[2 * blk:3 * blk] = kn_ref[...]
    v_scr[0:blk] = vp_ref[...]
    v_scr[blk:2 * blk] = vc_ref[...]
    v_scr[2 * blk:3 * blk] = vn_ref[...]
    lane = lax.broadcasted_iota(jnp.int32, (1, LANES), 1)
    lo = lane < HEAD_DIM
    gn = gn_ref[...]

    n_pairs = N_HEADS_NA // 2
    n_keys = NA_WIN_R * GRID_W

    def rows_body(jj, carry):
        js = [jj * NA_ROW_UNROLL + u for u in range(NA_ROW_UNROLL)]
        geo = []
        for j in js:
            r = i * R + j
            row_start = jnp.clip(r - NA_WIN_R // 2, 0, rows - NA_WIN_R)
            geo.append((r - row_start,
                        pl.multiple_of((row_start - i * R + R) * GRID_W, GRID_W),
                        pl.multiple_of(j * GRID_W, GRID_W)))
        scores = []
        for shift, koff, qoff in geo:
            for p in range(n_pairs):
                cs = slice(p * LANES, (p + 1) * LANES)
                q2 = q_ref[pl.ds(qoff, GRID_W), cs]
                zero = jnp.zeros_like(q2)
                qq = jnp.concatenate([jnp.where(lo, q2, zero), jnp.where(lo, zero, q2)], axis=0)
                k2 = k_scr[pl.ds(koff, n_keys), cs]
                scores.append(lax.dot_general(qq, k2, (((1,), (1,)), ((), ())),
                                              preferred_element_type=F32))
        probs = []
        for u, (shift, koff, qoff) in enumerate(geo):
            for p in range(n_pairs):
                sc = scores[u * n_pairs + p]
                parts = []
                for m in range(NA_WIN_R // 2):
                    d = 2 * m - shift + (NA_WIN_R - 1)
                    parts.append(sc[:, m * LANES:(m + 1) * LANES] + bias_ref[p, d])
                sc = jnp.concatenate(parts, axis=-1)
                e = jnp.exp2(sc - jnp.max(sc, axis=-1, keepdims=True))
                probs.append((e.astype(BF16), jnp.sum(e, axis=-1, keepdims=True)))
        for u, (shift, koff, qoff) in enumerate(geo):
            outs = []
            for p in range(n_pairs):
                e, l = probs[u * n_pairs + p]
                v2 = v_scr[pl.ds(koff, n_keys), p * LANES:(p + 1) * LANES]
                o2 = jnp.dot(e, v2, preferred_element_type=F32) / l
                outs.append(jnp.where(lo, o2[:GRID_W], o2[GRID_W:]))
            o = jnp.concatenate(outs, axis=-1)
            o_ref[pl.ds(qoff, GRID_W), :] = _rms(o, gn).astype(BF16)
        return carry

    lax.fori_loop(0, R // NA_ROW_UNROLL, rows_body, 0)


def _na_attention(proj, bias_tab, gn_na, groups):
    n_tok = proj.shape[0]
    blk = NA_ROWS_PER_STEP * GRID_W

    def kv(col, delta):
        def index_map(s):
            start, length = _seq_bounds(s, groups, blk)
            return (jnp.clip(s + delta, start, start + length - 1), col)
        return pl.BlockSpec((blk, D_NA), index_map)

    return pl.pallas_call(
        functools.partial(_na_kernel, groups=groups),
        grid=(n_tok // blk,),
        in_specs=[pl.BlockSpec((blk, D_NA), lambda s: (s, 0)),
                  kv(1, -1), kv(1, 0), kv(1, 1), kv(2, -1), kv(2, 0), kv(2, 1),
                  pl.BlockSpec(bias_tab.shape, lambda s: (0, 0, 0, 0)),
                  pl.BlockSpec((1, D_NA), lambda s: (0, 0))],
        out_specs=pl.BlockSpec((blk, D_NA), lambda s: (s, 0)),
        out_shape=jax.ShapeDtypeStruct((n_tok, D_NA), BF16),
        scratch_shapes=[pltpu.VMEM((3 * blk, D_NA), BF16), pltpu.VMEM((3 * blk, D_NA), BF16)],
        compiler_params=_cparams("arbitrary"),
        name="na_attn",
    )(proj, proj, proj, proj, proj, proj, proj, bias_tab, gn_na.reshape(1, D_NA))


def _wa_slab_order(a, axis):
    shape = a.shape
    split = shape[:axis] + (N_KV_WA, GQA_GROUP, HEAD_DIM) + shape[axis + 1:]
    return jnp.swapaxes(a.reshape(split), axis, axis + 1).reshape(shape)


def _t5_bucket(rel):
    half = T5_BUCKETS // 2
    ret = (rel > 0).astype(jnp.int32) * half
    n = jnp.abs(rel)
    nf = jnp.maximum(n, 1).astype(F32)
    large = T5_MAX_EXACT + (jnp.log(nf / T5_MAX_EXACT) / math.log(T5_MAX_DIST / T5_MAX_EXACT)
                            * (half - T5_MAX_EXACT)).astype(jnp.int32)
    large = jnp.minimum(large, half - 1)
    return ret + jnp.where(n < T5_MAX_EXACT, n, large)


def _wa_bias_table(t5_table):
    rel = jnp.arange(3 * WA_BLOCK)[None, :] - WA_BLOCK - jnp.arange(WA_BLOCK)[:, None]
    pick = (_t5_bucket(rel)[..., None] == jnp.arange(T5_BUCKETS)).astype(F32)
    bias = jnp.einsum('qkb,bh->qkh', pick, t5_table.astype(F32),
                      precision=lax.Precision.HIGHEST)
    bias = jnp.where((jnp.abs(rel) <= WA_WINDOW)[..., None], bias, NEG_INF)
    bias = jnp.transpose(bias, (2, 0, 1))
    return jnp.concatenate([bias[:GQA_GROUP], bias[GQA_GROUP:]], axis=1)


def _wa_kernel(sink_ref, q_ref, *refs, groups):
    s = pl.program_id(0)
    W = WA_BLOCK
    NB = WA_BLOCKS_PER_STEP
    k_refs, v_refs = refs[:NB + 2], refs[NB + 2:2 * NB + 4]
    bias_ref, gn_ref, o_ref = refs[2 * NB + 4:]
    first = s * NB
    seq_start, seq_blocks = _seq_bounds(first, groups, W)
    lane = lax.broadcasted_iota(jnp.int32, (1, LANES), 1)
    lo = lane < HEAD_DIM
    row = lax.broadcasted_iota(jnp.int32, (2 * W, 1), 0)
    col = lax.broadcasted_iota(jnp.int32, (1, 3 * W), 1)
    ks = [r[...] for r in k_refs]
    vs = [r[...] for r in v_refs]
    scores = []
    for b in range(NB):
        kcat = jnp.concatenate(ks[b:b + 3], axis=0)
        for p in range(GQA_GROUP):
            q2 = q_ref[b * W:(b + 1) * W, p * LANES:(p + 1) * LANES]
            zero = jnp.zeros_like(q2)
            qq = jnp.concatenate([jnp.where(lo, q2, zero), jnp.where(lo, zero, q2)], axis=0)
            scores.append(lax.dot_general(qq, kcat, (((1,), (1,)), ((), ())),
                                          preferred_element_type=F32))
    probs = []
    for b in range(NB):
        pen_prev = jnp.where(first + b == seq_start, NEG_INF, 0.0).astype(F32)
        pen_next = jnp.where(first + b == seq_start + seq_blocks - 1, NEG_INF, 0.0).astype(F32)
        pen = jnp.where(col < W, pen_prev, jnp.where(col >= 2 * W, pen_next, 0.0))
        for p in range(GQA_GROUP):
            sc = scores[b * GQA_GROUP + p] + bias_ref[p] + pen
            sink = jnp.where(row < W, sink_ref[p], sink_ref[p + GQA_GROUP])
            mx = jnp.maximum(jnp.max(sc, axis=-1, keepdims=True), sink)
            e = jnp.exp2(sc - mx)
            probs.append((e.astype(BF16),
                          jnp.sum(e, axis=-1, keepdims=True) + jnp.exp2(sink - mx)))
    for b in range(NB):
        vcat = jnp.concatenate(vs[b:b + 3], axis=0)
        outs = []
        for p in range(GQA_GROUP):
            e, l = probs[b * GQA_GROUP + p]
            o2 = jnp.dot(e, vcat, preferred_element_type=F32) / l
            outs.append(jnp.where(lo, o2[:W], o2[W:]))
        o = jnp.concatenate(outs, axis=-1)
        o_ref[b * W:(b + 1) * W] = _rms(o, gn_ref[...]).astype(BF16)


def _wa_attention(proj, bias_tab, sink, gn_wa_perm, groups):
    n_tok = proj.shape[0]
    W = WA_BLOCK
    q_col = 3 * D_NA // D_WA
    k_col = (3 * D_NA + D_WA) // D_KV_WA
    v_col = k_col + 1

    NB = WA_BLOCKS_PER_STEP

    def kv(col, delta):
        def index_map(s):
            start, length = _seq_bounds(s * NB, groups, W)
            return (jnp.clip(s * NB + delta, start, start + length - 1), col)
        return pl.BlockSpec((W, D_KV_WA), index_map)

    deltas = range(-1, NB + 1)
    return pl.pallas_call(
        functools.partial(_wa_kernel, groups=groups),
        grid=(n_tok // (NB * W),),
        in_specs=[pl.BlockSpec(memory_space=pltpu.SMEM),
                  pl.BlockSpec((NB * W, D_WA), lambda s: (s, q_col)),
                  *[kv(k_col, d) for d in deltas],
                  *[kv(v_col, d) for d in deltas],
                  pl.BlockSpec(bias_tab.shape, lambda s: (0, 0, 0)),
                  pl.BlockSpec((1, D_WA), lambda s: (0, 0))],
        out_specs=pl.BlockSpec((NB * W, D_WA), lambda s: (s, 0)),
        out_shape=jax.ShapeDtypeStruct((n_tok, D_WA), BF16),
        compiler_params=_cparams("arbitrary"),
        name="wa_attn",
    )(sink, *([proj] * (2 * NB + 5)), bias_tab, gn_wa_perm.reshape(1, D_WA))


def _postmix_kernel(na_ref, wa_ref, xp_ref, xs_ref, mod_ref, wo_ref, pm_ref, pf_ref, wr_ref, br_ref,
                    tri_ref, x1_ref, h2_ref, idx_ref, gate_ref, rank_ref, cnt_ref, carry, *, n_p):
    s = pl.program_id(0)

    @pl.when(s == 0)
    def _():
        carry[...] = jnp.zeros_like(carry)

    def body(x_ref):
        P = POST_MIX_PARTS
        H = TOKEN_TILE // P
        mixed = [jnp.dot(jnp.concatenate([na_ref[h * H:(h + 1) * H], wa_ref[h * H:(h + 1) * H]], axis=-1),
                         wo_ref[...], preferred_element_type=F32) for h in range(P)]
        h2s = []
        for h in range(P):
            x1 = x_ref[h * H:(h + 1) * H] + mod_ref[0, 2:3, :] * _rms(mixed[h], pm_ref[...])
            x1_ref[h * H:(h + 1) * H] = x1
            h2 = _rms(x1, pf_ref[...]) * (1.0 + mod_ref[0, 4:5, :]) + mod_ref[0, 3:4, :]
            for c in range(ROW_TILE):
                h2_ref[pl.ds(h * H * ROW_TILE + c, H, stride=ROW_TILE), :] = h2[:, c * LANES:(c + 1) * LANES]
            h2s.append(h2.astype(BF16))
        wr = wr_ref[...].astype(BF16)
        logits = jnp.concatenate(
            [lax.dot_general(wr, h2, (((1,), (1,)), ((), ())), preferred_element_type=F32)
             for h2 in h2s], axis=-1) + br_ref[...]
        eidx = lax.broadcasted_iota(jnp.int32, logits.shape, 0)
        tops, sels = [], []
        l = logits
        for _ in range(TOP_K):
            m = jnp.max(l, axis=0, keepdims=True)
            sel = jnp.min(jnp.where(l == m, eidx, N_EXPERTS), axis=0, keepdims=True)
            tops.append(m)
            sels.append(sel)
            l = jnp.where(eidx == sel, -jnp.inf, l)
        es = [jnp.exp(t - tops[0]) for t in tops]
        tot = es[0] + es[1] + es[2] + es[3]
        idx_ref[0] = jnp.concatenate(sels, axis=0)
        gate_ref[0] = jnp.concatenate([e / tot for e in es], axis=0)
        hits = [eidx == sel for sel in sels]
        onehot = sum(h.astype(F32) for h in hits)
        prefix = jnp.dot(onehot.astype(BF16), tri_ref[...], preferred_element_type=F32)
        rank_e = carry[...] + prefix - onehot
        rank_ref[0] = jnp.concatenate(
            [jnp.sum(jnp.where(h, rank_e, 0.0), axis=0, keepdims=True) for h in hits],
            axis=0).astype(jnp.int32)
        carry[...] = carry[...] + jnp.sum(onehot, axis=1, keepdims=True)
        cnt_ref[...] = carry[...]

    pl.when(s < n_p)(lambda: body(xp_ref))
    pl.when(s >= n_p)(lambda: body(xs_ref))


def _post_mix(na, wa, xp, xs, mod, w_out_bf16, post_mix, pre_ffn, w_router_t, b_router, groups):
    D = D_MODEL
    tm = TOKEN_TILE
    n_p, n_s = xp.shape[0] // tm, xs.shape[0] // tm
    nt = n_p + n_s
    tri = (np.arange(tm)[:, None] <= np.arange(tm)[None, :]).astype(np.float32)
    tok = lambda s: (s, 0)
    const2 = lambda s: (0, 0)
    tile3 = lambda s: (s, 0, 0)
    return pl.pallas_call(
        functools.partial(_postmix_kernel, n_p=n_p),
        grid=(nt,),
        in_specs=[pl.BlockSpec((tm, D_NA), tok),
                  pl.BlockSpec((tm, D_WA), tok),
                  pl.BlockSpec((tm, D), lambda s: (jnp.minimum(s, n_p - 1), 0)),
                  pl.BlockSpec((tm, D), lambda s: (jnp.maximum(s - n_p, 0), 0)),
                  pl.BlockSpec((1, 6, D), lambda s: (_batch_of_tile(s, groups, tm), 0, 0)),
                  pl.BlockSpec((D, D), const2),
                  pl.BlockSpec((1, D), const2),
                  pl.BlockSpec((1, D), const2),
                  pl.BlockSpec((N_EXPERTS, D), const2),
                  pl.BlockSpec((N_EXPERTS, 1), const2),
                  pl.BlockSpec((tm, tm), const2)],
        out_specs=[pl.BlockSpec((tm, D), tok),
                   pl.BlockSpec((tm * ROW_TILE, LANES), tok),
                   pl.BlockSpec((1, TOP_K, tm), tile3),
                   pl.BlockSpec((1, TOP_K, tm), tile3),
                   pl.BlockSpec((1, TOP_K, tm), tile3),
                   pl.BlockSpec((N_EXPERTS, 1), const2)],
        out_shape=[jax.ShapeDtypeStruct((nt * tm, D), F32),
                   jax.ShapeDtypeStruct((nt * tm * ROW_TILE, LANES), F32),
                   jax.ShapeDtypeStruct((nt, TOP_K, tm), jnp.int32),
                   jax.ShapeDtypeStruct((nt, TOP_K, tm), F32),
                   jax.ShapeDtypeStruct((nt, TOP_K, tm), jnp.int32),
                   jax.ShapeDtypeStruct((N_EXPERTS, 1), F32)],
        scratch_shapes=[pltpu.VMEM((N_EXPERTS, 1), F32)],
        compiler_params=_cparams("arbitrary"),
        name="post_mix",
    )(na, wa, xp, xs, mod, w_out_bf16, post_mix.reshape(1, D), pre_ffn.reshape(1, D),
      w_router_t, b_router.reshape(N_EXPERTS, 1), jnp.asarray(tri, BF16))


def _dispatch_kernel(padfirst_ref, padcnt_ref, nused_ref, dest_hbm, h_ref, xs_hbm, idx_smem, zero_blk,
                     idx_sem, sem, *, n_steps, n_blocks):
    s = pl.program_id(0)
    tm = TOKEN_TILE
    RT = ROW_TILE
    slot = lax.rem(s, 2)

    def idx_copy(step, sl):
        return pltpu.make_async_copy(dest_hbm.at[step], idx_smem.at[sl], idx_sem.at[sl])

    @pl.when(s == 0)
    def _():
        idx_copy(0, 0).start()
        zero_blk[...] = jnp.zeros_like(zero_blk)

        def pad_copies(e, wait):
            first = padfirst_ref[e]
            cnt = padcnt_ref[e]
            for b in reversed(range(EXPERT_BLOCK.bit_length() - 1)):
                size = (1 << b) * RT
                bit = lax.shift_right_logical(cnt, b) & 1

                @pl.when(bit == 1)
                def _():
                    cp = pltpu.make_async_copy(
                        zero_blk.at[pl.ds(0, size), :],
                        xs_hbm.at[pl.ds(pl.multiple_of(first * RT, RT), size), :], sem)
                    cp.wait() if wait else cp.start()

                first = first + bit * (1 << b)

        def issue(e, c):
            pad_copies(e, False)
            return c

        def drain(e, c):
            pad_copies(e, True)
            return c

        lax.fori_loop(0, N_EXPERTS, issue, 0)
        lax.fori_loop(0, N_EXPERTS, drain, 0)

        def blk_copy(b):
            start = pl.multiple_of(b * (EXPERT_BLOCK * RT), EXPERT_BLOCK * RT)
            return pltpu.make_async_copy(zero_blk, xs_hbm.at[pl.ds(start, EXPERT_BLOCK * RT), :], sem)

        def issue_blk(b, c):
            blk_copy(b).start()
            return c

        def drain_blk(b, c):
            blk_copy(b).wait()
            return c

        lax.fori_loop(nused_ref[0], n_blocks, issue_blk, 0)
        lax.fori_loop(nused_ref[0], n_blocks, drain_blk, 0)

    idx_copy(s, slot).wait()
    if n_steps > 1:
        @pl.when(s + 1 < n_steps)
        def _():
            idx_copy(s + 1, 1 - slot).start()

    for u in range(DISPATCH_TILES):
        for k in range(TOP_K):
            for j in range(tm):
                row = pl.multiple_of(idx_smem[slot, 0, (u * TOP_K + k) * tm + j], RT)
                pltpu.make_async_copy(h_ref.at[pl.ds((u * tm + j) * RT, RT), :],
                                      xs_hbm.at[pl.ds(row, RT), :], sem).start(priority=j % 2)
    for k in range(TOP_K):
        pltpu.make_async_copy(h_ref, xs_hbm.at[pl.ds(0, DISPATCH_TILES * tm * RT), :], sem).wait()


def _dispatch(h2, dest_rows, pad_first, pad_cnt, n_used, n_blocks):
    tm = DISPATCH_TILES * TOKEN_TILE
    n_steps = h2.shape[0] // (tm * ROW_TILE)
    grid_spec = pltpu.PrefetchScalarGridSpec(
        num_scalar_prefetch=3,
        grid=(n_steps,),
        in_specs=[pl.BlockSpec(memory_space=pl.ANY),
                  pl.BlockSpec((tm * ROW_TILE, LANES), lambda s, pf, pc, nused: (s, 0))],
        out_specs=pl.BlockSpec(memory_space=pl.ANY),
        scratch_shapes=[pltpu.SMEM((2, 1, TOP_K * tm), jnp.int32),
                        pltpu.VMEM((EXPERT_BLOCK * ROW_TILE, LANES), F32),
                        pltpu.SemaphoreType.DMA((2,)),
                        pltpu.SemaphoreType.DMA],
    )
    return pl.pallas_call(
        functools.partial(_dispatch_kernel, n_steps=n_steps, n_blocks=n_blocks),
        grid_spec=grid_spec,
        out_shape=jax.ShapeDtypeStruct((n_blocks * EXPERT_BLOCK * ROW_TILE, LANES), F32),
        compiler_params=_cparams("arbitrary"),
        name="dispatch",
    )(pad_first, pad_cnt, n_used, dest_rows.reshape(n_steps, 1, TOP_K * tm), h2)


def _mlp_kernel(blk_e_ref, nused_ref, x_ref, wgu_ref, bgu_ref, wd_ref, bd_ref, o_ref, wgu_bf, wd_bf):
    i = pl.program_id(0)
    BM = EXPERT_BLOCK
    RT = ROW_TILE

    @pl.when(i < nused_ref[0])
    def _():
        new_expert = jnp.logical_or(i == 0, blk_e_ref[i] != blk_e_ref[jnp.maximum(i - 1, 0)])

        @pl.when(new_expert)
        def _():
            wgu_bf[...] = wgu_ref[0].astype(BF16)
            wd_bf[...] = wd_ref[0].astype(BF16)

        x = jnp.concatenate([x_ref[pl.ds(c, BM, stride=RT), :] for c in range(RT)],
                            axis=-1).astype(BF16)
        H = D_EXPERT // MLP_CHUNKS
        ups = []
        for h in range(MLP_CHUNKS):
            g = jnp.dot(x, wgu_bf[:, h * H:(h + 1) * H], preferred_element_type=F32)
            u = jnp.dot(x, wgu_bf[:, D_EXPERT + h * H:D_EXPERT + (h + 1) * H],
                        preferred_element_type=F32)
            ups.append((g + bgu_ref[0, :, h * H:(h + 1) * H],
                        u + bgu_ref[0, :, D_EXPERT + h * H:D_EXPERT + (h + 1) * H]))
        y = bd_ref[0]
        for h, (g, u) in enumerate(ups):
            glu = jnp.minimum(g, SWIGLU_LIMIT)
            lin = jnp.clip(u, -SWIGLU_LIMIT, SWIGLU_LIMIT)
            act = glu * (1.0 / (1.0 + jnp.exp(-SWIGLU_ALPHA * glu))) * (lin + 1.0)
            y = y + jnp.dot(act.astype(BF16), wd_bf[h * H:(h + 1) * H, :], preferred_element_type=F32)
        for c in range(RT):
            o_ref[pl.ds(c, BM, stride=RT), :] = y[:, c * LANES:(c + 1) * LANES]

    @pl.when(i >= nused_ref[0])
    def _():
        o_ref[...] = jnp.zeros_like(o_ref)


def _expert_mlp(xs, blk_e, n_used, w_gate_up, b_gate_up, w_down, b_down):
    D = D_MODEL
    BM = EXPERT_BLOCK
    n_blocks = blk_e.shape[0]
    wmap = lambda i, be, nu: (be[i], 0, 0)
    grid_spec = pltpu.PrefetchScalarGridSpec(
        num_scalar_prefetch=2,
        grid=(n_blocks,),
        in_specs=[pl.BlockSpec((BM * ROW_TILE, LANES), lambda i, be, nu: (jnp.minimum(i, nu[0] - 1), 0)),
                  pl.BlockSpec((1, D, 2 * D_EXPERT), wmap),
                  pl.BlockSpec((1, 1, 2 * D_EXPERT), wmap),
                  pl.BlockSpec((1, D_EXPERT, D), wmap),
                  pl.BlockSpec((1, 1, D), wmap)],
        out_specs=pl.BlockSpec((BM * ROW_TILE, LANES), lambda i, be, nu: (i, 0)),
        scratch_shapes=[pltpu.VMEM((D, 2 * D_EXPERT), BF16),
                        pltpu.VMEM((D_EXPERT, D), BF16)],
    )
    return pl.pallas_call(
        _mlp_kernel,
        grid_spec=grid_spec,
        out_shape=jax.ShapeDtypeStruct(xs.shape, F32),
        compiler_params=_cparams("arbitrary"),
        name="expert_mlp",
    )(blk_e, n_used, xs, w_gate_up, b_gate_up.reshape(N_EXPERTS, 1, -1), w_down,
      b_down.reshape(N_EXPERTS, 1, -1))


def _gather_combine_kernel(dest_hbm, ys_hbm, gate_ref, x1_ref, mod_ref, pf_ref, op_ref, os_ref,
                           idx_smem, yb0, yb1, idx_sem, row_sem, *, n_steps, n_p):
    s = pl.program_id(0)
    tc = COMBINE_TILE
    RT = ROW_TILE
    n_rows = TOP_K * tc
    last = n_steps - 1
    ybufs = (yb0, yb1)
    n_idx = 3

    def idx_copy(step, sl):
        return pltpu.make_async_copy(dest_hbm.at[step], idx_smem.at[sl], idx_sem.at[sl])

    def start_rows(islot, b):
        for r in range(n_rows):
            row = pl.multiple_of(idx_smem[islot, 0, r], RT)
            pltpu.make_async_copy(ys_hbm.at[pl.ds(row, RT), :], ybufs[b].at[pl.ds(r * RT, RT), :],
                                  row_sem.at[b]).start(priority=r % 2)

    def wait_rows(b):
        pltpu.make_async_copy(ys_hbm.at[pl.ds(0, n_rows * RT), :], ybufs[b], row_sem.at[b]).wait()

    @pl.when(s == 0)
    def _():
        for t in range(min(2, n_steps)):
            idx_copy(t, t).start()
        idx_copy(0, 0).wait()
        start_rows(0, 0)

    @pl.when(s + 2 < n_steps)
    def _():
        idx_copy(s + 2, lax.rem(s + 2, n_idx)).start()

    @pl.when(s + 1 < n_steps)
    def _():
        idx_copy(s + 1, lax.rem(s + 1, n_idx)).wait()

    def body(cur):
        wait_rows(cur)
        start_rows(lax.rem(jnp.minimum(s + 1, last), n_idx), 1 - cur)
        g = gate_ref[...]

        def rows(k):
            return jnp.concatenate([ybufs[cur][pl.ds(k * tc * RT + c, tc, stride=RT), :]
                                    for c in range(RT)], axis=-1)

        m = rows(0) * g[:, 0:1]
        for k in range(1, TOP_K):
            m = m + rows(k) * g[:, k:k + 1]
        res = x1_ref[...] + mod_ref[0, 5:6, :] * _rms(m, pf_ref[...])

        @pl.when(s < n_p)
        def _():
            op_ref[...] = res

        @pl.when(s >= n_p)
        def _():
            os_ref[...] = res

    for cur in range(2):
        pl.when(lax.rem(s, 2) == cur)(functools.partial(body, cur))

    @pl.when(s == last)
    def _():
        wait_rows(1 - last % 2)


def _gather_combine(dest_rows, ys, gates_tok, x1, mod, post_ffn, n_tok_p, n_tok_s, groups):
    D = D_MODEL
    tc = COMBINE_TILE
    n_p, n_s = n_tok_p // tc, n_tok_s // tc
    nt = n_p + n_s
    return pl.pallas_call(
        functools.partial(_gather_combine_kernel, n_steps=nt, n_p=n_p),
        grid=(nt,),
        in_specs=[pl.BlockSpec(memory_space=pl.ANY),
                  pl.BlockSpec(memory_space=pl.ANY),
                  pl.BlockSpec((tc, TOP_K), lambda s: (s, 0)),
                  pl.BlockSpec((tc, D), lambda s: (s, 0)),
                  pl.BlockSpec((1, 6, D), lambda s: (_batch_of_tile(s, groups, tc), 0, 0)),
                  pl.BlockSpec((1, D), lambda s: (0, 0))],
        out_specs=[pl.BlockSpec((tc, D), lambda s: (jnp.minimum(s, n_p - 1), 0)),
                   pl.BlockSpec((tc, D), lambda s: (jnp.maximum(s - n_p, 0), 0))],
        out_shape=[jax.ShapeDtypeStruct((n_tok_p, D), F32),
                   jax.ShapeDtypeStruct((n_tok_s, D), F32)],
        scratch_shapes=[pltpu.SMEM((3, 1, TOP_K * tc), jnp.int32),
                        pltpu.VMEM((TOP_K * tc * ROW_TILE, LANES), F32),
                        pltpu.VMEM((TOP_K * tc * ROW_TILE, LANES), F32),
                        pltpu.SemaphoreType.DMA((3,)),
                        pltpu.SemaphoreType.DMA((2,))],
        compiler_params=_cparams("arbitrary"),
        name="combine",
    )(dest_rows.reshape(nt, 1, TOP_K * tc), ys, gates_tok, x1, mod, post_ffn.reshape(1, D))


def _routing_dest(idx_t, rank_t, counts):
    BM = EXPERT_BLOCK
    nt, _, tm = idx_t.shape
    n_blocks = (nt * tm * TOP_K) // BM + N_EXPERTS
    padded = ((counts + BM - 1) // BM) * BM
    pad_end = jnp.cumsum(padded)
    pad_start = pad_end - padded
    experts = jnp.arange(N_EXPERTS, dtype=jnp.int32)
    start_of = jnp.sum(jnp.where(idx_t[..., None] == experts, pad_start, 0), axis=-1)
    dest = (start_of + rank_t).astype(jnp.int32)
    blk_first = jnp.arange(n_blocks, dtype=jnp.int32) * BM
    blk_e = jnp.minimum(jnp.sum((pad_end[None, :] <= blk_first[:, None]).astype(jnp.int32), axis=1),
                        N_EXPERTS - 1)
    n_used = (pad_end[-1:] // BM).astype(jnp.int32)
    pad_first = (pad_start + counts).astype(jnp.int32)
    pad_cnt = (padded - counts).astype(jnp.int32)
    return dest, blk_e, n_used, pad_first, pad_cnt, n_blocks


def kernel(x_prompt, x_sample, c_prompt, c_sample, w_ada, b_ada, pre_mix, post_mix, pre_ffn, post_ffn,
           w_in, na_rel_bias, t5_rel_bias, wa_sink, gn_na, gn_wa, w_out, w_router, b_router,
           w_gate_up, b_gate_up, w_down, b_down):
    depth = w_ada.shape[0]
    D = D_MODEL
    groups = (x_prompt.shape[:2], x_sample.shape[:2])
    n_tok_p = x_prompt.shape[0] * x_prompt.shape[1]
    n_tok_s = x_sample.shape[0] * x_sample.shape[1]
    n_tok = n_tok_p + n_tok_s
    yp = x_prompt.reshape(n_tok_p, D)
    ys = x_sample.reshape(n_tok_s, D)
    c_all = jnp.concatenate([c_prompt, c_sample], axis=0)
    for l in range(depth):
        mod = _ada(c_all, w_ada[l], b_ada[l]).reshape(c_all.shape[0], 6, D)
        qb0 = 3 * D_NA
        scale = HEAD_DIM ** -0.5 * LOG2E
        w_in_l = jnp.concatenate([w_in[l][:, :D_NA] * scale, w_in[l][:, D_NA:qb0],
                                  _wa_slab_order(w_in[l][:, qb0:qb0 + D_WA] * scale, axis=1),
                                  w_in[l][:, qb0 + D_WA:]], axis=1).astype(BF16)
        w_out_l = jnp.concatenate([w_out[l][:D_NA], _wa_slab_order(w_out[l][D_NA:], axis=0)],
                                  axis=0).astype(BF16)
        proj = _in_proj(yp, ys, mod, pre_mix[l], w_in_l, groups)
        na = _na_attention(proj, _na_bias_table(na_rel_bias[l] * LOG2E), gn_na[l], groups)
        wa = _wa_attention(proj, _wa_bias_table(t5_rel_bias * LOG2E), wa_sink[l] * LOG2E,
                           _wa_slab_order(gn_wa[l], axis=0), groups)
        x1, h2, idx_t, gate_t, rank_t, counts = _post_mix(
            na, wa, yp, ys, mod, w_out_l, post_mix[l], pre_ffn[l], w_router[l].T, b_router[l], groups)
        dest, blk_e, n_used, pad_first, pad_cnt, n_blocks = _routing_dest(
            idx_t, rank_t, counts.reshape(N_EXPERTS).astype(jnp.int32))
        dest_rows = dest * ROW_TILE
        xs = _dispatch(h2, dest_rows, pad_first, pad_cnt, n_used, n_blocks)
        ys_rows = _expert_mlp(xs, blk_e, n_used, w_gate_up[l], b_gate_up[l], w_down[l], b_down[l])
        gates_tok = jnp.transpose(gate_t, (0, 2, 1)).reshape(n_tok, TOP_K)
        halves = TOKEN_TILE // COMBINE_TILE
        dest_steps = jnp.transpose(dest_rows.reshape(-1, TOP_K, halves, COMBINE_TILE), (0, 2, 1, 3))
        yp, ys = _gather_combine(dest_steps, ys_rows, gates_tok, x1, mod, post_ffn[l], n_tok_p, n_tok_s,
                                 groups)
    return (yp.reshape(x_prompt.shape), ys.reshape(x_sample.shape))
```

```python
import functools
import math

import jax
import jax.numpy as jnp
import numpy as np
from jax import lax
from jax.experimental import pallas as pl
from jax.experimental.pallas import tpu as pltpu

F32 = jnp.float32
BF16 = jnp.bfloat16

D_MODEL = 1024
HEAD_DIM = 64
D_NA = 512
D_WA = 512
N_HEADS_NA = 8
N_HEADS_WA = 8
N_KV_WA = 2
GQA_GROUP = 4
D_KV_WA = 128
D_IN_PROJ = 3 * D_NA + D_WA + 2 * D_KV_WA
GRID_W = 64
NA_WIN_R = 8
NA_WIN_C = 16
WA_WINDOW = 128
WA_BLOCK = 128
T5_BUCKETS = 32
T5_MAX_EXACT = 8
T5_MAX_DIST = 128
N_EXPERTS = 32
TOP_K = 4
D_EXPERT = D_MODEL
SWIGLU_LIMIT = 7.0
SWIGLU_ALPHA = 1.702
RMS_EPS = 1e-6
NEG_INF = -1e30
LOG2E = math.log2(math.e)

VMEM_LIMIT_BYTES = 56 * 1024 * 1024
LANES = 128
ROW_TILE = D_MODEL // LANES

TOKEN_TILE = 512
IN_PROJ_PARTS = 2
POST_MIX_PARTS = 4
NA_ROWS_PER_STEP = 8
NA_ROW_UNROLL = 8
WA_BLOCKS_PER_STEP = 4
EXPERT_BLOCK = 512
MLP_CHUNKS = 2
DISPATCH_TILES = 2
COMBINE_TILE = 512


def _cparams(*sem):
    return pltpu.CompilerParams(dimension_semantics=sem, vmem_limit_bytes=VMEM_LIMIT_BYTES)


def _rms(x, g):
    return x * lax.rsqrt(jnp.mean(x * x, axis=-1, keepdims=True) + RMS_EPS) * g


def _seq_bounds(s, groups, block):
    base = 0
    start, length = None, None
    for n_seq, seq_len in groups:
        per = seq_len // block
        g_start = base + ((s - base) // per) * per
        if start is None:
            start, length = g_start, per
        else:
            inside = s >= base
            start = jnp.where(inside, g_start, start)
            length = jnp.where(inside, per, length)
        base += n_seq * per
    return start, length


def _batch_of_tile(s, groups, block):
    base_blk, base_seq = 0, 0
    out = None
    for n_seq, seq_len in groups:
        per = seq_len // block
        b = base_seq + (s - base_blk) // per
        out = b if out is None else jnp.where(s >= base_blk, b, out)
        base_blk += n_seq * per
        base_seq += n_seq
    return out


def _ada_kernel(c_ref, w_ref, b_ref, o_ref):
    c = c_ref[...]
    s = c * (1.0 / (1.0 + jnp.exp(-c)))
    o_ref[...] = jnp.dot(s, w_ref[...], preferred_element_type=F32,
                         precision=lax.Precision.HIGHEST) + b_ref[...]


def _ada(c_all, w_ada, b_ada):
    nb = c_all.shape[0]
    return pl.pallas_call(
        _ada_kernel,
        grid=(6,),
        in_specs=[pl.BlockSpec((nb, D_MODEL), lambda j: (0, 0)),
                  pl.BlockSpec((D_MODEL, D_MODEL), lambda j: (0, j)),
                  pl.BlockSpec((1, D_MODEL), lambda j: (0, j))],
        out_specs=pl.BlockSpec((nb, D_MODEL), lambda j: (0, j)),
        out_shape=jax.ShapeDtypeStruct((nb, 6 * D_MODEL), F32),
        compiler_params=_cparams("arbitrary"),
        name="ada",
    )(c_all, w_ada, b_ada.reshape(1, -1))


def _inproj_kernel(xp_ref, xs_ref, mod_ref, g_ref, w_ref, o_ref, *, n_p):
    s = pl.program_id(0)

    def body(x_ref):
        H = TOKEN_TILE // IN_PROJ_PARTS
        hs = [(_rms(x_ref[p * H:(p + 1) * H], g_ref[...]) * (1.0 + mod_ref[0, 1:2, :])
               + mod_ref[0, 0:1, :]).astype(BF16) for p in range(IN_PROJ_PARTS)]
        for p, h in enumerate(hs):
            o_ref[p * H:(p + 1) * H] = jnp.dot(h, w_ref[...], preferred_element_type=F32).astype(BF16)

    pl.when(s < n_p)(lambda: body(xp_ref))
    pl.when(s >= n_p)(lambda: body(xs_ref))


def _in_proj(xp, xs, mod, pre_mix, w_in_bf16, groups):
    D = D_MODEL
    tm = TOKEN_TILE
    n_p, n_s = xp.shape[0] // tm, xs.shape[0] // tm
    return pl.pallas_call(
        functools.partial(_inproj_kernel, n_p=n_p),
        grid=(n_p + n_s,),
        in_specs=[pl.BlockSpec((tm, D), lambda s: (jnp.minimum(s, n_p - 1), 0)),
                  pl.BlockSpec((tm, D), lambda s: (jnp.maximum(s - n_p, 0), 0)),
                  pl.BlockSpec((1, 6, D), lambda s: (_batch_of_tile(s, groups, tm), 0, 0)),
                  pl.BlockSpec((1, D), lambda s: (0, 0)),
                  pl.BlockSpec((D, D_IN_PROJ), lambda s: (0, 0))],
        out_specs=pl.BlockSpec((tm, D_IN_PROJ), lambda s: (s, 0)),
        out_shape=jax.ShapeDtypeStruct(((n_p + n_s) * tm, D_IN_PROJ), BF16),
        compiler_params=_cparams("arbitrary"),
        name="in_proj",
    )(xp, xs, mod, pre_mix.reshape(1, D), w_in_bf16)


def _na_bias_table(rpb):
    qc = np.arange(GRID_W)[:, None]
    kc = np.arange(GRID_W)[None, :]
    win_start = np.clip(qc - NA_WIN_C // 2, 0, GRID_W - NA_WIN_C)
    valid = (kc >= win_start) & (kc < win_start + NA_WIN_C)
    dc = np.clip(kc - qc, -(NA_WIN_C - 1), NA_WIN_C - 1) + NA_WIN_C - 1
    pick = (dc[None] == np.arange(2 * NA_WIN_C - 1)[:, None, None]).astype(np.float32)
    full = jnp.einsum('hrc,cqk->hrqk', rpb.astype(F32), pick,
                      precision=lax.Precision.HIGHEST)
    full = jnp.where(valid[None, None], full.astype(F32), NEG_INF)
    two = jnp.concatenate([full[:, :-1], full[:, 1:]], axis=-1)
    two = two.reshape(N_HEADS_NA // 2, 2, 2 * NA_WIN_R - 2, GRID_W, 2 * GRID_W)
    return jnp.transpose(two, (0, 2, 1, 3, 4)).reshape(
        N_HEADS_NA // 2, 2 * NA_WIN_R - 2, 2 * GRID_W, 2 * GRID_W)


def _na_kernel(q_ref, kp_ref, kc_ref, kn_ref, vp_ref, vc_ref, vn_ref, bias_ref, gn_ref, o_ref,
               k_scr, v_scr, *, groups):
    s = pl.program_id(0)
    R = NA_ROWS_PER_STEP
    blk = R * GRID_W
    seq_start, seq_blocks = _seq_bounds(s, groups, blk)
    i = s - seq_start
    rows = seq_blocks * R
    k_scr[0:blk] = kp_ref[...]
    k_scr[blk:2 * blk] = kc_ref[...]
    k_scr[2 * blk:3 * blk] = kn_ref[...]
    v_scr[0:blk] = vp_ref[...]
    v_scr[blk:2 * blk] = vc_ref[...]
    v_scr[2 * blk:3 * blk] = vn_ref[...]
    lane = lax.broadcasted_iota(jnp.int32, (1, LANES), 1)
    lo = lane < HEAD_DIM
    gn = gn_ref[...]

    n_pairs = N_HEADS_NA // 2
    n_keys = NA_WIN_R * GRID_W

    def rows_body(jj, carry):
        js = [jj * NA_ROW_UNROLL + u for u in range(NA_ROW_UNROLL)]
        geo = []
        for j in js:
            r = i * R + j
            row_start = jnp.clip(r - NA_WIN_R // 2, 0, rows - NA_WIN_R)
            geo.append((r - row_start,
                        pl.multiple_of((row_start - i * R + R) * GRID_W, GRID_W),
                        pl.multiple_of(j * GRID_W, GRID_W)))
        scores = []
        for shift, koff, qoff in geo:
            for p in range(n_pairs):
                cs = slice(p * LANES, (p + 1) * LANES)
                q2 = q_ref[pl.ds(qoff, GRID_W), cs]
                zero = jnp.zeros_like(q2)
                qq = jnp.concatenate([jnp.where(lo, q2, zero), jnp.where(lo, zero, q2)], axis=0)
                k2 = k_scr[pl.ds(koff, n_keys), cs]
                scores.append(lax.dot_general(qq, k2, (((1,), (1,)), ((), ())),
                                              preferred_element_type=F32))
        probs = []
        for u, (shift, koff, qoff) in enumerate(geo):
            for p in range(n_pairs):
                sc = scores[u * n_pairs + p]
                parts = []
                for m in range(NA_WIN_R // 2):
                    d = 2 * m - shift + (NA_WIN_R - 1)
                    parts.append(sc[:, m * LANES:(m + 1) * LANES] + bias_ref[p, d])
                sc = jnp.concatenate(parts, axis=-1)
                e = jnp.exp2(sc - jnp.max(sc, axis=-1, keepdims=True))
                probs.append((e.astype(BF16), jnp.sum(e, axis=-1, keepdims=True)))
        for u, (shift, koff, qoff) in enumerate(geo):
            outs = []
            for p in range(n_pairs):
                e, l = probs[u * n_pairs + p]
                v2 = v_scr[pl.ds(koff, n_keys), p * LANES:(p + 1) * LANES]
                o2 = jnp.dot(e, v2, preferred_element_type=F32) / l
                outs.append(jnp.where(lo, o2[:GRID_W], o2[GRID_W:]))
            o = jnp.concatenate(outs, axis=-1)
            o_ref[pl.ds(qoff, GRID_W), :] = _rms(o, gn).astype(BF16)
        return carry

    lax.fori_loop(0, R // NA_ROW_UNROLL, rows_body, 0)


def _na_attention(proj, bias_tab, gn_na, groups):
    n_tok = proj.shape[0]
    blk = NA_ROWS_PER_STEP * GRID_W

    def kv(col, delta):
        def index_map(s):
            start, length = _seq_bounds(s, groups, blk)
            return (jnp.clip(s + delta, start, start + length - 1), col)
        return pl.BlockSpec((blk, D_NA), index_map)

    return pl.pallas_call(
        functools.partial(_na_kernel, groups=groups),
        grid=(n_tok // blk,),
        in_specs=[pl.BlockSpec((blk, D_NA), lambda s: (s, 0)),
                  kv(1, -1), kv(1, 0), kv(1, 1), kv(2, -1), kv(2, 0), kv(2, 1),
                  pl.BlockSpec(bias_tab.shape, lambda s: (0, 0, 0, 0)),
                  pl.BlockSpec((1, D_NA), lambda s: (0, 0))],
        out_specs=pl.BlockSpec((blk, D_NA), lambda s: (s, 0)),
        out_shape=jax.ShapeDtypeStruct((n_tok, D_NA), BF16),
        scratch_shapes=[pltpu.VMEM((3 * blk, D_NA), BF16), pltpu.VMEM((3 * blk, D_NA), BF16)],
        compiler_params=_cparams("arbitrary"),
        name="na_attn",
    )(proj, proj, proj, proj, proj, proj, proj, bias_tab, gn_na.reshape(1, D_NA))


def _wa_slab_order(a, axis):
    shape = a.shape
    split = shape[:axis] + (N_KV_WA, GQA_GROUP, HEAD_DIM) + shape[axis + 1:]
    return jnp.swapaxes(a.reshape(split), axis, axis + 1).reshape(shape)


def _t5_bucket(rel):
    half = T5_BUCKETS // 2
    ret = (rel > 0).astype(jnp.int32) * half
    n = jnp.abs(rel)
    nf = jnp.maximum(n, 1).astype(F32)
    large = T5_MAX_EXACT + (jnp.log(nf / T5_MAX_EXACT) / math.log(T5_MAX_DIST / T5_MAX_EXACT)
                            * (half - T5_MAX_EXACT)).astype(jnp.int32)
    large = jnp.minimum(large, half - 1)
    return ret + jnp.where(n < T5_MAX_EXACT, n, large)


def _wa_bias_table(t5_table):
    rel = jnp.arange(3 * WA_BLOCK)[None, :] - WA_BLOCK - jnp.arange(WA_BLOCK)[:, None]
    pick = (_t5_bucket(rel)[..., None] == jnp.arange(T5_BUCKETS)).astype(F32)
    bias = jnp.einsum('qkb,bh->qkh', pick, t5_table.astype(F32),
                      precision=lax.Precision.HIGHEST)
    bias = jnp.where((jnp.abs(rel) <= WA_WINDOW)[..., None], bias, NEG_INF)
    bias = jnp.transpose(bias, (2, 0, 1))
    return jnp.concatenate([bias[:GQA_GROUP], bias[GQA_GROUP:]], axis=1)


def _wa_kernel(sink_ref, q_ref, *refs, groups):
    s = pl.program_id(0)
    W = WA_BLOCK
    NB = WA_BLOCKS_PER_STEP
    k_refs, v_refs = refs[:NB + 2], refs[NB + 2:2 * NB + 4]
    bias_ref, gn_ref, o_ref = refs[2 * NB + 4:]
    first = s * NB
    seq_start, seq_blocks = _seq_bounds(first, groups, W)
    lane = lax.broadcasted_iota(jnp.int32, (1, LANES), 1)
    lo = lane < HEAD_DIM
    row = lax.broadcasted_iota(jnp.int32, (2 * W, 1), 0)
    col = lax.broadcasted_iota(jnp.int32, (1, 3 * W), 1)
    ks = [r[...] for r in k_refs]
    vs = [r[...] for r in v_refs]
    scores = []
    for b in range(NB):
        kcat = jnp.concatenate(ks[b:b + 3], axis=0)
        for p in range(GQA_GROUP):
            q2 = q_ref[b * W:(b + 1) * W, p * LANES:(p + 1) * LANES]
            zero = jnp.zeros_like(q2)
            qq = jnp.concatenate([jnp.where(lo, q2, zero), jnp.where(lo, zero, q2)], axis=0)
            scores.append(lax.dot_general(qq, kcat, (((1,), (1,)), ((), ())),
                                          preferred_element_type=F32))
    probs = []
    for b in range(NB):
        pen_prev = jnp.where(first + b == seq_start, NEG_INF, 0.0).astype(F32)
        pen_next = jnp.where(first + b == seq_start + seq_blocks - 1, NEG_INF, 0.0).astype(F32)
        pen = jnp.where(col < W, pen_prev, jnp.where(col >= 2 * W, pen_next, 0.0))
        for p in range(GQA_GROUP):
            sc = scores[b * GQA_GROUP + p] + bias_ref[p] + pen
            sink = jnp.where(row < W, sink_ref[p], sink_ref[p + GQA_GROUP])
            mx = jnp.maximum(jnp.max(sc, axis=-1, keepdims=True), sink)
            e = jnp.exp2(sc - mx)
            probs.append((e.astype(BF16),
                          jnp.sum(e, axis=-1, keepdims=True) + jnp.exp2(sink - mx)))
    for b in range(NB):
        vcat = jnp.concatenate(vs[b:b + 3], axis=0)
        outs = []
        for p in range(GQA_GROUP):
            e, l = probs[b * GQA_GROUP + p]
            o2 = jnp.dot(e, vcat, preferred_element_type=F32) / l
            outs.append(jnp.where(lo, o2[:W], o2[W:]))
        o = jnp.concatenate(outs, axis=-1)
        o_ref[b * W:(b + 1) * W] = _rms(o, gn_ref[...]).astype(BF16)


def _wa_attention(proj, bias_tab, sink, gn_wa_perm, groups):
    n_tok = proj.shape[0]
    W = WA_BLOCK
    q_col = 3 * D_NA // D_WA
    k_col = (3 * D_NA + D_WA) // D_KV_WA
    v_col = k_col + 1

    NB = WA_BLOCKS_PER_STEP

    def kv(col, delta):
        def index_map(s):
            start, length = _seq_bounds(s * NB, groups, W)
            return (jnp.clip(s * NB + delta, start, start + length - 1), col)
        return pl.BlockSpec((W, D_KV_WA), index_map)

    deltas = range(-1, NB + 1)
    return pl.pallas_call(
        functools.partial(_wa_kernel, groups=groups),
        grid=(n_tok // (NB * W),),
        in_specs=[pl.BlockSpec(memory_space=pltpu.SMEM),
                  pl.BlockSpec((NB * W, D_WA), lambda s: (s, q_col)),
                  *[kv(k_col, d) for d in deltas],
                  *[kv(v_col, d) for d in deltas],
                  pl.BlockSpec(bias_tab.shape, lambda s: (0, 0, 0)),
                  pl.BlockSpec((1, D_WA), lambda s: (0, 0))],
        out_specs=pl.BlockSpec((NB * W, D_WA), lambda s: (s, 0)),
        out_shape=jax.ShapeDtypeStruct((n_tok, D_WA), BF16),
        compiler_params=_cparams("arbitrary"),
        name="wa_attn",
    )(sink, *([proj] * (2 * NB + 5)), bias_tab, gn_wa_perm.reshape(1, D_WA))


def _postmix_kernel(na_ref, wa_ref, xp_ref, xs_ref, mod_ref, wo_ref, pm_ref, pf_ref, wr_ref, br_ref,
                    tri_ref, x1_ref, h2_ref, idx_ref, gate_ref, rank_ref, cnt_ref, carry, *, n_p):
    s = pl.program_id(0)

    @pl.when(s == 0)
    def _():
        carry[...] = jnp.zeros_like(carry)

    def body(x_ref):
        P = POST_MIX_PARTS
        H = TOKEN_TILE // P
        mixed = [jnp.dot(jnp.concatenate([na_ref[h * H:(h + 1) * H], wa_ref[h * H:(h + 1) * H]], axis=-1),
                         wo_ref[...], preferred_element_type=F32) for h in range(P)]
        h2s = []
        for h in range(P):
            x1 = x_ref[h * H:(h + 1) * H] + mod_ref[0, 2:3, :] * _rms(mixed[h], pm_ref[...])
            x1_ref[h * H:(h + 1) * H] = x1
            h2 = _rms(x1, pf_ref[...]) * (1.0 + mod_ref[0, 4:5, :]) + mod_ref[0, 3:4, :]
            for c in range(ROW_TILE):
                h2_ref[pl.ds(h * H * ROW_TILE + c, H, stride=ROW_TILE), :] = h2[:, c * LANES:(c + 1) * LANES]
            h2s.append(h2.astype(BF16))
        wr = wr_ref[...].astype(BF16)
        logits = jnp.concatenate(
            [lax.dot_general(wr, h2, (((1,), (1,)), ((), ())), preferred_element_type=F32)
             for h2 in h2s], axis=-1) + br_ref[...]
        eidx = lax.broadcasted_iota(jnp.int32, logits.shape, 0)
        tops, sels = [], []
        l = logits
        for _ in range(TOP_K):
            m = jnp.max(l, axis=0, keepdims=True)
            sel = jnp.min(jnp.where(l == m, eidx, N_EXPERTS), axis=0, keepdims=True)
            tops.append(m)
            sels.append(sel)
            l = jnp.where(eidx == sel, -jnp.inf, l)
        es = [jnp.exp(t - tops[0]) for t in tops]
        tot = es[0] + es[1] + es[2] + es[3]
        idx_ref[0] = jnp.concatenate(sels, axis=0)
        gate_ref[0] = jnp.concatenate([e / tot for e in es], axis=0)
        hits = [eidx == sel for sel in sels]
        onehot = sum(h.astype(F32) for h in hits)
        prefix = jnp.dot(onehot.astype(BF16), tri_ref[...], preferred_element_type=F32)
        rank_e = carry[...] + prefix - onehot
        rank_ref[0] = jnp.concatenate(
            [jnp.sum(jnp.where(h, rank_e, 0.0), axis=0, keepdims=True) for h in hits],
            axis=0).astype(jnp.int32)
        carry[...] = carry[...] + jnp.sum(onehot, axis=1, keepdims=True)
        cnt_ref[...] = carry[...]

    pl.when(s < n_p)(lambda: body(xp_ref))
    pl.when(s >= n_p)(lambda: body(xs_ref))


def _post_mix(na, wa, xp, xs, mod, w_out_bf16, post_mix, pre_ffn, w_router_t, b_router, groups):
    D = D_MODEL
    tm = TOKEN_TILE
    n_p, n_s = xp.shape[0] // tm, xs.shape[0] // tm
    nt = n_p + n_s
    tri = (np.arange(tm)[:, None] <= np.arange(tm)[None, :]).astype(np.float32)
    tok = lambda s: (s, 0)
    const2 = lambda s: (0, 0)
    tile3 = lambda s: (s, 0, 0)
    return pl.pallas_call(
        functools.partial(_postmix_kernel, n_p=n_p),
        grid=(nt,),
        in_specs=[pl.BlockSpec((tm, D_NA), tok),
                  pl.BlockSpec((tm, D_WA), tok),
                  pl.BlockSpec((tm, D), lambda s: (jnp.minimum(s, n_p - 1), 0)),
                  pl.BlockSpec((tm, D), lambda s: (jnp.maximum(s - n_p, 0), 0)),
                  pl.BlockSpec((1, 6, D), lambda s: (_batch_of_tile(s, groups, tm), 0, 0)),
                  pl.BlockSpec((D, D), const2),
                  pl.BlockSpec((1, D), const2),
                  pl.BlockSpec((1, D), const2),
                  pl.BlockSpec((N_EXPERTS, D), const2),
                  pl.BlockSpec((N_EXPERTS, 1), const2),
                  pl.BlockSpec((tm, tm), const2)],
        out_specs=[pl.BlockSpec((tm, D), tok),
                   pl.BlockSpec((tm * ROW_TILE, LANES), tok),
                   pl.BlockSpec((1, TOP_K, tm), tile3),
                   pl.BlockSpec((1, TOP_K, tm), tile3),
                   pl.BlockSpec((1, TOP_K, tm), tile3),
                   pl.BlockSpec((N_EXPERTS, 1), const2)],
        out_shape=[jax.ShapeDtypeStruct((nt * tm, D), F32),
                   jax.ShapeDtypeStruct((nt * tm * ROW_TILE, LANES), F32),
                   jax.ShapeDtypeStruct((nt, TOP_K, tm), jnp.int32),
                   jax.ShapeDtypeStruct((nt, TOP_K, tm), F32),
                   jax.ShapeDtypeStruct((nt, TOP_K, tm), jnp.int32),
                   jax.ShapeDtypeStruct((N_EXPERTS, 1), F32)],
        scratch_shapes=[pltpu.VMEM((N_EXPERTS, 1), F32)],
        compiler_params=_cparams("arbitrary"),
        name="post_mix",
    )(na, wa, xp, xs, mod, w_out_bf16, post_mix.reshape(1, D), pre_ffn.reshape(1, D),
      w_router_t, b_router.reshape(N_EXPERTS, 1), jnp.asarray(tri, BF16))


def _dispatch_kernel(padfirst_ref, padcnt_ref, nused_ref, dest_hbm, h_ref, xs_hbm, idx_smem, zero_blk,
                     idx_sem, sem, *, n_steps, n_blocks):
    s = pl.program_id(0)
    tm = TOKEN_TILE
    RT = ROW_TILE
    slot = lax.rem(s, 2)

    def idx_copy(step, sl):
        return pltpu.make_async_copy(dest_hbm.at[step], idx_smem.at[sl], idx_sem.at[sl])

    @pl.when(s == 0)
    def _():
        idx_copy(0, 0).start()
        zero_blk[...] = jnp.zeros_like(zero_blk)

        def pad_copies(e, wait):
            first = padfirst_ref[e]
            cnt = padcnt_ref[e]
            for b in reversed(range(EXPERT_BLOCK.bit_length() - 1)):
                size = (1 << b) * RT
                bit = lax.shift_right_logical(cnt, b) & 1

                @pl.when(bit == 1)
                def _():
                    cp = pltpu.make_async_copy(
                        zero_blk.at[pl.ds(0, size), :],
                        xs_hbm.at[pl.ds(pl.multiple_of(first * RT, RT), size), :], sem)
                    cp.wait() if wait else cp.start()

                first = first + bit * (1 << b)

        def issue(e, c):
            pad_copies(e, False)
            return c

        def drain(e, c):
            pad_copies(e, True)
            return c

        lax.fori_loop(0, N_EXPERTS, issue, 0)
        lax.fori_loop(0, N_EXPERTS, drain, 0)

        def blk_copy(b):
            start = pl.multiple_of(b * (EXPERT_BLOCK * RT), EXPERT_BLOCK * RT)
            return pltpu.make_async_copy(zero_blk, xs_hbm.at[pl.ds(start, EXPERT_BLOCK * RT), :], sem)

        def issue_blk(b, c):
            blk_copy(b).start()
            return c

        def drain_blk(b, c):
            blk_copy(b).wait()
            return c

        lax.fori_loop(nused_ref[0], n_blocks, issue_blk, 0)
        lax.fori_loop(nused_ref[0], n_blocks, drain_blk, 0)

    idx_copy(s, slot).wait()
    if n_steps > 1:
        @pl.when(s + 1 < n_steps)
        def _():
            idx_copy(s + 1, 1 - slot).start()

    for u in range(DISPATCH_TILES):
        for k in range(TOP_K):
            for j in range(tm):
                row = pl.multiple_of(idx_smem[slot, 0, (u * TOP_K + k) * tm + j], RT)
                pltpu.make_async_copy(h_ref.at[pl.ds((u * tm + j) * RT, RT), :],
                                      xs_hbm.at[pl.ds(row, RT), :], sem).start(priority=j % 2)
    for k in range(TOP_K):
        pltpu.make_async_copy(h_ref, xs_hbm.at[pl.ds(0, DISPATCH_TILES * tm * RT), :], sem).wait()


def _dispatch(h2, dest_rows, pad_first, pad_cnt, n_used, n_blocks):
    tm = DISPATCH_TILES * TOKEN_TILE
    n_steps = h2.shape[0] // (tm * ROW_TILE)
    grid_spec = pltpu.PrefetchScalarGridSpec(
        num_scalar_prefetch=3,
        grid=(n_steps,),
        in_specs=[pl.BlockSpec(memory_space=pl.ANY),
                  pl.BlockSpec((tm * ROW_TILE, LANES), lambda s, pf, pc, nused: (s, 0))],
        out_specs=pl.BlockSpec(memory_space=pl.ANY),
        scratch_shapes=[pltpu.SMEM((2, 1, TOP_K * tm), jnp.int32),
                        pltpu.VMEM((EXPERT_BLOCK * ROW_TILE, LANES), F32),
                        pltpu.SemaphoreType.DMA((2,)),
                        pltpu.SemaphoreType.DMA],
    )
    return pl.pallas_call(
        functools.partial(_dispatch_kernel, n_steps=n_steps, n_blocks=n_blocks),
        grid_spec=grid_spec,
        out_shape=jax.ShapeDtypeStruct((n_blocks * EXPERT_BLOCK * ROW_TILE, LANES), F32),
        compiler_params=_cparams("arbitrary"),
        name="dispatch",
    )(pad_first, pad_cnt, n_used, dest_rows.reshape(n_steps, 1, TOP_K * tm), h2)


def _mlp_kernel(blk_e_ref, blk_rows_ref, nused_ref, x_ref, wgu_ref, bgu_ref, wd_ref, bd_ref, o_ref,
                wgu_bf, wd_bf):
    i = pl.program_id(0)
    BM = EXPERT_BLOCK
    RT = ROW_TILE
    n_real = blk_rows_ref[i]

    @pl.when(n_real > 0)
    def _():
        new_expert = jnp.logical_or(i == 0, blk_e_ref[i] != blk_e_ref[jnp.maximum(i - 1, 0)])

        @pl.when(new_expert)
        def _():
            wgu_bf[...] = wgu_ref[0].astype(BF16)
            wd_bf[...] = wd_ref[0].astype(BF16)

    def mlp(n_rows):
        x = jnp.concatenate([x_ref[pl.ds(c, n_rows, stride=RT), :] for c in range(RT)],
                            axis=-1).astype(BF16)
        H = D_EXPERT // MLP_CHUNKS
        ups = []
        for h in range(MLP_CHUNKS):
            g = jnp.dot(x, wgu_bf[:, h * H:(h + 1) * H], preferred_element_type=F32)
            u = jnp.dot(x, wgu_bf[:, D_EXPERT + h * H:D_EXPERT + (h + 1) * H],
                        preferred_element_type=F32)
            ups.append((g + bgu_ref[0, :, h * H:(h + 1) * H],
                        u + bgu_ref[0, :, D_EXPERT + h * H:D_EXPERT + (h + 1) * H]))
        y = bd_ref[0]
        for h, (g, u) in enumerate(ups):
            glu = jnp.minimum(g, SWIGLU_LIMIT)
            lin = jnp.clip(u, -SWIGLU_LIMIT, SWIGLU_LIMIT)
            act = glu * (1.0 / (1.0 + jnp.exp(-SWIGLU_ALPHA * glu))) * (lin + 1.0)
            y = y + jnp.dot(act.astype(BF16), wd_bf[h * H:(h + 1) * H, :], preferred_element_type=F32)
        for c in range(RT):
            o_ref[pl.ds(c, n_rows, stride=RT), :] = y[:, c * LANES:(c + 1) * LANES]
        if n_rows < BM:
            o_ref[n_rows * RT:, :] = jnp.zeros(((BM - n_rows) * RT, LANES), F32)

    pl.when(n_real > BM // 2)(functools.partial(mlp, BM))
    pl.when(jnp.logical_and(n_real > 0, n_real <= BM // 2))(functools.partial(mlp, BM // 2))

    @pl.when(n_real == 0)
    def _():
        o_ref[...] = jnp.zeros_like(o_ref)


def _expert_mlp(xs, blk_e, blk_rows, n_used, w_gate_up, b_gate_up, w_down, b_down):
    D = D_MODEL
    BM = EXPERT_BLOCK
    n_blocks = blk_e.shape[0]
    wmap = lambda i, be, br, nu: (be[i], 0, 0)
    grid_spec = pltpu.PrefetchScalarGridSpec(
        num_scalar_prefetch=3,
        grid=(n_blocks,),
        in_specs=[pl.BlockSpec((BM * ROW_TILE, LANES),
                               lambda i, be, br, nu: (jnp.minimum(i, nu[0] - 1), 0)),
                  pl.BlockSpec((1, D, 2 * D_EXPERT), wmap),
                  pl.BlockSpec((1, 1, 2 * D_EXPERT), wmap),
                  pl.BlockSpec((1, D_EXPERT, D), wmap),
                  pl.BlockSpec((1, 1, D), wmap)],
        out_specs=pl.BlockSpec((BM * ROW_TILE, LANES), lambda i, be, br, nu: (i, 0)),
        scratch_shapes=[pltpu.VMEM((D, 2 * D_EXPERT), BF16),
                        pltpu.VMEM((D_EXPERT, D), BF16)],
    )
    return pl.pallas_call(
        _mlp_kernel,
        grid_spec=grid_spec,
        out_shape=jax.ShapeDtypeStruct(xs.shape, F32),
        compiler_params=_cparams("arbitrary"),
        name="expert_mlp",
    )(blk_e, blk_rows, n_used, xs, w_gate_up, b_gate_up.reshape(N_EXPERTS, 1, -1), w_down,
      b_down.reshape(N_EXPERTS, 1, -1))


def _gather_combine_kernel(dest_hbm, ys_hbm, gate_ref, x1_ref, mod_ref, pf_ref, op_ref, os_ref,
                           idx_smem, yb0, yb1, idx_sem, row_sem, *, n_steps, n_p):
    s = pl.program_id(0)
    tc = COMBINE_TILE
    RT = ROW_TILE
    n_rows = TOP_K * tc
    last = n_steps - 1
    ybufs = (yb0, yb1)
    n_idx = 3

    def idx_copy(step, sl):
        return pltpu.make_async_copy(dest_hbm.at[step], idx_smem.at[sl], idx_sem.at[sl])

    def start_rows(islot, b):
        for r in range(n_rows):
            row = pl.multiple_of(idx_smem[islot, 0, r], RT)
            pltpu.make_async_copy(ys_hbm.at[pl.ds(row, RT), :], ybufs[b].at[pl.ds(r * RT, RT), :],
                                  row_sem.at[b]).start(priority=r % 2)

    def wait_rows(b):
        pltpu.make_async_copy(ys_hbm.at[pl.ds(0, n_rows * RT), :], ybufs[b], row_sem.at[b]).wait()

    @pl.when(s == 0)
    def _():
        for t in range(min(2, n_steps)):
            idx_copy(t, t).start()
        idx_copy(0, 0).wait()
        start_rows(0, 0)

    @pl.when(s + 2 < n_steps)
    def _():
        idx_copy(s + 2, lax.rem(s + 2, n_idx)).start()

    @pl.when(s + 1 < n_steps)
    def _():
        idx_copy(s + 1, lax.rem(s + 1, n_idx)).wait()

    def body(cur):
        wait_rows(cur)
        start_rows(lax.rem(jnp.minimum(s + 1, last), n_idx), 1 - cur)
        g = gate_ref[...]

        def rows(k):
            return jnp.concatenate([ybufs[cur][pl.ds(k * tc * RT + c, tc, stride=RT), :]
                                    for c in range(RT)], axis=-1)

        m = rows(0) * g[:, 0:1]
        for k in range(1, TOP_K):
            m = m + rows(k) * g[:, k:k + 1]
        res = x1_ref[...] + mod_ref[0, 5:6, :] * _rms(m, pf_ref[...])

        @pl.when(s < n_p)
        def _():
            op_ref[...] = res

        @pl.when(s >= n_p)
        def _():
            os_ref[...] = res

    for cur in range(2):
        pl.when(lax.rem(s, 2) == cur)(functools.partial(body, cur))

    @pl.when(s == last)
    def _():
        wait_rows(1 - last % 2)


def _gather_combine(dest_rows, ys, gates_tok, x1, mod, post_ffn, n_tok_p, n_tok_s, groups):
    D = D_MODEL
    tc = COMBINE_TILE
    n_p, n_s = n_tok_p // tc, n_tok_s // tc
    nt = n_p + n_s
    return pl.pallas_call(
        functools.partial(_gather_combine_kernel, n_steps=nt, n_p=n_p),
        grid=(nt,),
        in_specs=[pl.BlockSpec(memory_space=pl.ANY),
                  pl.BlockSpec(memory_space=pl.ANY),
                  pl.BlockSpec((tc, TOP_K), lambda s: (s, 0)),
                  pl.BlockSpec((tc, D), lambda s: (s, 0)),
                  pl.BlockSpec((1, 6, D), lambda s: (_batch_of_tile(s, groups, tc), 0, 0)),
                  pl.BlockSpec((1, D), lambda s: (0, 0))],
        out_specs=[pl.BlockSpec((tc, D), lambda s: (jnp.minimum(s, n_p - 1), 0)),
                   pl.BlockSpec((tc, D), lambda s: (jnp.maximum(s - n_p, 0), 0))],
        out_shape=[jax.ShapeDtypeStruct((n_tok_p, D), F32),
                   jax.ShapeDtypeStruct((n_tok_s, D), F32)],
        scratch_shapes=[pltpu.SMEM((3, 1, TOP_K * tc), jnp.int32),
                        pltpu.VMEM((TOP_K * tc * ROW_TILE, LANES), F32),
                        pltpu.VMEM((TOP_K * tc * ROW_TILE, LANES), F32),
                        pltpu.SemaphoreType.DMA((3,)),
                        pltpu.SemaphoreType.DMA((2,))],
        compiler_params=_cparams("arbitrary"),
        name="combine",
    )(dest_rows.reshape(nt, 1, TOP_K * tc), ys, gates_tok, x1, mod, post_ffn.reshape(1, D))


def _routing_dest(idx_t, rank_t, counts):
    BM = EXPERT_BLOCK
    nt, _, tm = idx_t.shape
    n_blocks = (nt * tm * TOP_K) // BM + N_EXPERTS
    padded = ((counts + BM - 1) // BM) * BM
    pad_end = jnp.cumsum(padded)
    pad_start = pad_end - padded
    experts = jnp.arange(N_EXPERTS, dtype=jnp.int32)
    start_of = jnp.sum(jnp.where(idx_t[..., None] == experts, pad_start, 0), axis=-1)
    dest = (start_of + rank_t).astype(jnp.int32)
    blk_first = jnp.arange(n_blocks, dtype=jnp.int32) * BM
    blk_e = jnp.minimum(jnp.sum((pad_end[None, :] <= blk_first[:, None]).astype(jnp.int32), axis=1),
                        N_EXPERTS - 1)
    n_used = (pad_end[-1:] // BM).astype(jnp.int32)
    pad_first = (pad_start + counts).astype(jnp.int32)
    pad_cnt = (padded - counts).astype(jnp.int32)
    real_end = jnp.sum(jnp.where(blk_e[:, None] == experts, pad_first, 0), axis=1)
    blk_rows = jnp.clip(real_end - blk_first, 0, BM).astype(jnp.int32)
    return dest, blk_e, blk_rows, n_used, pad_first, pad_cnt, n_blocks


def kernel(x_prompt, x_sample, c_prompt, c_sample, w_ada, b_ada, pre_mix, post_mix, pre_ffn, post_ffn,
           w_in, na_rel_bias, t5_rel_bias, wa_sink, gn_na, gn_wa, w_out, w_router, b_router,
           w_gate_up, b_gate_up, w_down, b_down):
    depth = w_ada.shape[0]
    D = D_MODEL
    groups = (x_prompt.shape[:2], x_sample.shape[:2])
    n_tok_p = x_prompt.shape[0] * x_prompt.shape[1]
    n_tok_s = x_sample.shape[0] * x_sample.shape[1]
    n_tok = n_tok_p + n_tok_s
    yp = x_prompt.reshape(n_tok_p, D)
    ys = x_sample.reshape(n_tok_s, D)
    c_all = jnp.concatenate([c_prompt, c_sample], axis=0)
    for l in range(depth):
        mod = _ada(c_all, w_ada[l], b_ada[l]).reshape(c_all.shape[0], 6, D)
        qb0 = 3 * D_NA
        scale = HEAD_DIM ** -0.5 * LOG2E
        w_in_l = jnp.concatenate([w_in[l][:, :D_NA] * scale, w_in[l][:, D_NA:qb0],
                                  _wa_slab_order(w_in[l][:, qb0:qb0 + D_WA] * scale, axis=1),
                                  w_in[l][:, qb0 + D_WA:]], axis=1).astype(BF16)
        w_out_l = jnp.concatenate([w_out[l][:D_NA], _wa_slab_order(w_out[l][D_NA:], axis=0)],
                                  axis=0).astype(BF16)
        proj = _in_proj(yp, ys, mod, pre_mix[l], w_in_l, groups)
        na = _na_attention(proj, _na_bias_table(na_rel_bias[l] * LOG2E), gn_na[l], groups)
        wa = _wa_attention(proj, _wa_bias_table(t5_rel_bias * LOG2E), wa_sink[l] * LOG2E,
                           _wa_slab_order(gn_wa[l], axis=0), groups)
        x1, h2, idx_t, gate_t, rank_t, counts = _post_mix(
            na, wa, yp, ys, mod, w_out_l, post_mix[l], pre_ffn[l], w_router[l].T, b_router[l], groups)
        dest, blk_e, blk_rows, n_used, pad_first, pad_cnt, n_blocks = _routing_dest(
            idx_t, rank_t, counts.reshape(N_EXPERTS).astype(jnp.int32))
        dest_rows = dest * ROW_TILE
        xs = _dispatch(h2, dest_rows, pad_first, pad_cnt, n_used, n_blocks)
        ys_rows = _expert_mlp(xs, blk_e, blk_rows, n_used, w_gate_up[l], b_gate_up[l], w_down[l], b_down[l])
        gates_tok = jnp.transpose(gate_t, (0, 2, 1)).reshape(n_tok, TOP_K)
        halves = TOKEN_TILE // COMBINE_TILE
        dest_steps = jnp.transpose(dest_rows.reshape(-1, TOP_K, halves, COMBINE_TILE), (0, 2, 1, 3))
        yp, ys = _gather_combine(dest_steps, ys_rows, gates_tok, x1, mod, post_ffn[l], n_tok_p, n_tok_s,
                                 groups)
    return (yp.reshape(x_prompt.shape), ys.reshape(x_sample.shape))
```

```python
import functools
import math

import jax
import jax.numpy as jnp
import numpy as np
from jax import lax
from jax.experimental import pallas as pl
from jax.experimental.pallas import tpu as pltpu

F32 = jnp.float32
BF16 = jnp.bfloat16

D_MODEL = 1024
HEAD_DIM = 64
D_NA = 512
D_WA = 512
N_HEADS_NA = 8
N_HEADS_WA = 8
N_KV_WA = 2
GQA_GROUP = 4
D_KV_WA = 128
D_IN_PROJ = 3 * D_NA + D_WA + 2 * D_KV_WA
GRID_W = 64
NA_WIN_R = 8
NA_WIN_C = 16
WA_WINDOW = 128
WA_BLOCK = 128
T5_BUCKETS = 32
T5_MAX_EXACT = 8
T5_MAX_DIST = 128
N_EXPERTS = 32
TOP_K = 4
D_EXPERT = D_MODEL
SWIGLU_LIMIT = 7.0
SWIGLU_ALPHA = 1.702
RMS_EPS = 1e-6
NEG_INF = -1e30
LOG2E = math.log2(math.e)

VMEM_LIMIT_BYTES = 56 * 1024 * 1024
LANES = 128
ROW_TILE = D_MODEL // LANES

TOKEN_TILE = 512
IN_PROJ_PARTS = 2
POST_MIX_PARTS = 4
NA_ROWS_PER_STEP = 8
NA_ROW_UNROLL = 8
WA_BLOCKS_PER_STEP = 4
EXPERT_BLOCK = 512
MLP_CHUNKS = 2
DISPATCH_TILES = 4
COMBINE_TILE = 512


def _cparams(*sem):
    return pltpu.CompilerParams(dimension_semantics=sem, vmem_limit_bytes=VMEM_LIMIT_BYTES)


def _rms(x, g):
    return x * lax.rsqrt(jnp.mean(x * x, axis=-1, keepdims=True) + RMS_EPS) * g


def _seq_bounds(s, groups, block):
    base = 0
    start, length = None, None
    for n_seq, seq_len in groups:
        per = seq_len // block
        g_start = base + ((s - base) // per) * per
        if start is None:
            start, length = g_start, per
        else:
            inside = s >= base
            start = jnp.where(inside, g_start, start)
            length = jnp.where(inside, per, length)
        base += n_seq * per
    return start, length


def _batch_of_tile(s, groups, block):
    base_blk, base_seq = 0, 0
    out = None
    for n_seq, seq_len in groups:
        per = seq_len // block
        b = base_seq + (s - base_blk) // per
        out = b if out is None else jnp.where(s >= base_blk, b, out)
        base_blk += n_seq * per
        base_seq += n_seq
    return out


def _ada_kernel(c_ref, w_ref, b_ref, o_ref):
    c = c_ref[...]
    s = c * (1.0 / (1.0 + jnp.exp(-c)))
    o_ref[...] = jnp.dot(s, w_ref[...], preferred_element_type=F32,
                         precision=lax.Precision.HIGHEST) + b_ref[...]


def _ada(c_all, w_ada, b_ada):
    nb = c_all.shape[0]
    return pl.pallas_call(
        _ada_kernel,
        grid=(6,),
        in_specs=[pl.BlockSpec((nb, D_MODEL), lambda j: (0, 0)),
                  pl.BlockSpec((D_MODEL, D_MODEL), lambda j: (0, j)),
                  pl.BlockSpec((1, D_MODEL), lambda j: (0, j))],
        out_specs=pl.BlockSpec((nb, D_MODEL), lambda j: (0, j)),
        out_shape=jax.ShapeDtypeStruct((nb, 6 * D_MODEL), F32),
        compiler_params=_cparams("arbitrary"),
        name="ada",
    )(c_all, w_ada, b_ada.reshape(1, -1))


def _inproj_kernel(xp_ref, xs_ref, mod_ref, g_ref, w_ref, o_ref, *, n_p):
    s = pl.program_id(0)

    def body(x_ref):
        H = TOKEN_TILE // IN_PROJ_PARTS
        hs = [(_rms(x_ref[p * H:(p + 1) * H], g_ref[...]) * (1.0 + mod_ref[0, 1:2, :])
               + mod_ref[0, 0:1, :]).astype(BF16) for p in range(IN_PROJ_PARTS)]
        for p, h in enumerate(hs):
            o_ref[p * H:(p + 1) * H] = jnp.dot(h, w_ref[...], preferred_element_type=F32).astype(BF16)

    pl.when(s < n_p)(lambda: body(xp_ref))
    pl.when(s >= n_p)(lambda: body(xs_ref))


def _in_proj(xp, xs, mod, pre_mix, w_in_bf16, groups):
    D = D_MODEL
    tm = TOKEN_TILE
    n_p, n_s = xp.shape[0] // tm, xs.shape[0] // tm
    return pl.pallas_call(
        functools.partial(_inproj_kernel, n_p=n_p),
        grid=(n_p + n_s,),
        in_specs=[pl.BlockSpec((tm, D), lambda s: (jnp.minimum(s, n_p - 1), 0)),
                  pl.BlockSpec((tm, D), lambda s: (jnp.maximum(s - n_p, 0), 0)),
                  pl.BlockSpec((1, 6, D), lambda s: (_batch_of_tile(s, groups, tm), 0, 0)),
                  pl.BlockSpec((1, D), lambda s: (0, 0)),
                  pl.BlockSpec((D, D_IN_PROJ), lambda s: (0, 0))],
        out_specs=pl.BlockSpec((tm, D_IN_PROJ), lambda s: (s, 0)),
        out_shape=jax.ShapeDtypeStruct(((n_p + n_s) * tm, D_IN_PROJ), BF16),
        compiler_params=_cparams("arbitrary"),
        name="in_proj",
    )(xp, xs, mod, pre_mix.reshape(1, D), w_in_bf16)


def _na_bias_table(rpb):
    qc = np.arange(GRID_W)[:, None]
    kc = np.arange(GRID_W)[None, :]
    win_start = np.clip(qc - NA_WIN_C // 2, 0, GRID_W - NA_WIN_C)
    valid = (kc >= win_start) & (kc < win_start + NA_WIN_C)
    dc = np.clip(kc - qc, -(NA_WIN_C - 1), NA_WIN_C - 1) + NA_WIN_C - 1
    pick = (dc[None] == np.arange(2 * NA_WIN_C - 1)[:, None, None]).astype(np.float32)
    full = jnp.einsum('hrc,cqk->hrqk', rpb.astype(F32), pick,
                      precision=lax.Precision.HIGHEST)
    full = jnp.where(valid[None, None], full.astype(F32), NEG_INF)
    two = jnp.concatenate([full[:, :-1], full[:, 1:]], axis=-1)
    two = two.reshape(N_HEADS_NA // 2, 2, 2 * NA_WIN_R - 2, GRID_W, 2 * GRID_W)
    return jnp.transpose(two, (0, 2, 1, 3, 4)).reshape(
        N_HEADS_NA // 2, 2 * NA_WIN_R - 2, 2 * GRID_W, 2 * GRID_W)


def _na_kernel(q_ref, kp_ref, kc_ref, kn_ref, vp_ref, vc_ref, vn_ref, bias_ref, gn_ref, o_ref,
               k_scr, v_scr, *, groups):
    s = pl.program_id(0)
    R = NA_ROWS_PER_STEP
    blk = R * GRID_W
    seq_start, seq_blocks = _seq_bounds(s, groups, blk)
    i = s - seq_start
    rows = seq_blocks * R
    k_scr[0:blk] = kp_ref[...]
    k_scr[blk:2 * blk] = kc_ref[...]
    k_scr[2 * blk:3 * blk] = kn_ref[...]
    v_scr[0:blk] = vp_ref[...]
    v_scr[blk:2 * blk] = vc_ref[...]
    v_scr[2 * blk:3 * blk] = vn_ref[...]
    lane = lax.broadcasted_iota(jnp.int32, (1, LANES), 1)
    lo = lane < HEAD_DIM
    gn = gn_ref[...]

    n_pairs = N_HEADS_NA // 2
    n_keys = NA_WIN_R * GRID_W

    def rows_body(jj, carry):
        js = [jj * NA_ROW_UNROLL + u for u in range(NA_ROW_UNROLL)]
        geo = []
        for j in js:
            r = i * R + j
            row_start = jnp.clip(r - NA_WIN_R // 2, 0, rows - NA_WIN_R)
            geo.append((r - row_start,
                        pl.multiple_of((row_start - i * R + R) * GRID_W, GRID_W),
                        pl.multiple_of(j * GRID_W, GRID_W)))
        scores = []
        for shift, koff, qoff in geo:
            for p in range(n_pairs):
                cs = slice(p * LANES, (p + 1) * LANES)
                q2 = q_ref[pl.ds(qoff, GRID_W), cs]
                zero = jnp.zeros_like(q2)
                qq = jnp.concatenate([jnp.where(lo, q2, zero), jnp.where(lo, zero, q2)], axis=0)
                k2 = k_scr[pl.ds(koff, n_keys), cs]
                scores.append(lax.dot_general(qq, k2, (((1,), (1,)), ((), ())),
                                              preferred_element_type=F32))
        probs = []
        for u, (shift, koff, qoff) in enumerate(geo):
            for p in range(n_pairs):
                sc = scores[u * n_pairs + p]
                parts = []
                for m in range(NA_WIN_R // 2):
                    d = 2 * m - shift + (NA_WIN_R - 1)
                    parts.append(sc[:, m * LANES:(m + 1) * LANES] + bias_ref[p, d])
                sc = jnp.concatenate(parts, axis=-1)
                e = jnp.exp2(sc - jnp.max(sc, axis=-1, keepdims=True))
                probs.append((e.astype(BF16), jnp.sum(e, axis=-1, keepdims=True)))
        for u, (shift, koff, qoff) in enumerate(geo):
            outs = []
            for p in range(n_pairs):
                e, l = probs[u * n_pairs + p]
                v2 = v_scr[pl.ds(koff, n_keys), p * LANES:(p + 1) * LANES]
                o2 = jnp.dot(e, v2, preferred_element_type=F32) / l
                outs.append(jnp.where(lo, o2[:GRID_W], o2[GRID_W:]))
            o = jnp.concatenate(outs, axis=-1)
            o_ref[pl.ds(qoff, GRID_W), :] = _rms(o, gn).astype(BF16)
        return carry

    lax.fori_loop(0, R // NA_ROW_UNROLL, rows_body, 0)


def _na_attention(proj, bias_tab, gn_na, groups):
    n_tok = proj.shape[0]
    blk = NA_ROWS_PER_STEP * GRID_W

    def kv(col, delta):
        def index_map(s):
            start, length = _seq_bounds(s, groups, blk)
            return (jnp.clip(s + delta, start, start + length - 1), col)
        return pl.BlockSpec((blk, D_NA), index_map)

    return pl.pallas_call(
        functools.partial(_na_kernel, groups=groups),
        grid=(n_tok // blk,),
        in_specs=[pl.BlockSpec((blk, D_NA), lambda s: (s, 0)),
                  kv(1, -1), kv(1, 0), kv(1, 1), kv(2, -1), kv(2, 0), kv(2, 1),
                  pl.BlockSpec(bias_tab.shape, lambda s: (0, 0, 0, 0)),
                  pl.BlockSpec((1, D_NA), lambda s: (0, 0))],
        out_specs=pl.BlockSpec((blk, D_NA), lambda s: (s, 0)),
        out_shape=jax.ShapeDtypeStruct((n_tok, D_NA), BF16),
        scratch_shapes=[pltpu.VMEM((3 * blk, D_NA), BF16), pltpu.VMEM((3 * blk, D_NA), BF16)],
        compiler_params=_cparams("arbitrary"),
        name="na_attn",
    )(proj, proj, proj, proj, proj, proj, proj, bias_tab, gn_na.reshape(1, D_NA))


def _wa_slab_order(a, axis):
    shape = a.shape
    split = shape[:axis] + (N_KV_WA, GQA_GROUP, HEAD_DIM) + shape[axis + 1:]
    return jnp.swapaxes(a.reshape(split), axis, axis + 1).reshape(shape)


def _t5_bucket(rel):
    half = T5_BUCKETS // 2
    ret = (rel > 0).astype(jnp.int32) * half
    n = jnp.abs(rel)
    nf = jnp.maximum(n, 1).astype(F32)
    large = T5_MAX_EXACT + (jnp.log(nf / T5_MAX_EXACT) / math.log(T5_MAX_DIST / T5_MAX_EXACT)
                            * (half - T5_MAX_EXACT)).astype(jnp.int32)
    large = jnp.minimum(large, half - 1)
    return ret + jnp.where(n < T5_MAX_EXACT, n, large)


def _wa_bias_table(t5_table):
    rel = jnp.arange(3 * WA_BLOCK)[None, :] - WA_BLOCK - jnp.arange(WA_BLOCK)[:, None]
    pick = (_t5_bucket(rel)[..., None] == jnp.arange(T5_BUCKETS)).astype(F32)
    bias = jnp.einsum('qkb,bh->qkh', pick, t5_table.astype(F32),
                      precision=lax.Precision.HIGHEST)
    bias = jnp.where((jnp.abs(rel) <= WA_WINDOW)[..., None], bias, NEG_INF)
    bias = jnp.transpose(bias, (2, 0, 1))
    return jnp.concatenate([bias[:GQA_GROUP], bias[GQA_GROUP:]], axis=1)


def _wa_kernel(sink_ref, q_ref, *refs, groups):
    s = pl.program_id(0)
    W = WA_BLOCK
    NB = WA_BLOCKS_PER_STEP
    k_refs, v_refs = refs[:NB + 2], refs[NB + 2:2 * NB + 4]
    bias_ref, gn_ref, o_ref = refs[2 * NB + 4:]
    first = s * NB
    seq_start, seq_blocks = _seq_bounds(first, groups, W)
    lane = lax.broadcasted_iota(jnp.int32, (1, LANES), 1)
    lo = lane < HEAD_DIM
    row = lax.broadcasted_iota(jnp.int32, (2 * W, 1), 0)
    col = lax.broadcasted_iota(jnp.int32, (1, 3 * W), 1)
    ks = [r[...] for r in k_refs]
    vs = [r[...] for r in v_refs]
    scores = []
    for b in range(NB):
        kcat = jnp.concatenate(ks[b:b + 3], axis=0)
        for p in range(GQA_GROUP):
            q2 = q_ref[b * W:(b + 1) * W, p * LANES:(p + 1) * LANES]
            zero = jnp.zeros_like(q2)
            qq = jnp.concatenate([jnp.where(lo, q2, zero), jnp.where(lo, zero, q2)], axis=0)
            scores.append(lax.dot_general(qq, kcat, (((1,), (1,)), ((), ())),
                                          preferred_element_type=F32))
    probs = []
    for b in range(NB):
        pen_prev = jnp.where(first + b == seq_start, NEG_INF, 0.0).astype(F32)
        pen_next = jnp.where(first + b == seq_start + seq_blocks - 1, NEG_INF, 0.0).astype(F32)
        pen = jnp.where(col < W, pen_prev, jnp.where(col >= 2 * W, pen_next, 0.0))
        for p in range(GQA_GROUP):
            sc = scores[b * GQA_GROUP + p] + bias_ref[p] + pen
            sink = jnp.where(row < W, sink_ref[p], sink_ref[p + GQA_GROUP])
            mx = jnp.maximum(jnp.max(sc, axis=-1, keepdims=True), sink)
            e = jnp.exp2(sc - mx)
            probs.append((e.astype(BF16),
                          jnp.sum(e, axis=-1, keepdims=True) + jnp.exp2(sink - mx)))
    for b in range(NB):
        vcat = jnp.concatenate(vs[b:b + 3], axis=0)
        outs = []
        for p in range(GQA_GROUP):
            e, l = probs[b * GQA_GROUP + p]
            o2 = jnp.dot(e, vcat, preferred_element_type=F32) / l
            outs.append(jnp.where(lo, o2[:W], o2[W:]))
        o = jnp.concatenate(outs, axis=-1)
        o_ref[b * W:(b + 1) * W] = _rms(o, gn_ref[...]).astype(BF16)


def _wa_attention(proj, bias_tab, sink, gn_wa_perm, groups):
    n_tok = proj.shape[0]
    W = WA_BLOCK
    q_col = 3 * D_NA // D_WA
    k_col = (3 * D_NA + D_WA) // D_KV_WA
    v_col = k_col + 1

    NB = WA_BLOCKS_PER_STEP

    def kv(col, delta):
        def index_map(s):
            start, length = _seq_bounds(s * NB, groups, W)
            return (jnp.clip(s * NB + delta, start, start + length - 1), col)
        return pl.BlockSpec((W, D_KV_WA), index_map)

    deltas = range(-1, NB + 1)
    return pl.pallas_call(
        functools.partial(_wa_kernel, groups=groups),
        grid=(n_tok // (NB * W),),
        in_specs=[pl.BlockSpec(memory_space=pltpu.SMEM),
                  pl.BlockSpec((NB * W, D_WA), lambda s: (s, q_col)),
                  *[kv(k_col, d) for d in deltas],
                  *[kv(v_col, d) for d in deltas],
                  pl.BlockSpec(bias_tab.shape, lambda s: (0, 0, 0)),
                  pl.BlockSpec((1, D_WA), lambda s: (0, 0))],
        out_specs=pl.BlockSpec((NB * W, D_WA), lambda s: (s, 0)),
        out_shape=jax.ShapeDtypeStruct((n_tok, D_WA), BF16),
        compiler_params=_cparams("arbitrary"),
        name="wa_attn",
    )(sink, *([proj] * (2 * NB + 5)), bias_tab, gn_wa_perm.reshape(1, D_WA))


def _postmix_kernel(na_ref, wa_ref, xp_ref, xs_ref, mod_ref, wo_ref, pm_ref, pf_ref, wr_ref, br_ref,
                    tri_ref, x1_ref, h2_ref, idx_ref, gate_ref, rank_ref, cnt_ref, carry, *, n_p):
    s = pl.program_id(0)

    @pl.when(s == 0)
    def _():
        carry[...] = jnp.zeros_like(carry)

    def body(x_ref):
        P = POST_MIX_PARTS
        H = TOKEN_TILE // P
        mixed = [jnp.dot(jnp.concatenate([na_ref[h * H:(h + 1) * H], wa_ref[h * H:(h + 1) * H]], axis=-1),
                         wo_ref[...], preferred_element_type=F32) for h in range(P)]
        h2s = []
        for h in range(P):
            x1 = x_ref[h * H:(h + 1) * H] + mod_ref[0, 2:3, :] * _rms(mixed[h], pm_ref[...])
            x1_ref[h * H:(h + 1) * H] = x1
            h2 = _rms(x1, pf_ref[...]) * (1.0 + mod_ref[0, 4:5, :]) + mod_ref[0, 3:4, :]
            for c in range(ROW_TILE):
                h2_ref[pl.ds(h * H * ROW_TILE + c, H, stride=ROW_TILE), :] = h2[:, c * LANES:(c + 1) * LANES]
            h2s.append(h2.astype(BF16))
        wr = wr_ref[...].astype(BF16)
        logits = jnp.concatenate(
            [lax.dot_general(wr, h2, (((1,), (1,)), ((), ())), preferred_element_type=F32)
             for h2 in h2s], axis=-1) + br_ref[...]
        eidx = lax.broadcasted_iota(jnp.int32, logits.shape, 0)
        tops, sels = [], []
        l = logits
        for _ in range(TOP_K):
            m = jnp.max(l, axis=0, keepdims=True)
            sel = jnp.min(jnp.where(l == m, eidx, N_EXPERTS), axis=0, keepdims=True)
            tops.append(m)
            sels.append(sel)
            l = jnp.where(eidx == sel, -jnp.inf, l)
        es = [jnp.exp(t - tops[0]) for t in tops]
        tot = es[0] + es[1] + es[2] + es[3]
        idx_ref[0] = jnp.concatenate(sels, axis=0)
        gate_ref[0] = jnp.concatenate([e / tot for e in es], axis=0)
        hits = [eidx == sel for sel in sels]
        onehot = sum(h.astype(F32) for h in hits)
        prefix = jnp.dot(onehot.astype(BF16), tri_ref[...], preferred_element_type=F32)
        rank_e = carry[...] + prefix - onehot
        rank_ref[0] = jnp.concatenate(
            [jnp.sum(jnp.where(h, rank_e, 0.0), axis=0, keepdims=True) for h in hits],
            axis=0).astype(jnp.int32)
        carry[...] = carry[...] + jnp.sum(onehot, axis=1, keepdims=True)
        cnt_ref[...] = carry[...]

    pl.when(s < n_p)(lambda: body(xp_ref))
    pl.when(s >= n_p)(lambda: body(xs_ref))


def _post_mix(na, wa, xp, xs, mod, w_out_bf16, post_mix, pre_ffn, w_router_t, b_router, groups):
    D = D_MODEL
    tm = TOKEN_TILE
    n_p, n_s = xp.shape[0] // tm, xs.shape[0] // tm
    nt = n_p + n_s
    tri = (np.arange(tm)[:, None] <= np.arange(tm)[None, :]).astype(np.float32)
    tok = lambda s: (s, 0)
    const2 = lambda s: (0, 0)
    tile3 = lambda s: (s, 0, 0)
    return pl.pallas_call(
        functools.partial(_postmix_kernel, n_p=n_p),
        grid=(nt,),
        in_specs=[pl.BlockSpec((tm, D_NA), tok),
                  pl.BlockSpec((tm, D_WA), tok),
                  pl.BlockSpec((tm, D), lambda s: (jnp.minimum(s, n_p - 1), 0)),
                  pl.BlockSpec((tm, D), lambda s: (jnp.maximum(s - n_p, 0), 0)),
                  pl.BlockSpec((1, 6, D), lambda s: (_batch_of_tile(s, groups, tm), 0, 0)),
                  pl.BlockSpec((D, D), const2),
                  pl.BlockSpec((1, D), const2),
                  pl.BlockSpec((1, D), const2),
                  pl.BlockSpec((N_EXPERTS, D), const2),
                  pl.BlockSpec((N_EXPERTS, 1), const2),
                  pl.BlockSpec((tm, tm), const2)],
        out_specs=[pl.BlockSpec((tm, D), tok),
                   pl.BlockSpec((tm * ROW_TILE, LANES), tok),
                   pl.BlockSpec((1, TOP_K, tm), tile3),
                   pl.BlockSpec((1, TOP_K, tm), tile3),
                   pl.BlockSpec((1, TOP_K, tm), tile3),
                   pl.BlockSpec((N_EXPERTS, 1), const2)],
        out_shape=[jax.ShapeDtypeStruct((nt * tm, D), F32),
                   jax.ShapeDtypeStruct((nt * tm * ROW_TILE, LANES), F32),
                   jax.ShapeDtypeStruct((nt, TOP_K, tm), jnp.int32),
                   jax.ShapeDtypeStruct((nt, TOP_K, tm), F32),
                   jax.ShapeDtypeStruct((nt, TOP_K, tm), jnp.int32),
                   jax.ShapeDtypeStruct((N_EXPERTS, 1), F32)],
        scratch_shapes=[pltpu.VMEM((N_EXPERTS, 1), F32)],
        compiler_params=_cparams("arbitrary"),
        name="post_mix",
    )(na, wa, xp, xs, mod, w_out_bf16, post_mix.reshape(1, D), pre_ffn.reshape(1, D),
      w_router_t, b_router.reshape(N_EXPERTS, 1), jnp.asarray(tri, BF16))


def _dispatch_kernel(padfirst_ref, padcnt_ref, nused_ref, dest_hbm, h_ref, xs_hbm, idx_smem, zero_blk,
                     idx_sem, sem, *, n_steps, n_blocks):
    s = pl.program_id(0)
    tm = TOKEN_TILE
    RT = ROW_TILE
    slot = lax.rem(s, 2)

    def idx_copy(step, sl):
        return pltpu.make_async_copy(dest_hbm.at[step], idx_smem.at[sl], idx_sem.at[sl])

    @pl.when(s == 0)
    def _():
        idx_copy(0, 0).start()
        zero_blk[...] = jnp.zeros_like(zero_blk)

        def pad_copies(e, wait):
            first = padfirst_ref[e]
            cnt = padcnt_ref[e]
            for b in reversed(range(EXPERT_BLOCK.bit_length() - 1)):
                size = (1 << b) * RT
                bit = lax.shift_right_logical(cnt, b) & 1

                @pl.when(bit == 1)
                def _():
                    cp = pltpu.make_async_copy(
                        zero_blk.at[pl.ds(0, size), :],
                        xs_hbm.at[pl.ds(pl.multiple_of(first * RT, RT), size), :], sem)
                    cp.wait() if wait else cp.start()

                first = first + bit * (1 << b)

        def issue(e, c):
            pad_copies(e, False)
            return c

        def drain(e, c):
            pad_copies(e, True)
            return c

        lax.fori_loop(0, N_EXPERTS, issue, 0)
        lax.fori_loop(0, N_EXPERTS, drain, 0)

        def blk_copy(b):
            start = pl.multiple_of(b * (EXPERT_BLOCK * RT), EXPERT_BLOCK * RT)
            return pltpu.make_async_copy(zero_blk, xs_hbm.at[pl.ds(start, EXPERT_BLOCK * RT), :], sem)

        def issue_blk(b, c):
            blk_copy(b).start()
            return c

        def drain_blk(b, c):
            blk_copy(b).wait()
            return c

        lax.fori_loop(nused_ref[0], n_blocks, issue_blk, 0)
        lax.fori_loop(nused_ref[0], n_blocks, drain_blk, 0)

    idx_copy(s, slot).wait()
    if n_steps > 1:
        @pl.when(s + 1 < n_steps)
        def _():
            idx_copy(s + 1, 1 - slot).start()

    for u in range(DISPATCH_TILES):
        for k in range(TOP_K):
            for j in range(tm):
                row = pl.multiple_of(idx_smem[slot, 0, (u * TOP_K + k) * tm + j], RT)
                pltpu.make_async_copy(h_ref.at[pl.ds((u * tm + j) * RT, RT), :],
                                      xs_hbm.at[pl.ds(row, RT), :], sem).start(priority=j % 2)
    for k in range(TOP_K):
        pltpu.make_async_copy(h_ref, xs_hbm.at[pl.ds(0, DISPATCH_TILES * tm * RT), :], sem).wait()


def _dispatch(h2, dest_rows, pad_first, pad_cnt, n_used, n_blocks):
    tm = DISPATCH_TILES * TOKEN_TILE
    n_steps = h2.shape[0] // (tm * ROW_TILE)
    grid_spec = pltpu.PrefetchScalarGridSpec(
        num_scalar_prefetch=3,
        grid=(n_steps,),
        in_specs=[pl.BlockSpec(memory_space=pl.ANY),
                  pl.BlockSpec((tm * ROW_TILE, LANES), lambda s, pf, pc, nused: (s, 0))],
        out_specs=pl.BlockSpec(memory_space=pl.ANY),
        scratch_shapes=[pltpu.SMEM((2, 1, TOP_K * tm), jnp.int32),
                        pltpu.VMEM((EXPERT_BLOCK * ROW_TILE, LANES), F32),
                        pltpu.SemaphoreType.DMA((2,)),
                        pltpu.SemaphoreType.DMA],
    )
    return pl.pallas_call(
        functools.partial(_dispatch_kernel, n_steps=n_steps, n_blocks=n_blocks),
        grid_spec=grid_spec,
        out_shape=jax.ShapeDtypeStruct((n_blocks * EXPERT_BLOCK * ROW_TILE, LANES), F32),
        compiler_params=_cparams("arbitrary"),
        name="dispatch",
    )(pad_first, pad_cnt, n_used, dest_rows.reshape(n_steps, 1, TOP_K * tm), h2)


def _mlp_kernel(blk_e_ref, nused_ref, x_ref, wgu_ref, bgu_ref, wd_ref, bd_ref, o_ref, wgu_bf, wd_bf):
    i = pl.program_id(0)
    BM = EXPERT_BLOCK
    RT = ROW_TILE

    @pl.when(i < nused_ref[0])
    def _():
        new_expert = jnp.logical_or(i == 0, blk_e_ref[i] != blk_e_ref[jnp.maximum(i - 1, 0)])

        @pl.when(new_expert)
        def _():
            wgu_bf[...] = wgu_ref[0].astype(BF16)
            wd_bf[...] = wd_ref[0].astype(BF16)

        x = jnp.concatenate([x_ref[pl.ds(c, BM, stride=RT), :] for c in range(RT)],
                            axis=-1).astype(BF16)
        H = D_EXPERT // MLP_CHUNKS
        ups = []
        for h in range(MLP_CHUNKS):
            g = jnp.dot(x, wgu_bf[:, h * H:(h + 1) * H], preferred_element_type=F32)
            u = jnp.dot(x, wgu_bf[:, D_EXPERT + h * H:D_EXPERT + (h + 1) * H],
                        preferred_element_type=F32)
            ups.append((g + bgu_ref[0, :, h * H:(h + 1) * H],
                        u + bgu_ref[0, :, D_EXPERT + h * H:D_EXPERT + (h + 1) * H]))
        y = bd_ref[0]
        for h, (g, u) in enumerate(ups):
            glu = jnp.minimum(g, SWIGLU_LIMIT)
            lin = jnp.clip(u, -SWIGLU_LIMIT, SWIGLU_LIMIT)
            act = glu * (1.0 / (1.0 + jnp.exp(-SWIGLU_ALPHA * glu))) * (lin + 1.0)
            y = y + jnp.dot(act.astype(BF16), wd_bf[h * H:(h + 1) * H, :], preferred_element_type=F32)
        for c in range(RT):
            o_ref[pl.ds(c, BM, stride=RT), :] = y[:, c * LANES:(c + 1) * LANES]

    @pl.when(i >= nused_ref[0])
    def _():
        o_ref[...] = jnp.zeros_like(o_ref)


def _expert_mlp(xs, blk_e, n_used, w_gate_up, b_gate_up, w_down, b_down):
    D = D_MODEL
    BM = EXPERT_BLOCK
    n_blocks = blk_e.shape[0]
    wmap = lambda i, be, nu: (be[i], 0, 0)
    grid_spec = pltpu.PrefetchScalarGridSpec(
        num_scalar_prefetch=2,
        grid=(n_blocks,),
        in_specs=[pl.BlockSpec((BM * ROW_TILE, LANES), lambda i, be, nu: (jnp.minimum(i, nu[0] - 1), 0)),
                  pl.BlockSpec((1, D, 2 * D_EXPERT), wmap),
                  pl.BlockSpec((1, 1, 2 * D_EXPERT), wmap),
                  pl.BlockSpec((1, D_EXPERT, D), wmap),
                  pl.BlockSpec((1, 1, D), wmap)],
        out_specs=pl.BlockSpec((BM * ROW_TILE, LANES), lambda i, be, nu: (i, 0)),
        scratch_shapes=[pltpu.VMEM((D, 2 * D_EXPERT), BF16),
                        pltpu.VMEM((D_EXPERT, D), BF16)],
    )
    return pl.pallas_call(
        _mlp_kernel,
        grid_spec=grid_spec,
        out_shape=jax.ShapeDtypeStruct(xs.shape, F32),
        compiler_params=_cparams("arbitrary"),
        name="expert_mlp",
    )(blk_e, n_used, xs, w_gate_up, b_gate_up.reshape(N_EXPERTS, 1, -1), w_down,
      b_down.reshape(N_EXPERTS, 1, -1))


def _gather_combine_kernel(dest_hbm, ys_hbm, gate_ref, x1_ref, mod_ref, pf_ref, op_ref, os_ref,
                           idx_smem, yb0, yb1, idx_sem, row_sem, *, n_steps, n_p):
    s = pl.program_id(0)
    tc = COMBINE_TILE
    RT = ROW_TILE
    n_rows = TOP_K * tc
    last = n_steps - 1
    ybufs = (yb0, yb1)
    n_idx = 3

    def idx_copy(step, sl):
        return pltpu.make_async_copy(dest_hbm.at[step], idx_smem.at[sl], idx_sem.at[sl])

    def start_rows(islot, b):
        for r in range(n_rows):
            row = pl.multiple_of(idx_smem[islot, 0, r], RT)
            pltpu.make_async_copy(ys_hbm.at[pl.ds(row, RT), :], ybufs[b].at[pl.ds(r * RT, RT), :],
                                  row_sem.at[b]).start(priority=r % 2)

    def wait_rows(b):
        pltpu.make_async_copy(ys_hbm.at[pl.ds(0, n_rows * RT), :], ybufs[b], row_sem.at[b]).wait()

    @pl.when(s == 0)
    def _():
        for t in range(min(2, n_steps)):
            idx_copy(t, t).start()
        idx_copy(0, 0).wait()
        start_rows(0, 0)

    @pl.when(s + 2 < n_steps)
    def _():
        idx_copy(s + 2, lax.rem(s + 2, n_idx)).start()

    @pl.when(s + 1 < n_steps)
    def _():
        idx_copy(s + 1, lax.rem(s + 1, n_idx)).wait()

    def body(cur):
        wait_rows(cur)
        start_rows(lax.rem(jnp.minimum(s + 1, last), n_idx), 1 - cur)
        g = gate_ref[...]

        def rows(k):
            return jnp.concatenate([ybufs[cur][pl.ds(k * tc * RT + c, tc, stride=RT), :]
                                    for c in range(RT)], axis=-1)

        m = rows(0) * g[:, 0:1]
        for k in range(1, TOP_K):
            m = m + rows(k) * g[:, k:k + 1]
        res = x1_ref[...] + mod_ref[0, 5:6, :] * _rms(m, pf_ref[...])

        @pl.when(s < n_p)
        def _():
            op_ref[...] = res

        @pl.when(s >= n_p)
        def _():
            os_ref[...] = res

    for cur in range(2):
        pl.when(lax.rem(s, 2) == cur)(functools.partial(body, cur))

    @pl.when(s == last)
    def _():
        wait_rows(1 - last % 2)


def _gather_combine(dest_rows, ys, gates_tok, x1, mod, post_ffn, n_tok_p, n_tok_s, groups):
    D = D_MODEL
    tc = COMBINE_TILE
    n_p, n_s = n_tok_p // tc, n_tok_s // tc
    nt = n_p + n_s
    return pl.pallas_call(
        functools.partial(_gather_combine_kernel, n_steps=nt, n_p=n_p),
        grid=(nt,),
        in_specs=[pl.BlockSpec(memory_space=pl.ANY),
                  pl.BlockSpec(memory_space=pl.ANY),
                  pl.BlockSpec((tc, TOP_K), lambda s: (s, 0)),
                  pl.BlockSpec((tc, D), lambda s: (s, 0)),
                  pl.BlockSpec((1, 6, D), lambda s: (_batch_of_tile(s, groups, tc), 0, 0)),
                  pl.BlockSpec((1, D), lambda s: (0, 0))],
        out_specs=[pl.BlockSpec((tc, D), lambda s: (jnp.minimum(s, n_p - 1), 0)),
                   pl.BlockSpec((tc, D), lambda s: (jnp.maximum(s - n_p, 0), 0))],
        out_shape=[jax.ShapeDtypeStruct((n_tok_p, D), F32),
                   jax.ShapeDtypeStruct((n_tok_s, D), F32)],
        scratch_shapes=[pltpu.SMEM((3, 1, TOP_K * tc), jnp.int32),
                        pltpu.VMEM((TOP_K * tc * ROW_TILE, LANES), F32),
                        pltpu.VMEM((TOP_K * tc * ROW_TILE, LANES), F32),
                        pltpu.SemaphoreType.DMA((3,)),
                        pltpu.SemaphoreType.DMA((2,))],
        compiler_params=_cparams("arbitrary"),
        name="combine",
    )(dest_rows.reshape(nt, 1, TOP_K * tc), ys, gates_tok, x1, mod, post_ffn.reshape(1, D))


def _routing_dest(idx_t, rank_t, counts):
    BM = EXPERT_BLOCK
    nt, _, tm = idx_t.shape
    n_blocks = (nt * tm * TOP_K) // BM + N_EXPERTS
    padded = ((counts + BM - 1) // BM) * BM
    pad_end = jnp.cumsum(padded)
    pad_start = pad_end - padded
    experts = jnp.arange(N_EXPERTS, dtype=jnp.int32)
    start_of = jnp.sum(jnp.where(idx_t[..., None] == experts, pad_start, 0), axis=-1)
    dest = (start_of + rank_t).astype(jnp.int32)
    blk_first = jnp.arange(n_blocks, dtype=jnp.int32) * BM
    blk_e = jnp.minimum(jnp.sum((pad_end[None, :] <= blk_first[:, None]).astype(jnp.int32), axis=1),
                        N_EXPERTS - 1)
    n_used = (pad_end[-1:] // BM).astype(jnp.int32)
    pad_first = (pad_start + counts).astype(jnp.int32)
    pad_cnt = (padded - counts).astype(jnp.int32)
    return dest, blk_e, n_used, pad_first, pad_cnt, n_blocks


def kernel(x_prompt, x_sample, c_prompt, c_sample, w_ada, b_ada, pre_mix, post_mix, pre_ffn, post_ffn,
           w_in, na_rel_bias, t5_rel_bias, wa_sink, gn_na, gn_wa, w_out, w_router, b_router,
           w_gate_up, b_gate_up, w_down, b_down):
    depth = w_ada.shape[0]
    D = D_MODEL
    groups = (x_prompt.shape[:2], x_sample.shape[:2])
    n_tok_p = x_prompt.shape[0] * x_prompt.shape[1]
    n_tok_s = x_sample.shape[0] * x_sample.shape[1]
    n_tok = n_tok_p + n_tok_s
    yp = x_prompt.reshape(n_tok_p, D)
    ys = x_sample.reshape(n_tok_s, D)
    c_all = jnp.concatenate([c_prompt, c_sample], axis=0)
    for l in range(depth):
        mod = _ada(c_all, w_ada[l], b_ada[l]).reshape(c_all.shape[0], 6, D)
        qb0 = 3 * D_NA
        scale = HEAD_DIM ** -0.5 * LOG2E
        w_in_l = jnp.concatenate([w_in[l][:, :D_NA] * scale, w_in[l][:, D_NA:qb0],
                                  _wa_slab_order(w_in[l][:, qb0:qb0 + D_WA] * scale, axis=1),
                                  w_in[l][:, qb0 + D_WA:]], axis=1).astype(BF16)
        w_out_l = jnp.concatenate([w_out[l][:D_NA], _wa_slab_order(w_out[l][D_NA:], axis=0)],
                                  axis=0).astype(BF16)
        proj = _in_proj(yp, ys, mod, pre_mix[l], w_in_l, groups)
        na = _na_attention(proj, _na_bias_table(na_rel_bias[l] * LOG2E), gn_na[l], groups)
        wa = _wa_attention(proj, _wa_bias_table(t5_rel_bias * LOG2E), wa_sink[l] * LOG2E,
                           _wa_slab_order(gn_wa[l], axis=0), groups)
        x1, h2, idx_t, gate_t, rank_t, counts = _post_mix(
            na, wa, yp, ys, mod, w_out_l, post_mix[l], pre_ffn[l], w_router[l].T, b_router[l], groups)
        dest, blk_e, n_used, pad_first, pad_cnt, n_blocks = _routing_dest(
            idx_t, rank_t, counts.reshape(N_EXPERTS).astype(jnp.int32))
        dest_rows = dest * ROW_TILE
        xs = _dispatch(h2, dest_rows, pad_first, pad_cnt, n_used, n_blocks)
        ys_rows = _expert_mlp(xs, blk_e, n_used, w_gate_up[l], b_gate_up[l], w_down[l], b_down[l])
        gates_tok = jnp.transpose(gate_t, (0, 2, 1)).reshape(n_tok, TOP_K)
        halves = TOKEN_TILE // COMBINE_TILE
        dest_steps = jnp.transpose(dest_rows.reshape(-1, TOP_K, halves, COMBINE_TILE), (0, 2, 1, 3))
        yp, ys = _gather_combine(dest_steps, ys_rows, gates_tok, x1, mod, post_ffn[l], n_tok_p, n_tok_s,
                                 groups)
    return (yp.reshape(x_prompt.shape), ys.reshape(x_sample.shape))
```
